```python
import math
import jax
import jax.numpy as jnp
from jax import lax
import numpy as np

D_MODEL = 2048
BATCH = 2
SEQ = 4096
DEPTH = 1
DEC_BATCH = 32
DEC_SEQ = 1
PAST_LEN = 16384
PAGE_SIZE = 128

HEAD_DIM = 128
N_HEADS_A = 8
N_HEADS_B = 8
DK_B = 128
DV_B = 128
WIDTH_A = N_HEADS_A * HEAD_DIM
WIDTH_B = N_HEADS_B * DV_B
MIX_WIDTH = WIDTH_A + WIDTH_B
DILATIONS = ((128, 1), (512, 4), (2048, 16))
MAX_WINDOW = 2048
ROPE_THETA = 500000.0
ROPE_DIM = HEAD_DIM // 4
CONV_WIDTH = 4
CONV_CH = 2 * N_HEADS_B * DK_B + N_HEADS_B * DV_B
CHUNK = 64
EPS = 1e-6
IN_COLS = 4 * WIDTH_A + CONV_CH + WIDTH_B + 2 * N_HEADS_B

kernel_name = 'hybrid_dilated_swa_gated_deltanet_step'


def rmsnorm(x, g):
    xf = x.astype(jnp.float32)
    y = xf * lax.rsqrt(jnp.mean(xf * xf, axis=-1, keepdims=True) + EPS)
    return (y * g.astype(jnp.float32)).astype(x.dtype)


def l2norm(x):
    return x * lax.rsqrt(jnp.sum(x * x, axis=-1, keepdims=True) + EPS)


def partial_rope(x, pos):
    half = ROPE_DIM // 2
    inv = ROPE_THETA ** (-jnp.arange(half, dtype=jnp.float32) / half)
    ang = pos.astype(jnp.float32)[:, None] * inv[None, :]
    cos = jnp.cos(ang)[None, :, None, :]
    sin = jnp.sin(ang)[None, :, None, :]
    xf = x.astype(jnp.float32)
    x1, x2, rest = xf[..., :half], xf[..., half:ROPE_DIM], xf[..., ROPE_DIM:]
    out = jnp.concatenate([x1 * cos - x2 * sin, x2 * cos + x1 * sin, rest], axis=-1)
    return out.astype(x.dtype)


def split_projection(h, w_in):
    p = jnp.einsum('btd,de->bte', h, w_in)
    b, t, _ = p.shape
    cuts = [WIDTH_A, 2 * WIDTH_A, 3 * WIDTH_A, 4 * WIDTH_A, 4 * WIDTH_A + CONV_CH,
            4 * WIDTH_A + CONV_CH + WIDTH_B, 4 * WIDTH_A + CONV_CH + WIDTH_B + N_HEADS_B]
    q_a, k_a, v_a, z_a, qkv_b, z_b, beta_in, a_in = jnp.split(p, cuts, axis=-1)
    heads = lambda u: u.reshape(b, t, N_HEADS_A, HEAD_DIM)
    return heads(q_a), heads(k_a), heads(v_a), z_a, qkv_b, z_b, beta_in, a_in


def dilated_window_prefill(q, k, v, dilation, n_back):
    b, s, h, d = q.shape
    length = s // dilation
    blk = n_back
    nb = -(-length // blk)
    lp = nb * blk

    def split(x):
        x = jnp.swapaxes(x.reshape(b, length, dilation, h, d), 1, 2)
        x = jnp.pad(x, ((0, 0), (0, 0), (0, lp - length), (0, 0), (0, 0)))
        return x.reshape(b, dilation, nb, blk, h, d)

    def with_prev(x):
        prev = jnp.pad(x, ((0, 0), (0, 0), (1, 0), (0, 0), (0, 0), (0, 0)))[:, :, :-1]
        return jnp.concatenate([prev, x], axis=3)

    def merge(x):
        x = x.reshape((b, dilation, lp) + x.shape[4:])[:, :, :length]
        return jnp.swapaxes(x, 1, 2).reshape((b, s) + x.shape[3:])

    qb = split(q)
    kk = with_prev(split(k))
    vv = with_prev(split(v))
    scores = jnp.einsum('brnqhd,brnkhd->brnhqk', qb, kk) * (HEAD_DIM ** -0.5)
    qi = jnp.arange(blk)[:, None]
    kj = jnp.arange(2 * blk)[None, :]
    dist = blk + qi - kj
    band = (dist >= 0) & (dist <= n_back)
    has_prev = jnp.arange(nb)[:, None, None] > 0
    mask = band[None] & (has_prev | (kj[None] >= blk))
    scores = jnp.where(mask[None, None, :, None], scores, -jnp.inf)
    m = jnp.max(scores, axis=-1)
    p = jnp.exp(scores - m[..., None])
    den = jnp.sum(p, axis=-1)
    o = jnp.einsum('brnhqk,brnkhd->brnqhd', p, vv) / jnp.swapaxes(den, -1, -2)[..., None]
    lse = jnp.swapaxes(m + jnp.log(den), -1, -2)
    return merge(o), merge(lse)


def dilated_window_decode(q, k_all, v_all, n_past, dilation, n_back):
    t = q.shape[1]
    idx = n_past + jnp.arange(t)[:, None] - dilation * jnp.arange(n_back + 1)[None, :]
    valid = idx >= 0
    idx = jnp.maximum(idx, 0)
    kg = k_all[:, idx]
    vg = v_all[:, idx]
    scores = jnp.einsum('bthd,btmhd->bthm', q, kg) * (HEAD_DIM ** -0.5)
    scores = jnp.where(valid[None, :, None, :], scores, -jnp.inf)
    m = jnp.max(scores, axis=-1)
    p = jnp.exp(scores - m[..., None])
    den = jnp.sum(p, axis=-1)
    o = jnp.einsum('bthm,btmhd->bthd', p, vg) / den[..., None]
    return o, m + jnp.log(den)


def merge_by_denominator(outs, lses):
    wts = jax.nn.softmax(jnp.stack(lses), axis=0)
    return jnp.sum(wts[..., None] * jnp.stack(outs), axis=0)


def dilated_attention_prompt(q, k, v):
    qf, kf, vf = q.astype(jnp.float32), k.astype(jnp.float32), v.astype(jnp.float32)
    res = [dilated_window_prefill(qf, kf, vf, dil, win // dil) for win, dil in DILATIONS]
    return merge_by_denominator([r[0] for r in res], [r[1] for r in res])


def dilated_attention_sample(q, k_all, v_all, n_past):
    qf, kf, vf = q.astype(jnp.float32), k_all.astype(jnp.float32), v_all.astype(jnp.float32)
    res = [dilated_window_decode(qf, kf, vf, n_past, dil, win // dil) for win, dil in DILATIONS]
    return merge_by_denominator([r[0] for r in res], [r[1] for r in res])


def gated_delta_rule(q, k, v, g, beta, s0):
    b, t, h, _ = q.shape
    n = -(-t // CHUNK)
    tp = n * CHUNK

    def chunks(x):
        x = jnp.pad(x.astype(jnp.float32), [(0, 0), (0, tp - t)] + [(0, 0)] * (x.ndim - 2))
        x = x.reshape((b, n, CHUNK) + x.shape[2:])
        return jnp.moveaxis(x, 3, 1)

    qc = chunks(q) * (DK_B ** -0.5)
    kc, vc, gc, bc = chunks(k), chunks(v), chunks(g), chunks(beta)
    gcum = jnp.cumsum(gc, axis=-1)
    ar = jnp.arange(CHUNK)
    incl = ar[:, None] >= ar[None, :]
    strict = ar[:, None] > ar[None, :]
    decay = jnp.exp(jnp.where(incl, gcum[..., :, None] - gcum[..., None, :], -jnp.inf))
    kbeta = kc * bc[..., None]
    a_mat = jnp.where(strict, jnp.einsum('bhnid,bhnjd->bhnij', kbeta, kc) * decay, 0.0)
    eye = jnp.eye(CHUNK, dtype=jnp.float32)
    t_mat = lax.linalg.triangular_solve(a_mat + eye, jnp.broadcast_to(eye, a_mat.shape),
                                        left_side=True, lower=True)
    u = jnp.einsum('bhnij,bhnje->bhnie', t_mat, vc * bc[..., None])
    w = jnp.einsum('bhnij,bhnjd->bhnid', t_mat, kbeta * jnp.exp(gcum)[..., None])
    qk = jnp.einsum('bhnid,bhnjd->bhnij', qc, kc) * decay
    q_dec = qc * jnp.exp(gcum)[..., None]
    k_dec = kc * jnp.exp(gcum[..., -1:] - gcum)[..., None]
    g_tot = jnp.exp(gcum[..., -1])
    xs = tuple(jnp.moveaxis(a, 2, 0) for a in (q_dec, k_dec, u, w, qk, g_tot))

    def step(state, inp):
        q_i, k_i, u_i, w_i, qk_i, gt_i = inp
        v_new = u_i - jnp.einsum('bhcd,bhde->bhce', w_i, state)
        o_i = jnp.einsum('bhcd,bhde->bhce', q_i, state) + jnp.einsum('bhcj,bhje->bhce', qk_i, v_new)
        state = state * gt_i[..., None, None] + jnp.einsum('bhcd,bhce->bhde', k_i, v_new)
        return state, o_i

    s_fin, o = lax.scan(step, s0.astype(jnp.float32), xs)
    o = jnp.moveaxis(jnp.moveaxis(o, 0, 2), 1, 3).reshape(b, tp, h, DV_B)[:, :t]
    return o, s_fin


def deltanet_branch(qkv_in, conv_buf, s0, beta_in, a_in, conv_w, a_log, dt_bias, onorm_w):
    b, t, _ = qkv_in.shape
    ext = jnp.concatenate([conv_buf.astype(qkv_in.dtype), qkv_in], axis=1)
    extf = ext.astype(jnp.float32)
    wf = conv_w.astype(jnp.float32)
    conv = extf[:, 0:t] * wf[0]
    for i in range(1, CONV_WIDTH):
        conv = conv + extf[:, i:i + t] * wf[i]
    c = jax.nn.silu(conv)
    q_b, k_b, v_b = jnp.split(c, [N_HEADS_B * DK_B, 2 * N_HEADS_B * DK_B], axis=-1)
    q_b = l2norm(q_b.reshape(b, t, N_HEADS_B, DK_B))
    k_b = l2norm(k_b.reshape(b, t, N_HEADS_B, DK_B))
    v_b = v_b.reshape(b, t, N_HEADS_B, DV_B)
    beta = jax.nn.sigmoid(beta_in.astype(jnp.float32))
    g = -jnp.exp(a_log.astype(jnp.float32)) * jax.nn.softplus(
        a_in.astype(jnp.float32) + dt_bias.astype(jnp.float32))
    o, s_new = gated_delta_rule(q_b, k_b, v_b, g, beta, s0)
    o = o * lax.rsqrt(jnp.mean(o * o, axis=-1, keepdims=True) + EPS) * onorm_w.astype(jnp.float32)
    return o, ext[:, -(CONV_WIDTH - 1):], s_new


def output_sublayer(x, o_a, z_a, o_b, z_b, w_out, g_post):
    b, t, _ = x.shape
    gated = jnp.concatenate([o_a.reshape(b, t, WIDTH_A) * jax.nn.silu(z_a.astype(jnp.float32)),
                             o_b.reshape(b, t, WIDTH_B) * jax.nn.silu(z_b.astype(jnp.float32))],
                            axis=-1).astype(x.dtype)
    y = jnp.einsum('bte,ed->btd', gated, w_out)
    return x + rmsnorm(y, g_post)


def setup_inputs(seed: int = 0) -> dict:
    key = jax.random.key(seed)
    ks = jax.random.split(key, 14)
    win_buf = min(MAX_WINDOW, PAST_LEN)
    f32 = jnp.float32
    dt = jnp.exp(jax.random.uniform(ks[10], (DEPTH, N_HEADS_B), f32, math.log(1e-3), math.log(1e-1)))
    return {
        'x_prompt': jax.random.normal(ks[0], (BATCH, SEQ, D_MODEL), f32),
        'x_sample': jax.random.normal(ks[1], (DEC_BATCH, DEC_SEQ, D_MODEL), f32),
        'cache_win_k': jax.random.normal(ks[2], (DEPTH, DEC_BATCH, win_buf, N_HEADS_A, HEAD_DIM), f32),
        'cache_win_v': jax.random.normal(ks[3], (DEPTH, DEC_BATCH, win_buf, N_HEADS_A, HEAD_DIM), f32),
        'state_conv': jax.random.normal(ks[4], (DEPTH, DEC_BATCH, CONV_WIDTH - 1, CONV_CH), f32),
        'state_delta': 0.1 * jax.random.normal(ks[5], (DEPTH, DEC_BATCH, N_HEADS_B, DK_B, DV_B), f32),
        'g_pre': 1.0 + 0.02 * jax.random.normal(ks[6], (DEPTH, D_MODEL), f32),
        'w_in': jax.random.normal(ks[7], (DEPTH, D_MODEL, IN_COLS), f32) * D_MODEL ** -0.5,
        'conv_w': jax.random.normal(ks[8], (DEPTH, CONV_WIDTH, CONV_CH), f32) * CONV_WIDTH ** -0.5,
        'a_log': jnp.log(jax.random.uniform(ks[9], (DEPTH, N_HEADS_B), f32, 1.0, 16.0)),
        'dt_bias': dt + jnp.log(-jnp.expm1(-dt)),
        'onorm_w': 1.0 + 0.02 * jax.random.normal(ks[11], (DEPTH, DV_B), f32),
        'w_out': jax.random.normal(ks[12], (DEPTH, MIX_WIDTH, D_MODEL), f32) * MIX_WIDTH ** -0.5,
        'g_post': 1.0 + 0.02 * jax.random.normal(ks[13], (DEPTH, D_MODEL), f32),
    }


def reference(x_prompt, x_sample, cache_win_k, cache_win_v, state_conv, state_delta,
              g_pre, w_in, conv_w, a_log, dt_bias, onorm_w, w_out, g_post):
    b, s, _ = x_prompt.shape
    db, t, _ = x_sample.shape
    keep = min(MAX_WINDOW, s)
    n_past = cache_win_k.shape[2]
    pos_p = jnp.arange(s, dtype=jnp.int32)
    pos_s = PAST_LEN + jnp.arange(t, dtype=jnp.int32)
    yp, ys = x_prompt, x_sample
    wk_p, wv_p, cv_p, st_p = [], [], [], []
    wk_s, wv_s, cv_s, st_s = [], [], [], []
    for l in range(DEPTH):
        q_a, k_a, v_a, z_a, qkv_b, z_b, beta_in, a_in = split_projection(rmsnorm(yp, g_pre[l]), w_in[l])
        q_a, k_a = partial_rope(q_a, pos_p), partial_rope(k_a, pos_p)
        o_a = dilated_attention_prompt(q_a, k_a, v_a)
        conv0 = jnp.zeros((b, CONV_WIDTH - 1, CONV_CH), qkv_b.dtype)
        s0 = jnp.zeros((b, N_HEADS_B, DK_B, DV_B), jnp.float32)
        o_b, conv_new, s_new = deltanet_branch(qkv_b, conv0, s0, beta_in, a_in, conv_w[l],
                                               a_log[l], dt_bias[l], onorm_w[l])
        yp = output_sublayer(yp, o_a, z_a, o_b, z_b, w_out[l], g_post[l])
        wk_p.append(k_a[:, s - keep:])
        wv_p.append(v_a[:, s - keep:])
        cv_p.append(conv_new)
        st_p.append(s_new)

        q_a, k_a, v_a, z_a, qkv_b, z_b, beta_in, a_in = split_projection(rmsnorm(ys, g_pre[l]), w_in[l])
        q_a, k_a = partial_rope(q_a, pos_s), partial_rope(k_a, pos_s)
        k_all = jnp.concatenate([cache_win_k[l].astype(k_a.dtype), k_a], axis=1)
        v_all = jnp.concatenate([cache_win_v[l].astype(v_a.dtype), v_a], axis=1)
        o_a = dilated_attention_sample(q_a, k_all, v_all, n_past)
        o_b, conv_new, s_new = deltanet_branch(qkv_b, state_conv[l], state_delta[l], beta_in, a_in,
                                               conv_w[l], a_log[l], dt_bias[l], onorm_w[l])
        ys = output_sublayer(ys, o_a, z_a, o_b, z_b, w_out[l], g_post[l])
        wk_s.append(k_a)
        wv_s.append(v_a)
        cv_s.append(conv_new)
        st_s.append(s_new)
    return (yp, ys, jnp.stack(wk_p), jnp.stack(wv_p), jnp.stack(cv_p), jnp.stack(st_p),
            jnp.stack(wk_s), jnp.stack(wv_s), jnp.stack(cv_s), jnp.stack(st_s))
```

```python
import functools
import math

import jax
import jax.numpy as jnp
from jax import lax
from jax.experimental import pallas as pl
from jax.experimental.pallas import tpu as pltpu

D_MODEL = 2048
HEAD_DIM = 128
N_HEADS_A = 8
N_HEADS_B = 8
DK_B = 128
DV_B = 128
WIDTH_A = N_HEADS_A * HEAD_DIM
WIDTH_B = N_HEADS_B * DV_B
DILATIONS = ((128, 1), (512, 4), (2048, 16))
MAX_WINDOW = 2048
ROPE_THETA = 500000.0
ROPE_DIM = HEAD_DIM // 4
CONV_WIDTH = 4
CONV_CH = 2 * N_HEADS_B * DK_B + N_HEADS_B * DV_B
CHUNK = 64
EPS = 1e-6
PAST_LEN = 16384
MAIN_COLS = 4 * WIDTH_A + CONV_CH + WIDTH_B
TAIL_COLS = 2 * N_HEADS_B
LANES = 128
NEG_BIG = -1e30
VMEM_LIMIT = 56 * 1024 * 1024

COL_Q, COL_K, COL_V, COL_ZA = 0, WIDTH_A, 2 * WIDTH_A, 3 * WIDTH_A
COL_QKVB = 4 * WIDTH_A
COL_ZB = COL_QKVB + CONV_CH


def _cparams(sem):
    return pltpu.CompilerParams(dimension_semantics=sem, vmem_limit_bytes=VMEM_LIMIT)


def _proj_kernel(x_ref, g_ref, w_ref, wt_ref, cos_ref, sin_ref,
                 p_ref, tail_ref, tailt_ref, h_ref, *, n_rope_tiles, tn):
    j = pl.program_id(1)

    @pl.when(j == 0)
    def _():
        x = x_ref[...]
        y = x * lax.rsqrt(jnp.mean(x * x, axis=-1, keepdims=True) + EPS)
        h = (y * g_ref[...]).astype(jnp.bfloat16)
        h_ref[...] = h
        wt = wt_ref[...]
        tail_ref[...] = lax.dot_general(h, wt, (((1,), (1,)), ((), ())),
                                        preferred_element_type=jnp.float32)
        tailt_ref[...] = lax.dot_general(wt[:2 * 8], h, (((1,), (1,)), ((), ())),
                                         preferred_element_type=jnp.float32)

    acc = jnp.dot(h_ref[...], w_ref[...], preferred_element_type=jnp.float32)

    @pl.when(j < n_rope_tiles)
    def _():
        c = cos_ref[...]
        s = sin_ref[...]
        lane = lax.broadcasted_iota(jnp.int32, c.shape, 1)
        scale = jnp.where(j * tn < WIDTH_A, HEAD_DIM ** -0.5, 1.0).astype(jnp.float32)
        for hd in range(tn // LANES):
            a = acc[:, hd * LANES:(hd + 1) * LANES]
            swapped = jnp.where(lane < ROPE_DIM // 2,
                                pltpu.roll(a, LANES - ROPE_DIM // 2, 1),
                                pltpu.roll(a, ROPE_DIM // 2, 1))
            r = (a * c + swapped * s) * scale
            p_ref[:, hd * LANES:(hd + 1) * LANES] = r.astype(p_ref.dtype)

    @pl.when(j >= n_rope_tiles)
    def _():
        p_ref[...] = acc.astype(p_ref.dtype)


def _rope_tables(pos):
    half = ROPE_DIM // 2
    inv = ROPE_THETA ** (-jnp.arange(half, dtype=jnp.float32) / half)
    ang = pos.astype(jnp.float32)[:, None] * inv[None, :]
    cos, sin = jnp.cos(ang), jnp.sin(ang)
    n = pos.shape[0]
    ones = jnp.ones((n, LANES - ROPE_DIM), jnp.float32)
    c = jnp.concatenate([cos, cos, ones], axis=1)
    s = jnp.concatenate([-sin, sin, jnp.zeros_like(ones)], axis=1)
    return c, s


def _project(x2d, g_pre, w_main, w_tail_t, cos_t, sin_t, *, tm, tn=1024):
    m = x2d.shape[0]
    n_i, n_j = m // tm, MAIN_COLS // tn
    n_rope_tiles = (2 * WIDTH_A) // tn
    kern = functools.partial(_proj_kernel, n_rope_tiles=n_rope_tiles, tn=tn)
    return pl.pallas_call(
        kern,
        grid=(n_i, n_j),
        in_specs=[
            pl.BlockSpec((tm, D_MODEL), lambda i, j: (i, 0)),
            pl.BlockSpec((1, D_MODEL), lambda i, j: (0, 0)),
            pl.BlockSpec((D_MODEL, tn), lambda i, j: (0, j)),
            pl.BlockSpec((LANES, D_MODEL), lambda i, j: (0, 0)),
            pl.BlockSpec((tm, LANES), lambda i, j: (i, 0)),
            pl.BlockSpec((tm, LANES), lambda i, j: (i, 0)),
        ],
        out_specs=[
            pl.BlockSpec((tm, tn), lambda i, j: (i, j)),
            pl.BlockSpec((tm, LANES), lambda i, j: (i, 0)),
            pl.BlockSpec((TAIL_COLS, tm), lambda i, j: (0, i)),
        ],
        out_shape=[
            jax.ShapeDtypeStruct((m, MAIN_COLS), jnp.bfloat16),
            jax.ShapeDtypeStruct((m, LANES), jnp.float32),
            jax.ShapeDtypeStruct((TAIL_COLS, m), jnp.float32),
        ],
        scratch_shapes=[pltpu.VMEM((tm, D_MODEL), jnp.bfloat16)],
        compiler_params=_cparams(("arbitrary", "arbitrary")),
        name="proj",
    )(x2d, g_pre, w_main, w_tail_t, cos_t, sin_t)


RES = 16
SLAB = 128


def _band_bias(kind):
    import numpy as np
    if kind == 16:
        nq, nk = 128, 256
        lq = np.arange(nq)[:, None]
        kap = np.arange(nk)[None, :]
        lk = kap - 128
        prev = kap < 128
    elif kind == 4:
        nq, nk = 128, 256
        rho = np.arange(nq)[:, None]
        lq = 4 * (rho % 32) + rho // 32
        kap = np.arange(nk)[None, :]
        lk = 4 * (kap % 64 - 32) + kap // 64
        prev = (kap % 64) < 32
    else:
        nq, nk = 256, 512
        rho = np.arange(nq)[:, None]
        lq = 16 * (rho % 16) + rho // 16
        kap = np.arange(nk)[None, :]
        lk = 16 * (kap % 32 - 16) + kap // 32
        prev = (kap % 32) < 16
    dist = lq - lk
    band = (dist >= 0) & (dist <= 128)
    out = np.stack([band & ~prev, band])
    return jnp.asarray(np.where(out, 0.0, NEG_BIG), dtype=jnp.float32)


def _nt_dot(a, b):
    return lax.dot_general(a, b, (((1,), (1,)), ((), ())), preferred_element_type=jnp.float32)


def _attn_kernel(q_ref, kc_ref, kp_ref, vc_ref, vp_ref, b16_ref, b4_ref, b1_ref,
                 o_ref, acc_ref, m_ref, l_ref):
    hp = jnp.minimum(pl.program_id(2), 1)

    def block(q, k, v, bias, prev):
        s = _nt_dot(q, k) + bias
        m_blk = jnp.max(s, axis=1, keepdims=True)
        if prev is None:
            m_new = m_blk
        else:
            m_prev, l_prev, acc_prev = prev
            m_new = jnp.maximum(m_prev, m_blk)
        p = jnp.exp(s - m_new)
        l_new = jnp.sum(p, axis=1, keepdims=True)
        acc_new = jnp.dot(p.astype(v.dtype), v, preferred_element_type=jnp.float32)
        if prev is not None:
            alpha = jnp.exp(m_prev - m_new)
            l_new = alpha * l_prev + l_new
            acc_new = alpha * acc_prev + acc_new
        return m_new, l_new, acc_new

    for r in range(RES):
        k = jnp.concatenate([kp_ref[0, r], kc_ref[0, r]], axis=0)
        v = jnp.concatenate([vp_ref[0, r], vc_ref[0, r]], axis=0)
        m, l, acc = block(q_ref[0, r], k, v, b16_ref[hp], None)
        m_ref[r], l_ref[r], acc_ref[r] = m, l, acc

    def update(slabs, sl, q, k, v, bias):
        m_prev = jnp.concatenate([m_ref[r, sl, :] for r in slabs], axis=0)
        l_prev = jnp.concatenate([l_ref[r, sl, :] for r in slabs], axis=0)
        acc_prev = jnp.concatenate([acc_ref[r, sl, :] for r in slabs], axis=0)
        m, l, acc = block(q, k, v, bias, (m_prev, l_prev, acc_prev))
        n = sl.stop - sl.start
        for i, r in enumerate(slabs):
            m_ref[r, sl, :] = m[i * n:(i + 1) * n]
            l_ref[r, sl, :] = l[i * n:(i + 1) * n]
            acc_ref[r, sl, :] = acc[i * n:(i + 1) * n]

    def gather(cur_ref, prev_ref, slabs, start, n):
        if start == 0:
            parts = [jnp.concatenate([prev_ref[0, r, SLAB - n:SLAB, :], cur_ref[0, r, 0:n, :]], axis=0)
                     for r in slabs]
        else:
            parts = [cur_ref[0, r, start - n:start + n, :] for r in slabs]
        return jnp.concatenate(parts, axis=0)

    for r4 in range(4):
        slabs = [r4 + 4 * i for i in range(4)]
        for jj in range(SLAB // 32):
            sl = slice(32 * jj, 32 * jj + 32)
            q = jnp.concatenate([q_ref[0, r, sl, :] for r in slabs], axis=0)
            k = gather(kc_ref, kp_ref, slabs, 32 * jj, 32)
            v = gather(vc_ref, vp_ref, slabs, 32 * jj, 32)
            update(slabs, sl, q, k, v, b4_ref[hp] if jj == 0 else b4_ref[1])

    slabs = list(range(RES))
    for jj in range(SLAB // 16):
        sl = slice(16 * jj, 16 * jj + 16)
        q = jnp.concatenate([q_ref[0, r, sl, :] for r in slabs], axis=0)
        k = gather(kc_ref, kp_ref, slabs, 16 * jj, 16)
        v = gather(vc_ref, vp_ref, slabs, 16 * jj, 16)
        update(slabs, sl, q, k, v, b1_ref[hp] if jj == 0 else b1_ref[1])

    for r in range(RES):
        o_ref[0, r] = (acc_ref[r] / l_ref[r]).astype(o_ref.dtype)


def _prompt_attention(qkv16):
    b, _, rows, _ = qkv16.shape
    n_sb = rows // SLAB
    kcol, vcol = WIDTH_A // LANES, 2 * WIDTH_A // LANES
    blk = (1, RES, SLAB, LANES)
    cur = lambda off: pl.BlockSpec(blk, lambda bi, h, n: (bi, 0, n, off + h))
    prv = lambda off: pl.BlockSpec(blk, lambda bi, h, n: (bi, 0, jnp.maximum(n - 1, 0), off + h))
    const = lambda shp: pl.BlockSpec(shp, lambda bi, h, n: (0, 0, 0))
    return pl.pallas_call(
        _attn_kernel,
        grid=(b, N_HEADS_A, n_sb),
        in_specs=[cur(0), cur(kcol), prv(kcol), cur(vcol), prv(vcol),
                  const((2, 128, 256)), const((2, 128, 256)), const((2, 256, 512))],
        out_specs=pl.BlockSpec(blk, lambda bi, h, n: (bi, 0, n, h)),
        out_shape=jax.ShapeDtypeStruct((b, RES, rows, WIDTH_A), jnp.bfloat16),
        scratch_shapes=[pltpu.VMEM((RES, SLAB, LANES), jnp.float32),
                        pltpu.VMEM((RES, SLAB, 1), jnp.float32),
                        pltpu.VMEM((RES, SLAB, 1), jnp.float32)],
        compiler_params=_cparams(("arbitrary", "arbitrary", "arbitrary")),
        name="prompt_attn",
    )(qkv16, qkv16, qkv16, qkv16, qkv16, _band_bias(16), _band_bias(4), _band_bias(1))


DN_STEP = 128
CARRY = 8
INV_PASSES = 3


def _split_bf16(x, n):
    parts, r = [], x
    for _ in range(n):
        hi = r.astype(jnp.bfloat16)
        parts.append(hi)
        r = r - hi.astype(jnp.float32)
    return parts


def _mm(a, b, passes):
    if passes == 1:
        return jnp.dot(a.astype(jnp.bfloat16), b.astype(jnp.bfloat16), preferred_element_type=jnp.float32)
    a1, a2 = _split_bf16(a, 2)
    b1, b2 = _split_bf16(b, 2)
    d = functools.partial(jnp.dot, preferred_element_type=jnp.float32)
    return d(a1, b1) + (d(a1, b2) + d(a2, b1))


def _softplus(x):
    return jnp.maximum(x, 0.0) + jnp.log1p(jnp.exp(-jnp.abs(x)))


def _silu(x):
    return x * jax.nn.sigmoid(x)


def _unit_lower_inverse(a):
    n = a.shape[0]
    row = lax.broadcasted_iota(jnp.int32, (n, n), 0)
    col = lax.broadcasted_iota(jnp.int32, (n, n), 1)
    x = jnp.where(row == col, 1.0, 0.0) - a
    p = _mm(a, a, INV_PASSES)
    k = 2
    while True:
        x = x + _mm(x, p, INV_PASSES)
        k *= 2
        if k >= n:
            return x
        p = _mm(p, p, INV_PASSES)


def _dn_kernel(qb_ref, kb_ref, vb_ref, tail_ref, tailt_ref, cw_ref, prow_ref, pcol_ref, onw_ref,
               o_ref, s_ref, carry_ref):
    c = pl.program_id(1)

    @pl.when(c == 0)
    def _():
        s_ref[...] = jnp.zeros_like(s_ref)
        carry_ref[...] = jnp.zeros_like(carry_ref)

    f32 = jnp.float32
    tb = DN_STEP
    cw = cw_ref[...]
    conv = []
    for part, ref in enumerate((qb_ref, kb_ref, vb_ref)):
        x = ref[...].astype(f32)
        xe = jnp.concatenate([carry_ref[part], x], axis=0)
        w = cw[:, part * WIDTH_B:(part + 1) * WIDTH_B]
        y = xe[CARRY - 3:CARRY - 3 + tb] * w[0:1]
        for i in range(1, CONV_WIDTH):
            y = y + xe[CARRY - 3 + i:CARRY - 3 + i + tb] * w[i:i + 1]
        conv.append(_silu(y))
        carry_ref[part] = x[tb - CARRY:]
    qc, kc, vc = conv

    t = tail_ref[...]
    beta_c = jax.nn.sigmoid(t)
    g_c = -jnp.exp(prow_ref[0:1]) * _softplus(t + prow_ref[1:2])
    tt = tailt_ref[...]
    g_r = -jnp.exp(pcol_ref[:, 0:1]) * _softplus(tt + pcol_ref[:, 1:2])

    n = CHUNK
    row = lax.broadcasted_iota(jnp.int32, (n, n), 0)
    col = lax.broadcasted_iota(jnp.int32, (n, n), 1)
    incl = row >= col
    strict = row > col
    ltri = jnp.where(incl, 1.0, 0.0).astype(jnp.bfloat16)
    utri = jnp.where(row <= col, 1.0, 0.0).astype(jnp.bfloat16)
    dot = functools.partial(jnp.dot, preferred_element_type=f32)

    for ci in range(tb // n):
        rs = slice(ci * n, (ci + 1) * n)
        gcum_c = sum(dot(ltri, part) for part in _split_bf16(g_c[rs], 3))
        gcum_r = sum(dot(part, utri) for part in _split_bf16(g_r[:, rs], 3))
        for h in range(N_HEADS_B):
            hs = slice(h * DK_B, (h + 1) * DK_B)
            q = qc[rs, hs]
            k = kc[rs, hs]
            v = vc[rs, hs]
            q = q * lax.rsqrt(jnp.sum(q * q, axis=-1, keepdims=True) + EPS) * (DK_B ** -0.5)
            k = k * lax.rsqrt(jnp.sum(k * k, axis=-1, keepdims=True) + EPS)
            beta = beta_c[rs, h:h + 1]
            gc = gcum_c[:, N_HEADS_B + h:N_HEADS_B + h + 1]
            gr = gcum_r[N_HEADS_B + h:N_HEADS_B + h + 1, :]
            g_last = gc[n - 1:n, :]
            decay = jnp.exp(jnp.where(incl, gc - gr, NEG_BIG))
            kbeta = k * beta
            kq = _nt_dot(jnp.concatenate([kbeta, q], axis=0).astype(jnp.bfloat16),
                         k.astype(jnp.bfloat16))
            a_mat = jnp.where(strict, kq[:n] * decay, 0.0)
            qk = kq[n:] * decay
            t_mat = _unit_lower_inverse(a_mat)
            eg = jnp.exp(gc)
            rhs = jnp.concatenate([v * beta, kbeta * eg], axis=1).astype(jnp.bfloat16)
            uw = dot(t_mat.astype(jnp.bfloat16), rhs)
            u, w = uw[:, :DV_B], uw[:, DV_B:]
            state = s_ref[0, h]
            wq = jnp.concatenate([w, q * eg], axis=0).astype(jnp.bfloat16)
            ws = dot(wq, state.astype(jnp.bfloat16))
            v_new = u - ws[:n]
            o = ws[n:] + dot(qk.astype(jnp.bfloat16), v_new.astype(jnp.bfloat16))
            k_dec = (k * jnp.exp(g_last - gc)).astype(jnp.bfloat16)
            s_ref[0, h] = state * jnp.exp(g_last) + lax.dot_general(
                k_dec, v_new.astype(jnp.bfloat16), (((0,), (0,)), ((), ())),
                preferred_element_type=f32)
            o = o * lax.rsqrt(jnp.mean(o * o, axis=-1, keepdims=True) + EPS) * onw_ref[...]
            o_ref[rs, hs] = o.astype(o_ref.dtype)


def _deltanet_prompt(p, tail, tail_t, conv_w, a_log, dt_bias, onorm_w, batch):
    m = p.shape[0]
    steps = m // batch // DN_STEP
    wb = WIDTH_B // 1
    qcol = COL_QKVB // wb
    zeros8 = jnp.zeros((N_HEADS_B,), jnp.float32)
    gate = jnp.stack([jnp.concatenate([zeros8, a_log.astype(jnp.float32)]),
                      jnp.concatenate([zeros8, dt_bias.astype(jnp.float32)])])
    prow = jnp.pad(gate, ((0, 6), (0, LANES - TAIL_COLS)))
    pcol = jnp.pad(gate.T, ((0, 0), (0, LANES - 2)))
    tok = lambda cb: pl.BlockSpec((DN_STEP, wb), lambda b, c: (b * steps + c, cb))
    full = lambda shp: pl.BlockSpec(shp, lambda b, c: (0,) * len(shp))
    return pl.pallas_call(
        _dn_kernel,
        grid=(batch, steps),
        in_specs=[tok(qcol), tok(qcol + 1), tok(qcol + 2),
                  pl.BlockSpec((DN_STEP, LANES), lambda b, c: (b * steps + c, 0)),
                  pl.BlockSpec((TAIL_COLS, DN_STEP), lambda b, c: (0, b * steps + c)),
                  full((CONV_WIDTH, CONV_CH)), full((8, LANES)), full((TAIL_COLS, LANES)),
                  full((1, DV_B))],
        out_specs=[pl.BlockSpec((DN_STEP, WIDTH_B), lambda b, c: (b * steps + c, 0)),
                   pl.BlockSpec((1, N_HEADS_B, DK_B, DV_B), lambda b, c: (b, 0, 0, 0))],
        out_shape=[jax.ShapeDtypeStruct((m, WIDTH_B), jnp.bfloat16),
                   jax.ShapeDtypeStruct((batch, N_HEADS_B, DK_B, DV_B), jnp.float32)],
        scratch_shapes=[pltpu.VMEM((3, CARRY, WIDTH_B), jnp.float32)],
        compiler_params=_cparams(("arbitrary", "arbitrary")),
        name="deltanet_prompt",
    )(p, p, p, tail, tail_t, conv_w.astype(jnp.float32), prow, pcol,
      onorm_w.astype(jnp.float32)[None])


def _out_kernel(oa_ref, za_ref, ob_ref, zb_ref, w_ref, g_ref, x_ref, y_ref):
    f32 = jnp.float32
    ga = (oa_ref[...].astype(f32) * _silu(za_ref[...].astype(f32))).astype(jnp.bfloat16)
    gb = (ob_ref[...].astype(f32) * _silu(zb_ref[...].astype(f32))).astype(jnp.bfloat16)
    y = (jnp.dot(ga, w_ref[:WIDTH_A, :], preferred_element_type=f32)
         + jnp.dot(gb, w_ref[WIDTH_A:, :], preferred_element_type=f32))
    y = y * lax.rsqrt(jnp.mean(y * y, axis=-1, keepdims=True) + EPS) * g_ref[...]
    y_ref[...] = x_ref[...] + y


def _output_sublayer(o_a, p, o_b, w_out, g_post, x2d, *, tm):
    m = x2d.shape[0]
    row = lambda width, cb: pl.BlockSpec((tm, width), lambda i: (i, cb))
    return pl.pallas_call(
        _out_kernel,
        grid=(m // tm,),
        in_specs=[row(WIDTH_A, 0), row(WIDTH_A, COL_ZA // WIDTH_A), row(WIDTH_B, 0),
                  row(WIDTH_B, COL_ZB // WIDTH_B),
                  pl.BlockSpec((WIDTH_A + WIDTH_B, D_MODEL), lambda i: (0, 0)),
                  pl.BlockSpec((1, D_MODEL), lambda i: (0, 0)),
                  row(D_MODEL, 0)],
        out_specs=row(D_MODEL, 0),
        out_shape=jax.ShapeDtypeStruct((m, D_MODEL), jnp.float32),
        compiler_params=_cparams(("arbitrary",)),
        name="out_proj",
    )(o_a, p, o_b, p, w_out, g_post, x2d)


def _decode_attn_kernel(q_ref, kn_ref, vn_ref, k1_ref, v1_ref, k4_ref, v4_ref, k16_ref, v16_ref, o_ref):
    f32 = jnp.float32
    q = q_ref[0].astype(f32)
    kn, vn = kn_ref[0].astype(f32), vn_ref[0].astype(f32)
    s_new = jnp.sum(q * kn, axis=-1, keepdims=True)
    scores = [jnp.sum(k_ref[0] * q[None], axis=-1, keepdims=True) for k_ref in (k1_ref, k4_ref, k16_ref)]
    m = s_new
    for s in scores:
        m = jnp.maximum(m, jnp.max(s, axis=0))
    p_new = len(DILATIONS) * jnp.exp(s_new - m)
    den = p_new
    acc = p_new * vn
    for s, v_ref in zip(scores, (v1_ref, v4_ref, v16_ref)):
        p = jnp.exp(s - m[None])
        den = den + jnp.sum(p, axis=0)
        acc = acc + jnp.sum(p * v_ref[0], axis=0)
    o_ref[0] = (acc / den).astype(o_ref.dtype)


def _decode_attention(q, k_new, v_new, cache_k, cache_v):
    b, win, h, dd = cache_k.shape
    nb = 128
    views, specs = [], []
    for window, dil in DILATIONS:
        assert window // dil == nb and win % (nb * dil) == 0
        blk_idx = win // (nb * dil) - 1
        if dil == 1:
            spec = pl.BlockSpec((1, nb, h, dd), lambda i, bi=blk_idx: (i, bi, 0, 0))
            view = lambda c: c
        else:
            spec = pl.BlockSpec((1, nb, None, h, dd), lambda i, bi=blk_idx: (i, bi, 0, 0, 0))
            view = lambda c, dil=dil: c.reshape(b, win // dil, dil, h, dd)
        specs += [spec, spec]
        views += [view(cache_k), view(cache_v)]
    tok = pl.BlockSpec((1, h, dd), lambda i: (i, 0, 0))
    return pl.pallas_call(
        _decode_attn_kernel,
        grid=(b,),
        in_specs=[tok, tok, tok] + specs,
        out_specs=tok,
        out_shape=jax.ShapeDtypeStruct((b, h, dd), jnp.bfloat16),
        compiler_params=_cparams(("arbitrary",)),
        name="decode_attn",
    )(q, k_new, v_new, *views)


def _decode_dn_kernel(x_ref, cb_ref, cw_ref, gate_ref, prm_ref, onw_ref, s_ref,
                      o_ref, cbo_ref, so_ref):
    f32 = jnp.float32
    hh = N_HEADS_B
    xn = x_ref[0]
    cw = cw_ref[...]
    y = xn * cw[CONV_WIDTH - 1]
    for i in range(CONV_WIDTH - 1):
        y = y + cb_ref[0, i] * cw[i]
        cbo_ref[0, i] = cb_ref[0, i + 1] if i + 1 < CONV_WIDTH - 1 else xn
    y = _silu(y)
    q, k, v = y[:hh], y[hh:2 * hh], y[2 * hh:]
    q = q * lax.rsqrt(jnp.sum(q * q, axis=-1, keepdims=True) + EPS) * (DK_B ** -0.5)
    k = k * lax.rsqrt(jnp.sum(k * k, axis=-1, keepdims=True) + EPS)
    gate = gate_ref[0]
    beta = jax.nn.sigmoid(gate[:hh])
    decay = jnp.exp(-jnp.exp(prm_ref[:hh]) * _softplus(gate[hh:] + prm_ref[hh:]))
    qk = jnp.sum(q * k, axis=-1, keepdims=True)
    zeros = jnp.zeros((8 - 2, DK_B), jnp.bfloat16)
    outs = []
    for h in range(hh):
        state = s_ref[0, h]
        a = decay[h:h + 1]
        lhs = jnp.concatenate([k[h:h + 1].astype(jnp.bfloat16), q[h:h + 1].astype(jnp.bfloat16), zeros], axis=0)
        ks = jnp.dot(lhs, state.astype(jnp.bfloat16), preferred_element_type=f32)
        v_new = beta[h:h + 1] * (v[h:h + 1] - a * ks[0:1])
        upd = lax.dot_general(lhs[0:8], jnp.concatenate([v_new.astype(jnp.bfloat16), jnp.zeros((7, DV_B), jnp.bfloat16)], axis=0),
                              (((0,), (0,)), ((), ())), preferred_element_type=f32)
        so_ref[0, h] = state * a[:, 0:1] + upd
        outs.append(a * ks[1:2] + qk[h:h + 1] * v_new)
    o = jnp.concatenate(outs, axis=0)
    o = o * lax.rsqrt(jnp.mean(o * o, axis=-1, keepdims=True) + EPS) * onw_ref[...]
    o_ref[0] = o.astype(o_ref.dtype)


def _decode_deltanet(x_new, conv_buf, state, beta_in, a_in, conv_w, a_log, dt_bias, onorm_w):
    b = x_new.shape[0]
    f32 = jnp.float32
    g3 = CONV_CH // LANES
    hh = N_HEADS_B
    gate = jnp.broadcast_to(jnp.concatenate([beta_in, a_in], axis=1).astype(f32)[:, :, None], (b, 2 * hh, LANES))
    prm = jnp.broadcast_to(jnp.concatenate([a_log, dt_bias]).astype(f32)[:, None], (2 * hh, LANES))
    full = lambda shp: pl.BlockSpec(shp, lambda i: (0,) * len(shp))
    per = lambda shp: pl.BlockSpec((1,) + shp, lambda i: (i,) + (0,) * len(shp))
    o, cb_new, s_new = pl.pallas_call(
        _decode_dn_kernel,
        grid=(b,),
        in_specs=[per((g3, LANES)), per((CONV_WIDTH - 1, g3, LANES)), full((CONV_WIDTH, g3, LANES)),
                  per((2 * hh, LANES)), full((2 * hh, LANES)), full((1, DV_B)), per((hh, DK_B, DV_B))],
        out_specs=[per((hh, DV_B)), per((CONV_WIDTH - 1, g3, LANES)), per((hh, DK_B, DV_B))],
        out_shape=[jax.ShapeDtypeStruct((b, hh, DV_B), jnp.bfloat16),
                   jax.ShapeDtypeStruct((b, CONV_WIDTH - 1, g3, LANES), f32),
                   jax.ShapeDtypeStruct((b, hh, DK_B, DV_B), f32)],
        compiler_params=_cparams(("arbitrary",)),
        name="decode_deltanet",
    )(x_new.astype(f32).reshape(b, g3, LANES), conv_buf.astype(f32).reshape(b, CONV_WIDTH - 1, g3, LANES),
      conv_w.astype(f32).reshape(CONV_WIDTH, g3, LANES), gate, prm, onorm_w.astype(f32)[None], state.astype(f32))
    return o, cb_new.reshape(b, CONV_WIDTH - 1, CONV_CH), s_new


def kernel(x_prompt, x_sample, cache_win_k, cache_win_v, state_conv, state_delta,
           g_pre, w_in, conv_w, a_log, dt_bias, onorm_w, w_out, g_post):
    f32, bf16 = jnp.float32, jnp.bfloat16
    b, s, _ = x_prompt.shape
    db, t, _ = x_sample.shape
    depth = w_in.shape[0]
    n_past = cache_win_k.shape[2]
    assert t == 1 and n_past == MAX_WINDOW and s % (RES * SLAB) == 0
    keep = min(MAX_WINDOW, s)

    cos_p, sin_p = _rope_tables(jnp.arange(s, dtype=jnp.int32))
    cos_p, sin_p = jnp.tile(cos_p, (b, 1)), jnp.tile(sin_p, (b, 1))
    cos_s, sin_s = _rope_tables(jnp.full((db,), PAST_LEN, jnp.int32))

    yp = x_prompt.reshape(b * s, D_MODEL)
    ys = x_sample.reshape(db, D_MODEL)
    outs = [[] for _ in range(8)]
    for l in range(depth):
        w_main = w_in[l][:, :MAIN_COLS].astype(bf16)
        w_tail_t = jnp.pad(w_in[l][:, MAIN_COLS:].T.astype(bf16), ((0, LANES - TAIL_COLS), (0, 0)))
        w_o = w_out[l].astype(bf16)
        gp, go = g_pre[l].astype(f32)[None], g_post[l].astype(f32)[None]

        p, tail, tail_t = _project(yp, gp, w_main, w_tail_t, cos_p, sin_p, tm=1024)
        qkv16 = p[:, :3 * WIDTH_A].reshape(b, s // RES, RES, 3 * WIDTH_A).transpose(0, 2, 1, 3)
        o_a = _prompt_attention(qkv16).transpose(0, 2, 1, 3).reshape(b * s, WIDTH_A)
        o_b, s_fin = _deltanet_prompt(p, tail, tail_t, conv_w[l], a_log[l], dt_bias[l], onorm_w[l], b)
        p3 = p.reshape(b, s, MAIN_COLS)
        outs[0].append(p3[:, s - keep:, COL_K:COL_K + WIDTH_A].astype(f32).reshape(b, keep, N_HEADS_A, HEAD_DIM))
        outs[1].append(p3[:, s - keep:, COL_V:COL_V + WIDTH_A].astype(f32).reshape(b, keep, N_HEADS_A, HEAD_DIM))
        tail_rows = p3[:, max(s - (CONV_WIDTH - 1), 0):, COL_QKVB:COL_QKVB + CONV_CH].astype(f32)
        outs[2].append(jnp.pad(tail_rows, ((0, 0), (CONV_WIDTH - 1 - tail_rows.shape[1], 0), (0, 0))))
        outs[3].append(s_fin)
        yp = _output_sublayer(o_a, p, o_b, w_o, go, yp, tm=256)

        ps, tail_s, _ = _project(ys, gp, w_main, w_tail_t, cos_s, sin_s, tm=db)
        heads = lambda c0: ps[:, c0:c0 + WIDTH_A].reshape(db, N_HEADS_A, HEAD_DIM)
        k_new, v_new = heads(COL_K).astype(f32), heads(COL_V).astype(f32)
        o_as = _decode_attention(heads(COL_Q), k_new, v_new, cache_win_k[l], cache_win_v[l])
        o_bs, cb_new, st_new = _decode_deltanet(
            ps[:, COL_QKVB:COL_QKVB + CONV_CH], state_conv[l], state_delta[l],
            tail_s[:, :N_HEADS_B], tail_s[:, N_HEADS_B:TAIL_COLS], conv_w[l], a_log[l], dt_bias[l], onorm_w[l])
        outs[4].append(k_new.reshape(db, t, N_HEADS_A, HEAD_DIM))
        outs[5].append(v_new.reshape(db, t, N_HEADS_A, HEAD_DIM))
        outs[6].append(cb_new)
        outs[7].append(st_new)
        ys = _output_sublayer(o_as.reshape(db, WIDTH_A), ps, o_bs.reshape(db, WIDTH_B), w_o, go, ys, tm=db)

    stk = [jnp.stack(o) for o in outs]
    return (yp.reshape(b, s, D_MODEL), ys.reshape(db, t, D_MODEL),
            stk[0], stk[1], stk[2], stk[3], stk[4], stk[5], stk[6], stk[7])
```

```python
import functools
import math

import jax
import jax.numpy as jnp
from jax import lax
from jax.experimental import pallas as pl
from jax.experimental.pallas import tpu as pltpu

D_MODEL = 2048
HEAD_DIM = 128
N_HEADS_A = 8
N_HEADS_B = 8
DK_B = 128
DV_B = 128
WIDTH_A = N_HEADS_A * HEAD_DIM
WIDTH_B = N_HEADS_B * DV_B
DILATIONS = ((128, 1), (512, 4), (2048, 16))
MAX_WINDOW = 2048
ROPE_THETA = 500000.0
ROPE_DIM = HEAD_DIM // 4
CONV_WIDTH = 4
CONV_CH = 2 * N_HEADS_B * DK_B + N_HEADS_B * DV_B
CHUNK = 64
EPS = 1e-6
PAST_LEN = 16384
MAIN_COLS = 4 * WIDTH_A + CONV_CH + WIDTH_B
TAIL_COLS = 2 * N_HEADS_B
LANES = 128
NEG_BIG = -1e30
VMEM_LIMIT = 56 * 1024 * 1024

COL_Q, COL_K, COL_V, COL_ZA = 0, WIDTH_A, 2 * WIDTH_A, 3 * WIDTH_A
COL_QKVB = 4 * WIDTH_A
COL_ZB = COL_QKVB + CONV_CH


def _cparams(sem):
    return pltpu.CompilerParams(dimension_semantics=sem, vmem_limit_bytes=VMEM_LIMIT)


def _proj_kernel(x_ref, g_ref, w_ref, wt_ref, cos_ref, sin_ref,
                 p_ref, tail_ref, tailt_ref, h_ref, *, n_rope_tiles, tn):
    j = pl.program_id(1)

    @pl.when(j == 0)
    def _():
        x = x_ref[...]
        y = x * lax.rsqrt(jnp.mean(x * x, axis=-1, keepdims=True) + EPS)
        h = (y * g_ref[...]).astype(jnp.bfloat16)
        h_ref[...] = h
        wt = wt_ref[...]
        tail_ref[...] = lax.dot_general(h, wt, (((1,), (1,)), ((), ())),
                                        preferred_element_type=jnp.float32)
        tailt_ref[...] = lax.dot_general(wt[:2 * 8], h, (((1,), (1,)), ((), ())),
                                         preferred_element_type=jnp.float32)

    acc = jnp.dot(h_ref[...], w_ref[...], preferred_element_type=jnp.float32)

    @pl.when(j < n_rope_tiles)
    def _():
        c = cos_ref[...]
        s = sin_ref[...]
        lane = lax.broadcasted_iota(jnp.int32, c.shape, 1)
        scale = jnp.where(j * tn < WIDTH_A, HEAD_DIM ** -0.5, 1.0).astype(jnp.float32)
        for hd in range(tn // LANES):
            a = acc[:, hd * LANES:(hd + 1) * LANES]
            swapped = jnp.where(lane < ROPE_DIM // 2,
                                pltpu.roll(a, LANES - ROPE_DIM // 2, 1),
                                pltpu.roll(a, ROPE_DIM // 2, 1))
            r = (a * c + swapped * s) * scale
            p_ref[:, hd * LANES:(hd + 1) * LANES] = r.astype(p_ref.dtype)

    @pl.when(j >= n_rope_tiles)
    def _():
        p_ref[...] = acc.astype(p_ref.dtype)


def _rope_tables(pos):
    half = ROPE_DIM // 2
    inv = ROPE_THETA ** (-jnp.arange(half, dtype=jnp.float32) / half)
    ang = pos.astype(jnp.float32)[:, None] * inv[None, :]
    cos, sin = jnp.cos(ang), jnp.sin(ang)
    n = pos.shape[0]
    ones = jnp.ones((n, LANES - ROPE_DIM), jnp.float32)
    c = jnp.concatenate([cos, cos, ones], axis=1)
    s = jnp.concatenate([-sin, sin, jnp.zeros_like(ones)], axis=1)
    return c, s


def _project(x2d, g_pre, w_main, w_tail_t, cos_t, sin_t, *, tm, tn=1024):
    m = x2d.shape[0]
    n_i, n_j = m // tm, MAIN_COLS // tn
    n_rope_tiles = (2 * WIDTH_A) // tn
    kern = functools.partial(_proj_kernel, n_rope_tiles=n_rope_tiles, tn=tn)
    return pl.pallas_call(
        kern,
        grid=(n_i, n_j),
        in_specs=[
            pl.BlockSpec((tm, D_MODEL), lambda i, j: (i, 0)),
            pl.BlockSpec((1, D_MODEL), lambda i, j: (0, 0)),
            pl.BlockSpec((D_MODEL, tn), lambda i, j: (0, j)),
            pl.BlockSpec((LANES, D_MODEL), lambda i, j: (0, 0)),
            pl.BlockSpec((tm, LANES), lambda i, j: (i, 0)),
            pl.BlockSpec((tm, LANES), lambda i, j: (i, 0)),
        ],
        out_specs=[
            pl.BlockSpec((tm, tn), lambda i, j: (i, j)),
            pl.BlockSpec((tm, LANES), lambda i, j: (i, 0)),
            pl.BlockSpec((TAIL_COLS, tm), lambda i, j: (0, i)),
        ],
        out_shape=[
            jax.ShapeDtypeStruct((m, MAIN_COLS), jnp.bfloat16),
            jax.ShapeDtypeStruct((m, LANES), jnp.float32),
            jax.ShapeDtypeStruct((TAIL_COLS, m), jnp.float32),
        ],
        scratch_shapes=[pltpu.VMEM((tm, D_MODEL), jnp.bfloat16)],
        compiler_params=_cparams(("arbitrary", "arbitrary")),
        name="proj",
    )(x2d, g_pre, w_main, w_tail_t, cos_t, sin_t)


RES = 16
SLAB = 128


def _band_bias(kind):
    import numpy as np
    if kind == 16:
        nq, nk = 128, 256
        lq = np.arange(nq)[:, None]
        kap = np.arange(nk)[None, :]
        lk = kap - 128
        prev = kap < 128
    elif kind == 4:
        nq, nk = 128, 256
        rho = np.arange(nq)[:, None]
        lq = 4 * (rho % 32) + rho // 32
        kap = np.arange(nk)[None, :]
        lk = 4 * (kap % 64 - 32) + kap // 64
        prev = (kap % 64) < 32
    else:
        nq, nk = 256, 512
        rho = np.arange(nq)[:, None]
        lq = 16 * (rho % 16) + rho // 16
        kap = np.arange(nk)[None, :]
        lk = 16 * (kap % 32 - 16) + kap // 32
        prev = (kap % 32) < 16
    dist = lq - lk
    band = (dist >= 0) & (dist <= 128)
    out = np.stack([band & ~prev, band])
    return jnp.asarray(np.where(out, 0.0, NEG_BIG), dtype=jnp.float32)


def _nt_dot(a, b):
    return lax.dot_general(a, b, (((1,), (1,)), ((), ())), preferred_element_type=jnp.float32)


def _attn_kernel(q_ref, kc_ref, kp_ref, vc_ref, vp_ref, b16_ref, b4_ref, b1_ref,
                 o_ref, acc_ref, m_ref, l_ref):
    hp = jnp.minimum(pl.program_id(2), 1)

    def block(q, k, v, bias, prev):
        s = _nt_dot(q, k) + bias
        m_blk = jnp.max(s, axis=1, keepdims=True)
        if prev is None:
            m_new = m_blk
        else:
            m_prev, l_prev, acc_prev = prev
            m_new = jnp.maximum(m_prev, m_blk)
        p = jnp.exp(s - m_new)
        l_new = jnp.sum(p, axis=1, keepdims=True)
        acc_new = jnp.dot(p.astype(v.dtype), v, preferred_element_type=jnp.float32)
        if prev is not None:
            alpha = jnp.exp(m_prev - m_new)
            l_new = alpha * l_prev + l_new
            acc_new = alpha * acc_prev + acc_new
        return m_new, l_new, acc_new

    for r in range(RES):
        k = jnp.concatenate([kp_ref[0, r], kc_ref[0, r]], axis=0)
        v = jnp.concatenate([vp_ref[0, r], vc_ref[0, r]], axis=0)
        m, l, acc = block(q_ref[0, r], k, v, b16_ref[hp], None)
        m_ref[r], l_ref[r], acc_ref[r] = m, l, acc

    def update(slabs, sl, q, k, v, bias):
        m_prev = jnp.concatenate([m_ref[r, sl, :] for r in slabs], axis=0)
        l_prev = jnp.concatenate([l_ref[r, sl, :] for r in slabs], axis=0)
        acc_prev = jnp.concatenate([acc_ref[r, sl, :] for r in slabs], axis=0)
        m, l, acc = block(q, k, v, bias, (m_prev, l_prev, acc_prev))
        n = sl.stop - sl.start
        for i, r in enumerate(slabs):
            m_ref[r, sl, :] = m[i * n:(i + 1) * n]
            l_ref[r, sl, :] = l[i * n:(i + 1) * n]
            acc_ref[r, sl, :] = acc[i * n:(i + 1) * n]

    def gather(cur_ref, prev_ref, slabs, start, n):
        if start == 0:
            parts = [jnp.concatenate([prev_ref[0, r, SLAB - n:SLAB, :], cur_ref[0, r, 0:n, :]], axis=0)
                     for r in slabs]
        else:
            parts = [cur_ref[0, r, start - n:start + n, :] for r in slabs]
        return jnp.concatenate(parts, axis=0)

    for r4 in range(4):
        slabs = [r4 + 4 * i for i in range(4)]
        for jj in range(SLAB // 32):
            sl = slice(32 * jj, 32 * jj + 32)
            q = jnp.concatenate([q_ref[0, r, sl, :] for r in slabs], axis=0)
            k = gather(kc_ref, kp_ref, slabs, 32 * jj, 32)
            v = gather(vc_ref, vp_ref, slabs, 32 * jj, 32)
            update(slabs, sl, q, k, v, b4_ref[hp] if jj == 0 else b4_ref[1])

    slabs = list(range(RES))
    for jj in range(SLAB // 16):
        sl = slice(16 * jj, 16 * jj + 16)
        q = jnp.concatenate([q_ref[0, r, sl, :] for r in slabs], axis=0)
        k = gather(kc_ref, kp_ref, slabs, 16 * jj, 16)
        v = gather(vc_ref, vp_ref, slabs, 16 * jj, 16)
        update(slabs, sl, q, k, v, b1_ref[hp] if jj == 0 else b1_ref[1])

    for r in range(RES):
        o_ref[0, r] = (acc_ref[r] / l_ref[r]).astype(o_ref.dtype)


def _prompt_attention(qkv16):
    b, _, rows, _ = qkv16.shape
    n_sb = rows // SLAB
    kcol, vcol = WIDTH_A // LANES, 2 * WIDTH_A // LANES
    blk = (1, RES, SLAB, LANES)
    cur = lambda off: pl.BlockSpec(blk, lambda bi, h, n: (bi, 0, n, off + h))
    prv = lambda off: pl.BlockSpec(blk, lambda bi, h, n: (bi, 0, jnp.maximum(n - 1, 0), off + h))
    const = lambda shp: pl.BlockSpec(shp, lambda bi, h, n: (0, 0, 0))
    return pl.pallas_call(
        _attn_kernel,
        grid=(b, N_HEADS_A, n_sb),
        in_specs=[cur(0), cur(kcol), prv(kcol), cur(vcol), prv(vcol),
                  const((2, 128, 256)), const((2, 128, 256)), const((2, 256, 512))],
        out_specs=pl.BlockSpec(blk, lambda bi, h, n: (bi, 0, n, h)),
        out_shape=jax.ShapeDtypeStruct((b, RES, rows, WIDTH_A), jnp.bfloat16),
        scratch_shapes=[pltpu.VMEM((RES, SLAB, LANES), jnp.float32),
                        pltpu.VMEM((RES, SLAB, 1), jnp.float32),
                        pltpu.VMEM((RES, SLAB, 1), jnp.float32)],
        compiler_params=_cparams(("arbitrary", "arbitrary", "arbitrary")),
        name="prompt_attn",
    )(qkv16, qkv16, qkv16, qkv16, qkv16, _band_bias(16), _band_bias(4), _band_bias(1))


DN_STEP = 256
CARRY = 8


def _split_bf16(x, n):
    parts, r = [], x
    for _ in range(n):
        hi = r.astype(jnp.bfloat16)
        parts.append(hi)
        r = r - hi.astype(jnp.float32)
    return parts


def _bdot(a, b):
    return jnp.dot(a.astype(jnp.bfloat16), b.astype(jnp.bfloat16), preferred_element_type=jnp.float32)


def _softplus(x):
    return jnp.maximum(x, 0.0) + jnp.log1p(jnp.exp(-jnp.abs(x)))


def _silu(x):
    return x * jax.nn.sigmoid(x)


def _unit_lower_inverses(a_list):
    n = a_list[0].shape[0]
    row = lax.broadcasted_iota(jnp.int32, (n, n), 0)
    col = lax.broadcasted_iota(jnp.int32, (n, n), 1)
    eye = jnp.where(row == col, 1.0, 0.0)
    xs = [eye - a for a in a_list]
    ps = [_bdot(a, a) for a in a_list]
    k = 2
    while True:
        xs = [x + _bdot(x, p) for x, p in zip(xs, ps)]
        k *= 2
        if k >= n:
            return xs
        ps = [_bdot(p, p) for p in ps]


def _dn_kernel(qb_ref, kb_ref, vb_ref, tail_ref, tailt_ref, cw_ref, prow_ref, pcol_ref, onw_ref,
               o_ref, s_ref, xe_ref):
    c = pl.program_id(1)
    f32, bf16 = jnp.float32, jnp.bfloat16
    tb, n, hh = DN_STEP, CHUNK, N_HEADS_B
    n_ch = tb // n

    @pl.when(c == 0)
    def _():
        s_ref[...] = jnp.zeros_like(s_ref)
        xe_ref[0:CARRY, :] = jnp.zeros((CARRY, CONV_CH), f32)

    @pl.when(c > 0)
    def _():
        xe_ref[0:CARRY, :] = xe_ref[tb:tb + CARRY, :]

    conv = []
    for part, ref in enumerate((qb_ref, kb_ref, vb_ref)):
        cs = slice(part * WIDTH_B, (part + 1) * WIDTH_B)
        xe_ref[CARRY:CARRY + tb, cs] = ref[...].astype(f32)
        y = xe_ref[CARRY:CARRY + tb, cs] * cw_ref[CONV_WIDTH - 1:CONV_WIDTH, cs]
        for i in range(CONV_WIDTH - 1):
            off = CARRY - (CONV_WIDTH - 1) + i
            y = y + xe_ref[off:off + tb, cs] * cw_ref[i:i + 1, cs]
        conv.append(_silu(y))
    qc, kc, vc = conv

    t = tail_ref[...]
    beta_c = jax.nn.sigmoid(t)
    g_c = -jnp.exp(prow_ref[0:1]) * _softplus(t + prow_ref[1:2])
    tt = tailt_ref[...]
    g_r = -jnp.exp(pcol_ref[:, 0:1]) * _softplus(tt + pcol_ref[:, 1:2])

    row = lax.broadcasted_iota(jnp.int32, (n, n), 0)
    col = lax.broadcasted_iota(jnp.int32, (n, n), 1)
    incl = row >= col
    strict = row > col
    ltri = jnp.where(incl, 1.0, 0.0).astype(bf16)
    utri = jnp.where(row <= col, 1.0, 0.0).astype(bf16)
    dot = functools.partial(jnp.dot, preferred_element_type=f32)

    gcs, grs, eg_c, ed_c, glast = [], [], [], [], []
    for ci in range(n_ch):
        rs = slice(ci * n, (ci + 1) * n)
        gc = sum(dot(ltri, part) for part in _split_bf16(g_c[rs], 3))
        gr = sum(dot(part, utri) for part in _split_bf16(g_r[:, rs], 3))
        gl = gc[n - 1:n, :]
        gcs.append(gc)
        grs.append(gr)
        glast.append(gl)
        eg_c.append(jnp.exp(gc))
        ed_c.append(jnp.exp(gl - gc))

    ids = [(ci, h) for ci in range(n_ch) for h in range(hh)]

    def head_slice(x, ci, h):
        return x[ci * n:(ci + 1) * n, h * DK_B:(h + 1) * DK_B]

    def lane_col(x, lane):
        return x[:, lane:lane + 1]

    qs, ks, kbetas, vbetas, egs = [], [], [], [], []
    for ci, h in ids:
        q = head_slice(qc, ci, h)
        k = head_slice(kc, ci, h)
        v = head_slice(vc, ci, h)
        q = q * (lax.rsqrt(jnp.sum(q * q, axis=-1, keepdims=True) + EPS) * (DK_B ** -0.5))
        k = k * lax.rsqrt(jnp.sum(k * k, axis=-1, keepdims=True) + EPS)
        beta = lane_col(beta_c[ci * n:(ci + 1) * n], h)
        qs.append(q)
        ks.append(k)
        kbetas.append(k * beta)
        vbetas.append(v * beta)
        egs.append(lane_col(eg_c[ci], hh + h))

    kqs = [_nt_dot(jnp.concatenate([kb, q], axis=0).astype(bf16), k.astype(bf16))
           for kb, q, k in zip(kbetas, qs, ks)]
    a_mats, qks = [], []
    for (ci, h), kq in zip(ids, kqs):
        gcol = lane_col(gcs[ci], hh + h)
        grow = grs[ci][hh + h:hh + h + 1, :]
        decay = jnp.exp(jnp.where(incl, gcol - grow, NEG_BIG))
        a_mats.append(jnp.where(strict, kq[:n] * decay, 0.0))
        qks.append((kq[n:] * decay).astype(bf16))
    t_mats = _unit_lower_inverses(a_mats)
    uws = [dot(tm.astype(bf16), jnp.concatenate([vb, kb * eg], axis=1).astype(bf16))
           for tm, vb, kb, eg in zip(t_mats, vbetas, kbetas, egs)]

    states = [s_ref[0, h] for h in range(hh)]
    onw = onw_ref[...]
    for ci in range(n_ch):
        base = ci * hh
        wqs = [jnp.concatenate([uws[base + h][:, DV_B:], qs[base + h] * egs[base + h]], axis=0).astype(bf16)
               for h in range(hh)]
        wss = [dot(wq, st.astype(bf16)) for wq, st in zip(wqs, states)]
        v_news = [(uws[base + h][:, :DV_B] - wss[h][:n]).astype(bf16) for h in range(hh)]
        o_in = [dot(qks[base + h], v_news[h]) for h in range(hh)]
        k_decs = [(ks[base + h] * lane_col(ed_c[ci], hh + h)).astype(bf16) for h in range(hh)]
        upds = [lax.dot_general(k_decs[h], v_news[h], (((0,), (0,)), ((), ())), preferred_element_type=f32)
                for h in range(hh)]
        e_last = jnp.exp(glast[ci])
        states = [states[h] * lane_col(e_last, hh + h) + upds[h] for h in range(hh)]
        for h in range(hh):
            o = wss[h][n:] + o_in[h]
            o = o * lax.rsqrt(jnp.mean(o * o, axis=-1, keepdims=True) + EPS) * onw
            o_ref[ci * n:(ci + 1) * n, h * DV_B:(h + 1) * DV_B] = o.astype(o_ref.dtype)
    for h in range(hh):
        s_ref[0, h] = states[h]


def _deltanet_prompt(p, tail, tail_t, conv_w, a_log, dt_bias, onorm_w, batch):
    m = p.shape[0]
    steps = m // batch // DN_STEP
    wb = WIDTH_B // 1
    qcol = COL_QKVB // wb
    zeros8 = jnp.zeros((N_HEADS_B,), jnp.float32)
    gate = jnp.stack([jnp.concatenate([zeros8, a_log.astype(jnp.float32)]),
                      jnp.concatenate([zeros8, dt_bias.astype(jnp.float32)])])
    prow = jnp.pad(gate, ((0, 6), (0, LANES - TAIL_COLS)))
    pcol = jnp.pad(gate.T, ((0, 0), (0, LANES - 2)))
    tok = lambda cb: pl.BlockSpec((DN_STEP, wb), lambda b, c: (b * steps + c, cb))
    full = lambda shp: pl.BlockSpec(shp, lambda b, c: (0,) * len(shp))
    return pl.pallas_call(
        _dn_kernel,
        grid=(batch, steps),
        in_specs=[tok(qcol), tok(qcol + 1), tok(qcol + 2),
                  pl.BlockSpec((DN_STEP, LANES), lambda b, c: (b * steps + c, 0)),
                  pl.BlockSpec((TAIL_COLS, DN_STEP), lambda b, c: (0, b * steps + c)),
                  full((CONV_WIDTH, CONV_CH)), full((8, LANES)), full((TAIL_COLS, LANES)),
                  full((1, DV_B))],
        out_specs=[pl.BlockSpec((DN_STEP, WIDTH_B), lambda b, c: (b * steps + c, 0)),
                   pl.BlockSpec((1, N_HEADS_B, DK_B, DV_B), lambda b, c: (b, 0, 0, 0))],
        out_shape=[jax.ShapeDtypeStruct((m, WIDTH_B), jnp.bfloat16),
                   jax.ShapeDtypeStruct((batch, N_HEADS_B, DK_B, DV_B), jnp.float32)],
        scratch_shapes=[pltpu.VMEM((CARRY + DN_STEP, CONV_CH), jnp.float32)],
        compiler_params=_cparams(("arbitrary", "arbitrary")),
        name="deltanet_prompt",
    )(p, p, p, tail, tail_t, conv_w.astype(jnp.float32), prow, pcol,
      onorm_w.astype(jnp.float32)[None])


def _out_kernel(oa_ref, za_ref, ob_ref, zb_ref, w_ref, g_ref, x_ref, y_ref):
    f32 = jnp.float32
    ga = (oa_ref[...].astype(f32) * _silu(za_ref[...].astype(f32))).astype(jnp.bfloat16)
    gb = (ob_ref[...].astype(f32) * _silu(zb_ref[...].astype(f32))).astype(jnp.bfloat16)
    y = (jnp.dot(ga, w_ref[:WIDTH_A, :], preferred_element_type=f32)
         + jnp.dot(gb, w_ref[WIDTH_A:, :], preferred_element_type=f32))
    y = y * lax.rsqrt(jnp.mean(y * y, axis=-1, keepdims=True) + EPS) * g_ref[...]
    y_ref[...] = x_ref[...] + y


def _output_sublayer(o_a, p, o_b, w_out, g_post, x2d, *, tm):
    m = x2d.shape[0]
    row = lambda width, cb: pl.BlockSpec((tm, width), lambda i: (i, cb))
    return pl.pallas_call(
        _out_kernel,
        grid=(m // tm,),
        in_specs=[row(WIDTH_A, 0), row(WIDTH_A, COL_ZA // WIDTH_A), row(WIDTH_B, 0),
                  row(WIDTH_B, COL_ZB // WIDTH_B),
                  pl.BlockSpec((WIDTH_A + WIDTH_B, D_MODEL), lambda i: (0, 0)),
                  pl.BlockSpec((1, D_MODEL), lambda i: (0, 0)),
                  row(D_MODEL, 0)],
        out_specs=row(D_MODEL, 0),
        out_shape=jax.ShapeDtypeStruct((m, D_MODEL), jnp.float32),
        compiler_params=_cparams(("arbitrary",)),
        name="out_proj",
    )(o_a, p, o_b, p, w_out, g_post, x2d)


def _decode_attn_kernel(q_ref, kn_ref, vn_ref, k1_ref, v1_ref, k4_ref, v4_ref, k16_ref, v16_ref, o_ref):
    f32 = jnp.float32
    q = q_ref[0].astype(f32)
    kn, vn = kn_ref[0].astype(f32), vn_ref[0].astype(f32)
    s_new = jnp.sum(q * kn, axis=-1, keepdims=True)
    scores = [jnp.sum(k_ref[0] * q[None], axis=-1, keepdims=True) for k_ref in (k1_ref, k4_ref, k16_ref)]
    m = s_new
    for s in scores:
        m = jnp.maximum(m, jnp.max(s, axis=0))
    p_new = len(DILATIONS) * jnp.exp(s_new - m)
    den = p_new
    acc = p_new * vn
    for s, v_ref in zip(scores, (v1_ref, v4_ref, v16_ref)):
        p = jnp.exp(s - m[None])
        den = den + jnp.sum(p, axis=0)
        acc = acc + jnp.sum(p * v_ref[0], axis=0)
    o_ref[0] = (acc / den).astype(o_ref.dtype)


def _decode_attention(q, k_new, v_new, cache_k, cache_v):
    b, win, h, dd = cache_k.shape
    nb = 128
    views, specs = [], []
    for window, dil in DILATIONS:
        assert window // dil == nb and win % (nb * dil) == 0
        blk_idx = win // (nb * dil) - 1
        if dil == 1:
            spec = pl.BlockSpec((1, nb, h, dd), lambda i, bi=blk_idx: (i, bi, 0, 0))
            view = lambda c: c
        else:
            spec = pl.BlockSpec((1, nb, None, h, dd), lambda i, bi=blk_idx: (i, bi, 0, 0, 0))
            view = lambda c, dil=dil: c.reshape(b, win // dil, dil, h, dd)
        specs += [spec, spec]
        views += [view(cache_k), view(cache_v)]
    tok = pl.BlockSpec((1, h, dd), lambda i: (i, 0, 0))
    return pl.pallas_call(
        _decode_attn_kernel,
        grid=(b,),
        in_specs=[tok, tok, tok] + specs,
        out_specs=tok,
        out_shape=jax.ShapeDtypeStruct((b, h, dd), jnp.bfloat16),
        compiler_params=_cparams(("arbitrary",)),
        name="decode_attn",
    )(q, k_new, v_new, *views)


def _decode_dn_kernel(x_ref, cb_ref, cw_ref, gate_ref, prm_ref, onw_ref, s_ref,
                      o_ref, cbo_ref, so_ref):
    f32 = jnp.float32
    hh = N_HEADS_B
    xn = x_ref[0]
    cw = cw_ref[...]
    y = xn * cw[CONV_WIDTH - 1]
    for i in range(CONV_WIDTH - 1):
        y = y + cb_ref[0, i] * cw[i]
        cbo_ref[0, i] = cb_ref[0, i + 1] if i + 1 < CONV_WIDTH - 1 else xn
    y = _silu(y)
    q, k, v = y[:hh], y[hh:2 * hh], y[2 * hh:]
    q = q * lax.rsqrt(jnp.sum(q * q, axis=-1, keepdims=True) + EPS) * (DK_B ** -0.5)
    k = k * lax.rsqrt(jnp.sum(k * k, axis=-1, keepdims=True) + EPS)
    gate = gate_ref[0]
    beta = jax.nn.sigmoid(gate[:hh])
    decay = jnp.exp(-jnp.exp(prm_ref[:hh]) * _softplus(gate[hh:] + prm_ref[hh:]))
    qk = jnp.sum(q * k, axis=-1, keepdims=True)
    zeros = jnp.zeros((8 - 2, DK_B), jnp.bfloat16)
    outs = []
    for h in range(hh):
        state = s_ref[0, h]
        a = decay[h:h + 1]
        lhs = jnp.concatenate([k[h:h + 1].astype(jnp.bfloat16), q[h:h + 1].astype(jnp.bfloat16), zeros], axis=0)
        ks = jnp.dot(lhs, state.astype(jnp.bfloat16), preferred_element_type=f32)
        v_new = beta[h:h + 1] * (v[h:h + 1] - a * ks[0:1])
        upd = lax.dot_general(lhs[0:8], jnp.concatenate([v_new.astype(jnp.bfloat16), jnp.zeros((7, DV_B), jnp.bfloat16)], axis=0),
                              (((0,), (0,)), ((), ())), preferred_element_type=f32)
        so_ref[0, h] = state * a[:, 0:1] + upd
        outs.append(a * ks[1:2] + qk[h:h + 1] * v_new)
    o = jnp.concatenate(outs, axis=0)
    o = o * lax.rsqrt(jnp.mean(o * o, axis=-1, keepdims=True) + EPS) * onw_ref[...]
    o_ref[0] = o.astype(o_ref.dtype)


def _decode_deltanet(x_new, conv_buf, state, beta_in, a_in, conv_w, a_log, dt_bias, onorm_w):
    b = x_new.shape[0]
    f32 = jnp.float32
    g3 = CONV_CH // LANES
    hh = N_HEADS_B
    gate = jnp.broadcast_to(jnp.concatenate([beta_in, a_in], axis=1).astype(f32)[:, :, None], (b, 2 * hh, LANES))
    prm = jnp.broadcast_to(jnp.concatenate([a_log, dt_bias]).astype(f32)[:, None], (2 * hh, LANES))
    full = lambda shp: pl.BlockSpec(shp, lambda i: (0,) * len(shp))
    per = lambda shp: pl.BlockSpec((1,) + shp, lambda i: (i,) + (0,) * len(shp))
    o, cb_new, s_new = pl.pallas_call(
        _decode_dn_kernel,
        grid=(b,),
        in_specs=[per((g3, LANES)), per((CONV_WIDTH - 1, g3, LANES)), full((CONV_WIDTH, g3, LANES)),
                  per((2 * hh, LANES)), full((2 * hh, LANES)), full((1, DV_B)), per((hh, DK_B, DV_B))],
        out_specs=[per((hh, DV_B)), per((CONV_WIDTH - 1, g3, LANES)), per((hh, DK_B, DV_B))],
        out_shape=[jax.ShapeDtypeStruct((b, hh, DV_B), jnp.bfloat16),
                   jax.ShapeDtypeStruct((b, CONV_WIDTH - 1, g3, LANES), f32),
                   jax.ShapeDtypeStruct((b, hh, DK_B, DV_B), f32)],
        compiler_params=_cparams(("arbitrary",)),
        name="decode_deltanet",
    )(x_new.astype(f32).reshape(b, g3, LANES), conv_buf.astype(f32).reshape(b, CONV_WIDTH - 1, g3, LANES),
      conv_w.astype(f32).reshape(CONV_WIDTH, g3, LANES), gate, prm, onorm_w.astype(f32)[None], state.astype(f32))
    return o, cb_new.reshape(b, CONV_WIDTH - 1, CONV_CH), s_new


def kernel(x_prompt, x_sample, cache_win_k, cache_win_v, state_conv, state_delta,
           g_pre, w_in, conv_w, a_log, dt_bias, onorm_w, w_out, g_post):
    f32, bf16 = jnp.float32, jnp.bfloat16
    b, s, _ = x_prompt.shape
    db, t, _ = x_sample.shape
    depth = w_in.shape[0]
    n_past = cache_win_k.shape[2]
    assert t == 1 and n_past == MAX_WINDOW and s % (RES * SLAB) == 0
    keep = min(MAX_WINDOW, s)

    cos_p, sin_p = _rope_tables(jnp.arange(s, dtype=jnp.int32))
    cos_p, sin_p = jnp.tile(cos_p, (b, 1)), jnp.tile(sin_p, (b, 1))
    cos_s, sin_s = _rope_tables(jnp.full((db,), PAST_LEN, jnp.int32))

    yp = x_prompt.reshape(b * s, D_MODEL)
    ys = x_sample.reshape(db, D_MODEL)
    outs = [[] for _ in range(8)]
    for l in range(depth):
        w_main = w_in[l][:, :MAIN_COLS].astype(bf16)
        w_tail_t = jnp.pad(w_in[l][:, MAIN_COLS:].T.astype(bf16), ((0, LANES - TAIL_COLS), (0, 0)))
        w_o = w_out[l].astype(bf16)
        gp, go = g_pre[l].astype(f32)[None], g_post[l].astype(f32)[None]

        p, tail, tail_t = _project(yp, gp, w_main, w_tail_t, cos_p, sin_p, tm=1024)
        qkv16 = p[:, :3 * WIDTH_A].reshape(b, s // RES, RES, 3 * WIDTH_A).transpose(0, 2, 1, 3)
        o_a = _prompt_attention(qkv16).transpose(0, 2, 1, 3).reshape(b * s, WIDTH_A)
        o_b, s_fin = _deltanet_prompt(p, tail, tail_t, conv_w[l], a_log[l], dt_bias[l], onorm_w[l], b)
        p3 = p.reshape(b, s, MAIN_COLS)
        outs[0].append(p3[:, s - keep:, COL_K:COL_K + WIDTH_A].astype(f32).reshape(b, keep, N_HEADS_A, HEAD_DIM))
        outs[1].append(p3[:, s - keep:, COL_V:COL_V + WIDTH_A].astype(f32).reshape(b, keep, N_HEADS_A, HEAD_DIM))
        tail_rows = p3[:, max(s - (CONV_WIDTH - 1), 0):, COL_QKVB:COL_QKVB + CONV_CH].astype(f32)
        outs[2].append(jnp.pad(tail_rows, ((0, 0), (CONV_WIDTH - 1 - tail_rows.shape[1], 0), (0, 0))))
        outs[3].append(s_fin)
        yp = _output_sublayer(o_a, p, o_b, w_o, go, yp, tm=256)

        ps, tail_s, _ = _project(ys, gp, w_main, w_tail_t, cos_s, sin_s, tm=db)
        heads = lambda c0: ps[:, c0:c0 + WIDTH_A].reshape(db, N_HEADS_A, HEAD_DIM)
        k_new, v_new = heads(COL_K).astype(f32), heads(COL_V).astype(f32)
        o_as = _decode_attention(heads(COL_Q), k_new, v_new, cache_win_k[l], cache_win_v[l])
        o_bs, cb_new, st_new = _decode_deltanet(
            ps[:, COL_QKVB:COL_QKVB + CONV_CH], state_conv[l], state_delta[l],
            tail_s[:, :N_HEADS_B], tail_s[:, N_HEADS_B:TAIL_COLS], conv_w[l], a_log[l], dt_bias[l], onorm_w[l])
        outs[4].append(k_new.reshape(db, t, N_HEADS_A, HEAD_DIM))
        outs[5].append(v_new.reshape(db, t, N_HEADS_A, HEAD_DIM))
        outs[6].append(cb_new)
        outs[7].append(st_new)
        ys = _output_sublayer(o_as.reshape(db, WIDTH_A), ps, o_bs.reshape(db, WIDTH_B), w_o, go, ys, tm=db)

    stk = [jnp.stack(o) for o in outs]
    return (yp.reshape(b, s, D_MODEL), ys.reshape(db, t, D_MODEL),
            stk[0], stk[1], stk[2], stk[3], stk[4], stk[5], stk[6], stk[7])
```

```python
import functools
import math

import jax
import jax.numpy as jnp
from jax import lax
from jax.experimental import pallas as pl
from jax.experimental.pallas import tpu as pltpu

D_MODEL = 2048
HEAD_DIM = 128
N_HEADS_A = 8
N_HEADS_B = 8
DK_B = 128
DV_B = 128
WIDTH_A = N_HEADS_A * HEAD_DIM
WIDTH_B = N_HEADS_B * DV_B
DILATIONS = ((128, 1), (512, 4), (2048, 16))
MAX_WINDOW = 2048
ROPE_THETA = 500000.0
ROPE_DIM = HEAD_DIM // 4
CONV_WIDTH = 4
CONV_CH = 2 * N_HEADS_B * DK_B + N_HEADS_B * DV_B
CHUNK = 64
EPS = 1e-6
PAST_LEN = 16384
MAIN_COLS = 4 * WIDTH_A + CONV_CH + WIDTH_B
TAIL_COLS = 2 * N_HEADS_B
LANES = 128
NEG_BIG = -1e30
VMEM_LIMIT = 56 * 1024 * 1024

COL_Q, COL_K, COL_V, COL_ZA = 0, WIDTH_A, 2 * WIDTH_A, 3 * WIDTH_A
COL_QKVB = 4 * WIDTH_A
COL_ZB = COL_QKVB + CONV_CH


def _cparams(sem):
    return pltpu.CompilerParams(dimension_semantics=sem, vmem_limit_bytes=VMEM_LIMIT)


def _proj_kernel(x_ref, g_ref, w_ref, wt_ref, cos_ref, sin_ref,
                 p_ref, tail_ref, tailt_ref, h_ref, *, n_rope_tiles, tn):
    j = pl.program_id(1)

    @pl.when(j == 0)
    def _():
        x = x_ref[...]
        y = x * lax.rsqrt(jnp.mean(x * x, axis=-1, keepdims=True) + EPS)
        h = (y * g_ref[...]).astype(jnp.bfloat16)
        h_ref[...] = h
        wt = wt_ref[...]
        tail_ref[...] = lax.dot_general(h, wt, (((1,), (1,)), ((), ())),
                                        preferred_element_type=jnp.float32)
        tailt_ref[...] = lax.dot_general(wt[:2 * 8], h, (((1,), (1,)), ((), ())),
                                         preferred_element_type=jnp.float32)

    acc = jnp.dot(h_ref[...], w_ref[...], preferred_element_type=jnp.float32)

    @pl.when(j < n_rope_tiles)
    def _():
        c = cos_ref[...]
        s = sin_ref[...]
        lane = lax.broadcasted_iota(jnp.int32, c.shape, 1)
        scale = jnp.where(j * tn < WIDTH_A, HEAD_DIM ** -0.5, 1.0).astype(jnp.float32)
        for hd in range(tn // LANES):
            a = acc[:, hd * LANES:(hd + 1) * LANES]
            swapped = jnp.where(lane < ROPE_DIM // 2,
                                pltpu.roll(a, LANES - ROPE_DIM // 2, 1),
                                pltpu.roll(a, ROPE_DIM // 2, 1))
            r = (a * c + swapped * s) * scale
            p_ref[:, hd * LANES:(hd + 1) * LANES] = r.astype(p_ref.dtype)

    @pl.when(j >= n_rope_tiles)
    def _():
        p_ref[...] = acc.astype(p_ref.dtype)


def _rope_tables(pos):
    half = ROPE_DIM // 2
    inv = ROPE_THETA ** (-jnp.arange(half, dtype=jnp.float32) / half)
    ang = pos.astype(jnp.float32)[:, None] * inv[None, :]
    cos, sin = jnp.cos(ang), jnp.sin(ang)
    n = pos.shape[0]
    ones = jnp.ones((n, LANES - ROPE_DIM), jnp.float32)
    c = jnp.concatenate([cos, cos, ones], axis=1)
    s = jnp.concatenate([-sin, sin, jnp.zeros_like(ones)], axis=1)
    return c, s


def _project(x2d, g_pre, w_main, w_tail_t, cos_t, sin_t, *, tm, tn=1024):
    m = x2d.shape[0]
    n_i, n_j = m // tm, MAIN_COLS // tn
    n_rope_tiles = (2 * WIDTH_A) // tn
    kern = functools.partial(_proj_kernel, n_rope_tiles=n_rope_tiles, tn=tn)
    return pl.pallas_call(
        kern,
        grid=(n_i, n_j),
        in_specs=[
            pl.BlockSpec((tm, D_MODEL), lambda i, j: (i, 0)),
            pl.BlockSpec((1, D_MODEL), lambda i, j: (0, 0)),
            pl.BlockSpec((D_MODEL, tn), lambda i, j: (0, j)),
            pl.BlockSpec((LANES, D_MODEL), lambda i, j: (0, 0)),
            pl.BlockSpec((tm, LANES), lambda i, j: (i, 0)),
            pl.BlockSpec((tm, LANES), lambda i, j: (i, 0)),
        ],
        out_specs=[
            pl.BlockSpec((tm, tn), lambda i, j: (i, j)),
            pl.BlockSpec((tm, LANES), lambda i, j: (i, 0)),
            pl.BlockSpec((TAIL_COLS, tm), lambda i, j: (0, i)),
        ],
        out_shape=[
            jax.ShapeDtypeStruct((m, MAIN_COLS), jnp.bfloat16),
            jax.ShapeDtypeStruct((m, LANES), jnp.float32),
            jax.ShapeDtypeStruct((TAIL_COLS, m), jnp.float32),
        ],
        scratch_shapes=[pltpu.VMEM((tm, D_MODEL), jnp.bfloat16)],
        compiler_params=_cparams(("arbitrary", "arbitrary")),
        name="proj",
    )(x2d, g_pre, w_main, w_tail_t, cos_t, sin_t)


RES = 16
SLAB = 128


def _band_bias(kind):
    import numpy as np
    if kind == 16:
        nq, nk = 128, 256
        lq = np.arange(nq)[:, None]
        kap = np.arange(nk)[None, :]
        lk = kap - 128
        prev = kap < 128
    elif kind == 4:
        nq, nk = 128, 256
        rho = np.arange(nq)[:, None]
        lq = 4 * (rho % 32) + rho // 32
        kap = np.arange(nk)[None, :]
        lk = 4 * (kap % 64 - 32) + kap // 64
        prev = (kap % 64) < 32
    else:
        nq, nk = 256, 512
        rho = np.arange(nq)[:, None]
        lq = 16 * (rho % 16) + rho // 16
        kap = np.arange(nk)[None, :]
        lk = 16 * (kap % 32 - 16) + kap // 32
        prev = (kap % 32) < 16
    dist = lq - lk
    band = (dist >= 0) & (dist <= 128)
    out = np.stack([band & ~prev, band])
    return jnp.asarray(np.where(out, 0.0, NEG_BIG), dtype=jnp.float32)


def _nt_dot(a, b):
    return lax.dot_general(a, b, (((1,), (1,)), ((), ())), preferred_element_type=jnp.float32)


PIECE = 16


def _attn_kernel(q_ref, kc_ref, kp_ref, vc_ref, vp_ref, b16_ref, b4_ref, b1_ref, o_ref):
    hp = jnp.minimum(pl.program_id(2), 1)
    f32 = jnp.float32

    def block(q, k, v, bias, prev):
        s = _nt_dot(q, k) + bias
        rows, keys = s.shape
        m_new = jnp.broadcast_to(jnp.max(s, axis=1, keepdims=True), (rows, LANES))
        if prev is not None:
            m_prev, l_prev, acc_prev = prev
            m_new = jnp.maximum(m_prev, m_new)
        p = jnp.exp(s - jnp.concatenate([m_new] * (keys // LANES), axis=1))
        v1 = jnp.concatenate([v, jnp.ones_like(v)], axis=1)
        pv = jnp.dot(p.astype(v.dtype), v1, preferred_element_type=f32)
        acc_new, l_new = pv[:, :LANES], pv[:, LANES:]
        if prev is not None:
            alpha = jnp.exp(m_prev - m_new)
            l_new = alpha * l_prev + l_new
            acc_new = alpha * acc_prev + acc_new
        return m_new, l_new, acc_new

    def split(x, n_parts):
        n = x.shape[0] // n_parts
        return [x[i * n:(i + 1) * n] for i in range(n_parts)]

    state = {}
    pieces = SLAB // PIECE

    for r in range(RES):
        k = jnp.concatenate([kp_ref[0, r], kc_ref[0, r]], axis=0)
        v = jnp.concatenate([vp_ref[0, r], vc_ref[0, r]], axis=0)
        res = [split(x, pieces) for x in block(q_ref[0, r], k, v, b16_ref[hp], None)]
        for a in range(pieces):
            state[(r, a)] = tuple(x[a] for x in res)

    def update(keys_, q, k, v, bias):
        prev = tuple(jnp.concatenate([state[key][i] for key in keys_], axis=0) for i in range(3))
        res = [split(x, len(keys_)) for x in block(q, k, v, bias, prev)]
        for i, key in enumerate(keys_):
            state[key] = tuple(x[i] for x in res)

    def gather(cur_ref, prev_ref, slabs, start, n):
        if start == 0:
            parts = [jnp.concatenate([prev_ref[0, r, SLAB - n:SLAB, :], cur_ref[0, r, 0:n, :]], axis=0)
                     for r in slabs]
        else:
            parts = [cur_ref[0, r, start - n:start + n, :] for r in slabs]
        return jnp.concatenate(parts, axis=0)

    for r4 in range(4):
        slabs = [r4 + 4 * i for i in range(4)]
        for jj in range(SLAB // 32):
            sl = slice(32 * jj, 32 * jj + 32)
            q = jnp.concatenate([q_ref[0, r, sl, :] for r in slabs], axis=0)
            k = gather(kc_ref, kp_ref, slabs, 32 * jj, 32)
            v = gather(vc_ref, vp_ref, slabs, 32 * jj, 32)
            keys_ = [(r, 2 * jj + a) for r in slabs for a in range(2)]
            update(keys_, q, k, v, b4_ref[hp] if jj == 0 else b4_ref[1])

    slabs = list(range(RES))
    for jj in range(pieces):
        sl = slice(PIECE * jj, PIECE * jj + PIECE)
        q = jnp.concatenate([q_ref[0, r, sl, :] for r in slabs], axis=0)
        k = gather(kc_ref, kp_ref, slabs, PIECE * jj, PIECE)
        v = gather(vc_ref, vp_ref, slabs, PIECE * jj, PIECE)
        update([(r, jj) for r in slabs], q, k, v, b1_ref[hp] if jj == 0 else b1_ref[1])

    for r in range(RES):
        acc = jnp.concatenate([state[(r, a)][2] for a in range(pieces)], axis=0)
        l = jnp.concatenate([state[(r, a)][1] for a in range(pieces)], axis=0)
        o_ref[0, r] = (acc / l).astype(o_ref.dtype)


def _prompt_attention(qkv16):
    b, _, rows, _ = qkv16.shape
    n_sb = rows // SLAB
    kcol, vcol = WIDTH_A // LANES, 2 * WIDTH_A // LANES
    blk = (1, RES, SLAB, LANES)
    cur = lambda off: pl.BlockSpec(blk, lambda bi, h, n: (bi, 0, n, off + h))
    prv = lambda off: pl.BlockSpec(blk, lambda bi, h, n: (bi, 0, jnp.maximum(n - 1, 0), off + h))
    const = lambda shp: pl.BlockSpec(shp, lambda bi, h, n: (0, 0, 0))
    return pl.pallas_call(
        _attn_kernel,
        grid=(b, N_HEADS_A, n_sb),
        in_specs=[cur(0), cur(kcol), prv(kcol), cur(vcol), prv(vcol),
                  const((2, 128, 256)), const((2, 128, 256)), const((2, 256, 512))],
        out_specs=pl.BlockSpec(blk, lambda bi, h, n: (bi, 0, n, h)),
        out_shape=jax.ShapeDtypeStruct((b, RES, rows, WIDTH_A), jnp.bfloat16),
        compiler_params=_cparams(("arbitrary", "arbitrary", "arbitrary")),
        name="prompt_attn",
    )(qkv16, qkv16, qkv16, qkv16, qkv16, _band_bias(16), _band_bias(4), _band_bias(1))


DN_STEP = 256
CARRY = 8


def _split_bf16(x, n):
    parts, r = [], x
    for _ in range(n):
        hi = r.astype(jnp.bfloat16)
        parts.append(hi)
        r = r - hi.astype(jnp.float32)
    return parts


def _bdot(a, b):
    return jnp.dot(a.astype(jnp.bfloat16), b.astype(jnp.bfloat16), preferred_element_type=jnp.float32)


def _softplus(x):
    return jnp.maximum(x, 0.0) + jnp.log1p(jnp.exp(-jnp.abs(x)))


def _silu(x):
    return x * jax.nn.sigmoid(x)


def _unit_lower_inverses(a_list):
    n = a_list[0].shape[0]
    row = lax.broadcasted_iota(jnp.int32, (n, n), 0)
    col = lax.broadcasted_iota(jnp.int32, (n, n), 1)
    eye = jnp.where(row == col, 1.0, 0.0)
    xs = [eye - a for a in a_list]
    ps = [_bdot(a, a) for a in a_list]
    k = 2
    while True:
        xs = [x + _bdot(x, p) for x, p in zip(xs, ps)]
        k *= 2
        if k >= n:
            return xs
        ps = [_bdot(p, p) for p in ps]


def _dn_kernel(qb_ref, kb_ref, vb_ref, tail_ref, tailt_ref, cw_ref, prow_ref, pcol_ref, onw_ref,
               o_ref, s_ref, xe_ref):
    c = pl.program_id(1)
    f32, bf16 = jnp.float32, jnp.bfloat16
    tb, n, hh = DN_STEP, CHUNK, N_HEADS_B
    n_ch = tb // n

    @pl.when(c == 0)
    def _():
        s_ref[...] = jnp.zeros_like(s_ref)
        xe_ref[0:CARRY, :] = jnp.zeros((CARRY, CONV_CH), f32)

    @pl.when(c > 0)
    def _():
        xe_ref[0:CARRY, :] = xe_ref[tb:tb + CARRY, :]

    conv = []
    for part, ref in enumerate((qb_ref, kb_ref, vb_ref)):
        cs = slice(part * WIDTH_B, (part + 1) * WIDTH_B)
        xe_ref[CARRY:CARRY + tb, cs] = ref[...].astype(f32)
        y = xe_ref[CARRY:CARRY + tb, cs] * cw_ref[CONV_WIDTH - 1:CONV_WIDTH, cs]
        for i in range(CONV_WIDTH - 1):
            off = CARRY - (CONV_WIDTH - 1) + i
            y = y + xe_ref[off:off + tb, cs] * cw_ref[i:i + 1, cs]
        conv.append(_silu(y))
    qc, kc, vc = conv

    t = tail_ref[...]
    beta_c = jax.nn.sigmoid(t)
    g_c = -jnp.exp(prow_ref[0:1]) * _softplus(t + prow_ref[1:2])
    tt = tailt_ref[...]
    g_r = -jnp.exp(pcol_ref[:, 0:1]) * _softplus(tt + pcol_ref[:, 1:2])

    row = lax.broadcasted_iota(jnp.int32, (n, n), 0)
    col = lax.broadcasted_iota(jnp.int32, (n, n), 1)
    incl = row >= col
    strict = row > col
    ltri = jnp.where(incl, 1.0, 0.0).astype(bf16)
    utri = jnp.where(row <= col, 1.0, 0.0).astype(bf16)
    dot = functools.partial(jnp.dot, preferred_element_type=f32)

    gcs, grs, eg_c, ed_c, glast = [], [], [], [], []
    for ci in range(n_ch):
        rs = slice(ci * n, (ci + 1) * n)
        gc = sum(dot(ltri, part) for part in _split_bf16(g_c[rs], 3))
        gr = sum(dot(part, utri) for part in _split_bf16(g_r[:, rs], 3))
        gl = gc[n - 1:n, :]
        gcs.append(gc)
        grs.append(gr)
        glast.append(gl)
        eg_c.append(jnp.exp(gc))
        ed_c.append(jnp.exp(gl - gc))

    ids = [(ci, h) for ci in range(n_ch) for h in range(hh)]

    def head_slice(x, ci, h):
        return x[ci * n:(ci + 1) * n, h * DK_B:(h + 1) * DK_B]

    def lane_col(x, lane):
        return x[:, lane:lane + 1]

    qs, ks, kbetas, vbetas, egs = [], [], [], [], []
    for ci, h in ids:
        q = head_slice(qc, ci, h)
        k = head_slice(kc, ci, h)
        v = head_slice(vc, ci, h)
        q = q * (lax.rsqrt(jnp.sum(q * q, axis=-1, keepdims=True) + EPS) * (DK_B ** -0.5))
        k = k * lax.rsqrt(jnp.sum(k * k, axis=-1, keepdims=True) + EPS)
        beta = lane_col(beta_c[ci * n:(ci + 1) * n], h)
        qs.append(q)
        ks.append(k)
        kbetas.append(k * beta)
        vbetas.append(v * beta)
        egs.append(lane_col(eg_c[ci], hh + h))

    kqs = [_nt_dot(jnp.concatenate([kb, q], axis=0).astype(bf16), k.astype(bf16))
           for kb, q, k in zip(kbetas, qs, ks)]
    a_mats, qks = [], []
    for (ci, h), kq in zip(ids, kqs):
        gcol = lane_col(gcs[ci], hh + h)
        grow = grs[ci][hh + h:hh + h + 1, :]
        decay = jnp.exp(jnp.where(incl, gcol - grow, NEG_BIG))
        a_mats.append(jnp.where(strict, kq[:n] * decay, 0.0))
        qks.append((kq[n:] * decay).astype(bf16))
    t_mats = _unit_lower_inverses(a_mats)
    uws = [dot(tm.astype(bf16), jnp.concatenate([vb, kb * eg], axis=1).astype(bf16))
           for tm, vb, kb, eg in zip(t_mats, vbetas, kbetas, egs)]

    states = [s_ref[0, h] for h in range(hh)]
    onw = onw_ref[...]
    for ci in range(n_ch):
        base = ci * hh
        wqs = [jnp.concatenate([uws[base + h][:, DV_B:], qs[base + h] * egs[base + h]], axis=0).astype(bf16)
               for h in range(hh)]
        wss = [dot(wq, st.astype(bf16)) for wq, st in zip(wqs, states)]
        v_news = [(uws[base + h][:, :DV_B] - wss[h][:n]).astype(bf16) for h in range(hh)]
        o_in = [dot(qks[base + h], v_news[h]) for h in range(hh)]
        k_decs = [(ks[base + h] * lane_col(ed_c[ci], hh + h)).astype(bf16) for h in range(hh)]
        upds = [lax.dot_general(k_decs[h], v_news[h], (((0,), (0,)), ((), ())), preferred_element_type=f32)
                for h in range(hh)]
        e_last = jnp.exp(glast[ci])
        states = [states[h] * lane_col(e_last, hh + h) + upds[h] for h in range(hh)]
        for h in range(hh):
            o = wss[h][n:] + o_in[h]
            o = o * lax.rsqrt(jnp.mean(o * o, axis=-1, keepdims=True) + EPS) * onw
            o_ref[ci * n:(ci + 1) * n, h * DV_B:(h + 1) * DV_B] = o.astype(o_ref.dtype)
    for h in range(hh):
        s_ref[0, h] = states[h]


def _deltanet_prompt(p, tail, tail_t, conv_w, a_log, dt_bias, onorm_w, batch):
    m = p.shape[0]
    steps = m // batch // DN_STEP
    wb = WIDTH_B // 1
    qcol = COL_QKVB // wb
    zeros8 = jnp.zeros((N_HEADS_B,), jnp.float32)
    gate = jnp.stack([jnp.concatenate([zeros8, a_log.astype(jnp.float32)]),
                      jnp.concatenate([zeros8, dt_bias.astype(jnp.float32)])])
    prow = jnp.pad(gate, ((0, 6), (0, LANES - TAIL_COLS)))
    pcol = jnp.pad(gate.T, ((0, 0), (0, LANES - 2)))
    tok = lambda cb: pl.BlockSpec((DN_STEP, wb), lambda b, c: (b * steps + c, cb))
    full = lambda shp: pl.BlockSpec(shp, lambda b, c: (0,) * len(shp))
    return pl.pallas_call(
        _dn_kernel,
        grid=(batch, steps),
        in_specs=[tok(qcol), tok(qcol + 1), tok(qcol + 2),
                  pl.BlockSpec((DN_STEP, LANES), lambda b, c: (b * steps + c, 0)),
                  pl.BlockSpec((TAIL_COLS, DN_STEP), lambda b, c: (0, b * steps + c)),
                  full((CONV_WIDTH, CONV_CH)), full((8, LANES)), full((TAIL_COLS, LANES)),
                  full((1, DV_B))],
        out_specs=[pl.BlockSpec((DN_STEP, WIDTH_B), lambda b, c: (b * steps + c, 0)),
                   pl.BlockSpec((1, N_HEADS_B, DK_B, DV_B), lambda b, c: (b, 0, 0, 0))],
        out_shape=[jax.ShapeDtypeStruct((m, WIDTH_B), jnp.bfloat16),
                   jax.ShapeDtypeStruct((batch, N_HEADS_B, DK_B, DV_B), jnp.float32)],
        scratch_shapes=[pltpu.VMEM((CARRY + DN_STEP, CONV_CH), jnp.float32)],
        compiler_params=_cparams(("arbitrary", "arbitrary")),
        name="deltanet_prompt",
    )(p, p, p, tail, tail_t, conv_w.astype(jnp.float32), prow, pcol,
      onorm_w.astype(jnp.float32)[None])


def _out_kernel(oa_ref, za_ref, ob_ref, zb_ref, w_ref, g_ref, x_ref, y_ref):
    f32 = jnp.float32
    ga = (oa_ref[...].astype(f32) * _silu(za_ref[...].astype(f32))).astype(jnp.bfloat16)
    gb = (ob_ref[...].astype(f32) * _silu(zb_ref[...].astype(f32))).astype(jnp.bfloat16)
    y = (jnp.dot(ga, w_ref[:WIDTH_A, :], preferred_element_type=f32)
         + jnp.dot(gb, w_ref[WIDTH_A:, :], preferred_element_type=f32))
    y = y * lax.rsqrt(jnp.mean(y * y, axis=-1, keepdims=True) + EPS) * g_ref[...]
    y_ref[...] = x_ref[...] + y


def _output_sublayer(o_a, p, o_b, w_out, g_post, x2d, *, tm):
    m = x2d.shape[0]
    row = lambda width, cb: pl.BlockSpec((tm, width), lambda i: (i, cb))
    return pl.pallas_call(
        _out_kernel,
        grid=(m // tm,),
        in_specs=[row(WIDTH_A, 0), row(WIDTH_A, COL_ZA // WIDTH_A), row(WIDTH_B, 0),
                  row(WIDTH_B, COL_ZB // WIDTH_B),
                  pl.BlockSpec((WIDTH_A + WIDTH_B, D_MODEL), lambda i: (0, 0)),
                  pl.BlockSpec((1, D_MODEL), lambda i: (0, 0)),
                  row(D_MODEL, 0)],
        out_specs=row(D_MODEL, 0),
        out_shape=jax.ShapeDtypeStruct((m, D_MODEL), jnp.float32),
        compiler_params=_cparams(("arbitrary",)),
        name="out_proj",
    )(o_a, p, o_b, p, w_out, g_post, x2d)


def _decode_attn_kernel(q_ref, kn_ref, vn_ref, k1_ref, v1_ref, k4_ref, v4_ref, k16_ref, v16_ref, o_ref):
    f32 = jnp.float32
    q = q_ref[0].astype(f32)
    kn, vn = kn_ref[0].astype(f32), vn_ref[0].astype(f32)
    s_new = jnp.sum(q * kn, axis=-1, keepdims=True)
    scores = [jnp.sum(k_ref[0] * q[None], axis=-1, keepdims=True) for k_ref in (k1_ref, k4_ref, k16_ref)]
    m = s_new
    for s in scores:
        m = jnp.maximum(m, jnp.max(s, axis=0))
    p_new = len(DILATIONS) * jnp.exp(s_new - m)
    den = p_new
    acc = p_new * vn
    for s, v_ref in zip(scores, (v1_ref, v4_ref, v16_ref)):
        p = jnp.exp(s - m[None])
        den = den + jnp.sum(p, axis=0)
        acc = acc + jnp.sum(p * v_ref[0], axis=0)
    o_ref[0] = (acc / den).astype(o_ref.dtype)


def _decode_attention(q, k_new, v_new, cache_k, cache_v):
    b, win, h, dd = cache_k.shape
    nb = 128
    views, specs = [], []
    for window, dil in DILATIONS:
        assert window // dil == nb and win % (nb * dil) == 0
        blk_idx = win // (nb * dil) - 1
        if dil == 1:
            spec = pl.BlockSpec((1, nb, h, dd), lambda i, bi=blk_idx: (i, bi, 0, 0))
            view = lambda c: c
        else:
            spec = pl.BlockSpec((1, nb, None, h, dd), lambda i, bi=blk_idx: (i, bi, 0, 0, 0))
            view = lambda c, dil=dil: c.reshape(b, win // dil, dil, h, dd)
        specs += [spec, spec]
        views += [view(cache_k), view(cache_v)]
    tok = pl.BlockSpec((1, h, dd), lambda i: (i, 0, 0))
    return pl.pallas_call(
        _decode_attn_kernel,
        grid=(b,),
        in_specs=[tok, tok, tok] + specs,
        out_specs=tok,
        out_shape=jax.ShapeDtypeStruct((b, h, dd), jnp.bfloat16),
        compiler_params=_cparams(("arbitrary",)),
        name="decode_attn",
    )(q, k_new, v_new, *views)


def _decode_dn_kernel(x_ref, cb_ref, cw_ref, gate_ref, prm_ref, onw_ref, s_ref,
                      o_ref, cbo_ref, so_ref):
    f32 = jnp.float32
    hh = N_HEADS_B
    xn = x_ref[0]
    cw = cw_ref[...]
    y = xn * cw[CONV_WIDTH - 1]
    for i in range(CONV_WIDTH - 1):
        y = y + cb_ref[0, i] * cw[i]
        cbo_ref[0, i] = cb_ref[0, i + 1] if i + 1 < CONV_WIDTH - 1 else xn
    y = _silu(y)
    q, k, v = y[:hh], y[hh:2 * hh], y[2 * hh:]
    q = q * lax.rsqrt(jnp.sum(q * q, axis=-1, keepdims=True) + EPS) * (DK_B ** -0.5)
    k = k * lax.rsqrt(jnp.sum(k * k, axis=-1, keepdims=True) + EPS)
    gate = gate_ref[0]
    beta = jax.nn.sigmoid(gate[:hh])
    decay = jnp.exp(-jnp.exp(prm_ref[:hh]) * _softplus(gate[hh:] + prm_ref[hh:]))
    qk = jnp.sum(q * k, axis=-1, keepdims=True)
    zeros = jnp.zeros((8 - 2, DK_B), jnp.bfloat16)
    outs = []
    for h in range(hh):
        state = s_ref[0, h]
        a = decay[h:h + 1]
        lhs = jnp.concatenate([k[h:h + 1].astype(jnp.bfloat16), q[h:h + 1].astype(jnp.bfloat16), zeros], axis=0)
        ks = jnp.dot(lhs, state.astype(jnp.bfloat16), preferred_element_type=f32)
        v_new = beta[h:h + 1] * (v[h:h + 1] - a * ks[0:1])
        upd = lax.dot_general(lhs[0:8], jnp.concatenate([v_new.astype(jnp.bfloat16), jnp.zeros((7, DV_B), jnp.bfloat16)], axis=0),
                              (((0,), (0,)), ((), ())), preferred_element_type=f32)
        so_ref[0, h] = state * a[:, 0:1] + upd
        outs.append(a * ks[1:2] + qk[h:h + 1] * v_new)
    o = jnp.concatenate(outs, axis=0)
    o = o * lax.rsqrt(jnp.mean(o * o, axis=-1, keepdims=True) + EPS) * onw_ref[...]
    o_ref[0] = o.astype(o_ref.dtype)


def _decode_deltanet(x_new, conv_buf, state, beta_in, a_in, conv_w, a_log, dt_bias, onorm_w):
    b = x_new.shape[0]
    f32 = jnp.float32
    g3 = CONV_CH // LANES
    hh = N_HEADS_B
    gate = jnp.broadcast_to(jnp.concatenate([beta_in, a_in], axis=1).astype(f32)[:, :, None], (b, 2 * hh, LANES))
    prm = jnp.broadcast_to(jnp.concatenate([a_log, dt_bias]).astype(f32)[:, None], (2 * hh, LANES))
    full = lambda shp: pl.BlockSpec(shp, lambda i: (0,) * len(shp))
    per = lambda shp: pl.BlockSpec((1,) + shp, lambda i: (i,) + (0,) * len(shp))
    o, cb_new, s_new = pl.pallas_call(
        _decode_dn_kernel,
        grid=(b,),
        in_specs=[per((g3, LANES)), per((CONV_WIDTH - 1, g3, LANES)), full((CONV_WIDTH, g3, LANES)),
                  per((2 * hh, LANES)), full((2 * hh, LANES)), full((1, DV_B)), per((hh, DK_B, DV_B))],
        out_specs=[per((hh, DV_B)), per((CONV_WIDTH - 1, g3, LANES)), per((hh, DK_B, DV_B))],
        out_shape=[jax.ShapeDtypeStruct((b, hh, DV_B), jnp.bfloat16),
                   jax.ShapeDtypeStruct((b, CONV_WIDTH - 1, g3, LANES), f32),
                   jax.ShapeDtypeStruct((b, hh, DK_B, DV_B), f32)],
        compiler_params=_cparams(("arbitrary",)),
        name="decode_deltanet",
    )(x_new.astype(f32).reshape(b, g3, LANES), conv_buf.astype(f32).reshape(b, CONV_WIDTH - 1, g3, LANES),
      conv_w.astype(f32).reshape(CONV_WIDTH, g3, LANES), gate, prm, onorm_w.astype(f32)[None], state.astype(f32))
    return o, cb_new.reshape(b, CONV_WIDTH - 1, CONV_CH), s_new


def kernel(x_prompt, x_sample, cache_win_k, cache_win_v, state_conv, state_delta,
           g_pre, w_in, conv_w, a_log, dt_bias, onorm_w, w_out, g_post):
    f32, bf16 = jnp.float32, jnp.bfloat16
    b, s, _ = x_prompt.shape
    db, t, _ = x_sample.shape
    depth = w_in.shape[0]
    n_past = cache_win_k.shape[2]
    assert t == 1 and n_past == MAX_WINDOW and s % (RES * SLAB) == 0
    keep = min(MAX_WINDOW, s)

    cos_p, sin_p = _rope_tables(jnp.arange(s, dtype=jnp.int32))
    cos_p, sin_p = jnp.tile(cos_p, (b, 1)), jnp.tile(sin_p, (b, 1))
    cos_s, sin_s = _rope_tables(jnp.full((db,), PAST_LEN, jnp.int32))

    yp = x_prompt.reshape(b * s, D_MODEL)
    ys = x_sample.reshape(db, D_MODEL)
    outs = [[] for _ in range(8)]
    for l in range(depth):
        w_main = w_in[l][:, :MAIN_COLS].astype(bf16)
        w_tail_t = jnp.pad(w_in[l][:, MAIN_COLS:].T.astype(bf16), ((0, LANES - TAIL_COLS), (0, 0)))
        w_o = w_out[l].astype(bf16)
        gp, go = g_pre[l].astype(f32)[None], g_post[l].astype(f32)[None]

        p, tail, tail_t = _project(yp, gp, w_main, w_tail_t, cos_p, sin_p, tm=1024)
        qkv16 = p[:, :3 * WIDTH_A].reshape(b, s // RES, RES, 3 * WIDTH_A).transpose(0, 2, 1, 3)
        o_a = _prompt_attention(qkv16).transpose(0, 2, 1, 3).reshape(b * s, WIDTH_A)
        o_b, s_fin = _deltanet_prompt(p, tail, tail_t, conv_w[l], a_log[l], dt_bias[l], onorm_w[l], b)
        p3 = p.reshape(b, s, MAIN_COLS)
        outs[0].append(p3[:, s - keep:, COL_K:COL_K + WIDTH_A].astype(f32).reshape(b, keep, N_HEADS_A, HEAD_DIM))
        outs[1].append(p3[:, s - keep:, COL_V:COL_V + WIDTH_A].astype(f32).reshape(b, keep, N_HEADS_A, HEAD_DIM))
        tail_rows = p3[:, max(s - (CONV_WIDTH - 1), 0):, COL_QKVB:COL_QKVB + CONV_CH].astype(f32)
        outs[2].append(jnp.pad(tail_rows, ((0, 0), (CONV_WIDTH - 1 - tail_rows.shape[1], 0), (0, 0))))
        outs[3].append(s_fin)
        yp = _output_sublayer(o_a, p, o_b, w_o, go, yp, tm=256)

        ps, tail_s, _ = _project(ys, gp, w_main, w_tail_t, cos_s, sin_s, tm=db)
        heads = lambda c0: ps[:, c0:c0 + WIDTH_A].reshape(db, N_HEADS_A, HEAD_DIM)
        k_new, v_new = heads(COL_K).astype(f32), heads(COL_V).astype(f32)
        o_as = _decode_attention(heads(COL_Q), k_new, v_new, cache_win_k[l], cache_win_v[l])
        o_bs, cb_new, st_new = _decode_deltanet(
            ps[:, COL_QKVB:COL_QKVB + CONV_CH], state_conv[l], state_delta[l],
            tail_s[:, :N_HEADS_B], tail_s[:, N_HEADS_B:TAIL_COLS], conv_w[l], a_log[l], dt_bias[l], onorm_w[l])
        outs[4].append(k_new.reshape(db, t, N_HEADS_A, HEAD_DIM))
        outs[5].append(v_new.reshape(db, t, N_HEADS_A, HEAD_DIM))
        outs[6].append(cb_new)
        outs[7].append(st_new)
        ys = _output_sublayer(o_as.reshape(db, WIDTH_A), ps, o_bs.reshape(db, WIDTH_B), w_o, go, ys, tm=db)

    stk = [jnp.stack(o) for o in outs]
    return (yp.reshape(b, s, D_MODEL), ys.reshape(db, t, D_MODEL),
            stk[0], stk[1], stk[2], stk[3], stk[4], stk[5], stk[6], stk[7])
```

```python
import functools
import math

import jax
import jax.numpy as jnp
from jax import lax
from jax.experimental import pallas as pl
from jax.experimental.pallas import tpu as pltpu

D_MODEL = 2048
HEAD_DIM = 128
N_HEADS_A = 8
N_HEADS_B = 8
DK_B = 128
DV_B = 128
WIDTH_A = N_HEADS_A * HEAD_DIM
WIDTH_B = N_HEADS_B * DV_B
DILATIONS = ((128, 1), (512, 4), (2048, 16))
MAX_WINDOW = 2048
ROPE_THETA = 500000.0
ROPE_DIM = HEAD_DIM // 4
CONV_WIDTH = 4
CONV_CH = 2 * N_HEADS_B * DK_B + N_HEADS_B * DV_B
CHUNK = 64
EPS = 1e-6
PAST_LEN = 16384
MAIN_COLS = 4 * WIDTH_A + CONV_CH + WIDTH_B
TAIL_COLS = 2 * N_HEADS_B
LANES = 128
NEG_BIG = -1e30
VMEM_LIMIT = 56 * 1024 * 1024

REST_COLS = MAIN_COLS - 3 * WIDTH_A
REST_ZA, REST_QKVB, REST_ZB = 0, WIDTH_A, WIDTH_A + CONV_CH


def _cparams(sem):
    return pltpu.CompilerParams(dimension_semantics=sem, vmem_limit_bytes=VMEM_LIMIT)


RES = 16
QKV_TILES = 3


def _proj_kernel(x_ref, g_ref, w_ref, wt_ref, cos_ref, sin_ref,
                 qkv_ref, kf_ref, vf_ref, p_ref, tail_ref, tailt_ref, h_ref, de_ref,
                 *, tiles_per_seq, first_keep_tile, residue_major):
    i = pl.program_id(0)
    j = pl.program_id(1)
    tm = x_ref.shape[0]

    @pl.when(j == 0)
    def _():
        x = x_ref[...]
        y = x * lax.rsqrt(jnp.mean(x * x, axis=-1, keepdims=True) + EPS)
        h = (y * g_ref[...]).astype(jnp.bfloat16)
        h_ref[...] = h
        wt = wt_ref[...]
        tail_ref[...] = lax.dot_general(h, wt, (((1,), (1,)), ((), ())),
                                        preferred_element_type=jnp.float32)
        tailt_ref[...] = lax.dot_general(wt[:TAIL_COLS], h, (((1,), (1,)), ((), ())),
                                         preferred_element_type=jnp.float32)

    acc = jnp.dot(h_ref[...], w_ref[...], preferred_element_type=jnp.float32)
    keep_rows = (i % tiles_per_seq) >= first_keep_tile

    def emit_qkv(slabs, win_ref, win_cond):
        for hd, r in enumerate(slabs):
            cs = slice(hd * LANES, (hd + 1) * LANES)
            if residue_major:
                de_ref[hd] = r
                for res in range(RES):
                    qkv_ref[0, res, :, cs] = de_ref[hd, pl.ds(res, tm // RES, stride=RES), :].astype(qkv_ref.dtype)
            else:
                qkv_ref[:, cs] = r.astype(qkv_ref.dtype)

        @pl.when(win_cond)
        def _():
            for hd, r in enumerate(slabs):
                win_ref[pl.ds(hd, tm, stride=N_HEADS_A), :] = r

    @pl.when(j < QKV_TILES - 1)
    def _():
        c = cos_ref[...]
        s = sin_ref[...]
        lane = lax.broadcasted_iota(jnp.int32, c.shape, 1)
        scale = jnp.where(j == 0, HEAD_DIM ** -0.5, 1.0).astype(jnp.float32)
        slabs = []
        for hd in range(WIDTH_A // LANES):
            a = acc[:, hd * LANES:(hd + 1) * LANES]
            swapped = jnp.where(lane < ROPE_DIM // 2,
                                pltpu.roll(a, LANES - ROPE_DIM // 2, 1),
                                pltpu.roll(a, ROPE_DIM // 2, 1))
            slabs.append((a * c + swapped * s) * scale)
        emit_qkv(slabs, kf_ref, (j == 1) & keep_rows)

    @pl.when(j == QKV_TILES - 1)
    def _():
        emit_qkv([acc[:, hd * LANES:(hd + 1) * LANES] for hd in range(WIDTH_A // LANES)], vf_ref, keep_rows)

    @pl.when(j >= QKV_TILES)
    def _():
        p_ref[...] = acc.astype(p_ref.dtype)


def _rope_tables(pos):
    half = ROPE_DIM // 2
    inv = ROPE_THETA ** (-jnp.arange(half, dtype=jnp.float32) / half)
    ang = pos.astype(jnp.float32)[:, None] * inv[None, :]
    cos, sin = jnp.cos(ang), jnp.sin(ang)
    n = pos.shape[0]
    ones = jnp.ones((n, LANES - ROPE_DIM), jnp.float32)
    c = jnp.concatenate([cos, cos, ones], axis=1)
    s = jnp.concatenate([-sin, sin, jnp.zeros_like(ones)], axis=1)
    return c, s


def _project(x2d, g_pre, w_main, w_tail_t, cos_t, sin_t, *, seq, keep, tm, residue_major):
    m = x2d.shape[0]
    tn = WIDTH_A
    n_i, n_j = m // tm, MAIN_COLS // tn
    tiles_per_seq = seq // tm
    first_keep = (seq - keep) // tm
    keep_tiles = keep // tm
    assert seq % tm == 0 and keep % tm == 0 and (seq - keep) % tm == 0
    kern = functools.partial(_proj_kernel, tiles_per_seq=tiles_per_seq, first_keep_tile=first_keep,
                             residue_major=residue_major)
    qkv_col = lambda j: jnp.minimum(j, QKV_TILES - 1)
    if residue_major:
        assert tm % (RES * 16) == 0
        qkv_spec = pl.BlockSpec((1, RES, tm // RES, tn),
                                lambda i, j: (i // tiles_per_seq, 0, i % tiles_per_seq, qkv_col(j)))
        qkv_shape = jax.ShapeDtypeStruct((m // seq, RES, seq // RES, QKV_TILES * tn), jnp.bfloat16)
    else:
        qkv_spec = pl.BlockSpec((tm, tn), lambda i, j: (i, qkv_col(j)))
        qkv_shape = jax.ShapeDtypeStruct((m, QKV_TILES * tn), jnp.bfloat16)

    def win_index(i, j):
        il = i % tiles_per_seq
        return (i // tiles_per_seq) * keep_tiles + jnp.maximum(il - first_keep, 0), 0

    win_spec = pl.BlockSpec((tm * N_HEADS_A, LANES), win_index)
    win_shape = jax.ShapeDtypeStruct((m // seq * keep * N_HEADS_A, LANES), jnp.float32)

    return pl.pallas_call(
        kern,
        grid=(n_i, n_j),
        in_specs=[
            pl.BlockSpec((tm, D_MODEL), lambda i, j: (i, 0)),
            pl.BlockSpec((1, D_MODEL), lambda i, j: (0, 0)),
            pl.BlockSpec((D_MODEL, tn), lambda i, j: (0, j)),
            pl.BlockSpec((LANES, D_MODEL), lambda i, j: (0, 0)),
            pl.BlockSpec((tm, LANES), lambda i, j: (i, 0)),
            pl.BlockSpec((tm, LANES), lambda i, j: (i, 0)),
        ],
        out_specs=[
            qkv_spec,
            win_spec,
            win_spec,
            pl.BlockSpec((tm, tn), lambda i, j: (i, jnp.maximum(j - QKV_TILES, 0))),
            pl.BlockSpec((tm, LANES), lambda i, j: (i, 0)),
            pl.BlockSpec((TAIL_COLS, tm), lambda i, j: (0, i)),
        ],
        out_shape=[
            qkv_shape,
            win_shape,
            win_shape,
            jax.ShapeDtypeStruct((m, REST_COLS), jnp.bfloat16),
            jax.ShapeDtypeStruct((m, LANES), jnp.float32),
            jax.ShapeDtypeStruct((TAIL_COLS, m), jnp.float32),
        ],
        scratch_shapes=[pltpu.VMEM((tm, D_MODEL), jnp.bfloat16),
                        pltpu.VMEM((WIDTH_A // LANES, tm, LANES), jnp.float32)],
        compiler_params=_cparams(("arbitrary", "arbitrary")),
        name="proj",
    )(x2d, g_pre, w_main, w_tail_t, cos_t, sin_t)


SLAB = 128


def _band_bias(kind):
    import numpy as np
    if kind == 16:
        nq, nk = 128, 256
        lq = np.arange(nq)[:, None]
        kap = np.arange(nk)[None, :]
        lk = kap - 128
        prev = kap < 128
    elif kind == 4:
        nq, nk = 128, 256
        rho = np.arange(nq)[:, None]
        lq = 4 * (rho % 32) + rho // 32
        kap = np.arange(nk)[None, :]
        lk = 4 * (kap % 64 - 32) + kap // 64
        prev = (kap % 64) < 32
    else:
        nq, nk = 256, 512
        rho = np.arange(nq)[:, None]
        lq = 16 * (rho % 16) + rho // 16
        kap = np.arange(nk)[None, :]
        lk = 16 * (kap % 32 - 16) + kap // 32
        prev = (kap % 32) < 16
    dist = lq - lk
    band = (dist >= 0) & (dist <= 128)
    out = np.stack([band & ~prev, band])
    return jnp.asarray(np.where(out, 0.0, NEG_BIG), dtype=jnp.float32)


def _nt_dot(a, b):
    return lax.dot_general(a, b, (((1,), (1,)), ((), ())), preferred_element_type=jnp.float32)


PIECE = 16


def _attn_kernel(q_ref, kc_ref, kp_ref, vc_ref, vp_ref, b16_ref, b4_ref, b1_ref, o_ref, nat_ref):
    hp = jnp.minimum(pl.program_id(2), 1)
    f32 = jnp.float32

    def block(q, k, v, bias, prev):
        s = _nt_dot(q, k) + bias
        rows, keys = s.shape
        m_new = jnp.broadcast_to(jnp.max(s, axis=1, keepdims=True), (rows, LANES))
        if prev is not None:
            m_prev, l_prev, acc_prev = prev
            m_new = jnp.maximum(m_prev, m_new)
        p = jnp.exp(s - jnp.concatenate([m_new] * (keys // LANES), axis=1))
        v1 = jnp.concatenate([v, jnp.ones_like(v)], axis=1)
        pv = jnp.dot(p.astype(v.dtype), v1, preferred_element_type=f32)
        acc_new, l_new = pv[:, :LANES], pv[:, LANES:]
        if prev is not None:
            alpha = jnp.exp(m_prev - m_new)
            l_new = alpha * l_prev + l_new
            acc_new = alpha * acc_prev + acc_new
        return m_new, l_new, acc_new

    def split(x, n_parts):
        n = x.shape[0] // n_parts
        return [x[i * n:(i + 1) * n] for i in range(n_parts)]

    state = {}
    pieces = SLAB // PIECE

    for r in range(RES):
        k = jnp.concatenate([kp_ref[0, r], kc_ref[0, r]], axis=0)
        v = jnp.concatenate([vp_ref[0, r], vc_ref[0, r]], axis=0)
        res = [split(x, pieces) for x in block(q_ref[0, r], k, v, b16_ref[hp], None)]
        for a in range(pieces):
            state[(r, a)] = tuple(x[a] for x in res)

    def update(keys_, q, k, v, bias):
        prev = tuple(jnp.concatenate([state[key][i] for key in keys_], axis=0) for i in range(3))
        res = [split(x, len(keys_)) for x in block(q, k, v, bias, prev)]
        for i, key in enumerate(keys_):
            state[key] = tuple(x[i] for x in res)

    def gather(cur_ref, prev_ref, slabs, start, n):
        if start == 0:
            parts = [jnp.concatenate([prev_ref[0, r, SLAB - n:SLAB, :], cur_ref[0, r, 0:n, :]], axis=0)
                     for r in slabs]
        else:
            parts = [cur_ref[0, r, start - n:start + n, :] for r in slabs]
        return jnp.concatenate(parts, axis=0)

    for r4 in range(4):
        slabs = [r4 + 4 * i for i in range(4)]
        for jj in range(SLAB // 32):
            sl = slice(32 * jj, 32 * jj + 32)
            q = jnp.concatenate([q_ref[0, r, sl, :] for r in slabs], axis=0)
            k = gather(kc_ref, kp_ref, slabs, 32 * jj, 32)
            v = gather(vc_ref, vp_ref, slabs, 32 * jj, 32)
            keys_ = [(r, 2 * jj + a) for r in slabs for a in range(2)]
            update(keys_, q, k, v, b4_ref[hp] if jj == 0 else b4_ref[1])

    slabs = list(range(RES))
    for jj in range(pieces):
        sl = slice(PIECE * jj, PIECE * jj + PIECE)
        q = jnp.concatenate([q_ref[0, r, sl, :] for r in slabs], axis=0)
        k = gather(kc_ref, kp_ref, slabs, PIECE * jj, PIECE)
        v = gather(vc_ref, vp_ref, slabs, PIECE * jj, PIECE)
        update([(r, jj) for r in slabs], q, k, v, b1_ref[hp] if jj == 0 else b1_ref[1])

    for r in range(RES):
        acc = jnp.concatenate([state[(r, a)][2] for a in range(pieces)], axis=0)
        l = jnp.concatenate([state[(r, a)][1] for a in range(pieces)], axis=0)
        nat_ref[pl.ds(r, SLAB, stride=RES), :] = acc / l
    o_ref[...] = nat_ref[...].astype(o_ref.dtype)


def _prompt_attention(qkv16):
    b, _, rows, _ = qkv16.shape
    n_sb = rows // SLAB
    kcol, vcol = WIDTH_A // LANES, 2 * WIDTH_A // LANES
    blk = (1, RES, SLAB, LANES)
    cur = lambda off: pl.BlockSpec(blk, lambda bi, h, n: (bi, 0, n, off + h))
    prv = lambda off: pl.BlockSpec(blk, lambda bi, h, n: (bi, 0, jnp.maximum(n - 1, 0), off + h))
    const = lambda shp: pl.BlockSpec(shp, lambda bi, h, n: (0, 0, 0))
    return pl.pallas_call(
        _attn_kernel,
        grid=(b, N_HEADS_A, n_sb),
        in_specs=[cur(0), cur(kcol), prv(kcol), cur(vcol), prv(vcol),
                  const((2, 128, 256)), const((2, 128, 256)), const((2, 256, 512))],
        out_specs=pl.BlockSpec((RES * SLAB, LANES), lambda bi, h, n: (bi * n_sb + n, h)),
        out_shape=jax.ShapeDtypeStruct((b * rows * RES, WIDTH_A), jnp.bfloat16),
        scratch_shapes=[pltpu.VMEM((RES * SLAB, LANES), jnp.float32)],
        compiler_params=_cparams(("arbitrary", "arbitrary", "arbitrary")),
        name="prompt_attn",
    )(qkv16, qkv16, qkv16, qkv16, qkv16, _band_bias(16), _band_bias(4), _band_bias(1))


DN_STEP = 256
CARRY = 8


def _split_bf16(x, n):
    parts, r = [], x
    for _ in range(n):
        hi = r.astype(jnp.bfloat16)
        parts.append(hi)
        r = r - hi.astype(jnp.float32)
    return parts


def _bdot(a, b):
    return jnp.dot(a.astype(jnp.bfloat16), b.astype(jnp.bfloat16), preferred_element_type=jnp.float32)


def _softplus(x):
    return jnp.maximum(x, 0.0) + jnp.log1p(jnp.exp(-jnp.abs(x)))


def _silu(x):
    return x * jax.nn.sigmoid(x)


def _unit_lower_inverses(a_list):
    n = a_list[0].shape[0]
    row = lax.broadcasted_iota(jnp.int32, (n, n), 0)
    col = lax.broadcasted_iota(jnp.int32, (n, n), 1)
    eye = jnp.where(row == col, 1.0, 0.0)
    xs = [eye - a for a in a_list]
    ps = [_bdot(a, a) for a in a_list]
    k = 2
    while True:
        xs = [x + _bdot(x, p) for x, p in zip(xs, ps)]
        k *= 2
        if k >= n:
            return xs
        ps = [_bdot(p, p) for p in ps]


def _dn_kernel(qb_ref, kb_ref, vb_ref, tail_ref, tailt_ref, cw_ref, prow_ref, pcol_ref, onw_ref,
               o_ref, s_ref, xe_ref):
    c = pl.program_id(1)
    f32, bf16 = jnp.float32, jnp.bfloat16
    tb, n, hh = DN_STEP, CHUNK, N_HEADS_B
    n_ch = tb // n

    @pl.when(c == 0)
    def _():
        s_ref[...] = jnp.zeros_like(s_ref)
        xe_ref[0:CARRY, :] = jnp.zeros((CARRY, CONV_CH), f32)

    @pl.when(c > 0)
    def _():
        xe_ref[0:CARRY, :] = xe_ref[tb:tb + CARRY, :]

    conv = []
    for part, ref in enumerate((qb_ref, kb_ref, vb_ref)):
        cs = slice(part * WIDTH_B, (part + 1) * WIDTH_B)
        xe_ref[CARRY:CARRY + tb, cs] = ref[...].astype(f32)
        y = xe_ref[CARRY:CARRY + tb, cs] * cw_ref[CONV_WIDTH - 1:CONV_WIDTH, cs]
        for i in range(CONV_WIDTH - 1):
            off = CARRY - (CONV_WIDTH - 1) + i
            y = y + xe_ref[off:off + tb, cs] * cw_ref[i:i + 1, cs]
        conv.append(_silu(y))
    qc, kc, vc = conv

    t = tail_ref[...]
    beta_c = jax.nn.sigmoid(t)
    g_c = -jnp.exp(prow_ref[0:1]) * _softplus(t + prow_ref[1:2])
    tt = tailt_ref[...]
    g_r = -jnp.exp(pcol_ref[:, 0:1]) * _softplus(tt + pcol_ref[:, 1:2])

    row = lax.broadcasted_iota(jnp.int32, (n, n), 0)
    col = lax.broadcasted_iota(jnp.int32, (n, n), 1)
    incl = row >= col
    strict = row > col
    ltri = jnp.where(incl, 1.0, 0.0).astype(bf16)
    utri = jnp.where(row <= col, 1.0, 0.0).astype(bf16)
    dot = functools.partial(jnp.dot, preferred_element_type=f32)

    gcs, grs, eg_c, ed_c, glast = [], [], [], [], []
    for ci in range(n_ch):
        rs = slice(ci * n, (ci + 1) * n)
        gc = sum(dot(ltri, part) for part in _split_bf16(g_c[rs], 3))
        gr = sum(dot(part, utri) for part in _split_bf16(g_r[:, rs], 3))
        gl = gc[n - 1:n, :]
        gcs.append(gc)
        grs.append(gr)
        glast.append(gl)
        eg_c.append(jnp.exp(gc))
        ed_c.append(jnp.exp(gl - gc))

    ids = [(ci, h) for ci in range(n_ch) for h in range(hh)]

    def head_slice(x, ci, h):
        return x[ci * n:(ci + 1) * n, h * DK_B:(h + 1) * DK_B]

    def lane_col(x, lane):
        return x[:, lane:lane + 1]

    qs, ks, kbetas, vbetas, egs = [], [], [], [], []
    for ci, h in ids:
        q = head_slice(qc, ci, h)
        k = head_slice(kc, ci, h)
        v = head_slice(vc, ci, h)
        q = q * (lax.rsqrt(jnp.sum(q * q, axis=-1, keepdims=True) + EPS) * (DK_B ** -0.5))
        k = k * lax.rsqrt(jnp.sum(k * k, axis=-1, keepdims=True) + EPS)
        beta = lane_col(beta_c[ci * n:(ci + 1) * n], h)
        qs.append(q)
        ks.append(k)
        kbetas.append(k * beta)
        vbetas.append(v * beta)
        egs.append(lane_col(eg_c[ci], hh + h))

    kqs = [_nt_dot(jnp.concatenate([kb, q], axis=0).astype(bf16), k.astype(bf16))
           for kb, q, k in zip(kbetas, qs, ks)]
    a_mats, qks = [], []
    for (ci, h), kq in zip(ids, kqs):
        gcol = lane_col(gcs[ci], hh + h)
        grow = grs[ci][hh + h:hh + h + 1, :]
        decay = jnp.exp(jnp.where(incl, gcol - grow, NEG_BIG))
        a_mats.append(jnp.where(strict, kq[:n] * decay, 0.0))
        qks.append((kq[n:] * decay).astype(bf16))
    t_mats = _unit_lower_inverses(a_mats)
    uws = [dot(tm.astype(bf16), jnp.concatenate([vb, kb * eg], axis=1).astype(bf16))
           for tm, vb, kb, eg in zip(t_mats, vbetas, kbetas, egs)]

    states = [s_ref[0, h] for h in range(hh)]
    onw = onw_ref[...]
    for ci in range(n_ch):
        base = ci * hh
        wqs = [jnp.concatenate([uws[base + h][:, DV_B:], qs[base + h] * egs[base + h]], axis=0).astype(bf16)
               for h in range(hh)]
        wss = [dot(wq, st.astype(bf16)) for wq, st in zip(wqs, states)]
        v_news = [(uws[base + h][:, :DV_B] - wss[h][:n]).astype(bf16) for h in range(hh)]
        o_in = [dot(qks[base + h], v_news[h]) for h in range(hh)]
        k_decs = [(ks[base + h] * lane_col(ed_c[ci], hh + h)).astype(bf16) for h in range(hh)]
        upds = [lax.dot_general(k_decs[h], v_news[h], (((0,), (0,)), ((), ())), preferred_element_type=f32)
                for h in range(hh)]
        e_last = jnp.exp(glast[ci])
        states = [states[h] * lane_col(e_last, hh + h) + upds[h] for h in range(hh)]
        for h in range(hh):
            o = wss[h][n:] + o_in[h]
            o = o * lax.rsqrt(jnp.mean(o * o, axis=-1, keepdims=True) + EPS) * onw
            o_ref[ci * n:(ci + 1) * n, h * DV_B:(h + 1) * DV_B] = o.astype(o_ref.dtype)
    for h in range(hh):
        s_ref[0, h] = states[h]


def _deltanet_prompt(p, tail, tail_t, conv_w, a_log, dt_bias, onorm_w, batch):
    m = p.shape[0]
    steps = m // batch // DN_STEP
    wb = WIDTH_B // 1
    qcol = REST_QKVB // wb
    zeros8 = jnp.zeros((N_HEADS_B,), jnp.float32)
    gate = jnp.stack([jnp.concatenate([zeros8, a_log.astype(jnp.float32)]),
                      jnp.concatenate([zeros8, dt_bias.astype(jnp.float32)])])
    prow = jnp.pad(gate, ((0, 6), (0, LANES - TAIL_COLS)))
    pcol = jnp.pad(gate.T, ((0, 0), (0, LANES - 2)))
    tok = lambda cb: pl.BlockSpec((DN_STEP, wb), lambda b, c: (b * steps + c, cb))
    full = lambda shp: pl.BlockSpec(shp, lambda b, c: (0,) * len(shp))
    return pl.pallas_call(
        _dn_kernel,
        grid=(batch, steps),
        in_specs=[tok(qcol), tok(qcol + 1), tok(qcol + 2),
                  pl.BlockSpec((DN_STEP, LANES), lambda b, c: (b * steps + c, 0)),
                  pl.BlockSpec((TAIL_COLS, DN_STEP), lambda b, c: (0, b * steps + c)),
                  full((CONV_WIDTH, CONV_CH)), full((8, LANES)), full((TAIL_COLS, LANES)),
                  full((1, DV_B))],
        out_specs=[pl.BlockSpec((DN_STEP, WIDTH_B), lambda b, c: (b * steps + c, 0)),
                   pl.BlockSpec((1, N_HEADS_B, DK_B, DV_B), lambda b, c: (b, 0, 0, 0))],
        out_shape=[jax.ShapeDtypeStruct((m, WIDTH_B), jnp.bfloat16),
                   jax.ShapeDtypeStruct((batch, N_HEADS_B, DK_B, DV_B), jnp.float32)],
        scratch_shapes=[pltpu.VMEM((CARRY + DN_STEP, CONV_CH), jnp.float32)],
        compiler_params=_cparams(("arbitrary", "arbitrary")),
        name="deltanet_prompt",
    )(p, p, p, tail, tail_t, conv_w.astype(jnp.float32), prow, pcol,
      onorm_w.astype(jnp.float32)[None])


def _out_kernel(oa_ref, za_ref, ob_ref, zb_ref, w_ref, g_ref, x_ref, y_ref):
    f32 = jnp.float32
    ga = (oa_ref[...].astype(f32) * _silu(za_ref[...].astype(f32))).astype(jnp.bfloat16)
    gb = (ob_ref[...].astype(f32) * _silu(zb_ref[...].astype(f32))).astype(jnp.bfloat16)
    y = (jnp.dot(ga, w_ref[:WIDTH_A, :], preferred_element_type=f32)
         + jnp.dot(gb, w_ref[WIDTH_A:, :], preferred_element_type=f32))
    y = y * lax.rsqrt(jnp.mean(y * y, axis=-1, keepdims=True) + EPS) * g_ref[...]
    y_ref[...] = x_ref[...] + y


def _output_sublayer(o_a, p, o_b, w_out, g_post, x2d, *, tm):
    m = x2d.shape[0]
    row = lambda width, cb: pl.BlockSpec((tm, width), lambda i: (i, cb))
    return pl.pallas_call(
        _out_kernel,
        grid=(m // tm,),
        in_specs=[row(WIDTH_A, 0), row(WIDTH_A, REST_ZA // WIDTH_A), row(WIDTH_B, 0),
                  row(WIDTH_B, REST_ZB // WIDTH_B),
                  pl.BlockSpec((WIDTH_A + WIDTH_B, D_MODEL), lambda i: (0, 0)),
                  pl.BlockSpec((1, D_MODEL), lambda i: (0, 0)),
                  row(D_MODEL, 0)],
        out_specs=row(D_MODEL, 0),
        out_shape=jax.ShapeDtypeStruct((m, D_MODEL), jnp.float32),
        compiler_params=_cparams(("arbitrary",)),
        name="out_proj",
    )(o_a, p, o_b, p, w_out, g_post, x2d)


def _decode_attn_kernel(q_ref, kn_ref, vn_ref, k1_ref, v1_ref, k4_ref, v4_ref, k16_ref, v16_ref, o_ref):
    f32 = jnp.float32
    q = q_ref[0].astype(f32)
    kn, vn = kn_ref[0].astype(f32), vn_ref[0].astype(f32)
    s_new = jnp.sum(q * kn, axis=-1, keepdims=True)
    scores = [jnp.sum(k_ref[0] * q[None], axis=-1, keepdims=True) for k_ref in (k1_ref, k4_ref, k16_ref)]
    m = s_new
    for s in scores:
        m = jnp.maximum(m, jnp.max(s, axis=0))
    p_new = len(DILATIONS) * jnp.exp(s_new - m)
    den = p_new
    acc = p_new * vn
    for s, v_ref in zip(scores, (v1_ref, v4_ref, v16_ref)):
        p = jnp.exp(s - m[None])
        den = den + jnp.sum(p, axis=0)
        acc = acc + jnp.sum(p * v_ref[0], axis=0)
    o_ref[0] = (acc / den).astype(o_ref.dtype)


def _decode_attention(q, k_new, v_new, cache_k, cache_v):
    b, win, h, dd = cache_k.shape
    nb = 128
    views, specs = [], []
    for window, dil in DILATIONS:
        assert window // dil == nb and win % (nb * dil) == 0
        blk_idx = win // (nb * dil) - 1
        if dil == 1:
            spec = pl.BlockSpec((1, nb, h, dd), lambda i, bi=blk_idx: (i, bi, 0, 0))
            view = lambda c: c
        else:
            spec = pl.BlockSpec((1, nb, None, h, dd), lambda i, bi=blk_idx: (i, bi, 0, 0, 0))
            view = lambda c, dil=dil: c.reshape(b, win // dil, dil, h, dd)
        specs += [spec, spec]
        views += [view(cache_k), view(cache_v)]
    tok = pl.BlockSpec((1, h, dd), lambda i: (i, 0, 0))
    return pl.pallas_call(
        _decode_attn_kernel,
        grid=(b,),
        in_specs=[tok, tok, tok] + specs,
        out_specs=tok,
        out_shape=jax.ShapeDtypeStruct((b, h, dd), jnp.bfloat16),
        compiler_params=_cparams(("arbitrary",)),
        name="decode_attn",
    )(q, k_new, v_new, *views)


def _decode_dn_kernel(x_ref, cb_ref, cw_ref, gate_ref, prm_ref, onw_ref, s_ref,
                      o_ref, cbo_ref, so_ref):
    f32 = jnp.float32
    hh = N_HEADS_B
    xn = x_ref[0]
    cw = cw_ref[...]
    y = xn * cw[CONV_WIDTH - 1]
    for i in range(CONV_WIDTH - 1):
        y = y + cb_ref[0, i] * cw[i]
        cbo_ref[0, i] = cb_ref[0, i + 1] if i + 1 < CONV_WIDTH - 1 else xn
    y = _silu(y)
    q, k, v = y[:hh], y[hh:2 * hh], y[2 * hh:]
    q = q * lax.rsqrt(jnp.sum(q * q, axis=-1, keepdims=True) + EPS) * (DK_B ** -0.5)
    k = k * lax.rsqrt(jnp.sum(k * k, axis=-1, keepdims=True) + EPS)
    gate = gate_ref[0]
    beta = jax.nn.sigmoid(gate[:hh])
    decay = jnp.exp(-jnp.exp(prm_ref[:hh]) * _softplus(gate[hh:] + prm_ref[hh:]))
    qk = jnp.sum(q * k, axis=-1, keepdims=True)
    zeros = jnp.zeros((8 - 2, DK_B), jnp.bfloat16)
    outs = []
    for h in range(hh):
        state = s_ref[0, h]
        a = decay[h:h + 1]
        lhs = jnp.concatenate([k[h:h + 1].astype(jnp.bfloat16), q[h:h + 1].astype(jnp.bfloat16), zeros], axis=0)
        ks = jnp.dot(lhs, state.astype(jnp.bfloat16), preferred_element_type=f32)
        v_new = beta[h:h + 1] * (v[h:h + 1] - a * ks[0:1])
        upd = lax.dot_general(lhs[0:8], jnp.concatenate([v_new.astype(jnp.bfloat16), jnp.zeros((7, DV_B), jnp.bfloat16)], axis=0),
                              (((0,), (0,)), ((), ())), preferred_element_type=f32)
        so_ref[0, h] = state * a[:, 0:1] + upd
        outs.append(a * ks[1:2] + qk[h:h + 1] * v_new)
    o = jnp.concatenate(outs, axis=0)
    o = o * lax.rsqrt(jnp.mean(o * o, axis=-1, keepdims=True) + EPS) * onw_ref[...]
    o_ref[0] = o.astype(o_ref.dtype)


def _decode_deltanet(x_new, conv_buf, state, beta_in, a_in, conv_w, a_log, dt_bias, onorm_w):
    b = x_new.shape[0]
    f32 = jnp.float32
    g3 = CONV_CH // LANES
    hh = N_HEADS_B
    gate = jnp.broadcast_to(jnp.concatenate([beta_in, a_in], axis=1).astype(f32)[:, :, None], (b, 2 * hh, LANES))
    prm = jnp.broadcast_to(jnp.concatenate([a_log, dt_bias]).astype(f32)[:, None], (2 * hh, LANES))
    full = lambda shp: pl.BlockSpec(shp, lambda i: (0,) * len(shp))
    per = lambda shp: pl.BlockSpec((1,) + shp, lambda i: (i,) + (0,) * len(shp))
    o, cb_new, s_new = pl.pallas_call(
        _decode_dn_kernel,
        grid=(b,),
        in_specs=[per((g3, LANES)), per((CONV_WIDTH - 1, g3, LANES)), full((CONV_WIDTH, g3, LANES)),
                  per((2 * hh, LANES)), full((2 * hh, LANES)), full((1, DV_B)), per((hh, DK_B, DV_B))],
        out_specs=[per((hh, DV_B)), per((CONV_WIDTH - 1, g3, LANES)), per((hh, DK_B, DV_B))],
        out_shape=[jax.ShapeDtypeStruct((b, hh, DV_B), jnp.bfloat16),
                   jax.ShapeDtypeStruct((b, CONV_WIDTH - 1, g3, LANES), f32),
                   jax.ShapeDtypeStruct((b, hh, DK_B, DV_B), f32)],
        compiler_params=_cparams(("arbitrary",)),
        name="decode_deltanet",
    )(x_new.astype(f32).reshape(b, g3, LANES), conv_buf.astype(f32).reshape(b, CONV_WIDTH - 1, g3, LANES),
      conv_w.astype(f32).reshape(CONV_WIDTH, g3, LANES), gate, prm, onorm_w.astype(f32)[None], state.astype(f32))
    return o, cb_new.reshape(b, CONV_WIDTH - 1, CONV_CH), s_new


def kernel(x_prompt, x_sample, cache_win_k, cache_win_v, state_conv, state_delta,
           g_pre, w_in, conv_w, a_log, dt_bias, onorm_w, w_out, g_post):
    f32, bf16 = jnp.float32, jnp.bfloat16
    b, s, _ = x_prompt.shape
    db, t, _ = x_sample.shape
    depth = w_in.shape[0]
    n_past = cache_win_k.shape[2]
    assert t == 1 and n_past == MAX_WINDOW and s % (RES * SLAB) == 0
    keep = min(MAX_WINDOW, s)

    cos_p, sin_p = _rope_tables(jnp.arange(s, dtype=jnp.int32))
    cos_p, sin_p = jnp.tile(cos_p, (b, 1)), jnp.tile(sin_p, (b, 1))
    cos_s, sin_s = _rope_tables(jnp.full((db,), PAST_LEN, jnp.int32))

    yp = x_prompt.reshape(b * s, D_MODEL)
    ys = x_sample.reshape(db, D_MODEL)
    outs = [[] for _ in range(8)]
    for l in range(depth):
        w_main = w_in[l][:, :MAIN_COLS].astype(bf16)
        w_tail_t = jnp.pad(w_in[l][:, MAIN_COLS:].T.astype(bf16), ((0, LANES - TAIL_COLS), (0, 0)))
        w_o = w_out[l].astype(bf16)
        gp, go = g_pre[l].astype(f32)[None], g_post[l].astype(f32)[None]

        qkv16, kf, vf, rest, tail, tail_t = _project(yp, gp, w_main, w_tail_t, cos_p, sin_p,
                                                     seq=s, keep=keep, tm=512, residue_major=True)
        o_a = _prompt_attention(qkv16)
        o_b, s_fin = _deltanet_prompt(rest, tail, tail_t, conv_w[l], a_log[l], dt_bias[l], onorm_w[l], b)
        outs[0].append(kf.reshape(b, keep, N_HEADS_A, HEAD_DIM))
        outs[1].append(vf.reshape(b, keep, N_HEADS_A, HEAD_DIM))
        n_tail = min(CONV_WIDTH - 1, s)
        tail_rows = rest.reshape(b, s, REST_COLS)[:, s - n_tail:, REST_QKVB:REST_QKVB + CONV_CH].astype(f32)
        outs[2].append(jnp.pad(tail_rows, ((0, 0), (CONV_WIDTH - 1 - n_tail, 0), (0, 0))))
        outs[3].append(s_fin)
        yp = _output_sublayer(o_a, rest, o_b, w_o, go, yp, tm=256)

        qkv_s, kf_s, vf_s, rest_s, tail_s, _ = _project(ys, gp, w_main, w_tail_t, cos_s, sin_s,
                                                        seq=db, keep=db, tm=db, residue_major=False)
        k_new = kf_s.reshape(db, N_HEADS_A, HEAD_DIM)
        v_new = vf_s.reshape(db, N_HEADS_A, HEAD_DIM)
        o_as = _decode_attention(qkv_s[:, :WIDTH_A].reshape(db, N_HEADS_A, HEAD_DIM), k_new, v_new,
                                 cache_win_k[l], cache_win_v[l])
        o_bs, cb_new, st_new = _decode_deltanet(
            rest_s[:, REST_QKVB:REST_QKVB + CONV_CH], state_conv[l], state_delta[l],
            tail_s[:, :N_HEADS_B], tail_s[:, N_HEADS_B:TAIL_COLS], conv_w[l], a_log[l], dt_bias[l], onorm_w[l])
        outs[4].append(k_new.reshape(db, t, N_HEADS_A, HEAD_DIM))
        outs[5].append(v_new.reshape(db, t, N_HEADS_A, HEAD_DIM))
        outs[6].append(cb_new)
        outs[7].append(st_new)
        ys = _output_sublayer(o_as.reshape(db, WIDTH_A), rest_s, o_bs.reshape(db, WIDTH_B), w_o, go, ys, tm=db)

    stk = [jnp.stack(o) for o in outs]
    return (yp.reshape(b, s, D_MODEL), ys.reshape(db, t, D_MODEL),
            stk[0], stk[1], stk[2], stk[3], stk[4], stk[5], stk[6], stk[7])
```

```python
import functools
import math

import jax
import jax.numpy as jnp
from jax import lax
from jax.experimental import pallas as pl
from jax.experimental.pallas import tpu as pltpu

D_MODEL = 2048
HEAD_DIM = 128
N_HEADS_A = 8
N_HEADS_B = 8
DK_B = 128
DV_B = 128
WIDTH_A = N_HEADS_A * HEAD_DIM
WIDTH_B = N_HEADS_B * DV_B
DILATIONS = ((128, 1), (512, 4), (2048, 16))
MAX_WINDOW = 2048
ROPE_THETA = 500000.0
ROPE_DIM = HEAD_DIM // 4
CONV_WIDTH = 4
CONV_CH = 2 * N_HEADS_B * DK_B + N_HEADS_B * DV_B
CHUNK = 64
EPS = 1e-6
PAST_LEN = 16384
MAIN_COLS = 4 * WIDTH_A + CONV_CH + WIDTH_B
TAIL_COLS = 2 * N_HEADS_B
LANES = 128
NEG_BIG = -1e30
VMEM_LIMIT = 56 * 1024 * 1024

REST_COLS = MAIN_COLS - 3 * WIDTH_A
REST_ZA, REST_QKVB, REST_ZB = 0, WIDTH_A, WIDTH_A + CONV_CH


def _cparams(sem):
    return pltpu.CompilerParams(dimension_semantics=sem, vmem_limit_bytes=VMEM_LIMIT)


def _cast_kernel(x_ref, o_ref):
    o_ref[...] = x_ref[...].astype(o_ref.dtype)


def _to_bf16(w, cols, tn=1024):
    rows = w.shape[0]
    spec = pl.BlockSpec((rows, tn), lambda j: (0, j))
    return pl.pallas_call(
        _cast_kernel,
        grid=(cols // tn,),
        in_specs=[spec],
        out_specs=spec,
        out_shape=jax.ShapeDtypeStruct((rows, cols), jnp.bfloat16),
        compiler_params=_cparams(("arbitrary",)),
        name="cast_bf16",
    )(w)


RES = 16
QKV_TILES = 3
SUB_COLS = 256


def _proj_kernel(x_ref, g_ref, w_ref, wt_ref, cos_ref, sin_ref,
                 qkv_ref, kf_ref, vf_ref, p_ref, tail_ref, tailt_ref, h_ref, de_ref,
                 *, tiles_per_seq, first_keep_tile, residue_major):
    i = pl.program_id(0)
    j = pl.program_id(1)
    tm = x_ref.shape[0]

    @pl.when(j == 0)
    def _():
        x = x_ref[...]
        y = x * lax.rsqrt(jnp.mean(x * x, axis=-1, keepdims=True) + EPS)
        h = (y * g_ref[...]).astype(jnp.bfloat16)
        h_ref[...] = h
        wt = wt_ref[...]
        tail_ref[...] = lax.dot_general(h, wt, (((1,), (1,)), ((), ())),
                                        preferred_element_type=jnp.float32)
        tailt_ref[...] = lax.dot_general(wt[:TAIL_COLS], h, (((1,), (1,)), ((), ())),
                                         preferred_element_type=jnp.float32)

    keep_rows = (i % tiles_per_seq) >= first_keep_tile
    heads_per_sub = SUB_COLS // LANES

    def sub_dot(sub):
        return jnp.dot(h_ref[...], w_ref[:, sub * SUB_COLS:(sub + 1) * SUB_COLS],
                       preferred_element_type=jnp.float32)

    def emit_qkv(hd, r, win_ref):
        cs = slice(hd * LANES, (hd + 1) * LANES)
        if residue_major:
            de_ref[hd] = r
            for res in range(RES):
                qkv_ref[0, res, :, cs] = de_ref[hd, pl.ds(res, tm // RES, stride=RES), :].astype(qkv_ref.dtype)
        else:
            qkv_ref[:, cs] = r.astype(qkv_ref.dtype)
        if win_ref is not None:
            win_ref[pl.ds(hd, tm, stride=N_HEADS_A), :] = r

    def rotary_tile(scale, win_ref):
        c = cos_ref[...]
        s = sin_ref[...]
        lane = lax.broadcasted_iota(jnp.int32, c.shape, 1)
        for sub in range(WIDTH_A // SUB_COLS):
            acc = sub_dot(sub)
            for hs in range(heads_per_sub):
                a = acc[:, hs * LANES:(hs + 1) * LANES]
                swapped = jnp.where(lane < ROPE_DIM // 2,
                                    pltpu.roll(a, LANES - ROPE_DIM // 2, 1),
                                    pltpu.roll(a, ROPE_DIM // 2, 1))
                r = a * c + swapped * s
                emit_qkv(sub * heads_per_sub + hs, r if scale is None else r * scale, win_ref)

    def plain_tile(win_ref):
        for sub in range(WIDTH_A // SUB_COLS):
            acc = sub_dot(sub)
            for hs in range(heads_per_sub):
                emit_qkv(sub * heads_per_sub + hs, acc[:, hs * LANES:(hs + 1) * LANES], win_ref)

    pl.when(j == 0)(lambda: rotary_tile(HEAD_DIM ** -0.5, None))
    pl.when((j == 1) & keep_rows)(lambda: rotary_tile(None, kf_ref))
    pl.when((j == 1) & jnp.logical_not(keep_rows))(lambda: rotary_tile(None, None))
    pl.when((j == 2) & keep_rows)(lambda: plain_tile(vf_ref))
    pl.when((j == 2) & jnp.logical_not(keep_rows))(lambda: plain_tile(None))

    @pl.when(j >= QKV_TILES)
    def _():
        for sub in range(WIDTH_A // SUB_COLS):
            p_ref[:, sub * SUB_COLS:(sub + 1) * SUB_COLS] = sub_dot(sub).astype(p_ref.dtype)


def _rope_tables(pos):
    half = ROPE_DIM // 2
    inv = ROPE_THETA ** (-jnp.arange(half, dtype=jnp.float32) / half)
    ang = pos.astype(jnp.float32)[:, None] * inv[None, :]
    cos, sin = jnp.cos(ang), jnp.sin(ang)
    n = pos.shape[0]
    ones = jnp.ones((n, LANES - ROPE_DIM), jnp.float32)
    c = jnp.concatenate([cos, cos, ones], axis=1)
    s = jnp.concatenate([-sin, sin, jnp.zeros_like(ones)], axis=1)
    return c, s


def _project(x2d, g_pre, w_main, w_tail_t, cos_t, sin_t, *, seq, keep, tm, residue_major):
    m = x2d.shape[0]
    tn = WIDTH_A
    n_i, n_j = m // tm, MAIN_COLS // tn
    tiles_per_seq = seq // tm
    first_keep = (seq - keep) // tm
    keep_tiles = keep // tm
    assert seq % tm == 0 and keep % tm == 0 and (seq - keep) % tm == 0
    kern = functools.partial(_proj_kernel, tiles_per_seq=tiles_per_seq, first_keep_tile=first_keep,
                             residue_major=residue_major)
    qkv_col = lambda j: jnp.minimum(j, QKV_TILES - 1)
    if residue_major:
        assert tm % (RES * 16) == 0
        qkv_spec = pl.BlockSpec((1, RES, tm // RES, tn),
                                lambda i, j: (i // tiles_per_seq, 0, i % tiles_per_seq, qkv_col(j)))
        qkv_shape = jax.ShapeDtypeStruct((m // seq, RES, seq // RES, QKV_TILES * tn), jnp.bfloat16)
    else:
        qkv_spec = pl.BlockSpec((tm, tn), lambda i, j: (i, qkv_col(j)))
        qkv_shape = jax.ShapeDtypeStruct((m, QKV_TILES * tn), jnp.bfloat16)

    def win_index(i, j):
        il = i % tiles_per_seq
        return (i // tiles_per_seq) * keep_tiles + jnp.maximum(il - first_keep, 0), 0

    win_spec = pl.BlockSpec((tm * N_HEADS_A, LANES), win_index)
    win_shape = jax.ShapeDtypeStruct((m // seq * keep * N_HEADS_A, LANES), jnp.float32)

    return pl.pallas_call(
        kern,
        grid=(n_i, n_j),
        in_specs=[
            pl.BlockSpec((tm, D_MODEL), lambda i, j: (i, 0)),
            pl.BlockSpec((1, D_MODEL), lambda i, j: (0, 0)),
            pl.BlockSpec((D_MODEL, tn), lambda i, j: (0, j)),
            pl.BlockSpec((LANES, D_MODEL), lambda i, j: (0, 0)),
            pl.BlockSpec((tm, LANES), lambda i, j: (i, 0)),
            pl.BlockSpec((tm, LANES), lambda i, j: (i, 0)),
        ],
        out_specs=[
            qkv_spec,
            win_spec,
            win_spec,
            pl.BlockSpec((tm, tn), lambda i, j: (i, jnp.maximum(j - QKV_TILES, 0))),
            pl.BlockSpec((tm, LANES), lambda i, j: (i, 0)),
            pl.BlockSpec((TAIL_COLS, tm), lambda i, j: (0, i)),
        ],
        out_shape=[
            qkv_shape,
            win_shape,
            win_shape,
            jax.ShapeDtypeStruct((m, REST_COLS), jnp.bfloat16),
            jax.ShapeDtypeStruct((m, LANES), jnp.float32),
            jax.ShapeDtypeStruct((TAIL_COLS, m), jnp.float32),
        ],
        scratch_shapes=[pltpu.VMEM((tm, D_MODEL), jnp.bfloat16),
                        pltpu.VMEM((WIDTH_A // LANES, tm, LANES), jnp.float32)],
        compiler_params=_cparams(("arbitrary", "arbitrary")),
        name="proj",
    )(x2d, g_pre, w_main, w_tail_t, cos_t, sin_t)


SLAB = 128


def _band_bias(kind):
    import numpy as np
    if kind == 16:
        nq, nk = 128, 256
        lq = np.arange(nq)[:, None]
        kap = np.arange(nk)[None, :]
        lk = kap - 128
        prev = kap < 128
    elif kind == 4:
        nq, nk = 128, 256
        rho = np.arange(nq)[:, None]
        lq = 4 * (rho % 32) + rho // 32
        kap = np.arange(nk)[None, :]
        lk = 4 * (kap % 64 - 32) + kap // 64
        prev = (kap % 64) < 32
    else:
        nq, nk = 256, 512
        rho = np.arange(nq)[:, None]
        lq = 16 * (rho % 16) + rho // 16
        kap = np.arange(nk)[None, :]
        lk = 16 * (kap % 32 - 16) + kap // 32
        prev = (kap % 32) < 16
    dist = lq - lk
    band = (dist >= 0) & (dist <= 128)
    out = np.stack([band & ~prev, band])
    return jnp.asarray(np.where(out, 0.0, NEG_BIG), dtype=jnp.float32)


def _nt_dot(a, b):
    return lax.dot_general(a, b, (((1,), (1,)), ((), ())), preferred_element_type=jnp.float32)


PIECE = 16


def _attn_kernel(q_ref, kc_ref, kp_ref, vc_ref, vp_ref, b16_ref, b4_ref, b1_ref, o_ref, nat_ref):
    hp = jnp.minimum(pl.program_id(2), 1)
    f32 = jnp.float32

    def block(q, k, v, bias, prev):
        s = _nt_dot(q, k) + bias
        rows, keys = s.shape
        m_new = jnp.broadcast_to(jnp.max(s, axis=1, keepdims=True), (rows, LANES))
        if prev is not None:
            m_prev, l_prev, acc_prev = prev
            m_new = jnp.maximum(m_prev, m_new)
        p = jnp.exp(s - jnp.concatenate([m_new] * (keys // LANES), axis=1))
        v1 = jnp.concatenate([v, jnp.ones_like(v)], axis=1)
        pv = jnp.dot(p.astype(v.dtype), v1, preferred_element_type=f32)
        acc_new, l_new = pv[:, :LANES], pv[:, LANES:]
        if prev is not None:
            alpha = jnp.exp(m_prev - m_new)
            l_new = alpha * l_prev + l_new
            acc_new = alpha * acc_prev + acc_new
        return m_new, l_new, acc_new

    def split(x, n_parts):
        n = x.shape[0] // n_parts
        return [x[i * n:(i + 1) * n] for i in range(n_parts)]

    state = {}
    pieces = SLAB // PIECE

    for r in range(RES):
        k = jnp.concatenate([kp_ref[0, r], kc_ref[0, r]], axis=0)
        v = jnp.concatenate([vp_ref[0, r], vc_ref[0, r]], axis=0)
        res = [split(x, pieces) for x in block(q_ref[0, r], k, v, b16_ref[hp], None)]
        for a in range(pieces):
            state[(r, a)] = tuple(x[a] for x in res)

    def update(keys_, q, k, v, bias):
        prev = tuple(jnp.concatenate([state[key][i] for key in keys_], axis=0) for i in range(3))
        res = [split(x, len(keys_)) for x in block(q, k, v, bias, prev)]
        for i, key in enumerate(keys_):
            state[key] = tuple(x[i] for x in res)

    def gather(cur_ref, prev_ref, slabs, start, n):
        if start == 0:
            parts = [jnp.concatenate([prev_ref[0, r, SLAB - n:SLAB, :], cur_ref[0, r, 0:n, :]], axis=0)
                     for r in slabs]
        else:
            parts = [cur_ref[0, r, start - n:start + n, :] for r in slabs]
        return jnp.concatenate(parts, axis=0)

    for r4 in range(4):
        slabs = [r4 + 4 * i for i in range(4)]
        for jj in range(SLAB // 32):
            sl = slice(32 * jj, 32 * jj + 32)
            q = jnp.concatenate([q_ref[0, r, sl, :] for r in slabs], axis=0)
            k = gather(kc_ref, kp_ref, slabs, 32 * jj, 32)
            v = gather(vc_ref, vp_ref, slabs, 32 * jj, 32)
            keys_ = [(r, 2 * jj + a) for r in slabs for a in range(2)]
            update(keys_, q, k, v, b4_ref[hp] if jj == 0 else b4_ref[1])

    slabs = list(range(RES))
    for jj in range(pieces):
        sl = slice(PIECE * jj, PIECE * jj + PIECE)
        q = jnp.concatenate([q_ref[0, r, sl, :] for r in slabs], axis=0)
        k = gather(kc_ref, kp_ref, slabs, PIECE * jj, PIECE)
        v = gather(vc_ref, vp_ref, slabs, PIECE * jj, PIECE)
        update([(r, jj) for r in slabs], q, k, v, b1_ref[hp] if jj == 0 else b1_ref[1])

    for r in range(RES):
        acc = jnp.concatenate([state[(r, a)][2] for a in range(pieces)], axis=0)
        l = jnp.concatenate([state[(r, a)][1] for a in range(pieces)], axis=0)
        nat_ref[pl.ds(r, SLAB, stride=RES), :] = acc / l
    o_ref[...] = nat_ref[...].astype(o_ref.dtype)


def _prompt_attention(qkv16):
    b, _, rows, _ = qkv16.shape
    n_sb = rows // SLAB
    kcol, vcol = WIDTH_A // LANES, 2 * WIDTH_A // LANES
    blk = (1, RES, SLAB, LANES)
    cur = lambda off: pl.BlockSpec(blk, lambda bi, h, n: (bi, 0, n, off + h))
    prv = lambda off: pl.BlockSpec(blk, lambda bi, h, n: (bi, 0, jnp.maximum(n - 1, 0), off + h))
    const = lambda shp: pl.BlockSpec(shp, lambda bi, h, n: (0, 0, 0))
    return pl.pallas_call(
        _attn_kernel,
        grid=(b, N_HEADS_A, n_sb),
        in_specs=[cur(0), cur(kcol), prv(kcol), cur(vcol), prv(vcol),
                  const((2, 128, 256)), const((2, 128, 256)), const((2, 256, 512))],
        out_specs=pl.BlockSpec((RES * SLAB, LANES), lambda bi, h, n: (bi * n_sb + n, h)),
        out_shape=jax.ShapeDtypeStruct((b * rows * RES, WIDTH_A), jnp.bfloat16),
        scratch_shapes=[pltpu.VMEM((RES * SLAB, LANES), jnp.float32)],
        compiler_params=_cparams(("arbitrary", "arbitrary", "arbitrary")),
        name="prompt_attn",
    )(qkv16, qkv16, qkv16, qkv16, qkv16, _band_bias(16), _band_bias(4), _band_bias(1))


DN_STEP = 256
CARRY = 8


def _split_bf16(x, n):
    parts, r = [], x
    for _ in range(n):
        hi = r.astype(jnp.bfloat16)
        parts.append(hi)
        r = r - hi.astype(jnp.float32)
    return parts


def _bdot(a, b):
    return jnp.dot(a.astype(jnp.bfloat16), b.astype(jnp.bfloat16), preferred_element_type=jnp.float32)


def _softplus(x):
    return jnp.maximum(x, 0.0) + jnp.log1p(jnp.exp(-jnp.abs(x)))


def _silu(x):
    return x * jax.nn.sigmoid(x)


def _unit_lower_inverses(a_list):
    n = a_list[0].shape[0]
    row = lax.broadcasted_iota(jnp.int32, (n, n), 0)
    col = lax.broadcasted_iota(jnp.int32, (n, n), 1)
    eye = jnp.where(row == col, 1.0, 0.0)
    xs = [eye - a for a in a_list]
    ps = [_bdot(a, a) for a in a_list]
    k = 2
    while True:
        xs = [x + _bdot(x, p) for x, p in zip(xs, ps)]
        k *= 2
        if k >= n:
            return xs
        ps = [_bdot(p, p) for p in ps]


def _dn_kernel(qb_ref, kb_ref, vb_ref, tail_ref, tailt_ref, cw_ref, prow_ref, pcol_ref, onw_ref,
               o_ref, s_ref, xe_ref):
    c = pl.program_id(1)
    f32, bf16 = jnp.float32, jnp.bfloat16
    tb, n, hh = DN_STEP, CHUNK, N_HEADS_B
    n_ch = tb // n

    @pl.when(c == 0)
    def _():
        s_ref[...] = jnp.zeros_like(s_ref)
        xe_ref[0:CARRY, :] = jnp.zeros((CARRY, CONV_CH), f32)

    @pl.when(c > 0)
    def _():
        xe_ref[0:CARRY, :] = xe_ref[tb:tb + CARRY, :]

    conv = []
    for part, ref in enumerate((qb_ref, kb_ref, vb_ref)):
        cs = slice(part * WIDTH_B, (part + 1) * WIDTH_B)
        xe_ref[CARRY:CARRY + tb, cs] = ref[...].astype(f32)
        y = xe_ref[CARRY:CARRY + tb, cs] * cw_ref[CONV_WIDTH - 1:CONV_WIDTH, cs]
        for i in range(CONV_WIDTH - 1):
            off = CARRY - (CONV_WIDTH - 1) + i
            y = y + xe_ref[off:off + tb, cs] * cw_ref[i:i + 1, cs]
        conv.append(_silu(y))
    qc, kc, vc = conv

    t = tail_ref[...]
    beta_c = jax.nn.sigmoid(t)
    g_c = -jnp.exp(prow_ref[0:1]) * _softplus(t + prow_ref[1:2])
    tt = tailt_ref[...]
    g_r = -jnp.exp(pcol_ref[:, 0:1]) * _softplus(tt + pcol_ref[:, 1:2])

    row = lax.broadcasted_iota(jnp.int32, (n, n), 0)
    col = lax.broadcasted_iota(jnp.int32, (n, n), 1)
    incl = row >= col
    strict = row > col
    ltri = jnp.where(incl, 1.0, 0.0).astype(bf16)
    utri = jnp.where(row <= col, 1.0, 0.0).astype(bf16)
    dot = functools.partial(jnp.dot, preferred_element_type=f32)

    gcs, grs, eg_c, ed_c, glast = [], [], [], [], []
    for ci in range(n_ch):
        rs = slice(ci * n, (ci + 1) * n)
        gc = sum(dot(ltri, part) for part in _split_bf16(g_c[rs], 3))
        gr = sum(dot(part, utri) for part in _split_bf16(g_r[:, rs], 3))
        gl = gc[n - 1:n, :]
        gcs.append(gc)
        grs.append(gr)
        glast.append(gl)
        eg_c.append(jnp.exp(gc))
        ed_c.append(jnp.exp(gl - gc))

    ids = [(ci, h) for ci in range(n_ch) for h in range(hh)]

    def head_slice(x, ci, h):
        return x[ci * n:(ci + 1) * n, h * DK_B:(h + 1) * DK_B]

    def lane_col(x, lane):
        return x[:, lane:lane + 1]

    qs, ks, kbetas, vbetas, egs = [], [], [], [], []
    for ci, h in ids:
        q = head_slice(qc, ci, h)
        k = head_slice(kc, ci, h)
        v = head_slice(vc, ci, h)
        q = q * (lax.rsqrt(jnp.sum(q * q, axis=-1, keepdims=True) + EPS) * (DK_B ** -0.5))
        k = k * lax.rsqrt(jnp.sum(k * k, axis=-1, keepdims=True) + EPS)
        beta = lane_col(beta_c[ci * n:(ci + 1) * n], h)
        qs.append(q)
        ks.append(k)
        kbetas.append(k * beta)
        vbetas.append(v * beta)
        egs.append(lane_col(eg_c[ci], hh + h))

    kqs = [_nt_dot(jnp.concatenate([kb, q], axis=0).astype(bf16), k.astype(bf16))
           for kb, q, k in zip(kbetas, qs, ks)]
    a_mats, qks = [], []
    for (ci, h), kq in zip(ids, kqs):
        gcol = lane_col(gcs[ci], hh + h)
        grow = grs[ci][hh + h:hh + h + 1, :]
        decay = jnp.exp(jnp.where(incl, gcol - grow, NEG_BIG))
        a_mats.append(jnp.where(strict, kq[:n] * decay, 0.0))
        qks.append((kq[n:] * decay).astype(bf16))
    t_mats = _unit_lower_inverses(a_mats)
    uws = [dot(tm.astype(bf16), jnp.concatenate([vb, kb * eg], axis=1).astype(bf16))
           for tm, vb, kb, eg in zip(t_mats, vbetas, kbetas, egs)]

    states = [s_ref[0, h] for h in range(hh)]
    onw = onw_ref[...]
    for ci in range(n_ch):
        base = ci * hh
        wqs = [jnp.concatenate([uws[base + h][:, DV_B:], qs[base + h] * egs[base + h]], axis=0).astype(bf16)
               for h in range(hh)]
        wss = [dot(wq, st.astype(bf16)) for wq, st in zip(wqs, states)]
        v_news = [(uws[base + h][:, :DV_B] - wss[h][:n]).astype(bf16) for h in range(hh)]
        o_in = [dot(qks[base + h], v_news[h]) for h in range(hh)]
        k_decs = [(ks[base + h] * lane_col(ed_c[ci], hh + h)).astype(bf16) for h in range(hh)]
        upds = [lax.dot_general(k_decs[h], v_news[h], (((0,), (0,)), ((), ())), preferred_element_type=f32)
                for h in range(hh)]
        e_last = jnp.exp(glast[ci])
        states = [states[h] * lane_col(e_last, hh + h) + upds[h] for h in range(hh)]
        for h in range(hh):
            o = wss[h][n:] + o_in[h]
            o = o * lax.rsqrt(jnp.mean(o * o, axis=-1, keepdims=True) + EPS) * onw
            o_ref[ci * n:(ci + 1) * n, h * DV_B:(h + 1) * DV_B] = o.astype(o_ref.dtype)
    for h in range(hh):
        s_ref[0, h] = states[h]


def _deltanet_prompt(p, tail, tail_t, conv_w, a_log, dt_bias, onorm_w, batch):
    m = p.shape[0]
    steps = m // batch // DN_STEP
    wb = WIDTH_B // 1
    qcol = REST_QKVB // wb
    zeros8 = jnp.zeros((N_HEADS_B,), jnp.float32)
    gate = jnp.stack([jnp.concatenate([zeros8, a_log.astype(jnp.float32)]),
                      jnp.concatenate([zeros8, dt_bias.astype(jnp.float32)])])
    prow = jnp.pad(gate, ((0, 6), (0, LANES - TAIL_COLS)))
    pcol = jnp.pad(gate.T, ((0, 0), (0, LANES - 2)))
    tok = lambda cb: pl.BlockSpec((DN_STEP, wb), lambda b, c: (b * steps + c, cb))
    full = lambda shp: pl.BlockSpec(shp, lambda b, c: (0,) * len(shp))
    return pl.pallas_call(
        _dn_kernel,
        grid=(batch, steps),
        in_specs=[tok(qcol), tok(qcol + 1), tok(qcol + 2),
                  pl.BlockSpec((DN_STEP, LANES), lambda b, c: (b * steps + c, 0)),
                  pl.BlockSpec((TAIL_COLS, DN_STEP), lambda b, c: (0, b * steps + c)),
                  full((CONV_WIDTH, CONV_CH)), full((8, LANES)), full((TAIL_COLS, LANES)),
                  full((1, DV_B))],
        out_specs=[pl.BlockSpec((DN_STEP, WIDTH_B), lambda b, c: (b * steps + c, 0)),
                   pl.BlockSpec((1, N_HEADS_B, DK_B, DV_B), lambda b, c: (b, 0, 0, 0))],
        out_shape=[jax.ShapeDtypeStruct((m, WIDTH_B), jnp.bfloat16),
                   jax.ShapeDtypeStruct((batch, N_HEADS_B, DK_B, DV_B), jnp.float32)],
        scratch_shapes=[pltpu.VMEM((CARRY + DN_STEP, CONV_CH), jnp.float32)],
        compiler_params=_cparams(("arbitrary", "arbitrary")),
        name="deltanet_prompt",
    )(p, p, p, tail, tail_t, conv_w.astype(jnp.float32), prow, pcol,
      onorm_w.astype(jnp.float32)[None])


def _out_kernel(oa_ref, za_ref, ob_ref, zb_ref, w_ref, g_ref, x_ref, y_ref):
    f32 = jnp.float32
    ga = (oa_ref[...].astype(f32) * _silu(za_ref[...].astype(f32))).astype(jnp.bfloat16)
    gb = (ob_ref[...].astype(f32) * _silu(zb_ref[...].astype(f32))).astype(jnp.bfloat16)
    y = (jnp.dot(ga, w_ref[:WIDTH_A, :], preferred_element_type=f32)
         + jnp.dot(gb, w_ref[WIDTH_A:, :], preferred_element_type=f32))
    y = y * lax.rsqrt(jnp.mean(y * y, axis=-1, keepdims=True) + EPS) * g_ref[...]
    y_ref[...] = x_ref[...] + y


def _output_sublayer(o_a, p, o_b, w_out, g_post, x2d, *, tm):
    m = x2d.shape[0]
    row = lambda width, cb: pl.BlockSpec((tm, width), lambda i: (i, cb))
    return pl.pallas_call(
        _out_kernel,
        grid=(m // tm,),
        in_specs=[row(WIDTH_A, 0), row(WIDTH_A, REST_ZA // WIDTH_A), row(WIDTH_B, 0),
                  row(WIDTH_B, REST_ZB // WIDTH_B),
                  pl.BlockSpec((WIDTH_A + WIDTH_B, D_MODEL), lambda i: (0, 0)),
                  pl.BlockSpec((1, D_MODEL), lambda i: (0, 0)),
                  row(D_MODEL, 0)],
        out_specs=row(D_MODEL, 0),
        out_shape=jax.ShapeDtypeStruct((m, D_MODEL), jnp.float32),
        compiler_params=_cparams(("arbitrary",)),
        name="out_proj",
    )(o_a, p, o_b, p, w_out, g_post, x2d)


def _decode_attn_kernel(q_ref, kn_ref, vn_ref, k1_ref, v1_ref, k4_ref, v4_ref, k16_ref, v16_ref, o_ref):
    f32 = jnp.float32
    q = q_ref[0].astype(f32)
    kn, vn = kn_ref[0].astype(f32), vn_ref[0].astype(f32)
    s_new = jnp.sum(q * kn, axis=-1, keepdims=True)
    scores = [jnp.sum(k_ref[0] * q[None], axis=-1, keepdims=True) for k_ref in (k1_ref, k4_ref, k16_ref)]
    m = s_new
    for s in scores:
        m = jnp.maximum(m, jnp.max(s, axis=0))
    p_new = len(DILATIONS) * jnp.exp(s_new - m)
    den = p_new
    acc = p_new * vn
    for s, v_ref in zip(scores, (v1_ref, v4_ref, v16_ref)):
        p = jnp.exp(s - m[None])
        den = den + jnp.sum(p, axis=0)
        acc = acc + jnp.sum(p * v_ref[0], axis=0)
    o_ref[0] = (acc / den).astype(o_ref.dtype)


def _decode_attention(q, k_new, v_new, cache_k, cache_v):
    b, win, h, dd = cache_k.shape
    nb = 128
    views, specs = [], []
    for window, dil in DILATIONS:
        assert window // dil == nb and win % (nb * dil) == 0
        blk_idx = win // (nb * dil) - 1
        if dil == 1:
            spec = pl.BlockSpec((1, nb, h, dd), lambda i, bi=blk_idx: (i, bi, 0, 0))
            view = lambda c: c
        else:
            spec = pl.BlockSpec((1, nb, None, h, dd), lambda i, bi=blk_idx: (i, bi, 0, 0, 0))
            view = lambda c, dil=dil: c.reshape(b, win // dil, dil, h, dd)
        specs += [spec, spec]
        views += [view(cache_k), view(cache_v)]
    tok = pl.BlockSpec((1, h, dd), lambda i: (i, 0, 0))
    return pl.pallas_call(
        _decode_attn_kernel,
        grid=(b,),
        in_specs=[tok, tok, tok] + specs,
        out_specs=tok,
        out_shape=jax.ShapeDtypeStruct((b, h, dd), jnp.bfloat16),
        compiler_params=_cparams(("arbitrary",)),
        name="decode_attn",
    )(q, k_new, v_new, *views)


def _decode_dn_kernel(x_ref, cb_ref, cw_ref, gate_ref, prm_ref, onw_ref, s_ref,
                      o_ref, cbo_ref, so_ref):
    f32 = jnp.float32
    hh = N_HEADS_B
    xn = x_ref[0]
    cw = cw_ref[...]
    y = xn * cw[CONV_WIDTH - 1]
    for i in range(CONV_WIDTH - 1):
        y = y + cb_ref[0, i] * cw[i]
        cbo_ref[0, i] = cb_ref[0, i + 1] if i + 1 < CONV_WIDTH - 1 else xn
    y = _silu(y)
    q, k, v = y[:hh], y[hh:2 * hh], y[2 * hh:]
    q = q * lax.rsqrt(jnp.sum(q * q, axis=-1, keepdims=True) + EPS) * (DK_B ** -0.5)
    k = k * lax.rsqrt(jnp.sum(k * k, axis=-1, keepdims=True) + EPS)
    gate = gate_ref[0]
    beta = jax.nn.sigmoid(gate[:hh])
    decay = jnp.exp(-jnp.exp(prm_ref[:hh]) * _softplus(gate[hh:] + prm_ref[hh:]))
    qk = jnp.sum(q * k, axis=-1, keepdims=True)
    zeros = jnp.zeros((8 - 2, DK_B), jnp.bfloat16)
    outs = []
    for h in range(hh):
        state = s_ref[0, h]
        a = decay[h:h + 1]
        lhs = jnp.concatenate([k[h:h + 1].astype(jnp.bfloat16), q[h:h + 1].astype(jnp.bfloat16), zeros], axis=0)
        ks = jnp.dot(lhs, state.astype(jnp.bfloat16), preferred_element_type=f32)
        v_new = beta[h:h + 1] * (v[h:h + 1] - a * ks[0:1])
        upd = lax.dot_general(lhs[0:8], jnp.concatenate([v_new.astype(jnp.bfloat16), jnp.zeros((7, DV_B), jnp.bfloat16)], axis=0),
                              (((0,), (0,)), ((), ())), preferred_element_type=f32)
        so_ref[0, h] = state * a[:, 0:1] + upd
        outs.append(a * ks[1:2] + qk[h:h + 1] * v_new)
    o = jnp.concatenate(outs, axis=0)
    o = o * lax.rsqrt(jnp.mean(o * o, axis=-1, keepdims=True) + EPS) * onw_ref[...]
    o_ref[0] = o.astype(o_ref.dtype)


def _decode_deltanet(x_new, conv_buf, state, beta_in, a_in, conv_w, a_log, dt_bias, onorm_w):
    b = x_new.shape[0]
    f32 = jnp.float32
    g3 = CONV_CH // LANES
    hh = N_HEADS_B
    gate = jnp.broadcast_to(jnp.concatenate([beta_in, a_in], axis=1).astype(f32)[:, :, None], (b, 2 * hh, LANES))
    prm = jnp.broadcast_to(jnp.concatenate([a_log, dt_bias]).astype(f32)[:, None], (2 * hh, LANES))
    full = lambda shp: pl.BlockSpec(shp, lambda i: (0,) * len(shp))
    per = lambda shp: pl.BlockSpec((1,) + shp, lambda i: (i,) + (0,) * len(shp))
    o, cb_new, s_new = pl.pallas_call(
        _decode_dn_kernel,
        grid=(b,),
        in_specs=[per((g3, LANES)), per((CONV_WIDTH - 1, g3, LANES)), full((CONV_WIDTH, g3, LANES)),
                  per((2 * hh, LANES)), full((2 * hh, LANES)), full((1, DV_B)), per((hh, DK_B, DV_B))],
        out_specs=[per((hh, DV_B)), per((CONV_WIDTH - 1, g3, LANES)), per((hh, DK_B, DV_B))],
        out_shape=[jax.ShapeDtypeStruct((b, hh, DV_B), jnp.bfloat16),
                   jax.ShapeDtypeStruct((b, CONV_WIDTH - 1, g3, LANES), f32),
                   jax.ShapeDtypeStruct((b, hh, DK_B, DV_B), f32)],
        compiler_params=_cparams(("arbitrary",)),
        name="decode_deltanet",
    )(x_new.astype(f32).reshape(b, g3, LANES), conv_buf.astype(f32).reshape(b, CONV_WIDTH - 1, g3, LANES),
      conv_w.astype(f32).reshape(CONV_WIDTH, g3, LANES), gate, prm, onorm_w.astype(f32)[None], state.astype(f32))
    return o, cb_new.reshape(b, CONV_WIDTH - 1, CONV_CH), s_new


def kernel(x_prompt, x_sample, cache_win_k, cache_win_v, state_conv, state_delta,
           g_pre, w_in, conv_w, a_log, dt_bias, onorm_w, w_out, g_post):
    f32, bf16 = jnp.float32, jnp.bfloat16
    b, s, _ = x_prompt.shape
    db, t, _ = x_sample.shape
    depth = w_in.shape[0]
    n_past = cache_win_k.shape[2]
    assert t == 1 and n_past == MAX_WINDOW and s % (RES * SLAB) == 0
    keep = min(MAX_WINDOW, s)

    cos_p, sin_p = _rope_tables(jnp.arange(s, dtype=jnp.int32))
    cos_p, sin_p = jnp.tile(cos_p, (b, 1)), jnp.tile(sin_p, (b, 1))
    cos_s, sin_s = _rope_tables(jnp.full((db,), PAST_LEN, jnp.int32))

    yp = x_prompt.reshape(b * s, D_MODEL)
    ys = x_sample.reshape(db, D_MODEL)
    outs = [[] for _ in range(8)]
    for l in range(depth):
        w_main = _to_bf16(w_in[l], MAIN_COLS)
        w_tail_t = jnp.pad(w_in[l][:, MAIN_COLS:].T.astype(bf16), ((0, LANES - TAIL_COLS), (0, 0)))
        w_o = _to_bf16(w_out[l], D_MODEL)
        gp, go = g_pre[l].astype(f32)[None], g_post[l].astype(f32)[None]

        qkv16, kf, vf, rest, tail, tail_t = _project(yp, gp, w_main, w_tail_t, cos_p, sin_p,
                                                     seq=s, keep=keep, tm=512, residue_major=True)
        o_a = _prompt_attention(qkv16)
        o_b, s_fin = _deltanet_prompt(rest, tail, tail_t, conv_w[l], a_log[l], dt_bias[l], onorm_w[l], b)
        outs[0].append(kf.reshape(b, keep, N_HEADS_A, HEAD_DIM))
        outs[1].append(vf.reshape(b, keep, N_HEADS_A, HEAD_DIM))
        n_tail = min(CONV_WIDTH - 1, s)
        tail_rows = rest.reshape(b, s, REST_COLS)[:, s - n_tail:, REST_QKVB:REST_QKVB + CONV_CH].astype(f32)
        outs[2].append(jnp.pad(tail_rows, ((0, 0), (CONV_WIDTH - 1 - n_tail, 0), (0, 0))))
        outs[3].append(s_fin)
        yp = _output_sublayer(o_a, rest, o_b, w_o, go, yp, tm=256)

        qkv_s, kf_s, vf_s, rest_s, tail_s, _ = _project(ys, gp, w_main, w_tail_t, cos_s, sin_s,
                                                        seq=db, keep=db, tm=db, residue_major=False)
        k_new = kf_s.reshape(db, N_HEADS_A, HEAD_DIM)
        v_new = vf_s.reshape(db, N_HEADS_A, HEAD_DIM)
        o_as = _decode_attention(qkv_s[:, :WIDTH_A].reshape(db, N_HEADS_A, HEAD_DIM), k_new, v_new,
                                 cache_win_k[l], cache_win_v[l])
        o_bs, cb_new, st_new = _decode_deltanet(
            rest_s[:, REST_QKVB:REST_QKVB + CONV_CH], state_conv[l], state_delta[l],
            tail_s[:, :N_HEADS_B], tail_s[:, N_HEADS_B:TAIL_COLS], conv_w[l], a_log[l], dt_bias[l], onorm_w[l])
        outs[4].append(k_new.reshape(db, t, N_HEADS_A, HEAD_DIM))
        outs[5].append(v_new.reshape(db, t, N_HEADS_A, HEAD_DIM))
        outs[6].append(cb_new)
        outs[7].append(st_new)
        ys = _output_sublayer(o_as.reshape(db, WIDTH_A), rest_s, o_bs.reshape(db, WIDTH_B), w_o, go, ys, tm=db)

    stk = [jnp.stack(o) for o in outs]
    return (yp.reshape(b, s, D_MODEL), ys.reshape(db, t, D_MODEL),
            stk[0], stk[1], stk[2], stk[3], stk[4], stk[5], stk[6], stk[7])
```

```python
import functools
import math

import jax
import jax.numpy as jnp
from jax import lax
from jax.experimental import pallas as pl
from jax.experimental.pallas import tpu as pltpu

D_MODEL = 2048
HEAD_DIM = 128
N_HEADS_A = 8
N_HEADS_B = 8
DK_B = 128
DV_B = 128
WIDTH_A = N_HEADS_A * HEAD_DIM
WIDTH_B = N_HEADS_B * DV_B
DILATIONS = ((128, 1), (512, 4), (2048, 16))
MAX_WINDOW = 2048
ROPE_THETA = 500000.0
ROPE_DIM = HEAD_DIM // 4
CONV_WIDTH = 4
CONV_CH = 2 * N_HEADS_B * DK_B + N_HEADS_B * DV_B
CHUNK = 64
EPS = 1e-6
PAST_LEN = 16384
MAIN_COLS = 4 * WIDTH_A + CONV_CH + WIDTH_B
TAIL_COLS = 2 * N_HEADS_B
LANES = 128
NEG_BIG = -1e30
VMEM_LIMIT = 56 * 1024 * 1024

REST_COLS = MAIN_COLS - 3 * WIDTH_A
REST_ZA, REST_QKVB, REST_ZB = 0, WIDTH_A, WIDTH_A + CONV_CH


def _cparams(sem):
    return pltpu.CompilerParams(dimension_semantics=sem, vmem_limit_bytes=VMEM_LIMIT)


def _cast_kernel(x_ref, o_ref):
    o_ref[...] = x_ref[...].astype(o_ref.dtype)


def _to_bf16(w, layer, cols, tn=1024):
    rows = w.shape[1]
    return pl.pallas_call(
        _cast_kernel,
        grid=(cols // tn,),
        in_specs=[pl.BlockSpec((None, rows, tn), lambda j: (layer, 0, j))],
        out_specs=pl.BlockSpec((rows, tn), lambda j: (0, j)),
        out_shape=jax.ShapeDtypeStruct((rows, cols), jnp.bfloat16),
        compiler_params=_cparams(("arbitrary",)),
        name="cast_bf16",
    )(w)


RES = 16
QKV_TILES = 3
SUB_COLS = 256


def _proj_kernel(x_ref, g_ref, w_ref, wt_ref, cos_ref, sin_ref,
                 qkv_ref, win_ref, p_ref, tail_ref, tailt_ref, h_ref, de_ref,
                 *, tiles_per_seq, first_keep_tile, residue_major):
    i = pl.program_id(0)
    j = pl.program_id(1)
    tm = x_ref.shape[0]

    @pl.when(j == 0)
    def _():
        x = x_ref[...]
        y = x * lax.rsqrt(jnp.mean(x * x, axis=-1, keepdims=True) + EPS)
        h = (y * g_ref[...]).astype(jnp.bfloat16)
        h_ref[...] = h
        wt = wt_ref[...]
        tail_ref[...] = lax.dot_general(h, wt, (((1,), (1,)), ((), ())),
                                        preferred_element_type=jnp.float32)
        tailt_ref[...] = lax.dot_general(wt[:TAIL_COLS], h, (((1,), (1,)), ((), ())),
                                         preferred_element_type=jnp.float32)

    keep_rows = (i % tiles_per_seq) >= first_keep_tile
    heads_per_sub = SUB_COLS // LANES

    def sub_dot(sub):
        return jnp.dot(h_ref[...], w_ref[:, sub * SUB_COLS:(sub + 1) * SUB_COLS],
                       preferred_element_type=jnp.float32)

    def emit_qkv(hd, r, window):
        cs = slice(hd * LANES, (hd + 1) * LANES)
        if residue_major:
            slot = hd % de_ref.shape[0]
            de_ref[slot] = r
            for res in range(RES):
                qkv_ref[0, res, :, cs] = de_ref[slot, pl.ds(res, tm // RES, stride=RES), :].astype(qkv_ref.dtype)
        else:
            qkv_ref[:, cs] = r.astype(qkv_ref.dtype)
        if window:
            win_ref[0, pl.ds(hd, tm, stride=N_HEADS_A), :] = r

    def rotary_tile(scale, window):
        c = cos_ref[...]
        s = sin_ref[...]
        lane = lax.broadcasted_iota(jnp.int32, c.shape, 1)
        for sub in range(WIDTH_A // SUB_COLS):
            acc = sub_dot(sub)
            for hs in range(heads_per_sub):
                a = acc[:, hs * LANES:(hs + 1) * LANES]
                swapped = jnp.where(lane < ROPE_DIM // 2,
                                    pltpu.roll(a, LANES - ROPE_DIM // 2, 1),
                                    pltpu.roll(a, ROPE_DIM // 2, 1))
                r = a * c + swapped * s
                emit_qkv(sub * heads_per_sub + hs, r if scale is None else r * scale, window)

    def plain_tile(window):
        for sub in range(WIDTH_A // SUB_COLS):
            acc = sub_dot(sub)
            for hs in range(heads_per_sub):
                emit_qkv(sub * heads_per_sub + hs, acc[:, hs * LANES:(hs + 1) * LANES], window)

    pl.when(j == 0)(lambda: rotary_tile(HEAD_DIM ** -0.5, False))
    pl.when((j == 1) & keep_rows)(lambda: rotary_tile(None, True))
    pl.when((j == 1) & jnp.logical_not(keep_rows))(lambda: rotary_tile(None, False))
    pl.when((j == 2) & keep_rows)(lambda: plain_tile(True))
    pl.when((j == 2) & jnp.logical_not(keep_rows))(lambda: plain_tile(False))

    @pl.when(j >= QKV_TILES)
    def _():
        for sub in range(WIDTH_A // SUB_COLS):
            p_ref[:, sub * SUB_COLS:(sub + 1) * SUB_COLS] = sub_dot(sub).astype(p_ref.dtype)


def _rope_tables(pos):
    half = ROPE_DIM // 2
    inv = ROPE_THETA ** (-jnp.arange(half, dtype=jnp.float32) / half)
    ang = pos.astype(jnp.float32)[:, None] * inv[None, :]
    cos, sin = jnp.cos(ang), jnp.sin(ang)
    n = pos.shape[0]
    ones = jnp.ones((n, LANES - ROPE_DIM), jnp.float32)
    c = jnp.concatenate([cos, cos, ones], axis=1)
    s = jnp.concatenate([-sin, sin, jnp.zeros_like(ones)], axis=1)
    return c, s


def _project(x2d, g_pre, w_main, w_tail_t, cos_t, sin_t, *, seq, keep, tm, residue_major):
    m = x2d.shape[0]
    tn = WIDTH_A
    n_i, n_j = m // tm, MAIN_COLS // tn
    tiles_per_seq = seq // tm
    first_keep = (seq - keep) // tm
    keep_tiles = keep // tm
    assert seq % tm == 0 and keep % tm == 0 and (seq - keep) % tm == 0
    kern = functools.partial(_proj_kernel, tiles_per_seq=tiles_per_seq, first_keep_tile=first_keep,
                             residue_major=residue_major)
    qkv_col = lambda j: jnp.minimum(j, QKV_TILES - 1)
    if residue_major:
        assert tm % (RES * 16) == 0
        qkv_spec = pl.BlockSpec((1, RES, tm // RES, tn),
                                lambda i, j: (i // tiles_per_seq, 0, i % tiles_per_seq, qkv_col(j)))
        qkv_shape = jax.ShapeDtypeStruct((m // seq, RES, seq // RES, QKV_TILES * tn), jnp.bfloat16)
    else:
        qkv_spec = pl.BlockSpec((tm, tn), lambda i, j: (i, qkv_col(j)))
        qkv_shape = jax.ShapeDtypeStruct((m, QKV_TILES * tn), jnp.bfloat16)

    def win_index(i, j):
        il = i % tiles_per_seq
        row = (i // tiles_per_seq) * keep_tiles + jnp.maximum(il - first_keep, 0)
        return jnp.where(il >= first_keep, jnp.clip(j - 1, 0, 1), 0), row, 0

    win_spec = pl.BlockSpec((1, tm * N_HEADS_A, LANES), win_index)
    win_shape = jax.ShapeDtypeStruct((2, m // seq * keep * N_HEADS_A, LANES), jnp.float32)

    return pl.pallas_call(
        kern,
        grid=(n_i, n_j),
        in_specs=[
            pl.BlockSpec((tm, D_MODEL), lambda i, j: (i, 0)),
            pl.BlockSpec((1, D_MODEL), lambda i, j: (0, 0)),
            pl.BlockSpec((D_MODEL, tn), lambda i, j: (0, j)),
            pl.BlockSpec((LANES, D_MODEL), lambda i, j: (0, 0)),
            pl.BlockSpec((tm, LANES), lambda i, j: (i, 0)),
            pl.BlockSpec((tm, LANES), lambda i, j: (i, 0)),
        ],
        out_specs=[
            qkv_spec,
            win_spec,
            pl.BlockSpec((tm, tn), lambda i, j: (i, jnp.maximum(j - QKV_TILES, 0))),
            pl.BlockSpec((tm, LANES), lambda i, j: (i, 0)),
            pl.BlockSpec((TAIL_COLS, tm), lambda i, j: (0, i)),
        ],
        out_shape=[
            qkv_shape,
            win_shape,
            jax.ShapeDtypeStruct((m, REST_COLS), jnp.bfloat16),
            jax.ShapeDtypeStruct((m, LANES), jnp.float32),
            jax.ShapeDtypeStruct((TAIL_COLS, m), jnp.float32),
        ],
        scratch_shapes=[pltpu.VMEM((tm, D_MODEL), jnp.bfloat16),
                        pltpu.VMEM((2 * SUB_COLS // LANES, tm, LANES), jnp.float32)],
        compiler_params=_cparams(("arbitrary", "arbitrary")),
        name="proj",
    )(x2d, g_pre, w_main, w_tail_t, cos_t, sin_t)


SLAB = 128


def _band_bias(kind):
    import numpy as np
    if kind == 16:
        nq, nk = 128, 256
        lq = np.arange(nq)[:, None]
        kap = np.arange(nk)[None, :]
        lk = kap - 128
        prev = kap < 128
    elif kind == 4:
        nq, nk = 128, 256
        rho = np.arange(nq)[:, None]
        lq = 4 * (rho % 32) + rho // 32
        kap = np.arange(nk)[None, :]
        lk = 4 * (kap % 64 - 32) + kap // 64
        prev = (kap % 64) < 32
    else:
        nq, nk = 256, 512
        rho = np.arange(nq)[:, None]
        lq = 16 * (rho % 16) + rho // 16
        kap = np.arange(nk)[None, :]
        lk = 16 * (kap % 32 - 16) + kap // 32
        prev = (kap % 32) < 16
    dist = lq - lk
    band = (dist >= 0) & (dist <= 128)
    out = np.stack([band & ~prev, band])
    return jnp.asarray(np.where(out, 0.0, NEG_BIG), dtype=jnp.float32)


def _nt_dot(a, b):
    return lax.dot_general(a, b, (((1,), (1,)), ((), ())), preferred_element_type=jnp.float32)


PIECE = 16


def _attn_kernel(q_ref, kc_ref, kp_ref, vc_ref, vp_ref, b16_ref, b4_ref, b1_ref, o_ref, nat_ref):
    hp = jnp.minimum(pl.program_id(2), 1)
    f32 = jnp.float32

    def block(q, k, v, bias, prev):
        s = _nt_dot(q, k) + bias
        rows, keys = s.shape
        m_new = jnp.broadcast_to(jnp.max(s, axis=1, keepdims=True), (rows, LANES))
        if prev is not None:
            m_prev, l_prev, acc_prev = prev
            m_new = jnp.maximum(m_prev, m_new)
        p = jnp.exp(s - jnp.concatenate([m_new] * (keys // LANES), axis=1))
        v1 = jnp.concatenate([v, jnp.ones_like(v)], axis=1)
        pv = jnp.dot(p.astype(v.dtype), v1, preferred_element_type=f32)
        acc_new, l_new = pv[:, :LANES], pv[:, LANES:]
        if prev is not None:
            alpha = jnp.exp(m_prev - m_new)
            l_new = alpha * l_prev + l_new
            acc_new = alpha * acc_prev + acc_new
        return m_new, l_new, acc_new

    def split(x, n_parts):
        n = x.shape[0] // n_parts
        return [x[i * n:(i + 1) * n] for i in range(n_parts)]

    state = {}
    pieces = SLAB // PIECE

    for r in range(RES):
        k = jnp.concatenate([kp_ref[0, r], kc_ref[0, r]], axis=0)
        v = jnp.concatenate([vp_ref[0, r], vc_ref[0, r]], axis=0)
        res = [split(x, pieces) for x in block(q_ref[0, r], k, v, b16_ref[hp], None)]
        for a in range(pieces):
            state[(r, a)] = tuple(x[a] for x in res)

    def update(keys_, q, k, v, bias):
        prev = tuple(jnp.concatenate([state[key][i] for key in keys_], axis=0) for i in range(3))
        res = [split(x, len(keys_)) for x in block(q, k, v, bias, prev)]
        for i, key in enumerate(keys_):
            state[key] = tuple(x[i] for x in res)

    def gather(cur_ref, prev_ref, slabs, start, n):
        if start == 0:
            parts = [jnp.concatenate([prev_ref[0, r, SLAB - n:SLAB, :], cur_ref[0, r, 0:n, :]], axis=0)
                     for r in slabs]
        else:
            parts = [cur_ref[0, r, start - n:start + n, :] for r in slabs]
        return jnp.concatenate(parts, axis=0)

    for r4 in range(4):
        slabs = [r4 + 4 * i for i in range(4)]
        for jj in range(SLAB // 32):
            sl = slice(32 * jj, 32 * jj + 32)
            q = jnp.concatenate([q_ref[0, r, sl, :] for r in slabs], axis=0)
            k = gather(kc_ref, kp_ref, slabs, 32 * jj, 32)
            v = gather(vc_ref, vp_ref, slabs, 32 * jj, 32)
            keys_ = [(r, 2 * jj + a) for r in slabs for a in range(2)]
            update(keys_, q, k, v, b4_ref[hp] if jj == 0 else b4_ref[1])

    slabs = list(range(RES))
    for jj in range(pieces):
        sl = slice(PIECE * jj, PIECE * jj + PIECE)
        q = jnp.concatenate([q_ref[0, r, sl, :] for r in slabs], axis=0)
        k = gather(kc_ref, kp_ref, slabs, PIECE * jj, PIECE)
        v = gather(vc_ref, vp_ref, slabs, PIECE * jj, PIECE)
        update([(r, jj) for r in slabs], q, k, v, b1_ref[hp] if jj == 0 else b1_ref[1])

    for r in range(RES):
        acc = jnp.concatenate([state[(r, a)][2] for a in range(pieces)], axis=0)
        l = jnp.concatenate([state[(r, a)][1] for a in range(pieces)], axis=0)
        nat_ref[pl.ds(r, SLAB, stride=RES), :] = acc / l
    o_ref[...] = nat_ref[...].astype(o_ref.dtype)


def _prompt_attention(qkv16):
    b, _, rows, _ = qkv16.shape
    n_sb = rows // SLAB
    kcol, vcol = WIDTH_A // LANES, 2 * WIDTH_A // LANES
    blk = (1, RES, SLAB, LANES)
    cur = lambda off: pl.BlockSpec(blk, lambda bi, h, n: (bi, 0, n, off + h))
    prv = lambda off: pl.BlockSpec(blk, lambda bi, h, n: (bi, 0, jnp.maximum(n - 1, 0), off + h))
    const = lambda shp: pl.BlockSpec(shp, lambda bi, h, n: (0, 0, 0))
    return pl.pallas_call(
        _attn_kernel,
        grid=(b, N_HEADS_A, n_sb),
        in_specs=[cur(0), cur(kcol), prv(kcol), cur(vcol), prv(vcol),
                  const((2, 128, 256)), const((2, 128, 256)), const((2, 256, 512))],
        out_specs=pl.BlockSpec((RES * SLAB, LANES), lambda bi, h, n: (bi * n_sb + n, h)),
        out_shape=jax.ShapeDtypeStruct((b * rows * RES, WIDTH_A), jnp.bfloat16),
        scratch_shapes=[pltpu.VMEM((RES * SLAB, LANES), jnp.float32)],
        compiler_params=_cparams(("arbitrary", "arbitrary", "arbitrary")),
        name="prompt_attn",
    )(qkv16, qkv16, qkv16, qkv16, qkv16, _band_bias(16), _band_bias(4), _band_bias(1))


DN_STEP = 256
CARRY = 8


def _split_bf16(x, n):
    parts, r = [], x
    for _ in range(n):
        hi = r.astype(jnp.bfloat16)
        parts.append(hi)
        r = r - hi.astype(jnp.float32)
    return parts


def _bdot(a, b):
    return jnp.dot(a.astype(jnp.bfloat16), b.astype(jnp.bfloat16), preferred_element_type=jnp.float32)


def _softplus(x):
    return jnp.maximum(x, 0.0) + jnp.log1p(jnp.exp(-jnp.abs(x)))


def _silu(x):
    return x * jax.nn.sigmoid(x)


def _unit_lower_inverses(a_list):
    n = a_list[0].shape[0]
    row = lax.broadcasted_iota(jnp.int32, (n, n), 0)
    col = lax.broadcasted_iota(jnp.int32, (n, n), 1)
    eye = jnp.where(row == col, 1.0, 0.0)
    xs = [eye - a for a in a_list]
    ps = [_bdot(a, a) for a in a_list]
    k = 2
    while True:
        xs = [x + _bdot(x, p) for x, p in zip(xs, ps)]
        k *= 2
        if k >= n:
            return xs
        ps = [_bdot(p, p) for p in ps]


def _dn_kernel(qb_ref, kb_ref, vb_ref, tail_ref, tailt_ref, cw_ref, prow_ref, pcol_ref, onw_ref,
               o_ref, s_ref, xe_ref):
    c = pl.program_id(1)
    f32, bf16 = jnp.float32, jnp.bfloat16
    tb, n, hh = DN_STEP, CHUNK, N_HEADS_B
    n_ch = tb // n

    @pl.when(c == 0)
    def _():
        s_ref[...] = jnp.zeros_like(s_ref)
        xe_ref[0:CARRY, :] = jnp.zeros((CARRY, CONV_CH), f32)

    @pl.when(c > 0)
    def _():
        xe_ref[0:CARRY, :] = xe_ref[tb:tb + CARRY, :]

    conv = []
    for part, ref in enumerate((qb_ref, kb_ref, vb_ref)):
        cs = slice(part * WIDTH_B, (part + 1) * WIDTH_B)
        xe_ref[CARRY:CARRY + tb, cs] = ref[...].astype(f32)
        y = xe_ref[CARRY:CARRY + tb, cs] * cw_ref[CONV_WIDTH - 1:CONV_WIDTH, cs]
        for i in range(CONV_WIDTH - 1):
            off = CARRY - (CONV_WIDTH - 1) + i
            y = y + xe_ref[off:off + tb, cs] * cw_ref[i:i + 1, cs]
        conv.append(_silu(y))
    qc, kc, vc = conv

    t = tail_ref[...]
    beta_c = jax.nn.sigmoid(t)
    g_c = -jnp.exp(prow_ref[0:1]) * _softplus(t + prow_ref[1:2])
    tt = tailt_ref[...]
    g_r = -jnp.exp(pcol_ref[:, 0:1]) * _softplus(tt + pcol_ref[:, 1:2])

    row = lax.broadcasted_iota(jnp.int32, (n, n), 0)
    col = lax.broadcasted_iota(jnp.int32, (n, n), 1)
    incl = row >= col
    strict = row > col
    ltri = jnp.where(incl, 1.0, 0.0).astype(bf16)
    utri = jnp.where(row <= col, 1.0, 0.0).astype(bf16)
    dot = functools.partial(jnp.dot, preferred_element_type=f32)

    gcs, grs, eg_c, ed_c, glast = [], [], [], [], []
    for ci in range(n_ch):
        rs = slice(ci * n, (ci + 1) * n)
        gc = sum(dot(ltri, part) for part in _split_bf16(g_c[rs], 3))
        gr = sum(dot(part, utri) for part in _split_bf16(g_r[:, rs], 3))
        gl = gc[n - 1:n, :]
        gcs.append(gc)
        grs.append(gr)
        glast.append(gl)
        eg_c.append(jnp.exp(gc))
        ed_c.append(jnp.exp(gl - gc))

    ids = [(ci, h) for ci in range(n_ch) for h in range(hh)]

    def head_slice(x, ci, h):
        return x[ci * n:(ci + 1) * n, h * DK_B:(h + 1) * DK_B]

    def lane_col(x, lane):
        return x[:, lane:lane + 1]

    qs, ks, kbetas, vbetas, egs = [], [], [], [], []
    for ci, h in ids:
        q = head_slice(qc, ci, h)
        k = head_slice(kc, ci, h)
        v = head_slice(vc, ci, h)
        q = q * (lax.rsqrt(jnp.sum(q * q, axis=-1, keepdims=True) + EPS) * (DK_B ** -0.5))
        k = k * lax.rsqrt(jnp.sum(k * k, axis=-1, keepdims=True) + EPS)
        beta = lane_col(beta_c[ci * n:(ci + 1) * n], h)
        qs.append(q)
        ks.append(k)
        kbetas.append(k * beta)
        vbetas.append(v * beta)
        egs.append(lane_col(eg_c[ci], hh + h))

    kqs = [_nt_dot(jnp.concatenate([kb, q], axis=0).astype(bf16), k.astype(bf16))
           for kb, q, k in zip(kbetas, qs, ks)]
    a_mats, qks = [], []
    for (ci, h), kq in zip(ids, kqs):
        gcol = lane_col(gcs[ci], hh + h)
        grow = grs[ci][hh + h:hh + h + 1, :]
        decay = jnp.exp(jnp.where(incl, gcol - grow, NEG_BIG))
        a_mats.append(jnp.where(strict, kq[:n] * decay, 0.0))
        qks.append((kq[n:] * decay).astype(bf16))
    t_mats = _unit_lower_inverses(a_mats)
    uws = [dot(tm.astype(bf16), jnp.concatenate([vb, kb * eg], axis=1).astype(bf16))
           for tm, vb, kb, eg in zip(t_mats, vbetas, kbetas, egs)]

    states = [s_ref[0, h] for h in range(hh)]
    onw = onw_ref[...]
    for ci in range(n_ch):
        base = ci * hh
        wqs = [jnp.concatenate([uws[base + h][:, DV_B:], qs[base + h] * egs[base + h]], axis=0).astype(bf16)
               for h in range(hh)]
        wss = [dot(wq, st.astype(bf16)) for wq, st in zip(wqs, states)]
        v_news = [(uws[base + h][:, :DV_B] - wss[h][:n]).astype(bf16) for h in range(hh)]
        o_in = [dot(qks[base + h], v_news[h]) for h in range(hh)]
        k_decs = [(ks[base + h] * lane_col(ed_c[ci], hh + h)).astype(bf16) for h in range(hh)]
        upds = [lax.dot_general(k_decs[h], v_news[h], (((0,), (0,)), ((), ())), preferred_element_type=f32)
                for h in range(hh)]
        e_last = jnp.exp(glast[ci])
        states = [states[h] * lane_col(e_last, hh + h) + upds[h] for h in range(hh)]
        for h in range(hh):
            o = wss[h][n:] + o_in[h]
            o = o * lax.rsqrt(jnp.mean(o * o, axis=-1, keepdims=True) + EPS) * onw
            o_ref[ci * n:(ci + 1) * n, h * DV_B:(h + 1) * DV_B] = o.astype(o_ref.dtype)
    for h in range(hh):
        s_ref[0, h] = states[h]


def _deltanet_prompt(p, tail, tail_t, conv_w, a_log, dt_bias, onorm_w, batch):
    m = p.shape[0]
    steps = m // batch // DN_STEP
    wb = WIDTH_B // 1
    qcol = REST_QKVB // wb
    zeros8 = jnp.zeros((N_HEADS_B,), jnp.float32)
    gate = jnp.stack([jnp.concatenate([zeros8, a_log.astype(jnp.float32)]),
                      jnp.concatenate([zeros8, dt_bias.astype(jnp.float32)])])
    prow = jnp.pad(gate, ((0, 6), (0, LANES - TAIL_COLS)))
    pcol = jnp.pad(gate.T, ((0, 0), (0, LANES - 2)))
    tok = lambda cb: pl.BlockSpec((DN_STEP, wb), lambda b, c: (b * steps + c, cb))
    full = lambda shp: pl.BlockSpec(shp, lambda b, c: (0,) * len(shp))
    return pl.pallas_call(
        _dn_kernel,
        grid=(batch, steps),
        in_specs=[tok(qcol), tok(qcol + 1), tok(qcol + 2),
                  pl.BlockSpec((DN_STEP, LANES), lambda b, c: (b * steps + c, 0)),
                  pl.BlockSpec((TAIL_COLS, DN_STEP), lambda b, c: (0, b * steps + c)),
                  full((CONV_WIDTH, CONV_CH)), full((8, LANES)), full((TAIL_COLS, LANES)),
                  full((1, DV_B))],
        out_specs=[pl.BlockSpec((DN_STEP, WIDTH_B), lambda b, c: (b * steps + c, 0)),
                   pl.BlockSpec((1, N_HEADS_B, DK_B, DV_B), lambda b, c: (b, 0, 0, 0))],
        out_shape=[jax.ShapeDtypeStruct((m, WIDTH_B), jnp.bfloat16),
                   jax.ShapeDtypeStruct((batch, N_HEADS_B, DK_B, DV_B), jnp.float32)],
        scratch_shapes=[pltpu.VMEM((CARRY + DN_STEP, CONV_CH), jnp.float32)],
        compiler_params=_cparams(("arbitrary", "arbitrary")),
        name="deltanet_prompt",
    )(p, p, p, tail, tail_t, conv_w.astype(jnp.float32), prow, pcol,
      onorm_w.astype(jnp.float32)[None])


def _out_kernel(oa_ref, za_ref, ob_ref, zb_ref, w_ref, g_ref, x_ref, y_ref):
    f32 = jnp.float32
    ga = (oa_ref[...].astype(f32) * _silu(za_ref[...].astype(f32))).astype(jnp.bfloat16)
    gb = (ob_ref[...].astype(f32) * _silu(zb_ref[...].astype(f32))).astype(jnp.bfloat16)
    y = (jnp.dot(ga, w_ref[:WIDTH_A, :], preferred_element_type=f32)
         + jnp.dot(gb, w_ref[WIDTH_A:, :], preferred_element_type=f32))
    y = y * lax.rsqrt(jnp.mean(y * y, axis=-1, keepdims=True) + EPS) * g_ref[...]
    y_ref[...] = x_ref[...] + y


def _output_sublayer(o_a, p, o_b, w_out, g_post, x2d, *, tm):
    m = x2d.shape[0]
    row = lambda width, cb: pl.BlockSpec((tm, width), lambda i: (i, cb))
    return pl.pallas_call(
        _out_kernel,
        grid=(m // tm,),
        in_specs=[row(WIDTH_A, 0), row(WIDTH_A, REST_ZA // WIDTH_A), row(WIDTH_B, 0),
                  row(WIDTH_B, REST_ZB // WIDTH_B),
                  pl.BlockSpec((WIDTH_A + WIDTH_B, D_MODEL), lambda i: (0, 0)),
                  pl.BlockSpec((1, D_MODEL), lambda i: (0, 0)),
                  row(D_MODEL, 0)],
        out_specs=row(D_MODEL, 0),
        out_shape=jax.ShapeDtypeStruct((m, D_MODEL), jnp.float32),
        compiler_params=_cparams(("arbitrary",)),
        name="out_proj",
    )(o_a, p, o_b, p, w_out, g_post, x2d)


def _decode_attn_kernel(q_ref, kn_ref, vn_ref, k1_ref, v1_ref, k4_ref, v4_ref, k16_ref, v16_ref, o_ref):
    f32 = jnp.float32
    q = q_ref[0].astype(f32)
    kn, vn = kn_ref[0].astype(f32), vn_ref[0].astype(f32)
    s_new = jnp.sum(q * kn, axis=-1, keepdims=True)
    scores = [jnp.sum(k_ref[0] * q[None], axis=-1, keepdims=True) for k_ref in (k1_ref, k4_ref, k16_ref)]
    m = s_new
    for s in scores:
        m = jnp.maximum(m, jnp.max(s, axis=0))
    p_new = len(DILATIONS) * jnp.exp(s_new - m)
    den = p_new
    acc = p_new * vn
    for s, v_ref in zip(scores, (v1_ref, v4_ref, v16_ref)):
        p = jnp.exp(s - m[None])
        den = den + jnp.sum(p, axis=0)
        acc = acc + jnp.sum(p * v_ref[0], axis=0)
    o_ref[0] = (acc / den).astype(o_ref.dtype)


def _decode_attention(q, k_new, v_new, cache_k, cache_v):
    b, win, h, dd = cache_k.shape
    nb = 128
    views, specs = [], []
    for window, dil in DILATIONS:
        assert window // dil == nb and win % (nb * dil) == 0
        blk_idx = win // (nb * dil) - 1
        if dil == 1:
            spec = pl.BlockSpec((1, nb, h, dd), lambda i, bi=blk_idx: (i, bi, 0, 0))
            view = lambda c: c
        else:
            spec = pl.BlockSpec((1, nb, None, h, dd), lambda i, bi=blk_idx: (i, bi, 0, 0, 0))
            view = lambda c, dil=dil: c.reshape(b, win // dil, dil, h, dd)
        specs += [spec, spec]
        views += [view(cache_k), view(cache_v)]
    tok = pl.BlockSpec((1, h, dd), lambda i: (i, 0, 0))
    return pl.pallas_call(
        _decode_attn_kernel,
        grid=(b,),
        in_specs=[tok, tok, tok] + specs,
        out_specs=tok,
        out_shape=jax.ShapeDtypeStruct((b, h, dd), jnp.bfloat16),
        compiler_params=_cparams(("arbitrary",)),
        name="decode_attn",
    )(q, k_new, v_new, *views)


def _decode_dn_kernel(x_ref, cb_ref, cw_ref, gate_ref, prm_ref, onw_ref, s_ref,
                      o_ref, cbo_ref, so_ref):
    f32 = jnp.float32
    hh = N_HEADS_B
    xn = x_ref[0]
    cw = cw_ref[...]
    y = xn * cw[CONV_WIDTH - 1]
    for i in range(CONV_WIDTH - 1):
        y = y + cb_ref[0, i] * cw[i]
        cbo_ref[0, i] = cb_ref[0, i + 1] if i + 1 < CONV_WIDTH - 1 else xn
    y = _silu(y)
    q, k, v = y[:hh], y[hh:2 * hh], y[2 * hh:]
    q = q * lax.rsqrt(jnp.sum(q * q, axis=-1, keepdims=True) + EPS) * (DK_B ** -0.5)
    k = k * lax.rsqrt(jnp.sum(k * k, axis=-1, keepdims=True) + EPS)
    gate = gate_ref[0]
    beta = jax.nn.sigmoid(gate[:hh])
    decay = jnp.exp(-jnp.exp(prm_ref[:hh]) * _softplus(gate[hh:] + prm_ref[hh:]))
    qk = jnp.sum(q * k, axis=-1, keepdims=True)
    zeros = jnp.zeros((8 - 2, DK_B), jnp.bfloat16)
    outs = []
    for h in range(hh):
        state = s_ref[0, h]
        a = decay[h:h + 1]
        lhs = jnp.concatenate([k[h:h + 1].astype(jnp.bfloat16), q[h:h + 1].astype(jnp.bfloat16), zeros], axis=0)
        ks = jnp.dot(lhs, state.astype(jnp.bfloat16), preferred_element_type=f32)
        v_new = beta[h:h + 1] * (v[h:h + 1] - a * ks[0:1])
        upd = lax.dot_general(lhs[0:8], jnp.concatenate([v_new.astype(jnp.bfloat16), jnp.zeros((7, DV_B), jnp.bfloat16)], axis=0),
                              (((0,), (0,)), ((), ())), preferred_element_type=f32)
        so_ref[0, h] = state * a[:, 0:1] + upd
        outs.append(a * ks[1:2] + qk[h:h + 1] * v_new)
    o = jnp.concatenate(outs, axis=0)
    o = o * lax.rsqrt(jnp.mean(o * o, axis=-1, keepdims=True) + EPS) * onw_ref[...]
    o_ref[0] = o.astype(o_ref.dtype)


def _decode_deltanet(x_new, conv_buf, state, beta_in, a_in, conv_w, a_log, dt_bias, onorm_w):
    b = x_new.shape[0]
    f32 = jnp.float32
    g3 = CONV_CH // LANES
    hh = N_HEADS_B
    gate = jnp.broadcast_to(jnp.concatenate([beta_in, a_in], axis=1).astype(f32)[:, :, None], (b, 2 * hh, LANES))
    prm = jnp.broadcast_to(jnp.concatenate([a_log, dt_bias]).astype(f32)[:, None], (2 * hh, LANES))
    full = lambda shp: pl.BlockSpec(shp, lambda i: (0,) * len(shp))
    per = lambda shp: pl.BlockSpec((1,) + shp, lambda i: (i,) + (0,) * len(shp))
    o, cb_new, s_new = pl.pallas_call(
        _decode_dn_kernel,
        grid=(b,),
        in_specs=[per((g3, LANES)), per((CONV_WIDTH - 1, g3, LANES)), full((CONV_WIDTH, g3, LANES)),
                  per((2 * hh, LANES)), full((2 * hh, LANES)), full((1, DV_B)), per((hh, DK_B, DV_B))],
        out_specs=[per((hh, DV_B)), per((CONV_WIDTH - 1, g3, LANES)), per((hh, DK_B, DV_B))],
        out_shape=[jax.ShapeDtypeStruct((b, hh, DV_B), jnp.bfloat16),
                   jax.ShapeDtypeStruct((b, CONV_WIDTH - 1, g3, LANES), f32),
                   jax.ShapeDtypeStruct((b, hh, DK_B, DV_B), f32)],
        compiler_params=_cparams(("arbitrary",)),
        name="decode_deltanet",
    )(x_new.astype(f32).reshape(b, g3, LANES), conv_buf.astype(f32).reshape(b, CONV_WIDTH - 1, g3, LANES),
      conv_w.astype(f32).reshape(CONV_WIDTH, g3, LANES), gate, prm, onorm_w.astype(f32)[None], state.astype(f32))
    return o, cb_new.reshape(b, CONV_WIDTH - 1, CONV_CH), s_new


def kernel(x_prompt, x_sample, cache_win_k, cache_win_v, state_conv, state_delta,
           g_pre, w_in, conv_w, a_log, dt_bias, onorm_w, w_out, g_post):
    f32, bf16 = jnp.float32, jnp.bfloat16
    b, s, _ = x_prompt.shape
    db, t, _ = x_sample.shape
    depth = w_in.shape[0]
    n_past = cache_win_k.shape[2]
    assert t == 1 and n_past == MAX_WINDOW and s % (RES * SLAB) == 0
    keep = min(MAX_WINDOW, s)

    cos_p, sin_p = _rope_tables(jnp.arange(s, dtype=jnp.int32))
    cos_p, sin_p = jnp.tile(cos_p, (b, 1)), jnp.tile(sin_p, (b, 1))
    cos_s, sin_s = _rope_tables(jnp.full((db,), PAST_LEN, jnp.int32))

    yp = x_prompt.reshape(b * s, D_MODEL)
    ys = x_sample.reshape(db, D_MODEL)
    outs = [[] for _ in range(8)]
    for l in range(depth):
        w_main = _to_bf16(w_in, l, MAIN_COLS)
        w_tail_t = jnp.pad(w_in[l][:, MAIN_COLS:].T.astype(bf16), ((0, LANES - TAIL_COLS), (0, 0)))
        w_o = _to_bf16(w_out, l, D_MODEL)
        gp, go = g_pre[l].astype(f32)[None], g_post[l].astype(f32)[None]

        qkv16, win, rest, tail, tail_t = _project(yp, gp, w_main, w_tail_t, cos_p, sin_p,
                                                  seq=s, keep=keep, tm=1024, residue_major=True)
        o_a = _prompt_attention(qkv16)
        o_b, s_fin = _deltanet_prompt(rest, tail, tail_t, conv_w[l], a_log[l], dt_bias[l], onorm_w[l], b)
        outs[0].append(win[0].reshape(b, keep, N_HEADS_A, HEAD_DIM))
        outs[1].append(win[1].reshape(b, keep, N_HEADS_A, HEAD_DIM))
        n_tail = min(CONV_WIDTH - 1, s)
        tail_rows = rest.reshape(b, s, REST_COLS)[:, s - n_tail:, REST_QKVB:REST_QKVB + CONV_CH].astype(f32)
        outs[2].append(jnp.pad(tail_rows, ((0, 0), (CONV_WIDTH - 1 - n_tail, 0), (0, 0))))
        outs[3].append(s_fin)
        yp = _output_sublayer(o_a, rest, o_b, w_o, go, yp, tm=256)

        qkv_s, win_s, rest_s, tail_s, _ = _project(ys, gp, w_main, w_tail_t, cos_s, sin_s,
                                                   seq=db, keep=db, tm=db, residue_major=False)
        k_new = win_s[0].reshape(db, N_HEADS_A, HEAD_DIM)
        v_new = win_s[1].reshape(db, N_HEADS_A, HEAD_DIM)
        o_as = _decode_attention(qkv_s[:, :WIDTH_A].reshape(db, N_HEADS_A, HEAD_DIM), k_new, v_new,
                                 cache_win_k[l], cache_win_v[l])
        o_bs, cb_new, st_new = _decode_deltanet(
            rest_s[:, REST_QKVB:REST_QKVB + CONV_CH], state_conv[l], state_delta[l],
            tail_s[:, :N_HEADS_B], tail_s[:, N_HEADS_B:TAIL_COLS], conv_w[l], a_log[l], dt_bias[l], onorm_w[l])
        outs[4].append(k_new.reshape(db, t, N_HEADS_A, HEAD_DIM))
        outs[5].append(v_new.reshape(db, t, N_HEADS_A, HEAD_DIM))
        outs[6].append(cb_new)
        outs[7].append(st_new)
        ys = _output_sublayer(o_as.reshape(db, WIDTH_A), rest_s, o_bs.reshape(db, WIDTH_B), w_o, go, ys, tm=db)

    stk = [jnp.stack(o) for o in outs]
    return (yp.reshape(b, s, D_MODEL), ys.reshape(db, t, D_MODEL),
            stk[0], stk[1], stk[2], stk[3], stk[4], stk[5], stk[6], stk[7])
```

```python
import functools
import math

import jax
import jax.numpy as jnp
from jax import lax
from jax.experimental import pallas as pl
from jax.experimental.pallas import tpu as pltpu

D_MODEL = 2048
HEAD_DIM = 128
N_HEADS_A = 8
N_HEADS_B = 8
DK_B = 128
DV_B = 128
WIDTH_A = N_HEADS_A * HEAD_DIM
WIDTH_B = N_HEADS_B * DV_B
DILATIONS = ((128, 1), (512, 4), (2048, 16))
MAX_WINDOW = 2048
ROPE_THETA = 500000.0
ROPE_DIM = HEAD_DIM // 4
CONV_WIDTH = 4
CONV_CH = 2 * N_HEADS_B * DK_B + N_HEADS_B * DV_B
CHUNK = 64
EPS = 1e-6
PAST_LEN = 16384
MAIN_COLS = 4 * WIDTH_A + CONV_CH + WIDTH_B
TAIL_COLS = 2 * N_HEADS_B
LANES = 128
NEG_BIG = -1e30
VMEM_LIMIT = 56 * 1024 * 1024

REST_COLS = MAIN_COLS - 3 * WIDTH_A
REST_ZA, REST_QKVB, REST_ZB = 0, WIDTH_A, WIDTH_A + CONV_CH


def _cparams(sem):
    return pltpu.CompilerParams(dimension_semantics=sem, vmem_limit_bytes=VMEM_LIMIT)


def _nt_dot(a, b):
    return lax.dot_general(a, b, (((1,), (1,)), ((), ())), preferred_element_type=jnp.float32)


def _cast_kernel(x_ref, o_ref):
    o_ref[...] = x_ref[...].astype(o_ref.dtype)


def _to_bf16(w, layer, rows, tr=1024):
    cols = w.shape[2]
    return pl.pallas_call(
        _cast_kernel,
        grid=(rows // tr,),
        in_specs=[pl.BlockSpec((None, tr, cols), lambda j: (layer, j, 0))],
        out_specs=pl.BlockSpec((tr, cols), lambda j: (j, 0)),
        out_shape=jax.ShapeDtypeStruct((rows, cols), jnp.bfloat16),
        compiler_params=_cparams(("arbitrary",)),
        name="cast_bf16",
    )(w)


RES = 16
QKV_TILES = 3
SUB_COLS = 256


def _proj_kernel(x_ref, g_ref, w_ref, wt_ref, cos_ref, sin_ref,
                 qkv_ref, kf_ref, vf_ref, p_ref, tail_ref, tailt_ref, h_ref, de_ref,
                 *, tiles_per_seq, first_keep_tile, residue_major):
    i = pl.program_id(0)
    j = pl.program_id(1)
    tm = x_ref.shape[0]

    @pl.when(j == 0)
    def _():
        x = x_ref[...]
        y = x * lax.rsqrt(jnp.mean(x * x, axis=-1, keepdims=True) + EPS)
        h = (y * g_ref[...]).astype(jnp.bfloat16)
        h_ref[...] = h
        wt = wt_ref[...]
        tail_ref[...] = _nt_dot(h, wt)
        tailt_ref[...] = _nt_dot(wt[:TAIL_COLS], h)

    keep_rows = (i % tiles_per_seq) >= first_keep_tile
    heads_per_sub = SUB_COLS // LANES

    def sub_dot(sub):
        return _nt_dot(h_ref[...], w_ref[sub * SUB_COLS:(sub + 1) * SUB_COLS, :])

    def emit_qkv(hd, r, win_ref):
        cs = slice(hd * LANES, (hd + 1) * LANES)
        if residue_major:
            slot = hd % de_ref.shape[0]
            de_ref[slot] = r
            for res in range(RES):
                qkv_ref[0, res, :, cs] = de_ref[slot, pl.ds(res, tm // RES, stride=RES), :].astype(qkv_ref.dtype)
        else:
            qkv_ref[:, cs] = r.astype(qkv_ref.dtype)
        if win_ref is not None:
            win_ref[pl.ds(hd, tm, stride=N_HEADS_A), :] = r

    def rotary_tile(scale, win_ref):
        c = cos_ref[...]
        s = sin_ref[...]
        lane = lax.broadcasted_iota(jnp.int32, c.shape, 1)
        for sub in range(WIDTH_A // SUB_COLS):
            acc = sub_dot(sub)
            for hs in range(heads_per_sub):
                a = acc[:, hs * LANES:(hs + 1) * LANES]
                swapped = jnp.where(lane < ROPE_DIM // 2,
                                    pltpu.roll(a, LANES - ROPE_DIM // 2, 1),
                                    pltpu.roll(a, ROPE_DIM // 2, 1))
                r = a * c + swapped * s
                emit_qkv(sub * heads_per_sub + hs, r if scale is None else r * scale, win_ref)

    def plain_tile(win_ref):
        for sub in range(WIDTH_A // SUB_COLS):
            acc = sub_dot(sub)
            for hs in range(heads_per_sub):
                emit_qkv(sub * heads_per_sub + hs, acc[:, hs * LANES:(hs + 1) * LANES], win_ref)

    pl.when(j == 0)(lambda: rotary_tile(HEAD_DIM ** -0.5, None))
    pl.when((j == 1) & keep_rows)(lambda: rotary_tile(None, kf_ref))
    pl.when((j == 1) & jnp.logical_not(keep_rows))(lambda: rotary_tile(None, None))
    pl.when((j == 2) & keep_rows)(lambda: plain_tile(vf_ref))
    pl.when((j == 2) & jnp.logical_not(keep_rows))(lambda: plain_tile(None))

    @pl.when(j >= QKV_TILES)
    def _():
        for sub in range(WIDTH_A // SUB_COLS):
            p_ref[:, sub * SUB_COLS:(sub + 1) * SUB_COLS] = sub_dot(sub).astype(p_ref.dtype)


def _rope_tables(pos):
    half = ROPE_DIM // 2
    inv = ROPE_THETA ** (-jnp.arange(half, dtype=jnp.float32) / half)
    ang = pos.astype(jnp.float32)[:, None] * inv[None, :]
    cos, sin = jnp.cos(ang), jnp.sin(ang)
    n = pos.shape[0]
    ones = jnp.ones((n, LANES - ROPE_DIM), jnp.float32)
    c = jnp.concatenate([cos, cos, ones], axis=1)
    s = jnp.concatenate([-sin, sin, jnp.zeros_like(ones)], axis=1)
    return c, s


def _project(x2d, g_pre, w_main_t, w_tail_t, cos_t, sin_t, *, seq, keep, tm, residue_major):
    m = x2d.shape[0]
    tn = WIDTH_A
    n_i, n_j = m // tm, MAIN_COLS // tn
    tiles_per_seq = seq // tm
    first_keep = (seq - keep) // tm
    keep_tiles = keep // tm
    assert seq % tm == 0 and keep % tm == 0 and (seq - keep) % tm == 0
    kern = functools.partial(_proj_kernel, tiles_per_seq=tiles_per_seq, first_keep_tile=first_keep,
                             residue_major=residue_major)
    qkv_col = lambda j: jnp.minimum(j, QKV_TILES - 1)
    if residue_major:
        assert tm % (RES * 16) == 0
        qkv_spec = pl.BlockSpec((1, RES, tm // RES, tn),
                                lambda i, j: (i // tiles_per_seq, 0, i % tiles_per_seq, qkv_col(j)))
        qkv_shape = jax.ShapeDtypeStruct((m // seq, RES, seq // RES, QKV_TILES * tn), jnp.bfloat16)
    else:
        qkv_spec = pl.BlockSpec((tm, tn), lambda i, j: (i, qkv_col(j)))
        qkv_shape = jax.ShapeDtypeStruct((m, QKV_TILES * tn), jnp.bfloat16)

    def win_index(i, j):
        il = i % tiles_per_seq
        return (i // tiles_per_seq) * keep_tiles + jnp.maximum(il - first_keep, 0), 0

    win_spec = pl.BlockSpec((tm * N_HEADS_A, LANES), win_index, pipeline_mode=pl.Buffered(1))
    win_shape = jax.ShapeDtypeStruct((m // seq * keep * N_HEADS_A, LANES), jnp.float32)

    return pl.pallas_call(
        kern,
        grid=(n_i, n_j),
        in_specs=[
            pl.BlockSpec((tm, D_MODEL), lambda i, j: (i, 0)),
            pl.BlockSpec((1, D_MODEL), lambda i, j: (0, 0)),
            pl.BlockSpec((tn, D_MODEL), lambda i, j: (j, 0)),
            pl.BlockSpec((LANES, D_MODEL), lambda i, j: (0, 0)),
            pl.BlockSpec((tm, LANES), lambda i, j: (i, 0)),
            pl.BlockSpec((tm, LANES), lambda i, j: (i, 0)),
        ],
        out_specs=[
            qkv_spec,
            win_spec,
            win_spec,
            pl.BlockSpec((tm, tn), lambda i, j: (i, jnp.maximum(j - QKV_TILES, 0))),
            pl.BlockSpec((tm, LANES), lambda i, j: (i, 0)),
            pl.BlockSpec((TAIL_COLS, tm), lambda i, j: (0, i)),
        ],
        out_shape=[
            qkv_shape,
            win_shape,
            win_shape,
            jax.ShapeDtypeStruct((m, REST_COLS), jnp.bfloat16),
            jax.ShapeDtypeStruct((m, LANES), jnp.float32),
            jax.ShapeDtypeStruct((TAIL_COLS, m), jnp.float32),
        ],
        scratch_shapes=[pltpu.VMEM((tm, D_MODEL), jnp.bfloat16),
                        pltpu.VMEM((2 * SUB_COLS // LANES, tm, LANES), jnp.float32)],
        compiler_params=_cparams(("arbitrary", "arbitrary")),
        name="proj",
    )(x2d, g_pre, w_main_t, w_tail_t, cos_t, sin_t)


SLAB = 128


def _band_bias(kind):
    import numpy as np
    if kind == 16:
        nq, nk = 128, 256
        lq = np.arange(nq)[:, None]
        kap = np.arange(nk)[None, :]
        lk = kap - 128
        prev = kap < 128
    elif kind == 4:
        nq, nk = 128, 256
        rho = np.arange(nq)[:, None]
        lq = 4 * (rho % 32) + rho // 32
        kap = np.arange(nk)[None, :]
        lk = 4 * (kap % 64 - 32) + kap // 64
        prev = (kap % 64) < 32
    else:
        nq, nk = 256, 512
        rho = np.arange(nq)[:, None]
        lq = 16 * (rho % 16) + rho // 16
        kap = np.arange(nk)[None, :]
        lk = 16 * (kap % 32 - 16) + kap // 32
        prev = (kap % 32) < 16
    dist = lq - lk
    band = (dist >= 0) & (dist <= 128)
    out = np.stack([band & ~prev, band])
    return jnp.asarray(np.where(out, 0.0, NEG_BIG), dtype=jnp.float32)


PIECE = 16


def _attn_kernel(q_ref, kc_ref, kp_ref, vc_ref, vp_ref, b16_ref, b4_ref, b1_ref, o_ref, nat_ref):
    hp = jnp.minimum(pl.program_id(2), 1)
    f32 = jnp.float32

    def block(q, k, v, bias, prev):
        s = _nt_dot(q, k) + bias
        rows, keys = s.shape
        m_new = jnp.broadcast_to(jnp.max(s, axis=1, keepdims=True), (rows, LANES))
        if prev is not None:
            m_prev, l_prev, acc_prev = prev
            m_new = jnp.maximum(m_prev, m_new)
        p = jnp.exp(s - jnp.concatenate([m_new] * (keys // LANES), axis=1))
        v1 = jnp.concatenate([v, jnp.ones_like(v)], axis=1)
        pv = jnp.dot(p.astype(v.dtype), v1, preferred_element_type=f32)
        acc_new, l_new = pv[:, :LANES], pv[:, LANES:]
        if prev is not None:
            alpha = jnp.exp(m_prev - m_new)
            l_new = alpha * l_prev + l_new
            acc_new = alpha * acc_prev + acc_new
        return m_new, l_new, acc_new

    def split(x, n_parts):
        n = x.shape[0] // n_parts
        return [x[i * n:(i + 1) * n] for i in range(n_parts)]

    state = {}
    pieces = SLAB // PIECE

    for r in range(RES):
        k = jnp.concatenate([kp_ref[0, r], kc_ref[0, r]], axis=0)
        v = jnp.concatenate([vp_ref[0, r], vc_ref[0, r]], axis=0)
        res = [split(x, pieces) for x in block(q_ref[0, r], k, v, b16_ref[hp], None)]
        for a in range(pieces):
            state[(r, a)] = tuple(x[a] for x in res)

    def update(keys_, q, k, v, bias):
        prev = tuple(jnp.concatenate([state[key][i] for key in keys_], axis=0) for i in range(3))
        res = [split(x, len(keys_)) for x in block(q, k, v, bias, prev)]
        for i, key in enumerate(keys_):
            state[key] = tuple(x[i] for x in res)

    def gather(cur_ref, prev_ref, slabs, start, n):
        if start == 0:
            parts = [jnp.concatenate([prev_ref[0, r, SLAB - n:SLAB, :], cur_ref[0, r, 0:n, :]], axis=0)
                     for r in slabs]
        else:
            parts = [cur_ref[0, r, start - n:start + n, :] for r in slabs]
        return jnp.concatenate(parts, axis=0)

    for r4 in range(4):
        slabs = [r4 + 4 * i for i in range(4)]
        for jj in range(SLAB // 32):
            sl = slice(32 * jj, 32 * jj + 32)
            q = jnp.concatenate([q_ref[0, r, sl, :] for r in slabs], axis=0)
            k = gather(kc_ref, kp_ref, slabs, 32 * jj, 32)
            v = gather(vc_ref, vp_ref, slabs, 32 * jj, 32)
            keys_ = [(r, 2 * jj + a) for r in slabs for a in range(2)]
            update(keys_, q, k, v, b4_ref[hp] if jj == 0 else b4_ref[1])

    slabs = list(range(RES))
    for jj in range(pieces):
        sl = slice(PIECE * jj, PIECE * jj + PIECE)
        q = jnp.concatenate([q_ref[0, r, sl, :] for r in slabs], axis=0)
        k = gather(kc_ref, kp_ref, slabs, PIECE * jj, PIECE)
        v = gather(vc_ref, vp_ref, slabs, PIECE * jj, PIECE)
        update([(r, jj) for r in slabs], q, k, v, b1_ref[hp] if jj == 0 else b1_ref[1])

    for r in range(RES):
        acc = jnp.concatenate([state[(r, a)][2] for a in range(pieces)], axis=0)
        l = jnp.concatenate([state[(r, a)][1] for a in range(pieces)], axis=0)
        nat_ref[pl.ds(r, SLAB, stride=RES), :] = acc / l
    o_ref[...] = nat_ref[...].astype(o_ref.dtype)


def _prompt_attention(qkv16):
    b, _, rows, _ = qkv16.shape
    n_sb = rows // SLAB
    kcol, vcol = WIDTH_A // LANES, 2 * WIDTH_A // LANES
    blk = (1, RES, SLAB, LANES)
    cur = lambda off: pl.BlockSpec(blk, lambda bi, h, n: (bi, 0, n, off + h))
    prv = lambda off: pl.BlockSpec(blk, lambda bi, h, n: (bi, 0, jnp.maximum(n - 1, 0), off + h))
    const = lambda shp: pl.BlockSpec(shp, lambda bi, h, n: (0, 0, 0))
    return pl.pallas_call(
        _attn_kernel,
        grid=(b, N_HEADS_A, n_sb),
        in_specs=[cur(0), cur(kcol), prv(kcol), cur(vcol), prv(vcol),
                  const((2, 128, 256)), const((2, 128, 256)), const((2, 256, 512))],
        out_specs=pl.BlockSpec((RES * SLAB, LANES), lambda bi, h, n: (bi * n_sb + n, h)),
        out_shape=jax.ShapeDtypeStruct((b * rows * RES, WIDTH_A), jnp.bfloat16),
        scratch_shapes=[pltpu.VMEM((RES * SLAB, LANES), jnp.float32)],
        compiler_params=_cparams(("arbitrary", "arbitrary", "arbitrary")),
        name="prompt_attn",
    )(qkv16, qkv16, qkv16, qkv16, qkv16, _band_bias(16), _band_bias(4), _band_bias(1))


DN_STEP = 256
CARRY = 8


def _split_bf16(x, n):
    parts, r = [], x
    for _ in range(n):
        hi = r.astype(jnp.bfloat16)
        parts.append(hi)
        r = r - hi.astype(jnp.float32)
    return parts


def _bdot(a, b):
    return jnp.dot(a.astype(jnp.bfloat16), b.astype(jnp.bfloat16), preferred_element_type=jnp.float32)


def _softplus(x):
    return jnp.maximum(x, 0.0) + jnp.log1p(jnp.exp(-jnp.abs(x)))


def _silu(x):
    return x * jax.nn.sigmoid(x)


def _unit_lower_inverses(a_list):
    n = a_list[0].shape[0]
    row = lax.broadcasted_iota(jnp.int32, (n, n), 0)
    col = lax.broadcasted_iota(jnp.int32, (n, n), 1)
    eye = jnp.where(row == col, 1.0, 0.0)
    xs = [eye - a for a in a_list]
    ps = [_bdot(a, a) for a in a_list]
    k = 2
    while True:
        xs = [x + _bdot(x, p) for x, p in zip(xs, ps)]
        k *= 2
        if k >= n:
            return xs
        ps = [_bdot(p, p) for p in ps]


def _dn_kernel(qb_ref, kb_ref, vb_ref, tail_ref, tailt_ref, cw_ref, prow_ref, pcol_ref, onw_ref,
               o_ref, s_ref, xe_ref):
    c = pl.program_id(1)
    f32, bf16 = jnp.float32, jnp.bfloat16
    tb, n, hh = DN_STEP, CHUNK, N_HEADS_B
    n_ch = tb // n

    @pl.when(c == 0)
    def _():
        s_ref[...] = jnp.zeros_like(s_ref)
        xe_ref[0:CARRY, :] = jnp.zeros((CARRY, CONV_CH), f32)

    @pl.when(c > 0)
    def _():
        xe_ref[0:CARRY, :] = xe_ref[tb:tb + CARRY, :]

    conv = []
    for part, ref in enumerate((qb_ref, kb_ref, vb_ref)):
        cs = slice(part * WIDTH_B, (part + 1) * WIDTH_B)
        xe_ref[CARRY:CARRY + tb, cs] = ref[...].astype(f32)
        y = xe_ref[CARRY:CARRY + tb, cs] * cw_ref[CONV_WIDTH - 1:CONV_WIDTH, cs]
        for i in range(CONV_WIDTH - 1):
            off = CARRY - (CONV_WIDTH - 1) + i
            y = y + xe_ref[off:off + tb, cs] * cw_ref[i:i + 1, cs]
        conv.append(_silu(y))
    qc, kc, vc = conv

    t = tail_ref[...]
    beta_c = jax.nn.sigmoid(t)
    g_c = -jnp.exp(prow_ref[0:1]) * _softplus(t + prow_ref[1:2])
    tt = tailt_ref[...]
    g_r = -jnp.exp(pcol_ref[:, 0:1]) * _softplus(tt + pcol_ref[:, 1:2])

    row = lax.broadcasted_iota(jnp.int32, (n, n), 0)
    col = lax.broadcasted_iota(jnp.int32, (n, n), 1)
    incl = row >= col
    strict = row > col
    ltri = jnp.where(incl, 1.0, 0.0).astype(bf16)
    utri = jnp.where(row <= col, 1.0, 0.0).astype(bf16)
    dot = functools.partial(jnp.dot, preferred_element_type=f32)

    gcs, grs, eg_c, ed_c, glast = [], [], [], [], []
    for ci in range(n_ch):
        rs = slice(ci * n, (ci + 1) * n)
        gc = sum(dot(ltri, part) for part in _split_bf16(g_c[rs], 3))
        gr = sum(dot(part, utri) for part in _split_bf16(g_r[:, rs], 3))
        gl = gc[n - 1:n, :]
        gcs.append(gc)
        grs.append(gr)
        glast.append(gl)
        eg_c.append(jnp.exp(gc))
        ed_c.append(jnp.exp(gl - gc))

    ids = [(ci, h) for ci in range(n_ch) for h in range(hh)]

    def head_slice(x, ci, h):
        return x[ci * n:(ci + 1) * n, h * DK_B:(h + 1) * DK_B]

    def lane_col(x, lane):
        return x[:, lane:lane + 1]

    qs, ks, kbetas, vbetas, egs = [], [], [], [], []
    for ci, h in ids:
        q = head_slice(qc, ci, h)
        k = head_slice(kc, ci, h)
        v = head_slice(vc, ci, h)
        q = q * (lax.rsqrt(jnp.sum(q * q, axis=-1, keepdims=True) + EPS) * (DK_B ** -0.5))
        k = k * lax.rsqrt(jnp.sum(k * k, axis=-1, keepdims=True) + EPS)
        beta = lane_col(beta_c[ci * n:(ci + 1) * n], h)
        qs.append(q)
        ks.append(k)
        kbetas.append(k * beta)
        vbetas.append(v * beta)
        egs.append(lane_col(eg_c[ci], hh + h))

    kqs = [_nt_dot(jnp.concatenate([kb, q], axis=0).astype(bf16), k.astype(bf16))
           for kb, q, k in zip(kbetas, qs, ks)]
    a_mats, qks = [], []
    for (ci, h), kq in zip(ids, kqs):
        gcol = lane_col(gcs[ci], hh + h)
        grow = grs[ci][hh + h:hh + h + 1, :]
        decay = jnp.exp(jnp.where(incl, gcol - grow, NEG_BIG))
        a_mats.append(jnp.where(strict, kq[:n] * decay, 0.0))
        qks.append((kq[n:] * decay).astype(bf16))
    t_mats = _unit_lower_inverses(a_mats)
    uws = [dot(tm.astype(bf16), jnp.concatenate([vb, kb * eg], axis=1).astype(bf16))
           for tm, vb, kb, eg in zip(t_mats, vbetas, kbetas, egs)]

    states = [s_ref[0, h] for h in range(hh)]
    onw = onw_ref[...]
    for ci in range(n_ch):
        base = ci * hh
        wqs = [jnp.concatenate([uws[base + h][:, DV_B:], qs[base + h] * egs[base + h]], axis=0).astype(bf16)
               for h in range(hh)]
        wss = [dot(wq, st.astype(bf16)) for wq, st in zip(wqs, states)]
        v_news = [(uws[base + h][:, :DV_B] - wss[h][:n]).astype(bf16) for h in range(hh)]
        o_in = [dot(qks[base + h], v_news[h]) for h in range(hh)]
        k_decs = [(ks[base + h] * lane_col(ed_c[ci], hh + h)).astype(bf16) for h in range(hh)]
        upds = [lax.dot_general(k_decs[h], v_news[h], (((0,), (0,)), ((), ())), preferred_element_type=f32)
                for h in range(hh)]
        e_last = jnp.exp(glast[ci])
        states = [states[h] * lane_col(e_last, hh + h) + upds[h] for h in range(hh)]
        for h in range(hh):
            o = wss[h][n:] + o_in[h]
            o = o * lax.rsqrt(jnp.mean(o * o, axis=-1, keepdims=True) + EPS) * onw
            o_ref[ci * n:(ci + 1) * n, h * DV_B:(h + 1) * DV_B] = o.astype(o_ref.dtype)
    for h in range(hh):
        s_ref[0, h] = states[h]


def _deltanet_prompt(p, tail, tail_t, conv_w, a_log, dt_bias, onorm_w, batch):
    m = p.shape[0]
    steps = m // batch // DN_STEP
    wb = WIDTH_B // 1
    qcol = REST_QKVB // wb
    zeros8 = jnp.zeros((N_HEADS_B,), jnp.float32)
    gate = jnp.stack([jnp.concatenate([zeros8, a_log.astype(jnp.float32)]),
                      jnp.concatenate([zeros8, dt_bias.astype(jnp.float32)])])
    prow = jnp.pad(gate, ((0, 6), (0, LANES - TAIL_COLS)))
    pcol = jnp.pad(gate.T, ((0, 0), (0, LANES - 2)))
    tok = lambda cb: pl.BlockSpec((DN_STEP, wb), lambda b, c: (b * steps + c, cb))
    full = lambda shp: pl.BlockSpec(shp, lambda b, c: (0,) * len(shp))
    return pl.pallas_call(
        _dn_kernel,
        grid=(batch, steps),
        in_specs=[tok(qcol), tok(qcol + 1), tok(qcol + 2),
                  pl.BlockSpec((DN_STEP, LANES), lambda b, c: (b * steps + c, 0)),
                  pl.BlockSpec((TAIL_COLS, DN_STEP), lambda b, c: (0, b * steps + c)),
                  full((CONV_WIDTH, CONV_CH)), full((8, LANES)), full((TAIL_COLS, LANES)),
                  full((1, DV_B))],
        out_specs=[pl.BlockSpec((DN_STEP, WIDTH_B), lambda b, c: (b * steps + c, 0)),
                   pl.BlockSpec((1, N_HEADS_B, DK_B, DV_B), lambda b, c: (b, 0, 0, 0))],
        out_shape=[jax.ShapeDtypeStruct((m, WIDTH_B), jnp.bfloat16),
                   jax.ShapeDtypeStruct((batch, N_HEADS_B, DK_B, DV_B), jnp.float32)],
        scratch_shapes=[pltpu.VMEM((CARRY + DN_STEP, CONV_CH), jnp.float32)],
        compiler_params=_cparams(("arbitrary", "arbitrary")),
        name="deltanet_prompt",
    )(p, p, p, tail, tail_t, conv_w.astype(jnp.float32), prow, pcol,
      onorm_w.astype(jnp.float32)[None])


def _out_kernel(oa_ref, za_ref, ob_ref, zb_ref, w_ref, g_ref, x_ref, y_ref):
    f32 = jnp.float32
    ga = (oa_ref[...].astype(f32) * _silu(za_ref[...].astype(f32))).astype(jnp.bfloat16)
    gb = (ob_ref[...].astype(f32) * _silu(zb_ref[...].astype(f32))).astype(jnp.bfloat16)
    y = (jnp.dot(ga, w_ref[:WIDTH_A, :], preferred_element_type=f32)
         + jnp.dot(gb, w_ref[WIDTH_A:, :], preferred_element_type=f32))
    y = y * lax.rsqrt(jnp.mean(y * y, axis=-1, keepdims=True) + EPS) * g_ref[...]
    y_ref[...] = x_ref[...] + y


def _output_sublayer(o_a, p, o_b, w_out, g_post, x2d, *, tm):
    m = x2d.shape[0]
    row = lambda width, cb: pl.BlockSpec((tm, width), lambda i: (i, cb))
    return pl.pallas_call(
        _out_kernel,
        grid=(m // tm,),
        in_specs=[row(WIDTH_A, 0), row(WIDTH_A, REST_ZA // WIDTH_A), row(WIDTH_B, 0),
                  row(WIDTH_B, REST_ZB // WIDTH_B),
                  pl.BlockSpec((WIDTH_A + WIDTH_B, D_MODEL), lambda i: (0, 0)),
                  pl.BlockSpec((1, D_MODEL), lambda i: (0, 0)),
                  row(D_MODEL, 0)],
        out_specs=row(D_MODEL, 0),
        out_shape=jax.ShapeDtypeStruct((m, D_MODEL), jnp.float32),
        compiler_params=_cparams(("arbitrary",)),
        name="out_proj",
    )(o_a, p, o_b, p, w_out, g_post, x2d)


def _decode_attn_kernel(q_ref, kn_ref, vn_ref, k1_ref, v1_ref, k4_ref, v4_ref, k16_ref, v16_ref, o_ref):
    f32 = jnp.float32
    q = q_ref[0].astype(f32)
    kn, vn = kn_ref[0].astype(f32), vn_ref[0].astype(f32)
    s_new = jnp.sum(q * kn, axis=-1, keepdims=True)
    scores = [jnp.sum(k_ref[0] * q[None], axis=-1, keepdims=True) for k_ref in (k1_ref, k4_ref, k16_ref)]
    m = s_new
    for s in scores:
        m = jnp.maximum(m, jnp.max(s, axis=0))
    p_new = len(DILATIONS) * jnp.exp(s_new - m)
    den = p_new
    acc = p_new * vn
    for s, v_ref in zip(scores, (v1_ref, v4_ref, v16_ref)):
        p = jnp.exp(s - m[None])
        den = den + jnp.sum(p, axis=0)
        acc = acc + jnp.sum(p * v_ref[0], axis=0)
    o_ref[0] = (acc / den).astype(o_ref.dtype)


def _decode_attention(q, k_new, v_new, cache_k, cache_v):
    b, win, h, dd = cache_k.shape
    nb = 128
    views, specs = [], []
    for window, dil in DILATIONS:
        assert window // dil == nb and win % (nb * dil) == 0
        blk_idx = win // (nb * dil) - 1
        if dil == 1:
            spec = pl.BlockSpec((1, nb, h, dd), lambda i, bi=blk_idx: (i, bi, 0, 0))
            view = lambda c: c
        else:
            spec = pl.BlockSpec((1, nb, None, h, dd), lambda i, bi=blk_idx: (i, bi, 0, 0, 0))
            view = lambda c, dil=dil: c.reshape(b, win // dil, dil, h, dd)
        specs += [spec, spec]
        views += [view(cache_k), view(cache_v)]
    tok = pl.BlockSpec((1, h, dd), lambda i: (i, 0, 0))
    return pl.pallas_call(
        _decode_attn_kernel,
        grid=(b,),
        in_specs=[tok, tok, tok] + specs,
        out_specs=tok,
        out_shape=jax.ShapeDtypeStruct((b, h, dd), jnp.bfloat16),
        compiler_params=_cparams(("arbitrary",)),
        name="decode_attn",
    )(q, k_new, v_new, *views)


def _decode_dn_kernel(x_ref, cb_ref, cw_ref, gate_ref, prm_ref, onw_ref, s_ref,
                      o_ref, cbo_ref, so_ref):
    f32 = jnp.float32
    hh = N_HEADS_B
    xn = x_ref[0]
    cw = cw_ref[...]
    y = xn * cw[CONV_WIDTH - 1]
    for i in range(CONV_WIDTH - 1):
        y = y + cb_ref[0, i] * cw[i]
        cbo_ref[0, i] = cb_ref[0, i + 1] if i + 1 < CONV_WIDTH - 1 else xn
    y = _silu(y)
    q, k, v = y[:hh], y[hh:2 * hh], y[2 * hh:]
    q = q * lax.rsqrt(jnp.sum(q * q, axis=-1, keepdims=True) + EPS) * (DK_B ** -0.5)
    k = k * lax.rsqrt(jnp.sum(k * k, axis=-1, keepdims=True) + EPS)
    gate = gate_ref[0]
    beta = jax.nn.sigmoid(gate[:hh])
    decay = jnp.exp(-jnp.exp(prm_ref[:hh]) * _softplus(gate[hh:] + prm_ref[hh:]))
    qk = jnp.sum(q * k, axis=-1, keepdims=True)
    zeros = jnp.zeros((8 - 2, DK_B), jnp.bfloat16)
    outs = []
    for h in range(hh):
        state = s_ref[0, h]
        a = decay[h:h + 1]
        lhs = jnp.concatenate([k[h:h + 1].astype(jnp.bfloat16), q[h:h + 1].astype(jnp.bfloat16), zeros], axis=0)
        ks = jnp.dot(lhs, state.astype(jnp.bfloat16), preferred_element_type=f32)
        v_new = beta[h:h + 1] * (v[h:h + 1] - a * ks[0:1])
        upd = lax.dot_general(lhs[0:8], jnp.concatenate([v_new.astype(jnp.bfloat16), jnp.zeros((7, DV_B), jnp.bfloat16)], axis=0),
                              (((0,), (0,)), ((), ())), preferred_element_type=f32)
        so_ref[0, h] = state * a[:, 0:1] + upd
        outs.append(a * ks[1:2] + qk[h:h + 1] * v_new)
    o = jnp.concatenate(outs, axis=0)
    o = o * lax.rsqrt(jnp.mean(o * o, axis=-1, keepdims=True) + EPS) * onw_ref[...]
    o_ref[0] = o.astype(o_ref.dtype)


def _decode_deltanet(x_new, conv_buf, state, beta_in, a_in, conv_w, a_log, dt_bias, onorm_w):
    b = x_new.shape[0]
    f32 = jnp.float32
    g3 = CONV_CH // LANES
    hh = N_HEADS_B
    gate = jnp.broadcast_to(jnp.concatenate([beta_in, a_in], axis=1).astype(f32)[:, :, None], (b, 2 * hh, LANES))
    prm = jnp.broadcast_to(jnp.concatenate([a_log, dt_bias]).astype(f32)[:, None], (2 * hh, LANES))
    full = lambda shp: pl.BlockSpec(shp, lambda i: (0,) * len(shp))
    per = lambda shp: pl.BlockSpec((1,) + shp, lambda i: (i,) + (0,) * len(shp))
    o, cb_new, s_new = pl.pallas_call(
        _decode_dn_kernel,
        grid=(b,),
        in_specs=[per((g3, LANES)), per((CONV_WIDTH - 1, g3, LANES)), full((CONV_WIDTH, g3, LANES)),
                  per((2 * hh, LANES)), full((2 * hh, LANES)), full((1, DV_B)), per((hh, DK_B, DV_B))],
        out_specs=[per((hh, DV_B)), per((CONV_WIDTH - 1, g3, LANES)), per((hh, DK_B, DV_B))],
        out_shape=[jax.ShapeDtypeStruct((b, hh, DV_B), jnp.bfloat16),
                   jax.ShapeDtypeStruct((b, CONV_WIDTH - 1, g3, LANES), f32),
                   jax.ShapeDtypeStruct((b, hh, DK_B, DV_B), f32)],
        compiler_params=_cparams(("arbitrary",)),
        name="decode_deltanet",
    )(x_new.astype(f32).reshape(b, g3, LANES), conv_buf.astype(f32).reshape(b, CONV_WIDTH - 1, g3, LANES),
      conv_w.astype(f32).reshape(CONV_WIDTH, g3, LANES), gate, prm, onorm_w.astype(f32)[None], state.astype(f32))
    return o, cb_new.reshape(b, CONV_WIDTH - 1, CONV_CH), s_new


def kernel(x_prompt, x_sample, cache_win_k, cache_win_v, state_conv, state_delta,
           g_pre, w_in, conv_w, a_log, dt_bias, onorm_w, w_out, g_post):
    f32, bf16 = jnp.float32, jnp.bfloat16
    b, s, _ = x_prompt.shape
    db, t, _ = x_sample.shape
    depth = w_in.shape[0]
    n_past = cache_win_k.shape[2]
    assert t == 1 and n_past == MAX_WINDOW and s % (RES * SLAB) == 0
    keep = min(MAX_WINDOW, s)

    cos_p, sin_p = _rope_tables(jnp.arange(s, dtype=jnp.int32))
    cos_p, sin_p = jnp.tile(cos_p, (b, 1)), jnp.tile(sin_p, (b, 1))
    cos_s, sin_s = _rope_tables(jnp.full((db,), PAST_LEN, jnp.int32))

    yp = x_prompt.reshape(b * s, D_MODEL)
    ys = x_sample.reshape(db, D_MODEL)
    outs = [[] for _ in range(8)]
    for l in range(depth):
        w_in_t = jnp.swapaxes(w_in, 1, 2)
        w_main = _to_bf16(w_in_t, l, MAIN_COLS)
        w_tail = jnp.pad(w_in_t[l, MAIN_COLS:, :], ((0, LANES - TAIL_COLS), (0, 0))).astype(bf16)
        w_o = _to_bf16(w_out, l, WIDTH_A + WIDTH_B)
        gp, go = g_pre[l].astype(f32)[None], g_post[l].astype(f32)[None]

        qkv16, kf, vf, rest, tail, tail_t = _project(yp, gp, w_main, w_tail, cos_p, sin_p,
                                                     seq=s, keep=keep, tm=1024, residue_major=True)
        o_a = _prompt_attention(qkv16)
        o_b, s_fin = _deltanet_prompt(rest, tail, tail_t, conv_w[l], a_log[l], dt_bias[l], onorm_w[l], b)
        outs[0].append(kf.reshape(b, keep, N_HEADS_A, HEAD_DIM))
        outs[1].append(vf.reshape(b, keep, N_HEADS_A, HEAD_DIM))
        n_tail = min(CONV_WIDTH - 1, s)
        tail_rows = rest.reshape(b, s, REST_COLS)[:, s - n_tail:, REST_QKVB:REST_QKVB + CONV_CH].astype(f32)
        outs[2].append(jnp.pad(tail_rows, ((0, 0), (CONV_WIDTH - 1 - n_tail, 0), (0, 0))))
        outs[3].append(s_fin)
        yp = _output_sublayer(o_a, rest, o_b, w_o, go, yp, tm=256)

        qkv_s, kf_s, vf_s, rest_s, tail_s, _ = _project(ys, gp, w_main, w_tail, cos_s, sin_s,
                                                        seq=db, keep=db, tm=db, residue_major=False)
        k_new = kf_s.reshape(db, N_HEADS_A, HEAD_DIM)
        v_new = vf_s.reshape(db, N_HEADS_A, HEAD_DIM)
        o_as = _decode_attention(qkv_s[:, :WIDTH_A].reshape(db, N_HEADS_A, HEAD_DIM), k_new, v_new,
                                 cache_win_k[l], cache_win_v[l])
        o_bs, cb_new, st_new = _decode_deltanet(
            rest_s[:, REST_QKVB:REST_QKVB + CONV_CH], state_conv[l], state_delta[l],
            tail_s[:, :N_HEADS_B], tail_s[:, N_HEADS_B:TAIL_COLS], conv_w[l], a_log[l], dt_bias[l], onorm_w[l])
        outs[4].append(k_new.reshape(db, t, N_HEADS_A, HEAD_DIM))
        outs[5].append(v_new.reshape(db, t, N_HEADS_A, HEAD_DIM))
        outs[6].append(cb_new)
        outs[7].append(st_new)
        ys = _output_sublayer(o_as.reshape(db, WIDTH_A), rest_s, o_bs.reshape(db, WIDTH_B), w_o, go, ys, tm=db)

    stk = [jnp.stack(o) for o in outs]
    return (yp.reshape(b, s, D_MODEL), ys.reshape(db, t, D_MODEL),
            stk[0], stk[1], stk[2], stk[3], stk[4], stk[5], stk[6], stk[7])
```

```python
import functools
import math

import jax
import jax.numpy as jnp
from jax import lax
from jax.experimental import pallas as pl
from jax.experimental.pallas import tpu as pltpu

D_MODEL = 2048
HEAD_DIM = 128
N_HEADS_A = 8
N_HEADS_B = 8
DK_B = 128
DV_B = 128
WIDTH_A = N_HEADS_A * HEAD_DIM
WIDTH_B = N_HEADS_B * DV_B
DILATIONS = ((128, 1), (512, 4), (2048, 16))
MAX_WINDOW = 2048
ROPE_THETA = 500000.0
ROPE_DIM = HEAD_DIM // 4
CONV_WIDTH = 4
CONV_CH = 2 * N_HEADS_B * DK_B + N_HEADS_B * DV_B
CHUNK = 64
EPS = 1e-6
PAST_LEN = 16384
MAIN_COLS = 4 * WIDTH_A + CONV_CH + WIDTH_B
TAIL_COLS = 2 * N_HEADS_B
LANES = 128
NEG_BIG = -1e30
VMEM_LIMIT = 56 * 1024 * 1024

REST_COLS = MAIN_COLS - 3 * WIDTH_A
REST_ZA, REST_QKVB, REST_ZB = 0, WIDTH_A, WIDTH_A + CONV_CH


def _cparams(sem):
    return pltpu.CompilerParams(dimension_semantics=sem, vmem_limit_bytes=VMEM_LIMIT)


def _nt_dot(a, b):
    return lax.dot_general(a, b, (((1,), (1,)), ((), ())), preferred_element_type=jnp.float32)


def _cast_kernel(x_ref, o_ref):
    o_ref[...] = x_ref[...].astype(o_ref.dtype)


def _to_bf16(w, layer, rows, tr=1024):
    cols = w.shape[2]
    return pl.pallas_call(
        _cast_kernel,
        grid=(rows // tr,),
        in_specs=[pl.BlockSpec((None, tr, cols), lambda j: (layer, j, 0))],
        out_specs=pl.BlockSpec((tr, cols), lambda j: (j, 0)),
        out_shape=jax.ShapeDtypeStruct((rows, cols), jnp.bfloat16),
        compiler_params=_cparams(("arbitrary",)),
        name="cast_bf16",
    )(w)


RES = 16
QKV_TILES = 3
SUB_COLS = 256


def _proj_kernel(x_ref, g_ref, w_ref, wt_ref, cos_ref, sin_ref,
                 qkv_ref, kf_ref, vf_ref, p_ref, tail_ref, tailt_ref, h_ref, de_ref,
                 *, tiles_per_seq, first_keep_tile, residue_major):
    i = pl.program_id(0)
    j = pl.program_id(1)
    tm = x_ref.shape[0]

    @pl.when(j == 0)
    def _():
        x = x_ref[...]
        y = x * lax.rsqrt(jnp.mean(x * x, axis=-1, keepdims=True) + EPS)
        h = (y * g_ref[...]).astype(jnp.bfloat16)
        h_ref[...] = h
        wt = wt_ref[...]
        tail_ref[...] = _nt_dot(h, wt)
        tailt_ref[...] = _nt_dot(wt[:TAIL_COLS], h)

    keep_rows = (i % tiles_per_seq) >= first_keep_tile
    heads_per_sub = SUB_COLS // LANES

    def sub_dot(sub):
        return _nt_dot(h_ref[...], w_ref[sub * SUB_COLS:(sub + 1) * SUB_COLS, :])

    def emit_qkv(hd, r, win_ref):
        cs = slice(hd * LANES, (hd + 1) * LANES)
        if residue_major:
            slot = hd % de_ref.shape[0]
            de_ref[slot] = r
            for res in range(RES):
                qkv_ref[0, res, :, cs] = de_ref[slot, pl.ds(res, tm // RES, stride=RES), :].astype(qkv_ref.dtype)
        else:
            qkv_ref[:, cs] = r.astype(qkv_ref.dtype)
        if win_ref is not None:
            win_ref[pl.ds(hd, tm, stride=N_HEADS_A), :] = r

    def rotary_tile(scale, win_ref):
        c = cos_ref[...]
        s = sin_ref[...]
        lane = lax.broadcasted_iota(jnp.int32, c.shape, 1)
        for sub in range(WIDTH_A // SUB_COLS):
            acc = sub_dot(sub)
            for hs in range(heads_per_sub):
                a = acc[:, hs * LANES:(hs + 1) * LANES]
                swapped = jnp.where(lane < ROPE_DIM // 2,
                                    pltpu.roll(a, LANES - ROPE_DIM // 2, 1),
                                    pltpu.roll(a, ROPE_DIM // 2, 1))
                r = a * c + swapped * s
                emit_qkv(sub * heads_per_sub + hs, r if scale is None else r * scale, win_ref)

    def plain_tile(win_ref):
        for sub in range(WIDTH_A // SUB_COLS):
            acc = sub_dot(sub)
            for hs in range(heads_per_sub):
                emit_qkv(sub * heads_per_sub + hs, acc[:, hs * LANES:(hs + 1) * LANES], win_ref)

    pl.when(j == 0)(lambda: rotary_tile(HEAD_DIM ** -0.5, None))
    pl.when((j == 1) & keep_rows)(lambda: rotary_tile(None, kf_ref))
    pl.when((j == 1) & jnp.logical_not(keep_rows))(lambda: rotary_tile(None, None))
    pl.when((j == 2) & keep_rows)(lambda: plain_tile(vf_ref))
    pl.when((j == 2) & jnp.logical_not(keep_rows))(lambda: plain_tile(None))

    @pl.when(j >= QKV_TILES)
    def _():
        for sub in range(WIDTH_A // SUB_COLS):
            p_ref[:, sub * SUB_COLS:(sub + 1) * SUB_COLS] = sub_dot(sub).astype(p_ref.dtype)


def _rope_tables(pos):
    half = ROPE_DIM // 2
    inv = ROPE_THETA ** (-jnp.arange(half, dtype=jnp.float32) / half)
    ang = pos.astype(jnp.float32)[:, None] * inv[None, :]
    cos, sin = jnp.cos(ang), jnp.sin(ang)
    n = pos.shape[0]
    ones = jnp.ones((n, LANES - ROPE_DIM), jnp.float32)
    c = jnp.concatenate([cos, cos, ones], axis=1)
    s = jnp.concatenate([-sin, sin, jnp.zeros_like(ones)], axis=1)
    return c, s


def _project(x2d, g_pre, w_main_t, w_tail_t, cos_t, sin_t, *, seq, keep, tm, residue_major):
    m = x2d.shape[0]
    tn = WIDTH_A
    n_i, n_j = m // tm, MAIN_COLS // tn
    tiles_per_seq = seq // tm
    first_keep = (seq - keep) // tm
    keep_tiles = keep // tm
    assert seq % tm == 0 and keep % tm == 0 and (seq - keep) % tm == 0
    kern = functools.partial(_proj_kernel, tiles_per_seq=tiles_per_seq, first_keep_tile=first_keep,
                             residue_major=residue_major)
    qkv_col = lambda j: jnp.minimum(j, QKV_TILES - 1)
    if residue_major:
        assert tm % (RES * 16) == 0
        qkv_spec = pl.BlockSpec((1, RES, tm // RES, tn),
                                lambda i, j: (i // tiles_per_seq, 0, i % tiles_per_seq, qkv_col(j)))
        qkv_shape = jax.ShapeDtypeStruct((m // seq, RES, seq // RES, QKV_TILES * tn), jnp.bfloat16)
    else:
        qkv_spec = pl.BlockSpec((tm, tn), lambda i, j: (i, qkv_col(j)))
        qkv_shape = jax.ShapeDtypeStruct((m, QKV_TILES * tn), jnp.bfloat16)

    def win_index(i, j):
        il = i % tiles_per_seq
        return (i // tiles_per_seq) * keep_tiles + jnp.maximum(il - first_keep, 0), 0

    win_spec = pl.BlockSpec((tm * N_HEADS_A, LANES), win_index, pipeline_mode=pl.Buffered(1))
    win_shape = jax.ShapeDtypeStruct((m // seq * keep * N_HEADS_A, LANES), jnp.float32)

    return pl.pallas_call(
        kern,
        grid=(n_i, n_j),
        in_specs=[
            pl.BlockSpec((tm, D_MODEL), lambda i, j: (i, 0)),
            pl.BlockSpec((1, D_MODEL), lambda i, j: (0, 0)),
            pl.BlockSpec((tn, D_MODEL), lambda i, j: (j, 0)),
            pl.BlockSpec((LANES, D_MODEL), lambda i, j: (0, 0)),
            pl.BlockSpec((tm, LANES), lambda i, j: (i, 0)),
            pl.BlockSpec((tm, LANES), lambda i, j: (i, 0)),
        ],
        out_specs=[
            qkv_spec,
            win_spec,
            win_spec,
            pl.BlockSpec((tm, tn), lambda i, j: (i, jnp.maximum(j - QKV_TILES, 0))),
            pl.BlockSpec((tm, LANES), lambda i, j: (i, 0)),
            pl.BlockSpec((TAIL_COLS, tm), lambda i, j: (0, i)),
        ],
        out_shape=[
            qkv_shape,
            win_shape,
            win_shape,
            jax.ShapeDtypeStruct((m, REST_COLS), jnp.bfloat16),
            jax.ShapeDtypeStruct((m, LANES), jnp.float32),
            jax.ShapeDtypeStruct((TAIL_COLS, m), jnp.float32),
        ],
        scratch_shapes=[pltpu.VMEM((tm, D_MODEL), jnp.bfloat16),
                        pltpu.VMEM((2 * SUB_COLS // LANES, tm, LANES), jnp.float32)],
        compiler_params=_cparams(("arbitrary", "arbitrary")),
        name="proj",
    )(x2d, g_pre, w_main_t, w_tail_t, cos_t, sin_t)


SLAB = 128


def _band_bias(kind):
    import numpy as np
    if kind == 16:
        nq, nk = 128, 256
        lq = np.arange(nq)[:, None]
        kap = np.arange(nk)[None, :]
        lk = kap - 128
        prev = kap < 128
    elif kind == 4:
        nq, nk = 128, 256
        rho = np.arange(nq)[:, None]
        lq = 4 * (rho % 32) + rho // 32
        kap = np.arange(nk)[None, :]
        lk = 4 * (kap % 64 - 32) + kap // 64
        prev = (kap % 64) < 32
    else:
        nq, nk = 256, 512
        rho = np.arange(nq)[:, None]
        lq = 16 * (rho % 16) + rho // 16
        kap = np.arange(nk)[None, :]
        lk = 16 * (kap % 32 - 16) + kap // 32
        prev = (kap % 32) < 16
    dist = lq - lk
    band = (dist >= 0) & (dist <= 128)
    out = np.stack([band & ~prev, band])
    return jnp.asarray(np.where(out, 0.0, NEG_BIG), dtype=jnp.float32)


PIECE = 16


def _attn_kernel(q_ref, kc_ref, kp_ref, vc_ref, vp_ref, b16_ref, b4_ref, b1_ref, o_ref, nat_ref):
    hp = jnp.minimum(pl.program_id(2), 1)
    f32 = jnp.float32

    def block(q, k, v, bias, prev):
        s = _nt_dot(q, k) + bias
        rows, keys = s.shape
        m_new = jnp.broadcast_to(jnp.max(s, axis=1, keepdims=True), (rows, LANES))
        if prev is not None:
            m_prev, l_prev, acc_prev = prev
            m_new = jnp.maximum(m_prev, m_new)
        p = jnp.exp(s - jnp.concatenate([m_new] * (keys // LANES), axis=1))
        v1 = jnp.concatenate([v, jnp.ones_like(v)], axis=1)
        pv = jnp.dot(p.astype(v.dtype), v1, preferred_element_type=f32)
        acc_new, l_new = pv[:, :LANES], pv[:, LANES:]
        if prev is not None:
            alpha = jnp.exp(m_prev - m_new)
            l_new = alpha * l_prev + l_new
            acc_new = alpha * acc_prev + acc_new
        return m_new, l_new, acc_new

    def split(x, n_parts):
        n = x.shape[0] // n_parts
        return [x[i * n:(i + 1) * n] for i in range(n_parts)]

    state = {}
    pieces = SLAB // PIECE

    for r in range(RES):
        k = jnp.concatenate([kp_ref[0, r], kc_ref[0, r]], axis=0)
        v = jnp.concatenate([vp_ref[0, r], vc_ref[0, r]], axis=0)
        res = [split(x, pieces) for x in block(q_ref[0, r], k, v, b16_ref[hp], None)]
        for a in range(pieces):
            state[(r, a)] = tuple(x[a] for x in res)

    def update(keys_, q, k, v, bias):
        prev = tuple(jnp.concatenate([state[key][i] for key in keys_], axis=0) for i in range(3))
        res = [split(x, len(keys_)) for x in block(q, k, v, bias, prev)]
        for i, key in enumerate(keys_):
            state[key] = tuple(x[i] for x in res)

    def gather(cur_ref, prev_ref, slabs, start, n):
        if start == 0:
            parts = [jnp.concatenate([prev_ref[0, r, SLAB - n:SLAB, :], cur_ref[0, r, 0:n, :]], axis=0)
                     for r in slabs]
        else:
            parts = [cur_ref[0, r, start - n:start + n, :] for r in slabs]
        return jnp.concatenate(parts, axis=0)

    for r4 in range(4):
        slabs = [r4 + 4 * i for i in range(4)]
        for jj in range(SLAB // 32):
            sl = slice(32 * jj, 32 * jj + 32)
            q = jnp.concatenate([q_ref[0, r, sl, :] for r in slabs], axis=0)
            k = gather(kc_ref, kp_ref, slabs, 32 * jj, 32)
            v = gather(vc_ref, vp_ref, slabs, 32 * jj, 32)
            keys_ = [(r, 2 * jj + a) for r in slabs for a in range(2)]
            update(keys_, q, k, v, b4_ref[hp] if jj == 0 else b4_ref[1])

    slabs = list(range(RES))
    for jj in range(pieces):
        sl = slice(PIECE * jj, PIECE * jj + PIECE)
        q = jnp.concatenate([q_ref[0, r, sl, :] for r in slabs], axis=0)
        k = gather(kc_ref, kp_ref, slabs, PIECE * jj, PIECE)
        v = gather(vc_ref, vp_ref, slabs, PIECE * jj, PIECE)
        update([(r, jj) for r in slabs], q, k, v, b1_ref[hp] if jj == 0 else b1_ref[1])

    for r in range(RES):
        acc = jnp.concatenate([state[(r, a)][2] for a in range(pieces)], axis=0)
        l = jnp.concatenate([state[(r, a)][1] for a in range(pieces)], axis=0)
        nat_ref[pl.ds(r, SLAB, stride=RES), :] = acc / l
    o_ref[...] = nat_ref[...].astype(o_ref.dtype)


def _prompt_attention(qkv16):
    b, _, rows, _ = qkv16.shape
    n_sb = rows // SLAB
    kcol, vcol = WIDTH_A // LANES, 2 * WIDTH_A // LANES
    blk = (1, RES, SLAB, LANES)
    cur = lambda off: pl.BlockSpec(blk, lambda bi, h, n: (bi, 0, n, off + h))
    prv = lambda off: pl.BlockSpec(blk, lambda bi, h, n: (bi, 0, jnp.maximum(n - 1, 0), off + h))
    const = lambda shp: pl.BlockSpec(shp, lambda bi, h, n: (0, 0, 0))
    return pl.pallas_call(
        _attn_kernel,
        grid=(b, N_HEADS_A, n_sb),
        in_specs=[cur(0), cur(kcol), prv(kcol), cur(vcol), prv(vcol),
                  const((2, 128, 256)), const((2, 128, 256)), const((2, 256, 512))],
        out_specs=pl.BlockSpec((RES * SLAB, LANES), lambda bi, h, n: (bi * n_sb + n, h)),
        out_shape=jax.ShapeDtypeStruct((b * rows * RES, WIDTH_A), jnp.bfloat16),
        scratch_shapes=[pltpu.VMEM((RES * SLAB, LANES), jnp.float32)],
        compiler_params=_cparams(("arbitrary", "arbitrary", "arbitrary")),
        name="prompt_attn",
    )(qkv16, qkv16, qkv16, qkv16, qkv16, _band_bias(16), _band_bias(4), _band_bias(1))


DN_STEP = 256
CARRY = 8


def _split_bf16(x, n):
    parts, r = [], x
    for _ in range(n):
        hi = r.astype(jnp.bfloat16)
        parts.append(hi)
        r = r - hi.astype(jnp.float32)
    return parts


def _bdot(a, b):
    return jnp.dot(a.astype(jnp.bfloat16), b.astype(jnp.bfloat16), preferred_element_type=jnp.float32)


def _softplus(x):
    return jnp.maximum(x, 0.0) + jnp.log1p(jnp.exp(-jnp.abs(x)))


def _silu(x):
    return x * jax.nn.sigmoid(x)


def _unit_lower_inverses(a_list):
    n = a_list[0].shape[0]
    row = lax.broadcasted_iota(jnp.int32, (n, n), 0)
    col = lax.broadcasted_iota(jnp.int32, (n, n), 1)
    eye = jnp.where(row == col, 1.0, 0.0)
    xs = [eye - a for a in a_list]
    ps = [_bdot(a, a) for a in a_list]
    k = 2
    while True:
        xs = [x + _bdot(x, p) for x, p in zip(xs, ps)]
        k *= 2
        if k >= n:
            return xs
        ps = [_bdot(p, p) for p in ps]


def _dn_kernel(qb_ref, kb_ref, vb_ref, tail_ref, tailt_ref, cw_ref, prow_ref, pcol_ref, onw_ref,
               o_ref, s_ref, xe_ref):
    c = pl.program_id(1)
    f32, bf16 = jnp.float32, jnp.bfloat16
    tb, n, hh = DN_STEP, CHUNK, N_HEADS_B
    n_ch = tb // n

    @pl.when(c == 0)
    def _():
        s_ref[...] = jnp.zeros_like(s_ref)
        xe_ref[0:CARRY, :] = jnp.zeros((CARRY, CONV_CH), f32)

    @pl.when(c > 0)
    def _():
        xe_ref[0:CARRY, :] = xe_ref[tb:tb + CARRY, :]

    conv = []
    for part, ref in enumerate((qb_ref, kb_ref, vb_ref)):
        cs = slice(part * WIDTH_B, (part + 1) * WIDTH_B)
        xe_ref[CARRY:CARRY + tb, cs] = ref[...].astype(f32)
        y = xe_ref[CARRY:CARRY + tb, cs] * cw_ref[CONV_WIDTH - 1:CONV_WIDTH, cs]
        for i in range(CONV_WIDTH - 1):
            off = CARRY - (CONV_WIDTH - 1) + i
            y = y + xe_ref[off:off + tb, cs] * cw_ref[i:i + 1, cs]
        conv.append(_silu(y))
    qc, kc, vc = conv

    t = tail_ref[...]
    beta_c = jax.nn.sigmoid(t)
    g_c = -jnp.exp(prow_ref[0:1]) * _softplus(t + prow_ref[1:2])
    tt = tailt_ref[...]
    g_r = -jnp.exp(pcol_ref[:, 0:1]) * _softplus(tt + pcol_ref[:, 1:2])

    row = lax.broadcasted_iota(jnp.int32, (n, n), 0)
    col = lax.broadcasted_iota(jnp.int32, (n, n), 1)
    incl = row >= col
    strict = row > col
    ltri = jnp.where(incl, 1.0, 0.0).astype(bf16)
    utri = jnp.where(row <= col, 1.0, 0.0).astype(bf16)
    dot = functools.partial(jnp.dot, preferred_element_type=f32)

    gcs, grs, eg_c, ed_c, glast = [], [], [], [], []
    for ci in range(n_ch):
        rs = slice(ci * n, (ci + 1) * n)
        gc = sum(dot(ltri, part) for part in _split_bf16(g_c[rs], 3))
        gr = sum(dot(part, utri) for part in _split_bf16(g_r[:, rs], 3))
        gl = gc[n - 1:n, :]
        gcs.append(gc)
        grs.append(gr)
        glast.append(gl)
        eg_c.append(jnp.exp(gc))
        ed_c.append(jnp.exp(gl - gc))

    ids = [(ci, h) for ci in range(n_ch) for h in range(hh)]

    def head_slice(x, ci, h):
        return x[ci * n:(ci + 1) * n, h * DK_B:(h + 1) * DK_B]

    def lane_col(x, lane):
        return x[:, lane:lane + 1]

    qs, ks, kbetas, vbetas, egs = [], [], [], [], []
    for ci, h in ids:
        q = head_slice(qc, ci, h)
        k = head_slice(kc, ci, h)
        v = head_slice(vc, ci, h)
        q = q * (lax.rsqrt(jnp.sum(q * q, axis=-1, keepdims=True) + EPS) * (DK_B ** -0.5))
        k = k * lax.rsqrt(jnp.sum(k * k, axis=-1, keepdims=True) + EPS)
        beta = lane_col(beta_c[ci * n:(ci + 1) * n], h)
        qs.append(q)
        ks.append(k)
        kbetas.append(k * beta)
        vbetas.append(v * beta)
        egs.append(lane_col(eg_c[ci], hh + h))

    kqs = [_nt_dot(jnp.concatenate([kb, q], axis=0).astype(bf16), k.astype(bf16))
           for kb, q, k in zip(kbetas, qs, ks)]
    a_mats, qks = [], []
    for (ci, h), kq in zip(ids, kqs):
        gcol = lane_col(gcs[ci], hh + h)
        grow = grs[ci][hh + h:hh + h + 1, :]
        decay = jnp.exp(jnp.where(incl, gcol - grow, NEG_BIG))
        a_mats.append(jnp.where(strict, kq[:n] * decay, 0.0))
        qks.append((kq[n:] * decay).astype(bf16))
    t_mats = _unit_lower_inverses(a_mats)
    uws = [dot(tm.astype(bf16), jnp.concatenate([vb, kb * eg], axis=1).astype(bf16))
           for tm, vb, kb, eg in zip(t_mats, vbetas, kbetas, egs)]

    states = [s_ref[0, h] for h in range(hh)]
    onw = onw_ref[...]
    for ci in range(n_ch):
        base = ci * hh
        wqs = [jnp.concatenate([uws[base + h][:, DV_B:], qs[base + h] * egs[base + h]], axis=0).astype(bf16)
               for h in range(hh)]
        wss = [dot(wq, st.astype(bf16)) for wq, st in zip(wqs, states)]
        v_news = [(uws[base + h][:, :DV_B] - wss[h][:n]).astype(bf16) for h in range(hh)]
        o_in = [dot(qks[base + h], v_news[h]) for h in range(hh)]
        k_decs = [(ks[base + h] * lane_col(ed_c[ci], hh + h)).astype(bf16) for h in range(hh)]
        upds = [lax.dot_general(k_decs[h], v_news[h], (((0,), (0,)), ((), ())), preferred_element_type=f32)
                for h in range(hh)]
        e_last = jnp.exp(glast[ci])
        states = [states[h] * lane_col(e_last, hh + h) + upds[h] for h in range(hh)]
        for h in range(hh):
            o = wss[h][n:] + o_in[h]
            o = o * lax.rsqrt(jnp.mean(o * o, axis=-1, keepdims=True) + EPS) * onw
            o_ref[ci * n:(ci + 1) * n, h * DV_B:(h + 1) * DV_B] = o.astype(o_ref.dtype)
    for h in range(hh):
        s_ref[0, h] = states[h]


def _deltanet_prompt(p, tail, tail_t, conv_w, a_log, dt_bias, onorm_w, batch):
    m = p.shape[0]
    steps = m // batch // DN_STEP
    wb = WIDTH_B // 1
    qcol = REST_QKVB // wb
    zeros8 = jnp.zeros((N_HEADS_B,), jnp.float32)
    gate = jnp.stack([jnp.concatenate([zeros8, a_log.astype(jnp.float32)]),
                      jnp.concatenate([zeros8, dt_bias.astype(jnp.float32)])])
    prow = jnp.pad(gate, ((0, 6), (0, LANES - TAIL_COLS)))
    pcol = jnp.pad(gate.T, ((0, 0), (0, LANES - 2)))
    tok = lambda cb: pl.BlockSpec((DN_STEP, wb), lambda b, c: (b * steps + c, cb))
    full = lambda shp: pl.BlockSpec(shp, lambda b, c: (0,) * len(shp))
    return pl.pallas_call(
        _dn_kernel,
        grid=(batch, steps),
        in_specs=[tok(qcol), tok(qcol + 1), tok(qcol + 2),
                  pl.BlockSpec((DN_STEP, LANES), lambda b, c: (b * steps + c, 0)),
                  pl.BlockSpec((TAIL_COLS, DN_STEP), lambda b, c: (0, b * steps + c)),
                  full((CONV_WIDTH, CONV_CH)), full((8, LANES)), full((TAIL_COLS, LANES)),
                  full((1, DV_B))],
        out_specs=[pl.BlockSpec((DN_STEP, WIDTH_B), lambda b, c: (b * steps + c, 0)),
                   pl.BlockSpec((1, N_HEADS_B, DK_B, DV_B), lambda b, c: (b, 0, 0, 0))],
        out_shape=[jax.ShapeDtypeStruct((m, WIDTH_B), jnp.bfloat16),
                   jax.ShapeDtypeStruct((batch, N_HEADS_B, DK_B, DV_B), jnp.float32)],
        scratch_shapes=[pltpu.VMEM((CARRY + DN_STEP, CONV_CH), jnp.float32)],
        compiler_params=_cparams(("arbitrary", "arbitrary")),
        name="deltanet_prompt",
    )(p, p, p, tail, tail_t, conv_w.astype(jnp.float32), prow, pcol,
      onorm_w.astype(jnp.float32)[None])


def _out_kernel(oa_ref, za_ref, ob_ref, zb_ref, w_ref, g_ref, x_ref, y_ref):
    f32 = jnp.float32
    ga = (oa_ref[...].astype(f32) * _silu(za_ref[...].astype(f32))).astype(jnp.bfloat16)
    gb = (ob_ref[...].astype(f32) * _silu(zb_ref[...].astype(f32))).astype(jnp.bfloat16)
    y = (jnp.dot(ga, w_ref[:WIDTH_A, :], preferred_element_type=f32)
         + jnp.dot(gb, w_ref[WIDTH_A:, :], preferred_element_type=f32))
    y = y * lax.rsqrt(jnp.mean(y * y, axis=-1, keepdims=True) + EPS) * g_ref[...]
    y_ref[...] = x_ref[...] + y


def _output_sublayer(o_a, p, o_b, w_out, g_post, x2d, *, tm):
    m = x2d.shape[0]
    row = lambda width, cb: pl.BlockSpec((tm, width), lambda i: (i, cb))
    return pl.pallas_call(
        _out_kernel,
        grid=(m // tm,),
        in_specs=[row(WIDTH_A, 0), row(WIDTH_A, REST_ZA // WIDTH_A), row(WIDTH_B, 0),
                  row(WIDTH_B, REST_ZB // WIDTH_B),
                  pl.BlockSpec((WIDTH_A + WIDTH_B, D_MODEL), lambda i: (0, 0)),
                  pl.BlockSpec((1, D_MODEL), lambda i: (0, 0)),
                  row(D_MODEL, 0)],
        out_specs=row(D_MODEL, 0),
        out_shape=jax.ShapeDtypeStruct((m, D_MODEL), jnp.float32),
        compiler_params=_cparams(("arbitrary",)),
        name="out_proj",
    )(o_a, p, o_b, p, w_out, g_post, x2d)


def _decode_attn_kernel(q_ref, kn_ref, vn_ref, k1_ref, v1_ref, k4_ref, v4_ref, k16_ref, v16_ref, o_ref):
    f32 = jnp.float32
    q = q_ref[0].astype(f32)
    kn, vn = kn_ref[0].astype(f32), vn_ref[0].astype(f32)
    s_new = jnp.sum(q * kn, axis=-1, keepdims=True)
    scores = [jnp.sum(k_ref[0] * q[None], axis=-1, keepdims=True) for k_ref in (k1_ref, k4_ref, k16_ref)]
    m = s_new
    for s in scores:
        m = jnp.maximum(m, jnp.max(s, axis=0))
    p_new = len(DILATIONS) * jnp.exp(s_new - m)
    den = p_new
    acc = p_new * vn
    for s, v_ref in zip(scores, (v1_ref, v4_ref, v16_ref)):
        p = jnp.exp(s - m[None])
        den = den + jnp.sum(p, axis=0)
        acc = acc + jnp.sum(p * v_ref[0], axis=0)
    o_ref[0] = (acc / den).astype(o_ref.dtype)


def _decode_attention(q, k_new, v_new, cache_k, cache_v):
    b, win, h, dd = cache_k.shape
    nb = 128
    views, specs = [], []
    for window, dil in DILATIONS:
        assert window // dil == nb and win % (nb * dil) == 0
        blk_idx = win // (nb * dil) - 1
        if dil == 1:
            spec = pl.BlockSpec((1, nb, h, dd), lambda i, bi=blk_idx: (i, bi, 0, 0))
            view = lambda c: c
        else:
            spec = pl.BlockSpec((1, nb, None, h, dd), lambda i, bi=blk_idx: (i, bi, 0, 0, 0))
            view = lambda c, dil=dil: c.reshape(b, win // dil, dil, h, dd)
        specs += [spec, spec]
        views += [view(cache_k), view(cache_v)]
    tok = pl.BlockSpec((1, h, dd), lambda i: (i, 0, 0))
    return pl.pallas_call(
        _decode_attn_kernel,
        grid=(b,),
        in_specs=[tok, tok, tok] + specs,
        out_specs=tok,
        out_shape=jax.ShapeDtypeStruct((b, h, dd), jnp.bfloat16),
        compiler_params=_cparams(("arbitrary",)),
        name="decode_attn",
    )(q, k_new, v_new, *views)


DEC_BB = 4


def _decode_dn_kernel(x_ref, cb_ref, cw_ref, gate_ref, prm_ref, onw_ref, s_ref,
                      o_ref, cbo_ref, so_ref):
    f32, bf16 = jnp.float32, jnp.bfloat16
    hh = N_HEADS_B
    cw = cw_ref[...]
    zeros6 = jnp.zeros((8 - 2, DK_B), bf16)
    zeros7 = jnp.zeros((8 - 1, DV_B), bf16)
    pairs, lhss, a_s, betas, vs, qks = [], [], [], [], [], []
    for b in range(x_ref.shape[0]):
        xn = x_ref[b]
        y = xn * cw[CONV_WIDTH - 1]
        for i in range(CONV_WIDTH - 1):
            y = y + cb_ref[b, i] * cw[i]
            cbo_ref[b, i] = cb_ref[b, i + 1] if i + 1 < CONV_WIDTH - 1 else xn
        y = _silu(y)
        q, k, v = y[:hh], y[hh:2 * hh], y[2 * hh:]
        q = q * lax.rsqrt(jnp.sum(q * q, axis=-1, keepdims=True) + EPS) * (DK_B ** -0.5)
        k = k * lax.rsqrt(jnp.sum(k * k, axis=-1, keepdims=True) + EPS)
        gate = gate_ref[b]
        beta = jax.nn.sigmoid(gate[:hh])
        decay = jnp.exp(-jnp.exp(prm_ref[:hh]) * _softplus(gate[hh:] + prm_ref[hh:]))
        qk = jnp.sum(q * k, axis=-1, keepdims=True)
        for h in range(hh):
            pairs.append((b, h))
            lhss.append(jnp.concatenate([k[h:h + 1].astype(bf16), q[h:h + 1].astype(bf16), zeros6], axis=0))
            a_s.append(decay[h:h + 1])
            betas.append(beta[h:h + 1])
            vs.append(v[h:h + 1])
            qks.append(qk[h:h + 1])
    states = [s_ref[b, h] for b, h in pairs]
    kss = [jnp.dot(lhs, st.astype(bf16), preferred_element_type=f32) for lhs, st in zip(lhss, states)]
    v_news = [beta * (v - a * ks[0:1]) for beta, v, a, ks in zip(betas, vs, a_s, kss)]
    upds = [lax.dot_general(lhs, jnp.concatenate([vn.astype(bf16), zeros7], axis=0),
                            (((0,), (0,)), ((), ())), preferred_element_type=f32)
            for lhs, vn in zip(lhss, v_news)]
    for (b, h), st, a, upd in zip(pairs, states, a_s, upds):
        so_ref[b, h] = st * a[:, 0:1] + upd
    onw = onw_ref[...]
    for b in range(x_ref.shape[0]):
        o = jnp.concatenate([a_s[b * hh + h] * kss[b * hh + h][1:2] + qks[b * hh + h] * v_news[b * hh + h]
                             for h in range(hh)], axis=0)
        o = o * lax.rsqrt(jnp.mean(o * o, axis=-1, keepdims=True) + EPS) * onw
        o_ref[b] = o.astype(o_ref.dtype)


def _decode_deltanet(x_new, conv_buf, state, beta_in, a_in, conv_w, a_log, dt_bias, onorm_w):
    b = x_new.shape[0]
    f32 = jnp.float32
    g3 = CONV_CH // LANES
    hh = N_HEADS_B
    bb = DEC_BB if b % DEC_BB == 0 else 1
    gate = jnp.broadcast_to(jnp.concatenate([beta_in, a_in], axis=1).astype(f32)[:, :, None], (b, 2 * hh, LANES))
    prm = jnp.broadcast_to(jnp.concatenate([a_log, dt_bias]).astype(f32)[:, None], (2 * hh, LANES))
    full = lambda shp: pl.BlockSpec(shp, lambda i: (0,) * len(shp))
    per = lambda shp: pl.BlockSpec((bb,) + shp, lambda i: (i,) + (0,) * len(shp))
    o, cb_new, s_new = pl.pallas_call(
        _decode_dn_kernel,
        grid=(b // bb,),
        in_specs=[per((g3, LANES)), per((CONV_WIDTH - 1, g3, LANES)), full((CONV_WIDTH, g3, LANES)),
                  per((2 * hh, LANES)), full((2 * hh, LANES)), full((1, DV_B)), per((hh, DK_B, DV_B))],
        out_specs=[per((hh, DV_B)), per((CONV_WIDTH - 1, g3, LANES)), per((hh, DK_B, DV_B))],
        out_shape=[jax.ShapeDtypeStruct((b, hh, DV_B), jnp.bfloat16),
                   jax.ShapeDtypeStruct((b, CONV_WIDTH - 1, g3, LANES), f32),
                   jax.ShapeDtypeStruct((b, hh, DK_B, DV_B), f32)],
        compiler_params=_cparams(("arbitrary",)),
        name="decode_deltanet",
    )(x_new.astype(f32).reshape(b, g3, LANES), conv_buf.astype(f32).reshape(b, CONV_WIDTH - 1, g3, LANES),
      conv_w.astype(f32).reshape(CONV_WIDTH, g3, LANES), gate, prm, onorm_w.astype(f32)[None], state.astype(f32))
    return o, cb_new.reshape(b, CONV_WIDTH - 1, CONV_CH), s_new


def kernel(x_prompt, x_sample, cache_win_k, cache_win_v, state_conv, state_delta,
           g_pre, w_in, conv_w, a_log, dt_bias, onorm_w, w_out, g_post):
    f32, bf16 = jnp.float32, jnp.bfloat16
    b, s, _ = x_prompt.shape
    db, t, _ = x_sample.shape
    depth = w_in.shape[0]
    n_past = cache_win_k.shape[2]
    assert t == 1 and n_past == MAX_WINDOW and s % (RES * SLAB) == 0
    keep = min(MAX_WINDOW, s)

    cos_p, sin_p = _rope_tables(jnp.arange(s, dtype=jnp.int32))
    cos_p, sin_p = jnp.tile(cos_p, (b, 1)), jnp.tile(sin_p, (b, 1))
    cos_s, sin_s = _rope_tables(jnp.full((db,), PAST_LEN, jnp.int32))

    yp = x_prompt.reshape(b * s, D_MODEL)
    ys = x_sample.reshape(db, D_MODEL)
    outs = [[] for _ in range(8)]
    for l in range(depth):
        w_in_t = jnp.swapaxes(w_in, 1, 2)
        w_main = _to_bf16(w_in_t, l, MAIN_COLS)
        w_tail = jnp.pad(w_in_t[l, MAIN_COLS:, :], ((0, LANES - TAIL_COLS), (0, 0))).astype(bf16)
        w_o = _to_bf16(w_out, l, WIDTH_A + WIDTH_B)
        gp, go = g_pre[l].astype(f32)[None], g_post[l].astype(f32)[None]

        qkv16, kf, vf, rest, tail, tail_t = _project(yp, gp, w_main, w_tail, cos_p, sin_p,
                                                     seq=s, keep=keep, tm=1024, residue_major=True)
        o_a = _prompt_attention(qkv16)
        o_b, s_fin = _deltanet_prompt(rest, tail, tail_t, conv_w[l], a_log[l], dt_bias[l], onorm_w[l], b)
        outs[0].append(kf.reshape(b, keep, N_HEADS_A, HEAD_DIM))
        outs[1].append(vf.reshape(b, keep, N_HEADS_A, HEAD_DIM))
        n_tail = min(CONV_WIDTH - 1, s)
        tail_rows = rest.reshape(b, s, REST_COLS)[:, s - n_tail:, REST_QKVB:REST_QKVB + CONV_CH].astype(f32)
        outs[2].append(jnp.pad(tail_rows, ((0, 0), (CONV_WIDTH - 1 - n_tail, 0), (0, 0))))
        outs[3].append(s_fin)
        yp = _output_sublayer(o_a, rest, o_b, w_o, go, yp, tm=256)

        qkv_s, kf_s, vf_s, rest_s, tail_s, _ = _project(ys, gp, w_main, w_tail, cos_s, sin_s,
                                                        seq=db, keep=db, tm=db, residue_major=False)
        k_new = kf_s.reshape(db, N_HEADS_A, HEAD_DIM)
        v_new = vf_s.reshape(db, N_HEADS_A, HEAD_DIM)
        o_as = _decode_attention(qkv_s[:, :WIDTH_A].reshape(db, N_HEADS_A, HEAD_DIM), k_new, v_new,
                                 cache_win_k[l], cache_win_v[l])
        o_bs, cb_new, st_new = _decode_deltanet(
            rest_s[:, REST_QKVB:REST_QKVB + CONV_CH], state_conv[l], state_delta[l],
            tail_s[:, :N_HEADS_B], tail_s[:, N_HEADS_B:TAIL_COLS], conv_w[l], a_log[l], dt_bias[l], onorm_w[l])
        outs[4].append(k_new.reshape(db, t, N_HEADS_A, HEAD_DIM))
        outs[5].append(v_new.reshape(db, t, N_HEADS_A, HEAD_DIM))
        outs[6].append(cb_new)
        outs[7].append(st_new)
        ys = _output_sublayer(o_as.reshape(db, WIDTH_A), rest_s, o_bs.reshape(db, WIDTH_B), w_o, go, ys, tm=db)

    stk = [jnp.stack(o) for o in outs]
    return (yp.reshape(b, s, D_MODEL), ys.reshape(db, t, D_MODEL),
            stk[0], stk[1], stk[2], stk[3], stk[4], stk[5], stk[6], stk[7])
```

```python
import functools
import math

import jax
import jax.numpy as jnp
from jax import lax
from jax.experimental import pallas as pl
from jax.experimental.pallas import tpu as pltpu

D_MODEL = 2048
HEAD_DIM = 128
N_HEADS_A = 8
N_HEADS_B = 8
DK_B = 128
DV_B = 128
WIDTH_A = N_HEADS_A * HEAD_DIM
WIDTH_B = N_HEADS_B * DV_B
DILATIONS = ((128, 1), (512, 4), (2048, 16))
MAX_WINDOW = 2048
ROPE_THETA = 500000.0
ROPE_DIM = HEAD_DIM // 4
CONV_WIDTH = 4
CONV_CH = 2 * N_HEADS_B * DK_B + N_HEADS_B * DV_B
CHUNK = 64
EPS = 1e-6
PAST_LEN = 16384
MAIN_COLS = 4 * WIDTH_A + CONV_CH + WIDTH_B
TAIL_COLS = 2 * N_HEADS_B
LANES = 128
NEG_BIG = -1e30
VMEM_LIMIT = 58 * 1024 * 1024

REST_COLS = MAIN_COLS - 3 * WIDTH_A
REST_ZA, REST_QKVB, REST_ZB = 0, WIDTH_A, WIDTH_A + CONV_CH


def _cparams(sem):
    return pltpu.CompilerParams(dimension_semantics=sem, vmem_limit_bytes=VMEM_LIMIT)


def _nt_dot(a, b):
    return lax.dot_general(a, b, (((1,), (1,)), ((), ())), preferred_element_type=jnp.float32)


def _cast_kernel(x_ref, o_ref):
    o_ref[...] = x_ref[...].astype(o_ref.dtype)


def _to_bf16(w, layer, rows, tr=1024):
    cols = w.shape[2]
    return pl.pallas_call(
        _cast_kernel,
        grid=(rows // tr,),
        in_specs=[pl.BlockSpec((None, tr, cols), lambda j: (layer, j, 0))],
        out_specs=pl.BlockSpec((tr, cols), lambda j: (j, 0)),
        out_shape=jax.ShapeDtypeStruct((rows, cols), jnp.bfloat16),
        compiler_params=_cparams(("arbitrary",)),
        name="cast_bf16",
    )(w)


RES = 16
QKV_TILES = 3
SUB_COLS = 256


def _proj_kernel(*refs, tiles_per_seq, first_keep_tile, residue_major, n_dec):
    x_ref, g_ref, w_ref, wt_ref, cos_ref, sin_ref = refs[:6]
    dec_in = refs[6:6 + N_DEC_OPERANDS] if n_dec else ()
    outs = refs[6 + len(dec_in):]
    qkv_ref, kf_ref, vf_ref, p_ref, tail_ref, tailt_ref = outs[:6]
    dec_out = outs[6:7] if n_dec else ()
    h_ref, de_ref = outs[6 + len(dec_out):]
    i = pl.program_id(0)
    j = pl.program_id(1)
    tm = x_ref.shape[0]

    @pl.when(j == 0)
    def _():
        x = x_ref[...]
        y = x * lax.rsqrt(jnp.mean(x * x, axis=-1, keepdims=True) + EPS)
        h = (y * g_ref[...]).astype(jnp.bfloat16)
        h_ref[...] = h
        wt = wt_ref[...]
        tail_ref[...] = _nt_dot(h, wt)
        tailt_ref[...] = _nt_dot(wt[:TAIL_COLS], h)

    keep_rows = (i % tiles_per_seq) >= first_keep_tile
    heads_per_sub = SUB_COLS // LANES

    def sub_dot(sub):
        return _nt_dot(h_ref[...], w_ref[sub * SUB_COLS:(sub + 1) * SUB_COLS, :])

    def emit_qkv(hd, r, win_ref):
        cs = slice(hd * LANES, (hd + 1) * LANES)
        if residue_major:
            slot = hd % de_ref.shape[0]
            de_ref[slot] = r
            for res in range(RES):
                qkv_ref[0, res, :, cs] = de_ref[slot, pl.ds(res, tm // RES, stride=RES), :].astype(qkv_ref.dtype)
        else:
            qkv_ref[:, cs] = r.astype(qkv_ref.dtype)
        if win_ref is not None:
            win_ref[pl.ds(hd, tm, stride=N_HEADS_A), :] = r

    def rotary_tile(scale, win_ref):
        c = cos_ref[...]
        s = sin_ref[...]
        lane = lax.broadcasted_iota(jnp.int32, c.shape, 1)
        for sub in range(WIDTH_A // SUB_COLS):
            acc = sub_dot(sub)
            for hs in range(heads_per_sub):
                a = acc[:, hs * LANES:(hs + 1) * LANES]
                swapped = jnp.where(lane < ROPE_DIM // 2,
                                    pltpu.roll(a, LANES - ROPE_DIM // 2, 1),
                                    pltpu.roll(a, ROPE_DIM // 2, 1))
                r = a * c + swapped * s
                emit_qkv(sub * heads_per_sub + hs, r if scale is None else r * scale, win_ref)

    def plain_tile(win_ref):
        for sub in range(WIDTH_A // SUB_COLS):
            acc = sub_dot(sub)
            for hs in range(heads_per_sub):
                emit_qkv(sub * heads_per_sub + hs, acc[:, hs * LANES:(hs + 1) * LANES], win_ref)

    pl.when(j == 0)(lambda: rotary_tile(HEAD_DIM ** -0.5, None))
    pl.when((j == 1) & keep_rows)(lambda: rotary_tile(None, kf_ref))
    pl.when((j == 1) & jnp.logical_not(keep_rows))(lambda: rotary_tile(None, None))
    pl.when((j == 2) & keep_rows)(lambda: plain_tile(vf_ref))
    pl.when((j == 2) & jnp.logical_not(keep_rows))(lambda: plain_tile(None))

    def rest_tile(with_decode):
        for sub in range(WIDTH_A // SUB_COLS):
            p_ref[:, sub * SUB_COLS:(sub + 1) * SUB_COLS] = sub_dot(sub).astype(p_ref.dtype)
        if with_decode:
            _decode_attn_kernel(*dec_in, *dec_out)

    if n_dec:
        rest_step = i * (pl.num_programs(1) - QKV_TILES) + (j - QKV_TILES)
        pl.when((j >= QKV_TILES) & (rest_step < n_dec))(lambda: rest_tile(True))
        pl.when((j >= QKV_TILES) & (rest_step >= n_dec))(lambda: rest_tile(False))
    else:
        pl.when(j >= QKV_TILES)(lambda: rest_tile(False))


def _rope_tables(pos):
    half = ROPE_DIM // 2
    inv = ROPE_THETA ** (-jnp.arange(half, dtype=jnp.float32) / half)
    ang = pos.astype(jnp.float32)[:, None] * inv[None, :]
    cos, sin = jnp.cos(ang), jnp.sin(ang)
    n = pos.shape[0]
    ones = jnp.ones((n, LANES - ROPE_DIM), jnp.float32)
    c = jnp.concatenate([cos, cos, ones], axis=1)
    s = jnp.concatenate([-sin, sin, jnp.zeros_like(ones)], axis=1)
    return c, s


def _project(x2d, g_pre, w_main_t, w_tail_t, cos_t, sin_t, *, seq, keep, tm, residue_major, decode=None):
    m = x2d.shape[0]
    tn = WIDTH_A
    n_i, n_j = m // tm, MAIN_COLS // tn
    tiles_per_seq = seq // tm
    first_keep = (seq - keep) // tm
    keep_tiles = keep // tm
    assert seq % tm == 0 and keep % tm == 0 and (seq - keep) % tm == 0
    n_dec = 0
    dec_args, dec_in_specs, dec_out_specs, dec_out_shapes = [], [], [], []
    if decode is not None:
        n_dec = decode[0].shape[0]
        n_rest = n_j - QKV_TILES
        assert n_i * n_rest >= n_dec
        seq_of = lambda i, j: jnp.minimum(i * n_rest + jnp.maximum(j - QKV_TILES, 0), n_dec - 1)
        dec_args, dec_in_specs, dec_out_spec, dec_out_shape = _decode_operands(*decode, seq_of)
        dec_out_specs, dec_out_shapes = [dec_out_spec], [dec_out_shape]
    kern = functools.partial(_proj_kernel, tiles_per_seq=tiles_per_seq, first_keep_tile=first_keep,
                             residue_major=residue_major, n_dec=n_dec)
    qkv_col = lambda j: jnp.minimum(j, QKV_TILES - 1)
    if residue_major:
        assert tm % (RES * 16) == 0
        qkv_spec = pl.BlockSpec((1, RES, tm // RES, tn),
                                lambda i, j: (i // tiles_per_seq, 0, i % tiles_per_seq, qkv_col(j)))
        qkv_shape = jax.ShapeDtypeStruct((m // seq, RES, seq // RES, QKV_TILES * tn), jnp.bfloat16)
    else:
        qkv_spec = pl.BlockSpec((tm, tn), lambda i, j: (i, qkv_col(j)))
        qkv_shape = jax.ShapeDtypeStruct((m, QKV_TILES * tn), jnp.bfloat16)

    def win_index(i, j):
        il = i % tiles_per_seq
        return (i // tiles_per_seq) * keep_tiles + jnp.maximum(il - first_keep, 0), 0

    win_spec = pl.BlockSpec((tm * N_HEADS_A, LANES), win_index, pipeline_mode=pl.Buffered(1))
    win_shape = jax.ShapeDtypeStruct((m // seq * keep * N_HEADS_A, LANES), jnp.float32)

    return pl.pallas_call(
        kern,
        grid=(n_i, n_j),
        in_specs=[
            pl.BlockSpec((tm, D_MODEL), lambda i, j: (i, 0)),
            pl.BlockSpec((1, D_MODEL), lambda i, j: (0, 0)),
            pl.BlockSpec((tn, D_MODEL), lambda i, j: (j, 0)),
            pl.BlockSpec((LANES, D_MODEL), lambda i, j: (0, 0)),
            pl.BlockSpec((tm, LANES), lambda i, j: (i, 0)),
            pl.BlockSpec((tm, LANES), lambda i, j: (i, 0)),
        ] + dec_in_specs,
        out_specs=[
            qkv_spec,
            win_spec,
            win_spec,
            pl.BlockSpec((tm, tn), lambda i, j: (i, jnp.maximum(j - QKV_TILES, 0))),
            pl.BlockSpec((tm, LANES), lambda i, j: (i, 0)),
            pl.BlockSpec((TAIL_COLS, tm), lambda i, j: (0, i)),
        ] + dec_out_specs,
        out_shape=[
            qkv_shape,
            win_shape,
            win_shape,
            jax.ShapeDtypeStruct((m, REST_COLS), jnp.bfloat16),
            jax.ShapeDtypeStruct((m, LANES), jnp.float32),
            jax.ShapeDtypeStruct((TAIL_COLS, m), jnp.float32),
        ] + dec_out_shapes,
        scratch_shapes=[pltpu.VMEM((tm, D_MODEL), jnp.bfloat16),
                        pltpu.VMEM((SUB_COLS // LANES, tm, LANES), jnp.float32)],
        compiler_params=_cparams(("arbitrary", "arbitrary")),
        name="proj",
    )(x2d, g_pre, w_main_t, w_tail_t, cos_t, sin_t, *dec_args)


SLAB = 128


def _band_bias(kind):
    import numpy as np
    if kind == 16:
        nq, nk = 128, 256
        lq = np.arange(nq)[:, None]
        kap = np.arange(nk)[None, :]
        lk = kap - 128
        prev = kap < 128
    elif kind == 4:
        nq, nk = 128, 256
        rho = np.arange(nq)[:, None]
        lq = 4 * (rho % 32) + rho // 32
        kap = np.arange(nk)[None, :]
        lk = 4 * (kap % 64 - 32) + kap // 64
        prev = (kap % 64) < 32
    else:
        nq, nk = 256, 512
        rho = np.arange(nq)[:, None]
        lq = 16 * (rho % 16) + rho // 16
        kap = np.arange(nk)[None, :]
        lk = 16 * (kap % 32 - 16) + kap // 32
        prev = (kap % 32) < 16
    dist = lq - lk
    band = (dist >= 0) & (dist <= 128)
    out = np.stack([band & ~prev, band])
    return jnp.asarray(np.where(out, 0.0, NEG_BIG), dtype=jnp.float32)


PIECE = 16


def _attn_kernel(q_ref, kc_ref, kp_ref, vc_ref, vp_ref, b16_ref, b4_ref, b1_ref, o_ref, nat_ref):
    hp = jnp.minimum(pl.program_id(2), 1)
    f32 = jnp.float32

    def block(q, k, v, bias, prev):
        s = _nt_dot(q, k) + bias
        rows, keys = s.shape
        m_new = jnp.broadcast_to(jnp.max(s, axis=1, keepdims=True), (rows, LANES))
        if prev is not None:
            m_prev, l_prev, acc_prev = prev
            m_new = jnp.maximum(m_prev, m_new)
        p = jnp.exp(s - jnp.concatenate([m_new] * (keys // LANES), axis=1))
        v1 = jnp.concatenate([v, jnp.ones_like(v)], axis=1)
        pv = jnp.dot(p.astype(v.dtype), v1, preferred_element_type=f32)
        acc_new, l_new = pv[:, :LANES], pv[:, LANES:]
        if prev is not None:
            alpha = jnp.exp(m_prev - m_new)
            l_new = alpha * l_prev + l_new
            acc_new = alpha * acc_prev + acc_new
        return m_new, l_new, acc_new

    def split(x, n_parts):
        n = x.shape[0] // n_parts
        return [x[i * n:(i + 1) * n] for i in range(n_parts)]

    state = {}
    pieces = SLAB // PIECE

    for r in range(RES):
        k = jnp.concatenate([kp_ref[0, r], kc_ref[0, r]], axis=0)
        v = jnp.concatenate([vp_ref[0, r], vc_ref[0, r]], axis=0)
        res = [split(x, pieces) for x in block(q_ref[0, r], k, v, b16_ref[hp], None)]
        for a in range(pieces):
            state[(r, a)] = tuple(x[a] for x in res)

    def update(keys_, q, k, v, bias):
        prev = tuple(jnp.concatenate([state[key][i] for key in keys_], axis=0) for i in range(3))
        res = [split(x, len(keys_)) for x in block(q, k, v, bias, prev)]
        for i, key in enumerate(keys_):
            state[key] = tuple(x[i] for x in res)

    def gather(cur_ref, prev_ref, slabs, start, n):
        if start == 0:
            parts = [jnp.concatenate([prev_ref[0, r, SLAB - n:SLAB, :], cur_ref[0, r, 0:n, :]], axis=0)
                     for r in slabs]
        else:
            parts = [cur_ref[0, r, start - n:start + n, :] for r in slabs]
        return jnp.concatenate(parts, axis=0)

    for r4 in range(4):
        slabs = [r4 + 4 * i for i in range(4)]
        for jj in range(SLAB // 32):
            sl = slice(32 * jj, 32 * jj + 32)
            q = jnp.concatenate([q_ref[0, r, sl, :] for r in slabs], axis=0)
            k = gather(kc_ref, kp_ref, slabs, 32 * jj, 32)
            v = gather(vc_ref, vp_ref, slabs, 32 * jj, 32)
            keys_ = [(r, 2 * jj + a) for r in slabs for a in range(2)]
            update(keys_, q, k, v, b4_ref[hp] if jj == 0 else b4_ref[1])

    slabs = list(range(RES))
    for jj in range(pieces):
        sl = slice(PIECE * jj, PIECE * jj + PIECE)
        q = jnp.concatenate([q_ref[0, r, sl, :] for r in slabs], axis=0)
        k = gather(kc_ref, kp_ref, slabs, PIECE * jj, PIECE)
        v = gather(vc_ref, vp_ref, slabs, PIECE * jj, PIECE)
        update([(r, jj) for r in slabs], q, k, v, b1_ref[hp] if jj == 0 else b1_ref[1])

    for r in range(RES):
        acc = jnp.concatenate([state[(r, a)][2] for a in range(pieces)], axis=0)
        l = jnp.concatenate([state[(r, a)][1] for a in range(pieces)], axis=0)
        nat_ref[pl.ds(r, SLAB, stride=RES), :] = acc / l
    o_ref[...] = nat_ref[...].astype(o_ref.dtype)


def _prompt_attention(qkv16):
    b, _, rows, _ = qkv16.shape
    n_sb = rows // SLAB
    kcol, vcol = WIDTH_A // LANES, 2 * WIDTH_A // LANES
    blk = (1, RES, SLAB, LANES)
    cur = lambda off: pl.BlockSpec(blk, lambda bi, h, n: (bi, 0, n, off + h))
    prv = lambda off: pl.BlockSpec(blk, lambda bi, h, n: (bi, 0, jnp.maximum(n - 1, 0), off + h))
    const = lambda shp: pl.BlockSpec(shp, lambda bi, h, n: (0, 0, 0))
    return pl.pallas_call(
        _attn_kernel,
        grid=(b, N_HEADS_A, n_sb),
        in_specs=[cur(0), cur(kcol), prv(kcol), cur(vcol), prv(vcol),
                  const((2, 128, 256)), const((2, 128, 256)), const((2, 256, 512))],
        out_specs=pl.BlockSpec((RES * SLAB, LANES), lambda bi, h, n: (bi * n_sb + n, h)),
        out_shape=jax.ShapeDtypeStruct((b * rows * RES, WIDTH_A), jnp.bfloat16),
        scratch_shapes=[pltpu.VMEM((RES * SLAB, LANES), jnp.float32)],
        compiler_params=_cparams(("arbitrary", "arbitrary", "arbitrary")),
        name="prompt_attn",
    )(qkv16, qkv16, qkv16, qkv16, qkv16, _band_bias(16), _band_bias(4), _band_bias(1))


DN_STEP = 256
CARRY = 8


def _split_bf16(x, n):
    parts, r = [], x
    for _ in range(n):
        hi = r.astype(jnp.bfloat16)
        parts.append(hi)
        r = r - hi.astype(jnp.float32)
    return parts


def _bdot(a, b):
    return jnp.dot(a.astype(jnp.bfloat16), b.astype(jnp.bfloat16), preferred_element_type=jnp.float32)


def _softplus(x):
    return jnp.maximum(x, 0.0) + jnp.log1p(jnp.exp(-jnp.abs(x)))


def _silu(x):
    return x * jax.nn.sigmoid(x)


def _unit_lower_inverses(a_list):
    n = a_list[0].shape[0]
    row = lax.broadcasted_iota(jnp.int32, (n, n), 0)
    col = lax.broadcasted_iota(jnp.int32, (n, n), 1)
    eye = jnp.where(row == col, 1.0, 0.0)
    xs = [eye - a for a in a_list]
    ps = [_bdot(a, a) for a in a_list]
    k = 2
    while True:
        xs = [x + _bdot(x, p) for x, p in zip(xs, ps)]
        k *= 2
        if k >= n:
            return xs
        ps = [_bdot(p, p) for p in ps]


def _dn_kernel(qb_ref, kb_ref, vb_ref, tail_ref, tailt_ref, cw_ref, prow_ref, pcol_ref, onw_ref,
               o_ref, s_ref, xe_ref):
    c = pl.program_id(1)
    f32, bf16 = jnp.float32, jnp.bfloat16
    tb, n, hh = DN_STEP, CHUNK, N_HEADS_B
    n_ch = tb // n

    @pl.when(c == 0)
    def _():
        s_ref[...] = jnp.zeros_like(s_ref)
        xe_ref[0:CARRY, :] = jnp.zeros((CARRY, CONV_CH), f32)

    @pl.when(c > 0)
    def _():
        xe_ref[0:CARRY, :] = xe_ref[tb:tb + CARRY, :]

    conv = []
    for part, ref in enumerate((qb_ref, kb_ref, vb_ref)):
        cs = slice(part * WIDTH_B, (part + 1) * WIDTH_B)
        xe_ref[CARRY:CARRY + tb, cs] = ref[...].astype(f32)
        y = xe_ref[CARRY:CARRY + tb, cs] * cw_ref[CONV_WIDTH - 1:CONV_WIDTH, cs]
        for i in range(CONV_WIDTH - 1):
            off = CARRY - (CONV_WIDTH - 1) + i
            y = y + xe_ref[off:off + tb, cs] * cw_ref[i:i + 1, cs]
        conv.append(_silu(y))
    qc, kc, vc = conv

    t = tail_ref[...]
    beta_c = jax.nn.sigmoid(t)
    g_c = -jnp.exp(prow_ref[0:1]) * _softplus(t + prow_ref[1:2])
    tt = tailt_ref[...]
    g_r = -jnp.exp(pcol_ref[:, 0:1]) * _softplus(tt + pcol_ref[:, 1:2])

    row = lax.broadcasted_iota(jnp.int32, (n, n), 0)
    col = lax.broadcasted_iota(jnp.int32, (n, n), 1)
    incl = row >= col
    strict = row > col
    ltri = jnp.where(incl, 1.0, 0.0).astype(bf16)
    utri = jnp.where(row <= col, 1.0, 0.0).astype(bf16)
    dot = functools.partial(jnp.dot, preferred_element_type=f32)

    gcs, grs, eg_c, ed_c, glast = [], [], [], [], []
    for ci in range(n_ch):
        rs = slice(ci * n, (ci + 1) * n)
        gc = sum(dot(ltri, part) for part in _split_bf16(g_c[rs], 3))
        gr = sum(dot(part, utri) for part in _split_bf16(g_r[:, rs], 3))
        gl = gc[n - 1:n, :]
        gcs.append(gc)
        grs.append(gr)
        glast.append(gl)
        eg_c.append(jnp.exp(gc))
        ed_c.append(jnp.exp(gl - gc))

    ids = [(ci, h) for ci in range(n_ch) for h in range(hh)]

    def head_slice(x, ci, h):
        return x[ci * n:(ci + 1) * n, h * DK_B:(h + 1) * DK_B]

    def lane_col(x, lane):
        return x[:, lane:lane + 1]

    qs, ks, kbetas, vbetas, egs = [], [], [], [], []
    for ci, h in ids:
        q = head_slice(qc, ci, h)
        k = head_slice(kc, ci, h)
        v = head_slice(vc, ci, h)
        q = q * (lax.rsqrt(jnp.sum(q * q, axis=-1, keepdims=True) + EPS) * (DK_B ** -0.5))
        k = k * lax.rsqrt(jnp.sum(k * k, axis=-1, keepdims=True) + EPS)
        beta = lane_col(beta_c[ci * n:(ci + 1) * n], h)
        qs.append(q)
        ks.append(k)
        kbetas.append(k * beta)
        vbetas.append(v * beta)
        egs.append(lane_col(eg_c[ci], hh + h))

    kqs = [_nt_dot(jnp.concatenate([kb, q], axis=0).astype(bf16), k.astype(bf16))
           for kb, q, k in zip(kbetas, qs, ks)]
    a_mats, qks = [], []
    for (ci, h), kq in zip(ids, kqs):
        gcol = lane_col(gcs[ci], hh + h)
        grow = grs[ci][hh + h:hh + h + 1, :]
        decay = jnp.exp(jnp.where(incl, gcol - grow, NEG_BIG))
        a_mats.append(jnp.where(strict, kq[:n] * decay, 0.0))
        qks.append((kq[n:] * decay).astype(bf16))
    t_mats = _unit_lower_inverses(a_mats)
    uws = [dot(tm.astype(bf16), jnp.concatenate([vb, kb * eg], axis=1).astype(bf16))
           for tm, vb, kb, eg in zip(t_mats, vbetas, kbetas, egs)]

    states = [s_ref[0, h] for h in range(hh)]
    onw = onw_ref[...]
    for ci in range(n_ch):
        base = ci * hh
        wqs = [jnp.concatenate([uws[base + h][:, DV_B:], qs[base + h] * egs[base + h]], axis=0).astype(bf16)
               for h in range(hh)]
        wss = [dot(wq, st.astype(bf16)) for wq, st in zip(wqs, states)]
        v_news = [(uws[base + h][:, :DV_B] - wss[h][:n]).astype(bf16) for h in range(hh)]
        o_in = [dot(qks[base + h], v_news[h]) for h in range(hh)]
        k_decs = [(ks[base + h] * lane_col(ed_c[ci], hh + h)).astype(bf16) for h in range(hh)]
        upds = [lax.dot_general(k_decs[h], v_news[h], (((0,), (0,)), ((), ())), preferred_element_type=f32)
                for h in range(hh)]
        e_last = jnp.exp(glast[ci])
        states = [states[h] * lane_col(e_last, hh + h) + upds[h] for h in range(hh)]
        for h in range(hh):
            o = wss[h][n:] + o_in[h]
            o = o * lax.rsqrt(jnp.mean(o * o, axis=-1, keepdims=True) + EPS) * onw
            o_ref[ci * n:(ci + 1) * n, h * DV_B:(h + 1) * DV_B] = o.astype(o_ref.dtype)
    for h in range(hh):
        s_ref[0, h] = states[h]


def _deltanet_prompt(p, tail, tail_t, conv_w, a_log, dt_bias, onorm_w, batch):
    m = p.shape[0]
    steps = m // batch // DN_STEP
    wb = WIDTH_B // 1
    qcol = REST_QKVB // wb
    zeros8 = jnp.zeros((N_HEADS_B,), jnp.float32)
    gate = jnp.stack([jnp.concatenate([zeros8, a_log.astype(jnp.float32)]),
                      jnp.concatenate([zeros8, dt_bias.astype(jnp.float32)])])
    prow = jnp.pad(gate, ((0, 6), (0, LANES - TAIL_COLS)))
    pcol = jnp.pad(gate.T, ((0, 0), (0, LANES - 2)))
    tok = lambda cb: pl.BlockSpec((DN_STEP, wb), lambda b, c: (b * steps + c, cb))
    full = lambda shp: pl.BlockSpec(shp, lambda b, c: (0,) * len(shp))
    return pl.pallas_call(
        _dn_kernel,
        grid=(batch, steps),
        in_specs=[tok(qcol), tok(qcol + 1), tok(qcol + 2),
                  pl.BlockSpec((DN_STEP, LANES), lambda b, c: (b * steps + c, 0)),
                  pl.BlockSpec((TAIL_COLS, DN_STEP), lambda b, c: (0, b * steps + c)),
                  full((CONV_WIDTH, CONV_CH)), full((8, LANES)), full((TAIL_COLS, LANES)),
                  full((1, DV_B))],
        out_specs=[pl.BlockSpec((DN_STEP, WIDTH_B), lambda b, c: (b * steps + c, 0)),
                   pl.BlockSpec((1, N_HEADS_B, DK_B, DV_B), lambda b, c: (b, 0, 0, 0))],
        out_shape=[jax.ShapeDtypeStruct((m, WIDTH_B), jnp.bfloat16),
                   jax.ShapeDtypeStruct((batch, N_HEADS_B, DK_B, DV_B), jnp.float32)],
        scratch_shapes=[pltpu.VMEM((CARRY + DN_STEP, CONV_CH), jnp.float32)],
        compiler_params=_cparams(("arbitrary", "arbitrary")),
        name="deltanet_prompt",
    )(p, p, p, tail, tail_t, conv_w.astype(jnp.float32), prow, pcol,
      onorm_w.astype(jnp.float32)[None])


def _out_kernel(oa_ref, za_ref, ob_ref, zb_ref, w_ref, g_ref, x_ref, y_ref):
    f32 = jnp.float32
    ga = (oa_ref[...].astype(f32) * _silu(za_ref[...].astype(f32))).astype(jnp.bfloat16)
    gb = (ob_ref[...].astype(f32) * _silu(zb_ref[...].astype(f32))).astype(jnp.bfloat16)
    y = (jnp.dot(ga, w_ref[:WIDTH_A, :], preferred_element_type=f32)
         + jnp.dot(gb, w_ref[WIDTH_A:, :], preferred_element_type=f32))
    y = y * lax.rsqrt(jnp.mean(y * y, axis=-1, keepdims=True) + EPS) * g_ref[...]
    y_ref[...] = x_ref[...] + y


def _output_sublayer(o_a, p, o_b, w_out, g_post, x2d, *, tm):
    m = x2d.shape[0]
    row = lambda width, cb: pl.BlockSpec((tm, width), lambda i: (i, cb))
    return pl.pallas_call(
        _out_kernel,
        grid=(m // tm,),
        in_specs=[row(WIDTH_A, 0), row(WIDTH_A, REST_ZA // WIDTH_A), row(WIDTH_B, 0),
                  row(WIDTH_B, REST_ZB // WIDTH_B),
                  pl.BlockSpec((WIDTH_A + WIDTH_B, D_MODEL), lambda i: (0, 0)),
                  pl.BlockSpec((1, D_MODEL), lambda i: (0, 0)),
                  row(D_MODEL, 0)],
        out_specs=row(D_MODEL, 0),
        out_shape=jax.ShapeDtypeStruct((m, D_MODEL), jnp.float32),
        compiler_params=_cparams(("arbitrary",)),
        name="out_proj",
    )(o_a, p, o_b, p, w_out, g_post, x2d)


def _decode_attn_kernel(q_ref, kn_ref, vn_ref, k1_ref, v1_ref, k4_ref, v4_ref, k16_ref, v16_ref, o_ref):
    f32 = jnp.float32
    q = q_ref[0].astype(f32)
    kn, vn = kn_ref[0].astype(f32), vn_ref[0].astype(f32)
    s_new = jnp.sum(q * kn, axis=-1, keepdims=True)
    scores = [jnp.sum(k_ref[0] * q[None], axis=-1, keepdims=True) for k_ref in (k1_ref, k4_ref, k16_ref)]
    m = s_new
    for s in scores:
        m = jnp.maximum(m, jnp.max(s, axis=0))
    p_new = len(DILATIONS) * jnp.exp(s_new - m)
    den = p_new
    acc = p_new * vn
    for s, v_ref in zip(scores, (v1_ref, v4_ref, v16_ref)):
        p = jnp.exp(s - m[None])
        den = den + jnp.sum(p, axis=0)
        acc = acc + jnp.sum(p * v_ref[0], axis=0)
    o_ref[0] = (acc / den).astype(o_ref.dtype)


N_DEC_OPERANDS = 9


def _decode_operands(q, k_new, v_new, cache_k, cache_v, seq_of):
    b, win, h, dd = cache_k.shape
    nb = 128
    views, specs = [], []
    for window, dil in DILATIONS:
        assert window // dil == nb and win % (nb * dil) == 0
        blk = win // (nb * dil) - 1
        if dil == 1:
            spec = pl.BlockSpec((1, nb, h, dd), lambda *ids, blk=blk: (seq_of(*ids), blk, 0, 0))
            view = lambda c: c
        else:
            spec = pl.BlockSpec((1, nb, None, h, dd), lambda *ids, blk=blk: (seq_of(*ids), blk, 0, 0, 0))
            view = lambda c, dil=dil: c.reshape(b, win // dil, dil, h, dd)
        specs += [spec, spec]
        views += [view(cache_k), view(cache_v)]
    tok = pl.BlockSpec((1, h, dd), lambda *ids: (seq_of(*ids), 0, 0))
    return ([q, k_new, v_new] + views, [tok, tok, tok] + specs, tok,
            jax.ShapeDtypeStruct((b, h, dd), jnp.bfloat16))


DEC_BB = 4


def _decode_dn_kernel(x_ref, cb_ref, cw_ref, gate_ref, prm_ref, onw_ref, s_ref,
                      o_ref, cbo_ref, so_ref):
    f32, bf16 = jnp.float32, jnp.bfloat16
    hh = N_HEADS_B
    cw = cw_ref[...]
    zeros6 = jnp.zeros((8 - 2, DK_B), bf16)
    zeros7 = jnp.zeros((8 - 1, DV_B), bf16)
    pairs, lhss, a_s, betas, vs, qks = [], [], [], [], [], []
    for b in range(x_ref.shape[0]):
        xn = x_ref[b]
        y = xn * cw[CONV_WIDTH - 1]
        for i in range(CONV_WIDTH - 1):
            y = y + cb_ref[b, i] * cw[i]
            cbo_ref[b, i] = cb_ref[b, i + 1] if i + 1 < CONV_WIDTH - 1 else xn
        y = _silu(y)
        q, k, v = y[:hh], y[hh:2 * hh], y[2 * hh:]
        q = q * lax.rsqrt(jnp.sum(q * q, axis=-1, keepdims=True) + EPS) * (DK_B ** -0.5)
        k = k * lax.rsqrt(jnp.sum(k * k, axis=-1, keepdims=True) + EPS)
        gate = gate_ref[b]
        beta = jax.nn.sigmoid(gate[:hh])
        decay = jnp.exp(-jnp.exp(prm_ref[:hh]) * _softplus(gate[hh:] + prm_ref[hh:]))
        qk = jnp.sum(q * k, axis=-1, keepdims=True)
        for h in range(hh):
            pairs.append((b, h))
            lhss.append(jnp.concatenate([k[h:h + 1].astype(bf16), q[h:h + 1].astype(bf16), zeros6], axis=0))
            a_s.append(decay[h:h + 1])
            betas.append(beta[h:h + 1])
            vs.append(v[h:h + 1])
            qks.append(qk[h:h + 1])
    states = [s_ref[b, h] for b, h in pairs]
    kss = [jnp.dot(lhs, st.astype(bf16), preferred_element_type=f32) for lhs, st in zip(lhss, states)]
    v_news = [beta * (v - a * ks[0:1]) for beta, v, a, ks in zip(betas, vs, a_s, kss)]
    upds = [lax.dot_general(lhs, jnp.concatenate([vn.astype(bf16), zeros7], axis=0),
                            (((0,), (0,)), ((), ())), preferred_element_type=f32)
            for lhs, vn in zip(lhss, v_news)]
    for (b, h), st, a, upd in zip(pairs, states, a_s, upds):
        so_ref[b, h] = st * a[:, 0:1] + upd
    onw = onw_ref[...]
    for b in range(x_ref.shape[0]):
        o = jnp.concatenate([a_s[b * hh + h] * kss[b * hh + h][1:2] + qks[b * hh + h] * v_news[b * hh + h]
                             for h in range(hh)], axis=0)
        o = o * lax.rsqrt(jnp.mean(o * o, axis=-1, keepdims=True) + EPS) * onw
        o_ref[b] = o.astype(o_ref.dtype)


def _decode_deltanet(x_new, conv_buf, state, beta_in, a_in, conv_w, a_log, dt_bias, onorm_w):
    b = x_new.shape[0]
    f32 = jnp.float32
    g3 = CONV_CH // LANES
    hh = N_HEADS_B
    bb = DEC_BB if b % DEC_BB == 0 else 1
    gate = jnp.broadcast_to(jnp.concatenate([beta_in, a_in], axis=1).astype(f32)[:, :, None], (b, 2 * hh, LANES))
    prm = jnp.broadcast_to(jnp.concatenate([a_log, dt_bias]).astype(f32)[:, None], (2 * hh, LANES))
    full = lambda shp: pl.BlockSpec(shp, lambda i: (0,) * len(shp))
    per = lambda shp: pl.BlockSpec((bb,) + shp, lambda i: (i,) + (0,) * len(shp))
    o, cb_new, s_new = pl.pallas_call(
        _decode_dn_kernel,
        grid=(b // bb,),
        in_specs=[per((g3, LANES)), per((CONV_WIDTH - 1, g3, LANES)), full((CONV_WIDTH, g3, LANES)),
                  per((2 * hh, LANES)), full((2 * hh, LANES)), full((1, DV_B)), per((hh, DK_B, DV_B))],
        out_specs=[per((hh, DV_B)), per((CONV_WIDTH - 1, g3, LANES)), per((hh, DK_B, DV_B))],
        out_shape=[jax.ShapeDtypeStruct((b, hh, DV_B), jnp.bfloat16),
                   jax.ShapeDtypeStruct((b, CONV_WIDTH - 1, g3, LANES), f32),
                   jax.ShapeDtypeStruct((b, hh, DK_B, DV_B), f32)],
        compiler_params=_cparams(("arbitrary",)),
        name="decode_deltanet",
    )(x_new.astype(f32).reshape(b, g3, LANES), conv_buf.astype(f32).reshape(b, CONV_WIDTH - 1, g3, LANES),
      conv_w.astype(f32).reshape(CONV_WIDTH, g3, LANES), gate, prm, onorm_w.astype(f32)[None], state.astype(f32))
    return o, cb_new.reshape(b, CONV_WIDTH - 1, CONV_CH), s_new


def kernel(x_prompt, x_sample, cache_win_k, cache_win_v, state_conv, state_delta,
           g_pre, w_in, conv_w, a_log, dt_bias, onorm_w, w_out, g_post):
    f32, bf16 = jnp.float32, jnp.bfloat16
    b, s, _ = x_prompt.shape
    db, t, _ = x_sample.shape
    depth = w_in.shape[0]
    n_past = cache_win_k.shape[2]
    assert t == 1 and n_past == MAX_WINDOW and s % (RES * SLAB) == 0
    keep = min(MAX_WINDOW, s)

    cos_p, sin_p = _rope_tables(jnp.arange(s, dtype=jnp.int32))
    cos_p, sin_p = jnp.tile(cos_p, (b, 1)), jnp.tile(sin_p, (b, 1))
    cos_s, sin_s = _rope_tables(jnp.full((db,), PAST_LEN, jnp.int32))

    yp = x_prompt.reshape(b * s, D_MODEL)
    ys = x_sample.reshape(db, D_MODEL)
    outs = [[] for _ in range(8)]
    for l in range(depth):
        w_in_t = jnp.swapaxes(w_in, 1, 2)
        w_main = _to_bf16(w_in_t, l, MAIN_COLS)
        w_tail = jnp.pad(w_in_t[l, MAIN_COLS:, :], ((0, LANES - TAIL_COLS), (0, 0))).astype(bf16)
        w_o = _to_bf16(w_out, l, WIDTH_A + WIDTH_B)
        gp, go = g_pre[l].astype(f32)[None], g_post[l].astype(f32)[None]

        qkv_s, kf_s, vf_s, rest_s, tail_s, _ = _project(ys, gp, w_main, w_tail, cos_s, sin_s,
                                                        seq=db, keep=db, tm=db, residue_major=False)
        k_new = kf_s.reshape(db, N_HEADS_A, HEAD_DIM)
        v_new = vf_s.reshape(db, N_HEADS_A, HEAD_DIM)
        q_s = qkv_s[:, :WIDTH_A].reshape(db, N_HEADS_A, HEAD_DIM)

        qkv16, kf, vf, rest, tail, tail_t, o_as = _project(
            yp, gp, w_main, w_tail, cos_p, sin_p, seq=s, keep=keep, tm=1024, residue_major=True,
            decode=(q_s, k_new, v_new, cache_win_k[l], cache_win_v[l]))
        o_a = _prompt_attention(qkv16)
        o_b, s_fin = _deltanet_prompt(rest, tail, tail_t, conv_w[l], a_log[l], dt_bias[l], onorm_w[l], b)
        outs[0].append(kf.reshape(b, keep, N_HEADS_A, HEAD_DIM))
        outs[1].append(vf.reshape(b, keep, N_HEADS_A, HEAD_DIM))
        n_tail = min(CONV_WIDTH - 1, s)
        tail_rows = rest.reshape(b, s, REST_COLS)[:, s - n_tail:, REST_QKVB:REST_QKVB + CONV_CH].astype(f32)
        outs[2].append(jnp.pad(tail_rows, ((0, 0), (CONV_WIDTH - 1 - n_tail, 0), (0, 0))))
        outs[3].append(s_fin)
        yp = _output_sublayer(o_a, rest, o_b, w_o, go, yp, tm=256)

        o_bs, cb_new, st_new = _decode_deltanet(
            rest_s[:, REST_QKVB:REST_QKVB + CONV_CH], state_conv[l], state_delta[l],
            tail_s[:, :N_HEADS_B], tail_s[:, N_HEADS_B:TAIL_COLS], conv_w[l], a_log[l], dt_bias[l], onorm_w[l])
        outs[4].append(k_new.reshape(db, t, N_HEADS_A, HEAD_DIM))
        outs[5].append(v_new.reshape(db, t, N_HEADS_A, HEAD_DIM))
        outs[6].append(cb_new)
        outs[7].append(st_new)
        ys = _output_sublayer(o_as.reshape(db, WIDTH_A), rest_s, o_bs.reshape(db, WIDTH_B), w_o, go, ys, tm=db)

    stk = [jnp.stack(o) for o in outs]
    return (yp.reshape(b, s, D_MODEL), ys.reshape(db, t, D_MODEL),
            stk[0], stk[1], stk[2], stk[3], stk[4], stk[5], stk[6], stk[7])
```

```python
import functools
import math

import jax
import jax.numpy as jnp
from jax import lax
from jax.experimental import pallas as pl
from jax.experimental.pallas import tpu as pltpu

D_MODEL = 2048
HEAD_DIM = 128
N_HEADS_A = 8
N_HEADS_B = 8
DK_B = 128
DV_B = 128
WIDTH_A = N_HEADS_A * HEAD_DIM
WIDTH_B = N_HEADS_B * DV_B
DILATIONS = ((128, 1), (512, 4), (2048, 16))
MAX_WINDOW = 2048
ROPE_THETA = 500000.0
ROPE_DIM = HEAD_DIM // 4
CONV_WIDTH = 4
CONV_CH = 2 * N_HEADS_B * DK_B + N_HEADS_B * DV_B
CHUNK = 64
EPS = 1e-6
PAST_LEN = 16384
MAIN_COLS = 4 * WIDTH_A + CONV_CH + WIDTH_B
TAIL_COLS = 2 * N_HEADS_B
LANES = 128
NEG_BIG = -1e30
VMEM_LIMIT = 58 * 1024 * 1024

REST_COLS = MAIN_COLS - 3 * WIDTH_A
REST_ZA, REST_QKVB, REST_ZB = 0, WIDTH_A, WIDTH_A + CONV_CH


def _cparams(sem):
    return pltpu.CompilerParams(dimension_semantics=sem, vmem_limit_bytes=VMEM_LIMIT)


def _nt_dot(a, b):
    return lax.dot_general(a, b, (((1,), (1,)), ((), ())), preferred_element_type=jnp.float32)


def _cast_kernel(x_ref, o_ref):
    o_ref[...] = x_ref[...].astype(o_ref.dtype)


def _to_bf16(w, layer, rows, tr=1024):
    cols = w.shape[2]
    return pl.pallas_call(
        _cast_kernel,
        grid=(rows // tr,),
        in_specs=[pl.BlockSpec((None, tr, cols), lambda j: (layer, j, 0))],
        out_specs=pl.BlockSpec((tr, cols), lambda j: (j, 0)),
        out_shape=jax.ShapeDtypeStruct((rows, cols), jnp.bfloat16),
        compiler_params=_cparams(("arbitrary",)),
        name="cast_bf16",
    )(w)


RES = 16
QKV_TILES = 3
SUB_COLS = 256


def _proj_kernel(*refs, tiles_per_seq, first_keep_tile, residue_major, n_dec):
    x_ref, g_ref, w_ref, wt_ref, cos_ref, sin_ref = refs[:6]
    dec_in = refs[6:6 + N_DEC_OPERANDS] if n_dec else ()
    outs = refs[6 + len(dec_in):]
    qkv_ref, kf_ref, vf_ref, p_ref, tail_ref, tailt_ref = outs[:6]
    dec_out = outs[6:7] if n_dec else ()
    h_ref, de_ref = outs[6 + len(dec_out):]
    i = pl.program_id(0)
    j = pl.program_id(1)
    tm = x_ref.shape[0]

    @pl.when(j == 0)
    def _():
        x = x_ref[...]
        y = x * lax.rsqrt(jnp.mean(x * x, axis=-1, keepdims=True) + EPS)
        h = (y * g_ref[...]).astype(jnp.bfloat16)
        h_ref[...] = h
        wt = wt_ref[...]
        tail_ref[...] = _nt_dot(h, wt)
        tailt_ref[...] = _nt_dot(wt[:TAIL_COLS], h)

    keep_rows = (i % tiles_per_seq) >= first_keep_tile
    heads_per_sub = SUB_COLS // LANES

    def sub_dot(sub):
        return _nt_dot(h_ref[...], w_ref[sub * SUB_COLS:(sub + 1) * SUB_COLS, :])

    def emit_qkv(hd, r, win_ref):
        cs = slice(hd * LANES, (hd + 1) * LANES)
        if residue_major:
            slot = hd % de_ref.shape[0]
            de_ref[slot] = r
            for res in range(RES):
                qkv_ref[0, res, :, cs] = de_ref[slot, pl.ds(res, tm // RES, stride=RES), :].astype(qkv_ref.dtype)
        else:
            qkv_ref[:, cs] = r.astype(qkv_ref.dtype)
        if win_ref is not None:
            win_ref[pl.ds(hd, tm, stride=N_HEADS_A), :] = r

    def rotary_tile(scale, win_ref):
        c = cos_ref[...]
        s = sin_ref[...]
        lane = lax.broadcasted_iota(jnp.int32, c.shape, 1)
        for sub in range(WIDTH_A // SUB_COLS):
            acc = sub_dot(sub)
            for hs in range(heads_per_sub):
                a = acc[:, hs * LANES:(hs + 1) * LANES]
                swapped = jnp.where(lane < ROPE_DIM // 2,
                                    pltpu.roll(a, LANES - ROPE_DIM // 2, 1),
                                    pltpu.roll(a, ROPE_DIM // 2, 1))
                r = a * c + swapped * s
                emit_qkv(sub * heads_per_sub + hs, r if scale is None else r * scale, win_ref)

    def plain_tile(win_ref):
        for sub in range(WIDTH_A // SUB_COLS):
            acc = sub_dot(sub)
            for hs in range(heads_per_sub):
                emit_qkv(sub * heads_per_sub + hs, acc[:, hs * LANES:(hs + 1) * LANES], win_ref)

    pl.when(j == 0)(lambda: rotary_tile(HEAD_DIM ** -0.5, None))
    pl.when((j == 1) & keep_rows)(lambda: rotary_tile(None, kf_ref))
    pl.when((j == 1) & jnp.logical_not(keep_rows))(lambda: rotary_tile(None, None))
    pl.when((j == 2) & keep_rows)(lambda: plain_tile(vf_ref))
    pl.when((j == 2) & jnp.logical_not(keep_rows))(lambda: plain_tile(None))

    def rest_tile(with_decode):
        for sub in range(WIDTH_A // SUB_COLS):
            p_ref[:, sub * SUB_COLS:(sub + 1) * SUB_COLS] = sub_dot(sub).astype(p_ref.dtype)
        if with_decode:
            _decode_attn_kernel(*dec_in, *dec_out)

    if n_dec:
        rest_step = i * (pl.num_programs(1) - QKV_TILES) + (j - QKV_TILES)
        pl.when((j >= QKV_TILES) & (rest_step < n_dec))(lambda: rest_tile(True))
        pl.when((j >= QKV_TILES) & (rest_step >= n_dec))(lambda: rest_tile(False))
    else:
        pl.when(j >= QKV_TILES)(lambda: rest_tile(False))


def _rope_tables(pos):
    half = ROPE_DIM // 2
    inv = ROPE_THETA ** (-jnp.arange(half, dtype=jnp.float32) / half)
    ang = pos.astype(jnp.float32)[:, None] * inv[None, :]
    cos, sin = jnp.cos(ang), jnp.sin(ang)
    n = pos.shape[0]
    ones = jnp.ones((n, LANES - ROPE_DIM), jnp.float32)
    c = jnp.concatenate([cos, cos, ones], axis=1)
    s = jnp.concatenate([-sin, sin, jnp.zeros_like(ones)], axis=1)
    return c, s


def _project(x2d, g_pre, w_main_t, w_tail_t, cos_t, sin_t, *, seq, keep, tm, residue_major, decode=None):
    m = x2d.shape[0]
    tn = WIDTH_A
    n_i, n_j = m // tm, MAIN_COLS // tn
    tiles_per_seq = seq // tm
    first_keep = (seq - keep) // tm
    keep_tiles = keep // tm
    assert seq % tm == 0 and keep % tm == 0 and (seq - keep) % tm == 0
    n_dec = 0
    dec_args, dec_in_specs, dec_out_specs, dec_out_shapes = [], [], [], []
    if decode is not None:
        n_dec = decode[0].shape[0]
        n_rest = n_j - QKV_TILES
        assert n_i * n_rest >= n_dec
        seq_of = lambda i, j: jnp.minimum(i * n_rest + jnp.maximum(j - QKV_TILES, 0), n_dec - 1)
        dec_args, dec_in_specs, dec_out_spec, dec_out_shape = _decode_operands(*decode, seq_of)
        dec_out_specs, dec_out_shapes = [dec_out_spec], [dec_out_shape]
    kern = functools.partial(_proj_kernel, tiles_per_seq=tiles_per_seq, first_keep_tile=first_keep,
                             residue_major=residue_major, n_dec=n_dec)
    qkv_col = lambda j: jnp.minimum(j, QKV_TILES - 1)
    if residue_major:
        assert tm % (RES * 16) == 0
        qkv_spec = pl.BlockSpec((1, RES, tm // RES, tn),
                                lambda i, j: (i // tiles_per_seq, 0, i % tiles_per_seq, qkv_col(j)))
        qkv_shape = jax.ShapeDtypeStruct((m // seq, RES, seq // RES, QKV_TILES * tn), jnp.bfloat16)
    else:
        qkv_spec = pl.BlockSpec((tm, tn), lambda i, j: (i, qkv_col(j)))
        qkv_shape = jax.ShapeDtypeStruct((m, QKV_TILES * tn), jnp.bfloat16)

    def win_index(i, j):
        il = i % tiles_per_seq
        return (i // tiles_per_seq) * keep_tiles + jnp.maximum(il - first_keep, 0), 0

    win_spec = pl.BlockSpec((tm * N_HEADS_A, LANES), win_index, pipeline_mode=pl.Buffered(1))
    win_shape = jax.ShapeDtypeStruct((m // seq * keep * N_HEADS_A, LANES), jnp.float32)

    return pl.pallas_call(
        kern,
        grid=(n_i, n_j),
        in_specs=[
            pl.BlockSpec((tm, D_MODEL), lambda i, j: (i, 0)),
            pl.BlockSpec((1, D_MODEL), lambda i, j: (0, 0)),
            pl.BlockSpec((tn, D_MODEL), lambda i, j: (j, 0)),
            pl.BlockSpec((LANES, D_MODEL), lambda i, j: (0, 0)),
            pl.BlockSpec((tm, LANES), lambda i, j: (i, 0)),
            pl.BlockSpec((tm, LANES), lambda i, j: (i, 0)),
        ] + dec_in_specs,
        out_specs=[
            qkv_spec,
            win_spec,
            win_spec,
            pl.BlockSpec((tm, tn), lambda i, j: (i, jnp.maximum(j - QKV_TILES, 0))),
            pl.BlockSpec((tm, LANES), lambda i, j: (i, 0)),
            pl.BlockSpec((TAIL_COLS, tm), lambda i, j: (0, i)),
        ] + dec_out_specs,
        out_shape=[
            qkv_shape,
            win_shape,
            win_shape,
            jax.ShapeDtypeStruct((m, REST_COLS), jnp.bfloat16),
            jax.ShapeDtypeStruct((m, LANES), jnp.float32),
            jax.ShapeDtypeStruct((TAIL_COLS, m), jnp.float32),
        ] + dec_out_shapes,
        scratch_shapes=[pltpu.VMEM((tm, D_MODEL), jnp.bfloat16),
                        pltpu.VMEM((SUB_COLS // LANES, tm, LANES), jnp.float32)],
        compiler_params=_cparams(("arbitrary", "arbitrary")),
        name="proj",
    )(x2d, g_pre, w_main_t, w_tail_t, cos_t, sin_t, *dec_args)


SLAB = 128


def _band_bias(kind):
    import numpy as np
    if kind == 16:
        nq, nk = 128, 256
        lq = np.arange(nq)[:, None]
        kap = np.arange(nk)[None, :]
        lk = kap - 128
        prev = kap < 128
    elif kind == 4:
        nq, nk = 128, 256
        rho = np.arange(nq)[:, None]
        lq = 4 * (rho % 32) + rho // 32
        kap = np.arange(nk)[None, :]
        lk = 4 * (kap % 64 - 32) + kap // 64
        prev = (kap % 64) < 32
    else:
        nq, nk = 256, 512
        rho = np.arange(nq)[:, None]
        lq = 16 * (rho % 16) + rho // 16
        kap = np.arange(nk)[None, :]
        lk = 16 * (kap % 32 - 16) + kap // 32
        prev = (kap % 32) < 16
    dist = lq - lk
    band = (dist >= 0) & (dist <= 128)
    out = np.stack([band & ~prev, band])
    return jnp.asarray(np.where(out, 0.0, NEG_BIG), dtype=jnp.float32)


PIECE = 16


def _attn_body(hp, q_ref, kc_ref, kp_ref, vc_ref, vp_ref, b16_ref, b4_ref, b1_ref, o_ref, nat_ref):
    f32 = jnp.float32

    def block(q, k, v, bias, prev):
        s = _nt_dot(q, k) + bias
        rows, keys = s.shape
        m_new = jnp.broadcast_to(jnp.max(s, axis=1, keepdims=True), (rows, LANES))
        if prev is not None:
            m_prev, l_prev, acc_prev = prev
            m_new = jnp.maximum(m_prev, m_new)
        p = jnp.exp(s - jnp.concatenate([m_new] * (keys // LANES), axis=1))
        v1 = jnp.concatenate([v, jnp.ones_like(v)], axis=1)
        pv = jnp.dot(p.astype(v.dtype), v1, preferred_element_type=f32)
        acc_new, l_new = pv[:, :LANES], pv[:, LANES:]
        if prev is not None:
            alpha = jnp.exp(m_prev - m_new)
            l_new = alpha * l_prev + l_new
            acc_new = alpha * acc_prev + acc_new
        return m_new, l_new, acc_new

    def split(x, n_parts):
        n = x.shape[0] // n_parts
        return [x[i * n:(i + 1) * n] for i in range(n_parts)]

    state = {}
    pieces = SLAB // PIECE

    for r in range(RES):
        k = jnp.concatenate([kp_ref[0, r], kc_ref[0, r]], axis=0)
        v = jnp.concatenate([vp_ref[0, r], vc_ref[0, r]], axis=0)
        res = [split(x, pieces) for x in block(q_ref[0, r], k, v, b16_ref[hp], None)]
        for a in range(pieces):
            state[(r, a)] = tuple(x[a] for x in res)

    def update(keys_, q, k, v, bias):
        prev = tuple(jnp.concatenate([state[key][i] for key in keys_], axis=0) for i in range(3))
        res = [split(x, len(keys_)) for x in block(q, k, v, bias, prev)]
        for i, key in enumerate(keys_):
            state[key] = tuple(x[i] for x in res)

    def gather(cur_ref, prev_ref, slabs, start, n):
        if start == 0:
            parts = [jnp.concatenate([prev_ref[0, r, SLAB - n:SLAB, :], cur_ref[0, r, 0:n, :]], axis=0)
                     for r in slabs]
        else:
            parts = [cur_ref[0, r, start - n:start + n, :] for r in slabs]
        return jnp.concatenate(parts, axis=0)

    for r4 in range(4):
        slabs = [r4 + 4 * i for i in range(4)]
        for jj in range(SLAB // 32):
            sl = slice(32 * jj, 32 * jj + 32)
            q = jnp.concatenate([q_ref[0, r, sl, :] for r in slabs], axis=0)
            k = gather(kc_ref, kp_ref, slabs, 32 * jj, 32)
            v = gather(vc_ref, vp_ref, slabs, 32 * jj, 32)
            keys_ = [(r, 2 * jj + a) for r in slabs for a in range(2)]
            update(keys_, q, k, v, b4_ref[hp] if jj == 0 else b4_ref[1])

    slabs = list(range(RES))
    for jj in range(pieces):
        sl = slice(PIECE * jj, PIECE * jj + PIECE)
        q = jnp.concatenate([q_ref[0, r, sl, :] for r in slabs], axis=0)
        k = gather(kc_ref, kp_ref, slabs, PIECE * jj, PIECE)
        v = gather(vc_ref, vp_ref, slabs, PIECE * jj, PIECE)
        update([(r, jj) for r in slabs], q, k, v, b1_ref[hp] if jj == 0 else b1_ref[1])

    for r in range(RES):
        acc = jnp.concatenate([state[(r, a)][2] for a in range(pieces)], axis=0)
        l = jnp.concatenate([state[(r, a)][1] for a in range(pieces)], axis=0)
        nat_ref[pl.ds(r, SLAB, stride=RES), :] = acc / l
    o_ref[...] = nat_ref[...].astype(o_ref.dtype)


DN_STEP = 256
CARRY = 8


def _split_bf16(x, n):
    parts, r = [], x
    for _ in range(n):
        hi = r.astype(jnp.bfloat16)
        parts.append(hi)
        r = r - hi.astype(jnp.float32)
    return parts


def _bdot(a, b):
    return jnp.dot(a.astype(jnp.bfloat16), b.astype(jnp.bfloat16), preferred_element_type=jnp.float32)


def _softplus(x):
    return jnp.maximum(x, 0.0) + jnp.log1p(jnp.exp(-jnp.abs(x)))


def _silu(x):
    return x * jax.nn.sigmoid(x)


def _unit_lower_inverses(a_list):
    n = a_list[0].shape[0]
    row = lax.broadcasted_iota(jnp.int32, (n, n), 0)
    col = lax.broadcasted_iota(jnp.int32, (n, n), 1)
    eye = jnp.where(row == col, 1.0, 0.0)
    xs = [eye - a for a in a_list]
    ps = [_bdot(a, a) for a in a_list]
    k = 2
    while True:
        xs = [x + _bdot(x, p) for x, p in zip(xs, ps)]
        k *= 2
        if k >= n:
            return xs
        ps = [_bdot(p, p) for p in ps]


def _dn_init(c, s_ref, xe_ref):
    @pl.when(c == 0)
    def _():
        s_ref[...] = jnp.zeros_like(s_ref)
        xe_ref[0:CARRY, :] = jnp.zeros((CARRY, CONV_CH), jnp.float32)

    @pl.when(c > 0)
    def _():
        xe_ref[0:CARRY, :] = xe_ref[DN_STEP:DN_STEP + CARRY, :]


def _dn_body(qb_ref, kb_ref, vb_ref, tail_ref, tailt_ref, cw_ref, prow_ref, pcol_ref, onw_ref,
             o_ref, s_ref, xe_ref):
    f32, bf16 = jnp.float32, jnp.bfloat16
    tb, n, hh = DN_STEP, CHUNK, N_HEADS_B
    n_ch = tb // n

    conv = []
    for part, ref in enumerate((qb_ref, kb_ref, vb_ref)):
        cs = slice(part * WIDTH_B, (part + 1) * WIDTH_B)
        xe_ref[CARRY:CARRY + tb, cs] = ref[...].astype(f32)
        y = xe_ref[CARRY:CARRY + tb, cs] * cw_ref[CONV_WIDTH - 1:CONV_WIDTH, cs]
        for i in range(CONV_WIDTH - 1):
            off = CARRY - (CONV_WIDTH - 1) + i
            y = y + xe_ref[off:off + tb, cs] * cw_ref[i:i + 1, cs]
        conv.append(_silu(y))
    qc, kc, vc = conv

    t = tail_ref[...]
    beta_c = jax.nn.sigmoid(t)
    g_c = -jnp.exp(prow_ref[0:1]) * _softplus(t + prow_ref[1:2])
    tt = tailt_ref[...]
    g_r = -jnp.exp(pcol_ref[:, 0:1]) * _softplus(tt + pcol_ref[:, 1:2])

    row = lax.broadcasted_iota(jnp.int32, (n, n), 0)
    col = lax.broadcasted_iota(jnp.int32, (n, n), 1)
    incl = row >= col
    strict = row > col
    ltri = jnp.where(incl, 1.0, 0.0).astype(bf16)
    utri = jnp.where(row <= col, 1.0, 0.0).astype(bf16)
    dot = functools.partial(jnp.dot, preferred_element_type=f32)

    gcs, grs, eg_c, ed_c, glast = [], [], [], [], []
    for ci in range(n_ch):
        rs = slice(ci * n, (ci + 1) * n)
        gc = sum(dot(ltri, part) for part in _split_bf16(g_c[rs], 3))
        gr = sum(dot(part, utri) for part in _split_bf16(g_r[:, rs], 3))
        gl = gc[n - 1:n, :]
        gcs.append(gc)
        grs.append(gr)
        glast.append(gl)
        eg_c.append(jnp.exp(gc))
        ed_c.append(jnp.exp(gl - gc))

    ids = [(ci, h) for ci in range(n_ch) for h in range(hh)]

    def head_slice(x, ci, h):
        return x[ci * n:(ci + 1) * n, h * DK_B:(h + 1) * DK_B]

    def lane_col(x, lane):
        return x[:, lane:lane + 1]

    qs, ks, kbetas, vbetas, egs = [], [], [], [], []
    for ci, h in ids:
        q = head_slice(qc, ci, h)
        k = head_slice(kc, ci, h)
        v = head_slice(vc, ci, h)
        q = q * (lax.rsqrt(jnp.sum(q * q, axis=-1, keepdims=True) + EPS) * (DK_B ** -0.5))
        k = k * lax.rsqrt(jnp.sum(k * k, axis=-1, keepdims=True) + EPS)
        beta = lane_col(beta_c[ci * n:(ci + 1) * n], h)
        qs.append(q)
        ks.append(k)
        kbetas.append(k * beta)
        vbetas.append(v * beta)
        egs.append(lane_col(eg_c[ci], hh + h))

    kqs = [_nt_dot(jnp.concatenate([kb, q], axis=0).astype(bf16), k.astype(bf16))
           for kb, q, k in zip(kbetas, qs, ks)]
    a_mats, qks = [], []
    for (ci, h), kq in zip(ids, kqs):
        gcol = lane_col(gcs[ci], hh + h)
        grow = grs[ci][hh + h:hh + h + 1, :]
        decay = jnp.exp(jnp.where(incl, gcol - grow, NEG_BIG))
        a_mats.append(jnp.where(strict, kq[:n] * decay, 0.0))
        qks.append((kq[n:] * decay).astype(bf16))
    t_mats = _unit_lower_inverses(a_mats)
    uws = [dot(tm.astype(bf16), jnp.concatenate([vb, kb * eg], axis=1).astype(bf16))
           for tm, vb, kb, eg in zip(t_mats, vbetas, kbetas, egs)]

    states = [s_ref[0, h] for h in range(hh)]
    onw = onw_ref[...]
    for ci in range(n_ch):
        base = ci * hh
        wqs = [jnp.concatenate([uws[base + h][:, DV_B:], qs[base + h] * egs[base + h]], axis=0).astype(bf16)
               for h in range(hh)]
        wss = [dot(wq, st.astype(bf16)) for wq, st in zip(wqs, states)]
        v_news = [(uws[base + h][:, :DV_B] - wss[h][:n]).astype(bf16) for h in range(hh)]
        o_in = [dot(qks[base + h], v_news[h]) for h in range(hh)]
        k_decs = [(ks[base + h] * lane_col(ed_c[ci], hh + h)).astype(bf16) for h in range(hh)]
        upds = [lax.dot_general(k_decs[h], v_news[h], (((0,), (0,)), ((), ())), preferred_element_type=f32)
                for h in range(hh)]
        e_last = jnp.exp(glast[ci])
        states = [states[h] * lane_col(e_last, hh + h) + upds[h] for h in range(hh)]
        for h in range(hh):
            o = wss[h][n:] + o_in[h]
            o = o * lax.rsqrt(jnp.mean(o * o, axis=-1, keepdims=True) + EPS) * onw
            o_ref[ci * n:(ci + 1) * n, h * DV_B:(h + 1) * DV_B] = o.astype(o_ref.dtype)
    for h in range(hh):
        s_ref[0, h] = states[h]


N_ATTN_IN, N_DN_IN = 8, 9


def _mixers_kernel(*refs, steps):
    s = pl.program_id(0)
    attn_in = refs[:N_ATTN_IN]
    dn_in = refs[N_ATTN_IN:N_ATTN_IN + N_DN_IN]
    wo_ref = refs[N_ATTN_IN + N_DN_IN]
    o_a_ref, o_b_ref, s_ref, wo_bf_ref, nat_ref, xe_ref = refs[N_ATTN_IN + N_DN_IN + 1:]
    n_sb = steps // N_HEADS_A
    _dn_init(s % steps, s_ref, xe_ref)
    wo_bf_ref[...] = wo_ref[...].astype(wo_bf_ref.dtype)
    _attn_body(jnp.minimum(s % n_sb, 1), *attn_in, o_a_ref, nat_ref)
    _dn_body(*dn_in, o_b_ref, s_ref, xe_ref)


def _prompt_mixers(qkv16, p, tail, tail_t, conv_w, a_log, dt_bias, onorm_w, w_out, layer):
    batch, _, rows, _ = qkv16.shape
    m = p.shape[0]
    n_sb = rows // SLAB
    steps = m // batch // DN_STEP
    n_steps = batch * steps
    assert N_HEADS_A * n_sb == steps
    wo_rows, wo_cols = w_out.shape[1:]
    assert wo_rows % (n_steps * 16) == 0
    wo_blk = wo_rows // n_steps
    kcol, vcol = WIDTH_A // LANES, 2 * WIDTH_A // LANES
    blk = (1, RES, SLAB, LANES)
    a_idx = lambda s: (s // steps, (s // n_sb) % N_HEADS_A, s % n_sb)
    cur = lambda off: pl.BlockSpec(blk, lambda s: (a_idx(s)[0], 0, a_idx(s)[2], off + a_idx(s)[1]))
    prv = lambda off: pl.BlockSpec(
        blk, lambda s: (a_idx(s)[0], 0, jnp.maximum(a_idx(s)[2] - 1, 0), off + a_idx(s)[1]))
    const = lambda shp: pl.BlockSpec(shp, lambda s: (0,) * len(shp))
    wb = WIDTH_B
    qcol = REST_QKVB // wb
    zeros8 = jnp.zeros((N_HEADS_B,), jnp.float32)
    gate = jnp.stack([jnp.concatenate([zeros8, a_log.astype(jnp.float32)]),
                      jnp.concatenate([zeros8, dt_bias.astype(jnp.float32)])])
    prow = jnp.pad(gate, ((0, 6), (0, LANES - TAIL_COLS)))
    pcol = jnp.pad(gate.T, ((0, 0), (0, LANES - 2)))
    tok = lambda cb: pl.BlockSpec((DN_STEP, wb), lambda s: (s, cb))
    return pl.pallas_call(
        functools.partial(_mixers_kernel, steps=steps),
        grid=(n_steps,),
        in_specs=[cur(0), cur(kcol), prv(kcol), cur(vcol), prv(vcol),
                  const((2, 128, 256)), const((2, 128, 256)), const((2, 256, 512)),
                  tok(qcol), tok(qcol + 1), tok(qcol + 2),
                  pl.BlockSpec((DN_STEP, LANES), lambda s: (s, 0)),
                  pl.BlockSpec((TAIL_COLS, DN_STEP), lambda s: (0, s)),
                  const((CONV_WIDTH, CONV_CH)), const((8, LANES)), const((TAIL_COLS, LANES)),
                  const((1, DV_B)),
                  pl.BlockSpec((None, wo_blk, wo_cols), lambda s: (layer, s, 0))],
        out_specs=[pl.BlockSpec((RES * SLAB, LANES),
                                lambda s: (a_idx(s)[0] * n_sb + a_idx(s)[2], a_idx(s)[1])),
                   pl.BlockSpec((DN_STEP, WIDTH_B), lambda s: (s, 0)),
                   pl.BlockSpec((1, N_HEADS_B, DK_B, DV_B), lambda s: (s // steps, 0, 0, 0)),
                   pl.BlockSpec((wo_blk, wo_cols), lambda s: (s, 0))],
        out_shape=[jax.ShapeDtypeStruct((m, WIDTH_A), jnp.bfloat16),
                   jax.ShapeDtypeStruct((m, WIDTH_B), jnp.bfloat16),
                   jax.ShapeDtypeStruct((batch, N_HEADS_B, DK_B, DV_B), jnp.float32),
                   jax.ShapeDtypeStruct((wo_rows, wo_cols), jnp.bfloat16)],
        scratch_shapes=[pltpu.VMEM((RES * SLAB, LANES), jnp.float32),
                        pltpu.VMEM((CARRY + DN_STEP, CONV_CH), jnp.float32)],
        compiler_params=_cparams(("arbitrary",)),
        name="prompt_mixers",
    )(qkv16, qkv16, qkv16, qkv16, qkv16, _band_bias(16), _band_bias(4), _band_bias(1),
      p, p, p, tail, tail_t, conv_w.astype(jnp.float32), prow, pcol, onorm_w.astype(jnp.float32)[None],
      w_out)


N_OUT_IN, N_DDN_IN, N_DDN_OUT = 7, 7, 3


def _out_kernel(*refs, with_decode):
    oa_ref, za_ref, ob_ref, zb_ref, w_ref, g_ref, x_ref = refs[:N_OUT_IN]
    if with_decode:
        ddn_in = refs[N_OUT_IN:N_OUT_IN + N_DDN_IN]
        y_ref = refs[N_OUT_IN + N_DDN_IN]
        _decode_dn_kernel(*ddn_in, *refs[N_OUT_IN + N_DDN_IN + 1:])
    else:
        y_ref = refs[N_OUT_IN]
    f32 = jnp.float32
    ga = (oa_ref[...].astype(f32) * _silu(za_ref[...].astype(f32))).astype(jnp.bfloat16)
    gb = (ob_ref[...].astype(f32) * _silu(zb_ref[...].astype(f32))).astype(jnp.bfloat16)
    y = (jnp.dot(ga, w_ref[:WIDTH_A, :], preferred_element_type=f32)
         + jnp.dot(gb, w_ref[WIDTH_A:, :], preferred_element_type=f32))
    y = y * lax.rsqrt(jnp.mean(y * y, axis=-1, keepdims=True) + EPS) * g_ref[...]
    y_ref[...] = x_ref[...] + y


def _output_sublayer(o_a, p, o_b, w_out, g_post, x2d, *, tm, decode_dn=None):
    m = x2d.shape[0]
    row = lambda width, cb: pl.BlockSpec((tm, width), lambda i: (i, cb))
    dn_args, dn_in, dn_out, dn_shapes = decode_dn if decode_dn is not None else ([], [], [], [])
    res = pl.pallas_call(
        functools.partial(_out_kernel, with_decode=decode_dn is not None),
        grid=(m // tm,),
        in_specs=[row(WIDTH_A, 0), row(WIDTH_A, REST_ZA // WIDTH_A), row(WIDTH_B, 0),
                  row(WIDTH_B, REST_ZB // WIDTH_B),
                  pl.BlockSpec((WIDTH_A + WIDTH_B, D_MODEL), lambda i: (0, 0)),
                  pl.BlockSpec((1, D_MODEL), lambda i: (0, 0)),
                  row(D_MODEL, 0)] + dn_in,
        out_specs=[row(D_MODEL, 0)] + dn_out,
        out_shape=[jax.ShapeDtypeStruct((m, D_MODEL), jnp.float32)] + dn_shapes,
        compiler_params=_cparams(("arbitrary",)),
        name="out_proj",
    )(o_a, p, o_b, p, w_out, g_post, x2d, *dn_args)
    return res if decode_dn is not None else res[0]


def _decode_attn_kernel(q_ref, kn_ref, vn_ref, k1_ref, v1_ref, k4_ref, v4_ref, k16_ref, v16_ref, o_ref):
    f32 = jnp.float32
    q = q_ref[0].astype(f32)
    kn, vn = kn_ref[0].astype(f32), vn_ref[0].astype(f32)
    s_new = jnp.sum(q * kn, axis=-1, keepdims=True)
    scores = [jnp.sum(k_ref[0] * q[None], axis=-1, keepdims=True) for k_ref in (k1_ref, k4_ref, k16_ref)]
    m = s_new
    for s in scores:
        m = jnp.maximum(m, jnp.max(s, axis=0))
    p_new = len(DILATIONS) * jnp.exp(s_new - m)
    den = p_new
    acc = p_new * vn
    for s, v_ref in zip(scores, (v1_ref, v4_ref, v16_ref)):
        p = jnp.exp(s - m[None])
        den = den + jnp.sum(p, axis=0)
        acc = acc + jnp.sum(p * v_ref[0], axis=0)
    o_ref[0] = (acc / den).astype(o_ref.dtype)


N_DEC_OPERANDS = 9


def _decode_operands(q, k_new, v_new, cache_k, cache_v, seq_of):
    b, win, h, dd = cache_k.shape
    nb = 128
    views, specs = [], []
    for window, dil in DILATIONS:
        assert window // dil == nb and win % (nb * dil) == 0
        blk = win // (nb * dil) - 1
        if dil == 1:
            spec = pl.BlockSpec((1, nb, h, dd), lambda *ids, blk=blk: (seq_of(*ids), blk, 0, 0))
            view = lambda c: c
        else:
            spec = pl.BlockSpec((1, nb, None, h, dd), lambda *ids, blk=blk: (seq_of(*ids), blk, 0, 0, 0))
            view = lambda c, dil=dil: c.reshape(b, win // dil, dil, h, dd)
        specs += [spec, spec]
        views += [view(cache_k), view(cache_v)]
    tok = pl.BlockSpec((1, h, dd), lambda *ids: (seq_of(*ids), 0, 0))
    return ([q, k_new, v_new] + views, [tok, tok, tok] + specs, tok,
            jax.ShapeDtypeStruct((b, h, dd), jnp.bfloat16))


DEC_BB = 4


def _decode_dn_kernel(x_ref, cb_ref, cw_ref, gate_ref, prm_ref, onw_ref, s_ref,
                      o_ref, cbo_ref, so_ref):
    f32, bf16 = jnp.float32, jnp.bfloat16
    hh = N_HEADS_B
    cw = cw_ref[...]
    zeros6 = jnp.zeros((8 - 2, DK_B), bf16)
    zeros7 = jnp.zeros((8 - 1, DV_B), bf16)
    pairs, lhss, a_s, betas, vs, qks = [], [], [], [], [], []
    for b in range(x_ref.shape[0]):
        xn = x_ref[b]
        y = xn * cw[CONV_WIDTH - 1]
        for i in range(CONV_WIDTH - 1):
            y = y + cb_ref[b, i] * cw[i]
            cbo_ref[b, i] = cb_ref[b, i + 1] if i + 1 < CONV_WIDTH - 1 else xn
        y = _silu(y)
        q, k, v = y[:hh], y[hh:2 * hh], y[2 * hh:]
        q = q * lax.rsqrt(jnp.sum(q * q, axis=-1, keepdims=True) + EPS) * (DK_B ** -0.5)
        k = k * lax.rsqrt(jnp.sum(k * k, axis=-1, keepdims=True) + EPS)
        gate = gate_ref[b]
        beta = jax.nn.sigmoid(gate[:hh])
        decay = jnp.exp(-jnp.exp(prm_ref[:hh]) * _softplus(gate[hh:] + prm_ref[hh:]))
        qk = jnp.sum(q * k, axis=-1, keepdims=True)
        for h in range(hh):
            pairs.append((b, h))
            lhss.append(jnp.concatenate([k[h:h + 1].astype(bf16), q[h:h + 1].astype(bf16), zeros6], axis=0))
            a_s.append(decay[h:h + 1])
            betas.append(beta[h:h + 1])
            vs.append(v[h:h + 1])
            qks.append(qk[h:h + 1])
    states = [s_ref[b, h] for b, h in pairs]
    kss = [jnp.dot(lhs, st.astype(bf16), preferred_element_type=f32) for lhs, st in zip(lhss, states)]
    v_news = [beta * (v - a * ks[0:1]) for beta, v, a, ks in zip(betas, vs, a_s, kss)]
    upds = [lax.dot_general(lhs, jnp.concatenate([vn.astype(bf16), zeros7], axis=0),
                            (((0,), (0,)), ((), ())), preferred_element_type=f32)
            for lhs, vn in zip(lhss, v_news)]
    for (b, h), st, a, upd in zip(pairs, states, a_s, upds):
        so_ref[b, h] = st * a[:, 0:1] + upd
    onw = onw_ref[...]
    for b in range(x_ref.shape[0]):
        o = jnp.concatenate([a_s[b * hh + h] * kss[b * hh + h][1:2] + qks[b * hh + h] * v_news[b * hh + h]
                             for h in range(hh)], axis=0)
        o = o * lax.rsqrt(jnp.mean(o * o, axis=-1, keepdims=True) + EPS) * onw
        o_ref[b] = o.astype(o_ref.dtype)


def _decode_dn_operands(x_new, conv_buf, state, beta_in, a_in, conv_w, a_log, dt_bias, onorm_w, bb):
    b = x_new.shape[0]
    f32 = jnp.float32
    g3 = CONV_CH // LANES
    hh = N_HEADS_B
    gate = jnp.broadcast_to(jnp.concatenate([beta_in, a_in], axis=1).astype(f32)[:, :, None], (b, 2 * hh, LANES))
    prm = jnp.broadcast_to(jnp.concatenate([a_log, dt_bias]).astype(f32)[:, None], (2 * hh, LANES))
    full = lambda shp: pl.BlockSpec(shp, lambda i: (0,) * len(shp))
    per = lambda shp: pl.BlockSpec((bb,) + shp, lambda i: (i,) + (0,) * len(shp))
    args = [x_new.astype(f32).reshape(b, g3, LANES), conv_buf.astype(f32).reshape(b, CONV_WIDTH - 1, g3, LANES),
            conv_w.astype(f32).reshape(CONV_WIDTH, g3, LANES), gate, prm, onorm_w.astype(f32)[None],
            state.astype(f32)]
    in_specs = [per((g3, LANES)), per((CONV_WIDTH - 1, g3, LANES)), full((CONV_WIDTH, g3, LANES)),
                per((2 * hh, LANES)), full((2 * hh, LANES)), full((1, DV_B)), per((hh, DK_B, DV_B))]
    out_specs = [per((hh, DV_B)), per((CONV_WIDTH - 1, g3, LANES)), per((hh, DK_B, DV_B))]
    out_shapes = [jax.ShapeDtypeStruct((b, hh, DV_B), jnp.bfloat16),
                  jax.ShapeDtypeStruct((b, CONV_WIDTH - 1, g3, LANES), f32),
                  jax.ShapeDtypeStruct((b, hh, DK_B, DV_B), f32)]
    return args, in_specs, out_specs, out_shapes


def _decode_deltanet(*operands):
    b = operands[0].shape[0]
    bb = DEC_BB if b % DEC_BB == 0 else 1
    args, in_specs, out_specs, out_shapes = _decode_dn_operands(*operands, bb)
    return pl.pallas_call(
        _decode_dn_kernel,
        grid=(b // bb,),
        in_specs=in_specs,
        out_specs=out_specs,
        out_shape=out_shapes,
        compiler_params=_cparams(("arbitrary",)),
        name="decode_deltanet",
    )(*args)


def kernel(x_prompt, x_sample, cache_win_k, cache_win_v, state_conv, state_delta,
           g_pre, w_in, conv_w, a_log, dt_bias, onorm_w, w_out, g_post):
    f32, bf16 = jnp.float32, jnp.bfloat16
    b, s, _ = x_prompt.shape
    db, t, _ = x_sample.shape
    depth = w_in.shape[0]
    n_past = cache_win_k.shape[2]
    assert t == 1 and n_past == MAX_WINDOW and s % (RES * SLAB) == 0
    keep = min(MAX_WINDOW, s)

    cos_p, sin_p = _rope_tables(jnp.arange(s, dtype=jnp.int32))
    cos_p, sin_p = jnp.tile(cos_p, (b, 1)), jnp.tile(sin_p, (b, 1))
    cos_s, sin_s = _rope_tables(jnp.full((db,), PAST_LEN, jnp.int32))

    yp = x_prompt.reshape(b * s, D_MODEL)
    ys = x_sample.reshape(db, D_MODEL)
    outs = [[] for _ in range(8)]
    for l in range(depth):
        w_in_t = jnp.swapaxes(w_in, 1, 2)
        w_main = _to_bf16(w_in_t, l, MAIN_COLS)
        w_tail = jnp.pad(w_in_t[l, MAIN_COLS:, :], ((0, LANES - TAIL_COLS), (0, 0))).astype(bf16)
        gp, go = g_pre[l].astype(f32)[None], g_post[l].astype(f32)[None]

        qkv_s, kf_s, vf_s, rest_s, tail_s, _ = _project(ys, gp, w_main, w_tail, cos_s, sin_s,
                                                        seq=db, keep=db, tm=db, residue_major=False)
        k_new = kf_s.reshape(db, N_HEADS_A, HEAD_DIM)
        v_new = vf_s.reshape(db, N_HEADS_A, HEAD_DIM)
        q_s = qkv_s[:, :WIDTH_A].reshape(db, N_HEADS_A, HEAD_DIM)

        qkv16, kf, vf, rest, tail, tail_t, o_as = _project(
            yp, gp, w_main, w_tail, cos_p, sin_p, seq=s, keep=keep, tm=1024, residue_major=True,
            decode=(q_s, k_new, v_new, cache_win_k[l], cache_win_v[l]))
        o_a, o_b, s_fin, w_o = _prompt_mixers(qkv16, rest, tail, tail_t, conv_w[l], a_log[l], dt_bias[l],
                                              onorm_w[l], w_out, l)
        outs[0].append(kf.reshape(b, keep, N_HEADS_A, HEAD_DIM))
        outs[1].append(vf.reshape(b, keep, N_HEADS_A, HEAD_DIM))
        n_tail = min(CONV_WIDTH - 1, s)
        tail_rows = rest.reshape(b, s, REST_COLS)[:, s - n_tail:, REST_QKVB:REST_QKVB + CONV_CH].astype(f32)
        outs[2].append(jnp.pad(tail_rows, ((0, 0), (CONV_WIDTH - 1 - n_tail, 0), (0, 0))))
        outs[3].append(s_fin)
        dn_operands = (rest_s[:, REST_QKVB:REST_QKVB + CONV_CH], state_conv[l], state_delta[l],
                       tail_s[:, :N_HEADS_B], tail_s[:, N_HEADS_B:TAIL_COLS], conv_w[l], a_log[l],
                       dt_bias[l], onorm_w[l])
        tm_out = 256
        if (b * s) // tm_out == db:
            yp, o_bs, cb_new, st_new = _output_sublayer(
                o_a, rest, o_b, w_o, go, yp, tm=tm_out, decode_dn=_decode_dn_operands(*dn_operands, 1))
        else:
            yp = _output_sublayer(o_a, rest, o_b, w_o, go, yp, tm=tm_out)
            o_bs, cb_new, st_new = _decode_deltanet(*dn_operands)
        cb_new = cb_new.reshape(db, CONV_WIDTH - 1, CONV_CH)
        outs[4].append(k_new.reshape(db, t, N_HEADS_A, HEAD_DIM))
        outs[5].append(v_new.reshape(db, t, N_HEADS_A, HEAD_DIM))
        outs[6].append(cb_new)
        outs[7].append(st_new)
        ys = _output_sublayer(o_as.reshape(db, WIDTH_A), rest_s, o_bs.reshape(db, WIDTH_B), w_o, go, ys, tm=db)

    stk = [jnp.stack(o) for o in outs]
    return (yp.reshape(b, s, D_MODEL), ys.reshape(db, t, D_MODEL),
            stk[0], stk[1], stk[2], stk[3], stk[4], stk[5], stk[6], stk[7])
```

```python
import functools
import math

import jax
import jax.numpy as jnp
from jax import lax
from jax.experimental import pallas as pl
from jax.experimental.pallas import tpu as pltpu

D_MODEL = 2048
HEAD_DIM = 128
N_HEADS_A = 8
N_HEADS_B = 8
DK_B = 128
DV_B = 128
WIDTH_A = N_HEADS_A * HEAD_DIM
WIDTH_B = N_HEADS_B * DV_B
DILATIONS = ((128, 1), (512, 4), (2048, 16))
MAX_WINDOW = 2048
ROPE_THETA = 500000.0
ROPE_DIM = HEAD_DIM // 4
CONV_WIDTH = 4
CONV_CH = 2 * N_HEADS_B * DK_B + N_HEADS_B * DV_B
CHUNK = 64
EPS = 1e-6
PAST_LEN = 16384
MAIN_COLS = 4 * WIDTH_A + CONV_CH + WIDTH_B
TAIL_COLS = 2 * N_HEADS_B
LANES = 128
NEG_BIG = -1e30
VMEM_LIMIT = 60 * 1024 * 1024

REST_COLS = MAIN_COLS - 3 * WIDTH_A
REST_ZA, REST_QKVB, REST_ZB = 0, WIDTH_A, WIDTH_A + CONV_CH


def _cparams(sem):
    return pltpu.CompilerParams(dimension_semantics=sem, vmem_limit_bytes=VMEM_LIMIT)


def _nt_dot(a, b):
    return lax.dot_general(a, b, (((1,), (1,)), ((), ())), preferred_element_type=jnp.float32)


RES = 16
QKV_TILES = 3
SUB_COLS = 256


def _proj_kernel(*refs, tiles_per_seq, first_keep_tile, residue_major, n_dec, cast_w):
    x_ref, g_ref, w_ref, wt_ref, cos_ref, sin_ref = refs[:6]
    dec_in = refs[6:6 + N_DEC_OPERANDS] if n_dec else ()
    outs = refs[6 + len(dec_in):]
    qkv_ref, kf_ref, vf_ref, p_ref, tail_ref, tailt_ref = outs[:6]
    dec_out = outs[6:7] if n_dec else ()
    outs = outs[6 + len(dec_out):]
    if cast_w:
        wbf_ref, outs = outs[0], outs[1:]
        wbf_ref[...] = w_ref[...].astype(wbf_ref.dtype)
        w_ref = wbf_ref
    h_ref, de_ref, mid_ref = outs
    i = pl.program_id(0)
    j = pl.program_id(1)
    tm = x_ref.shape[0]

    @pl.when(j == 0)
    def _():
        x = x_ref[...]
        y = x * lax.rsqrt(jnp.mean(x * x, axis=-1, keepdims=True) + EPS)
        h = (y * g_ref[...]).astype(jnp.bfloat16)
        h_ref[...] = h
        wt = wt_ref[...]
        tail_ref[...] = _nt_dot(h, wt)
        tailt_ref[...] = _nt_dot(wt[:TAIL_COLS], h)

    keep_rows = (i % tiles_per_seq) >= first_keep_tile
    heads_per_sub = SUB_COLS // LANES

    def sub_dot(sub):
        return _nt_dot(h_ref[...], w_ref[sub * SUB_COLS:(sub + 1) * SUB_COLS, :])

    def emit_qkv(hd, r, win_ref):
        cs = slice(hd * LANES, (hd + 1) * LANES)
        if residue_major:
            slot = hd % de_ref.shape[0]
            de_ref[slot] = r
            q4 = tm // 4
            for a in range(4):
                mid_ref[slot, a * q4:(a + 1) * q4, :] = de_ref[slot, pl.ds(a, q4, stride=4), :]
            for a in range(4):
                for bq in range(4):
                    qkv_ref[0, a + 4 * bq, :, cs] = mid_ref[
                        slot, pl.ds(a * q4 + bq, tm // RES, stride=4), :].astype(qkv_ref.dtype)
        else:
            qkv_ref[:, cs] = r.astype(qkv_ref.dtype)
        if win_ref is not None:
            win_ref[pl.ds(hd, tm, stride=N_HEADS_A), :] = r

    def rotary_tile(scale, win_ref):
        c = cos_ref[...]
        s = sin_ref[...]
        lane = lax.broadcasted_iota(jnp.int32, c.shape, 1)
        for sub in range(WIDTH_A // SUB_COLS):
            acc = sub_dot(sub)
            for hs in range(heads_per_sub):
                a = acc[:, hs * LANES:(hs + 1) * LANES]
                swapped = jnp.where(lane < ROPE_DIM // 2,
                                    pltpu.roll(a, LANES - ROPE_DIM // 2, 1),
                                    pltpu.roll(a, ROPE_DIM // 2, 1))
                r = a * c + swapped * s
                emit_qkv(sub * heads_per_sub + hs, r if scale is None else r * scale, win_ref)

    def plain_tile(win_ref):
        for sub in range(WIDTH_A // SUB_COLS):
            acc = sub_dot(sub)
            for hs in range(heads_per_sub):
                emit_qkv(sub * heads_per_sub + hs, acc[:, hs * LANES:(hs + 1) * LANES], win_ref)

    pl.when(j == 0)(lambda: rotary_tile(HEAD_DIM ** -0.5, None))
    pl.when((j == 1) & keep_rows)(lambda: rotary_tile(None, kf_ref))
    pl.when((j == 1) & jnp.logical_not(keep_rows))(lambda: rotary_tile(None, None))
    pl.when((j == 2) & keep_rows)(lambda: plain_tile(vf_ref))
    pl.when((j == 2) & jnp.logical_not(keep_rows))(lambda: plain_tile(None))

    def rest_tile(with_decode):
        for sub in range(WIDTH_A // SUB_COLS):
            p_ref[:, sub * SUB_COLS:(sub + 1) * SUB_COLS] = sub_dot(sub).astype(p_ref.dtype)
        if with_decode:
            _decode_attn_kernel(*dec_in, *dec_out)

    if n_dec:
        rest_step = i * (pl.num_programs(1) - QKV_TILES) + (j - QKV_TILES)
        pl.when((j >= QKV_TILES) & (rest_step < n_dec))(lambda: rest_tile(True))
        pl.when((j >= QKV_TILES) & (rest_step >= n_dec))(lambda: rest_tile(False))
    else:
        pl.when(j >= QKV_TILES)(lambda: rest_tile(False))


def _rope_tables(pos):
    half = ROPE_DIM // 2
    inv = ROPE_THETA ** (-jnp.arange(half, dtype=jnp.float32) / half)
    ang = pos.astype(jnp.float32)[:, None] * inv[None, :]
    cos, sin = jnp.cos(ang), jnp.sin(ang)
    n = pos.shape[0]
    ones = jnp.ones((n, LANES - ROPE_DIM), jnp.float32)
    c = jnp.concatenate([cos, cos, ones], axis=1)
    s = jnp.concatenate([-sin, sin, jnp.zeros_like(ones)], axis=1)
    return c, s


def _project(x2d, g_pre, w_main_t, w_tail_t, cos_t, sin_t, *, seq, keep, tm, residue_major, decode=None,
             cast_layer=None):
    m = x2d.shape[0]
    tn = WIDTH_A
    n_i, n_j = m // tm, MAIN_COLS // tn
    tiles_per_seq = seq // tm
    first_keep = (seq - keep) // tm
    keep_tiles = keep // tm
    assert seq % tm == 0 and keep % tm == 0 and (seq - keep) % tm == 0
    n_dec = 0
    dec_args, dec_in_specs, dec_out_specs, dec_out_shapes = [], [], [], []
    if decode is not None:
        n_dec = decode[0].shape[0]
        n_rest = n_j - QKV_TILES
        assert n_i * n_rest >= n_dec
        seq_of = lambda i, j: jnp.minimum(i * n_rest + jnp.maximum(j - QKV_TILES, 0), n_dec - 1)
        dec_args, dec_in_specs, dec_out_spec, dec_out_shape = _decode_operands(*decode, seq_of)
        dec_out_specs, dec_out_shapes = [dec_out_spec], [dec_out_shape]
    cast_w = cast_layer is not None
    w_spec = pl.BlockSpec((tn, D_MODEL), lambda i, j: (j, 0))
    cast_specs, cast_shapes = [], []
    if cast_w:
        assert n_i == 1
        cast_specs, cast_shapes = [w_spec], [jax.ShapeDtypeStruct((MAIN_COLS, D_MODEL), jnp.bfloat16)]
        w_spec = pl.BlockSpec((None, tn, D_MODEL), lambda i, j: (cast_layer, j, 0))
    kern = functools.partial(_proj_kernel, tiles_per_seq=tiles_per_seq, first_keep_tile=first_keep,
                             residue_major=residue_major, n_dec=n_dec, cast_w=cast_w)
    qkv_col = lambda j: jnp.minimum(j, QKV_TILES - 1)
    if residue_major:
        assert tm % (RES * 16) == 0
        qkv_spec = pl.BlockSpec((1, RES, tm // RES, tn),
                                lambda i, j: (i // tiles_per_seq, 0, i % tiles_per_seq, qkv_col(j)))
        qkv_shape = jax.ShapeDtypeStruct((m // seq, RES, seq // RES, QKV_TILES * tn), jnp.bfloat16)
    else:
        qkv_spec = pl.BlockSpec((tm, tn), lambda i, j: (i, qkv_col(j)))
        qkv_shape = jax.ShapeDtypeStruct((m, QKV_TILES * tn), jnp.bfloat16)

    def win_index(i, j):
        il = i % tiles_per_seq
        return (i // tiles_per_seq) * keep_tiles + jnp.maximum(il - first_keep, 0), 0

    win_spec = pl.BlockSpec((tm * N_HEADS_A, LANES), win_index, pipeline_mode=pl.Buffered(1))
    win_shape = jax.ShapeDtypeStruct((m // seq * keep * N_HEADS_A, LANES), jnp.float32)

    return pl.pallas_call(
        kern,
        grid=(n_i, n_j),
        in_specs=[
            pl.BlockSpec((tm, D_MODEL), lambda i, j: (i, 0)),
            pl.BlockSpec((1, D_MODEL), lambda i, j: (0, 0)),
            w_spec,
            pl.BlockSpec((LANES, D_MODEL), lambda i, j: (0, 0)),
            pl.BlockSpec((tm, LANES), lambda i, j: (i, 0)),
            pl.BlockSpec((tm, LANES), lambda i, j: (i, 0)),
        ] + dec_in_specs,
        out_specs=[
            qkv_spec,
            win_spec,
            win_spec,
            pl.BlockSpec((tm, tn), lambda i, j: (i, jnp.maximum(j - QKV_TILES, 0))),
            pl.BlockSpec((tm, LANES), lambda i, j: (i, 0)),
            pl.BlockSpec((TAIL_COLS, tm), lambda i, j: (0, i)),
        ] + dec_out_specs + cast_specs,
        out_shape=[
            qkv_shape,
            win_shape,
            win_shape,
            jax.ShapeDtypeStruct((m, REST_COLS), jnp.bfloat16),
            jax.ShapeDtypeStruct((m, LANES), jnp.float32),
            jax.ShapeDtypeStruct((TAIL_COLS, m), jnp.float32),
        ] + dec_out_shapes + cast_shapes,
        scratch_shapes=[pltpu.VMEM((tm, D_MODEL), jnp.bfloat16),
                        pltpu.VMEM((SUB_COLS // LANES, tm, LANES), jnp.float32),
                        pltpu.VMEM((SUB_COLS // LANES, tm, LANES), jnp.float32)],
        compiler_params=_cparams(("arbitrary", "arbitrary")),
        name="proj",
    )(x2d, g_pre, w_main_t, w_tail_t, cos_t, sin_t, *dec_args)


SLAB = 128


def _band_bias(kind):
    import numpy as np
    if kind == 16:
        nq, nk = 128, 256
        lq = np.arange(nq)[:, None]
        kap = np.arange(nk)[None, :]
        lk = kap - 128
        prev = kap < 128
    elif kind == 4:
        nq, nk = 128, 256
        rho = np.arange(nq)[:, None]
        lq = 4 * (rho % 32) + rho // 32
        kap = np.arange(nk)[None, :]
        lk = 4 * (kap % 64 - 32) + kap // 64
        prev = (kap % 64) < 32
    else:
        nq, nk = 256, 512
        rho = np.arange(nq)[:, None]
        lq = 16 * (rho % 16) + rho // 16
        kap = np.arange(nk)[None, :]
        lk = 16 * (kap % 32 - 16) + kap // 32
        prev = (kap % 32) < 16
    dist = lq - lk
    band = (dist >= 0) & (dist <= 128)
    out = np.stack([band & ~prev, band])
    return jnp.asarray(np.where(out, 0.0, NEG_BIG), dtype=jnp.float32)


PIECE = 16


def _attn_body(hp, q_ref, kc_ref, kp_ref, vc_ref, vp_ref, b16_ref, b4_ref, b1_ref, o_ref, nat_ref):
    f32 = jnp.float32

    def block(q, k, v, bias, prev):
        s = _nt_dot(q, k) + bias
        rows, keys = s.shape
        m_new = jnp.broadcast_to(jnp.max(s, axis=1, keepdims=True), (rows, LANES))
        if prev is not None:
            m_prev, l_prev, acc_prev = prev
            m_new = jnp.maximum(m_prev, m_new)
        p = jnp.exp(s - jnp.concatenate([m_new] * (keys // LANES), axis=1))
        v1 = jnp.concatenate([v, jnp.ones_like(v)], axis=1)
        pv = jnp.dot(p.astype(v.dtype), v1, preferred_element_type=f32)
        acc_new, l_new = pv[:, :LANES], pv[:, LANES:]
        if prev is not None:
            alpha = jnp.exp(m_prev - m_new)
            l_new = alpha * l_prev + l_new
            acc_new = alpha * acc_prev + acc_new
        return m_new, l_new, acc_new

    def split(x, n_parts):
        n = x.shape[0] // n_parts
        return [x[i * n:(i + 1) * n] for i in range(n_parts)]

    state = {}
    pieces = SLAB // PIECE

    for r in range(RES):
        k = jnp.concatenate([kp_ref[0, r], kc_ref[0, r]], axis=0)
        v = jnp.concatenate([vp_ref[0, r], vc_ref[0, r]], axis=0)
        res = [split(x, pieces) for x in block(q_ref[0, r], k, v, b16_ref[hp], None)]
        for a in range(pieces):
            state[(r, a)] = tuple(x[a] for x in res)

    def update(keys_, q, k, v, bias):
        prev = tuple(jnp.concatenate([state[key][i] for key in keys_], axis=0) for i in range(3))
        res = [split(x, len(keys_)) for x in block(q, k, v, bias, prev)]
        for i, key in enumerate(keys_):
            state[key] = tuple(x[i] for x in res)

    def gather(cur_ref, prev_ref, slabs, start, n):
        if start == 0:
            parts = [jnp.concatenate([prev_ref[0, r, SLAB - n:SLAB, :], cur_ref[0, r, 0:n, :]], axis=0)
                     for r in slabs]
        else:
            parts = [cur_ref[0, r, start - n:start + n, :] for r in slabs]
        return jnp.concatenate(parts, axis=0)

    for r4 in range(4):
        slabs = [r4 + 4 * i for i in range(4)]
        for jj in range(SLAB // 32):
            sl = slice(32 * jj, 32 * jj + 32)
            q = jnp.concatenate([q_ref[0, r, sl, :] for r in slabs], axis=0)
            k = gather(kc_ref, kp_ref, slabs, 32 * jj, 32)
            v = gather(vc_ref, vp_ref, slabs, 32 * jj, 32)
            keys_ = [(r, 2 * jj + a) for r in slabs for a in range(2)]
            update(keys_, q, k, v, b4_ref[hp] if jj == 0 else b4_ref[1])

    slabs = list(range(RES))
    for jj in range(pieces):
        sl = slice(PIECE * jj, PIECE * jj + PIECE)
        q = jnp.concatenate([q_ref[0, r, sl, :] for r in slabs], axis=0)
        k = gather(kc_ref, kp_ref, slabs, PIECE * jj, PIECE)
        v = gather(vc_ref, vp_ref, slabs, PIECE * jj, PIECE)
        update([(r, jj) for r in slabs], q, k, v, b1_ref[hp] if jj == 0 else b1_ref[1])

    for r in range(RES):
        acc = jnp.concatenate([state[(r, a)][2] for a in range(pieces)], axis=0)
        l = jnp.concatenate([state[(r, a)][1] for a in range(pieces)], axis=0)
        nat_ref[pl.ds(r, SLAB, stride=RES), :] = acc / l
    o_ref[...] = nat_ref[...].astype(o_ref.dtype)


DN_STEP = 256
CARRY = 8


def _split_bf16(x, n):
    parts, r = [], x
    for _ in range(n):
        hi = r.astype(jnp.bfloat16)
        parts.append(hi)
        r = r - hi.astype(jnp.float32)
    return parts


def _bdot(a, b):
    return jnp.dot(a.astype(jnp.bfloat16), b.astype(jnp.bfloat16), preferred_element_type=jnp.float32)


def _softplus(x):
    return jnp.maximum(x, 0.0) + jnp.log1p(jnp.exp(-jnp.abs(x)))


def _silu(x):
    return x * jax.nn.sigmoid(x)


def _unit_lower_inverses(a_list):
    n = a_list[0].shape[0]
    row = lax.broadcasted_iota(jnp.int32, (n, n), 0)
    col = lax.broadcasted_iota(jnp.int32, (n, n), 1)
    eye = jnp.where(row == col, 1.0, 0.0)
    xs = [eye - a for a in a_list]
    ps = [_bdot(a, a) for a in a_list]
    k = 2
    while True:
        xs = [x + _bdot(x, p) for x, p in zip(xs, ps)]
        k *= 2
        if k >= n:
            return xs
        ps = [_bdot(p, p) for p in ps]


def _dn_init(c, s_ref, xe_ref):
    @pl.when(c == 0)
    def _():
        s_ref[...] = jnp.zeros_like(s_ref)
        xe_ref[0:CARRY, :] = jnp.zeros((CARRY, CONV_CH), jnp.float32)

    @pl.when(c > 0)
    def _():
        xe_ref[0:CARRY, :] = xe_ref[DN_STEP:DN_STEP + CARRY, :]


def _dn_body(qb_ref, kb_ref, vb_ref, tail_ref, tailt_ref, cw_ref, prow_ref, pcol_ref, onw_ref,
             o_ref, s_ref, xe_ref):
    f32, bf16 = jnp.float32, jnp.bfloat16
    tb, n, hh = DN_STEP, CHUNK, N_HEADS_B
    n_ch = tb // n

    conv = []
    for part, ref in enumerate((qb_ref, kb_ref, vb_ref)):
        cs = slice(part * WIDTH_B, (part + 1) * WIDTH_B)
        xe_ref[CARRY:CARRY + tb, cs] = ref[...].astype(f32)
        y = xe_ref[CARRY:CARRY + tb, cs] * cw_ref[CONV_WIDTH - 1:CONV_WIDTH, cs]
        for i in range(CONV_WIDTH - 1):
            off = CARRY - (CONV_WIDTH - 1) + i
            y = y + xe_ref[off:off + tb, cs] * cw_ref[i:i + 1, cs]
        conv.append(_silu(y))
    qc, kc, vc = conv

    t = tail_ref[...]
    beta_c = jax.nn.sigmoid(t)
    g_c = -jnp.exp(prow_ref[0:1]) * _softplus(t + prow_ref[1:2])
    tt = tailt_ref[...]
    g_r = -jnp.exp(pcol_ref[:, 0:1]) * _softplus(tt + pcol_ref[:, 1:2])

    row = lax.broadcasted_iota(jnp.int32, (n, n), 0)
    col = lax.broadcasted_iota(jnp.int32, (n, n), 1)
    incl = row >= col
    strict = row > col
    ltri = jnp.where(incl, 1.0, 0.0).astype(bf16)
    utri = jnp.where(row <= col, 1.0, 0.0).astype(bf16)
    dot = functools.partial(jnp.dot, preferred_element_type=f32)

    gcs, grs, eg_c, ed_c, glast = [], [], [], [], []
    for ci in range(n_ch):
        rs = slice(ci * n, (ci + 1) * n)
        gc = sum(dot(ltri, part) for part in _split_bf16(g_c[rs], 3))
        gr = sum(dot(part, utri) for part in _split_bf16(g_r[:, rs], 3))
        gl = gc[n - 1:n, :]
        gcs.append(gc)
        grs.append(gr)
        glast.append(gl)
        eg_c.append(jnp.exp(gc))
        ed_c.append(jnp.exp(gl - gc))

    ids = [(ci, h) for ci in range(n_ch) for h in range(hh)]

    def head_slice(x, ci, h):
        return x[ci * n:(ci + 1) * n, h * DK_B:(h + 1) * DK_B]

    def lane_col(x, lane):
        return x[:, lane:lane + 1]

    qs, ks, kbetas, vbetas, egs = [], [], [], [], []
    for ci, h in ids:
        q = head_slice(qc, ci, h)
        k = head_slice(kc, ci, h)
        v = head_slice(vc, ci, h)
        q = q * (lax.rsqrt(jnp.sum(q * q, axis=-1, keepdims=True) + EPS) * (DK_B ** -0.5))
        k = k * lax.rsqrt(jnp.sum(k * k, axis=-1, keepdims=True) + EPS)
        beta = lane_col(beta_c[ci * n:(ci + 1) * n], h)
        qs.append(q)
        ks.append(k)
        kbetas.append(k * beta)
        vbetas.append(v * beta)
        egs.append(lane_col(eg_c[ci], hh + h))

    kqs = [_nt_dot(jnp.concatenate([kb, q], axis=0).astype(bf16), k.astype(bf16))
           for kb, q, k in zip(kbetas, qs, ks)]
    a_mats, qks = [], []
    for (ci, h), kq in zip(ids, kqs):
        gcol = lane_col(gcs[ci], hh + h)
        grow = grs[ci][hh + h:hh + h + 1, :]
        decay = jnp.exp(jnp.where(incl, gcol - grow, NEG_BIG))
        a_mats.append(jnp.where(strict, kq[:n] * decay, 0.0))
        qks.append((kq[n:] * decay).astype(bf16))
    t_mats = _unit_lower_inverses(a_mats)
    uws = [dot(tm.astype(bf16), jnp.concatenate([vb, kb * eg], axis=1).astype(bf16))
           for tm, vb, kb, eg in zip(t_mats, vbetas, kbetas, egs)]

    states = [s_ref[0, h] for h in range(hh)]
    onw = onw_ref[...]
    for ci in range(n_ch):
        base = ci * hh
        wqs = [jnp.concatenate([uws[base + h][:, DV_B:], qs[base + h] * egs[base + h]], axis=0).astype(bf16)
               for h in range(hh)]
        wss = [dot(wq, st.astype(bf16)) for wq, st in zip(wqs, states)]
        v_news = [(uws[base + h][:, :DV_B] - wss[h][:n]).astype(bf16) for h in range(hh)]
        o_in = [dot(qks[base + h], v_news[h]) for h in range(hh)]
        k_decs = [(ks[base + h] * lane_col(ed_c[ci], hh + h)).astype(bf16) for h in range(hh)]
        upds = [lax.dot_general(k_decs[h], v_news[h], (((0,), (0,)), ((), ())), preferred_element_type=f32)
                for h in range(hh)]
        e_last = jnp.exp(glast[ci])
        states = [states[h] * lane_col(e_last, hh + h) + upds[h] for h in range(hh)]
        for h in range(hh):
            o = wss[h][n:] + o_in[h]
            o = o * lax.rsqrt(jnp.mean(o * o, axis=-1, keepdims=True) + EPS) * onw
            o_ref[ci * n:(ci + 1) * n, h * DV_B:(h + 1) * DV_B] = o.astype(o_ref.dtype)
    for h in range(hh):
        s_ref[0, h] = states[h]


N_ATTN_IN, N_DN_IN = 8, 9


def _mixers_kernel(*refs, steps):
    s = pl.program_id(0)
    attn_in = refs[:N_ATTN_IN]
    dn_in = refs[N_ATTN_IN:N_ATTN_IN + N_DN_IN]
    wo_ref = refs[N_ATTN_IN + N_DN_IN]
    o_a_ref, o_b_ref, s_ref, wo_bf_ref, nat_ref, xe_ref = refs[N_ATTN_IN + N_DN_IN + 1:]
    n_sb = steps // N_HEADS_A
    _dn_init(s % steps, s_ref, xe_ref)
    wo_bf_ref[...] = wo_ref[...].astype(wo_bf_ref.dtype)
    _attn_body(jnp.minimum(s % n_sb, 1), *attn_in, o_a_ref, nat_ref)
    _dn_body(*dn_in, o_b_ref, s_ref, xe_ref)


def _prompt_mixers(qkv16, p, tail, tail_t, conv_w, a_log, dt_bias, onorm_w, w_out, layer):
    batch, _, rows, _ = qkv16.shape
    m = p.shape[0]
    n_sb = rows // SLAB
    steps = m // batch // DN_STEP
    n_steps = batch * steps
    assert N_HEADS_A * n_sb == steps
    wo_rows, wo_cols = w_out.shape[1:]
    assert wo_rows % (n_steps * 16) == 0
    wo_blk = wo_rows // n_steps
    kcol, vcol = WIDTH_A // LANES, 2 * WIDTH_A // LANES
    blk = (1, RES, SLAB, LANES)
    a_idx = lambda s: (s // steps, (s // n_sb) % N_HEADS_A, s % n_sb)
    cur = lambda off: pl.BlockSpec(blk, lambda s: (a_idx(s)[0], 0, a_idx(s)[2], off + a_idx(s)[1]))
    prv = lambda off: pl.BlockSpec(
        blk, lambda s: (a_idx(s)[0], 0, jnp.maximum(a_idx(s)[2] - 1, 0), off + a_idx(s)[1]))
    const = lambda shp: pl.BlockSpec(shp, lambda s: (0,) * len(shp))
    wb = WIDTH_B
    qcol = REST_QKVB // wb
    zeros8 = jnp.zeros((N_HEADS_B,), jnp.float32)
    gate = jnp.stack([jnp.concatenate([zeros8, a_log.astype(jnp.float32)]),
                      jnp.concatenate([zeros8, dt_bias.astype(jnp.float32)])])
    prow = jnp.pad(gate, ((0, 6), (0, LANES - TAIL_COLS)))
    pcol = jnp.pad(gate.T, ((0, 0), (0, LANES - 2)))
    tok = lambda cb: pl.BlockSpec((DN_STEP, wb), lambda s: (s, cb))
    return pl.pallas_call(
        functools.partial(_mixers_kernel, steps=steps),
        grid=(n_steps,),
        in_specs=[cur(0), cur(kcol), prv(kcol), cur(vcol), prv(vcol),
                  const((2, 128, 256)), const((2, 128, 256)), const((2, 256, 512)),
                  tok(qcol), tok(qcol + 1), tok(qcol + 2),
                  pl.BlockSpec((DN_STEP, LANES), lambda s: (s, 0)),
                  pl.BlockSpec((TAIL_COLS, DN_STEP), lambda s: (0, s)),
                  const((CONV_WIDTH, CONV_CH)), const((8, LANES)), const((TAIL_COLS, LANES)),
                  const((1, DV_B)),
                  pl.BlockSpec((None, wo_blk, wo_cols), lambda s: (layer, s, 0))],
        out_specs=[pl.BlockSpec((RES * SLAB, LANES),
                                lambda s: (a_idx(s)[0] * n_sb + a_idx(s)[2], a_idx(s)[1])),
                   pl.BlockSpec((DN_STEP, WIDTH_B), lambda s: (s, 0)),
                   pl.BlockSpec((1, N_HEADS_B, DK_B, DV_B), lambda s: (s // steps, 0, 0, 0)),
                   pl.BlockSpec((wo_blk, wo_cols), lambda s: (s, 0))],
        out_shape=[jax.ShapeDtypeStruct((m, WIDTH_A), jnp.bfloat16),
                   jax.ShapeDtypeStruct((m, WIDTH_B), jnp.bfloat16),
                   jax.ShapeDtypeStruct((batch, N_HEADS_B, DK_B, DV_B), jnp.float32),
                   jax.ShapeDtypeStruct((wo_rows, wo_cols), jnp.bfloat16)],
        scratch_shapes=[pltpu.VMEM((RES * SLAB, LANES), jnp.float32),
                        pltpu.VMEM((CARRY + DN_STEP, CONV_CH), jnp.float32)],
        compiler_params=_cparams(("arbitrary",)),
        name="prompt_mixers",
    )(qkv16, qkv16, qkv16, qkv16, qkv16, _band_bias(16), _band_bias(4), _band_bias(1),
      p, p, p, tail, tail_t, conv_w.astype(jnp.float32), prow, pcol, onorm_w.astype(jnp.float32)[None],
      w_out)


N_OUT_IN, N_DDN_IN, N_DDN_OUT = 7, 7, 3


def _out_kernel(*refs, with_decode):
    oa_ref, za_ref, ob_ref, zb_ref, w_ref, g_ref, x_ref = refs[:N_OUT_IN]
    if with_decode:
        ddn_in = refs[N_OUT_IN:N_OUT_IN + N_DDN_IN]
        y_ref = refs[N_OUT_IN + N_DDN_IN]
        _decode_dn_kernel(*ddn_in, *refs[N_OUT_IN + N_DDN_IN + 1:])
    else:
        y_ref = refs[N_OUT_IN]
    f32 = jnp.float32
    ga = (oa_ref[...].astype(f32) * _silu(za_ref[...].astype(f32))).astype(jnp.bfloat16)
    gb = (ob_ref[...].astype(f32) * _silu(zb_ref[...].astype(f32))).astype(jnp.bfloat16)
    y = (jnp.dot(ga, w_ref[:WIDTH_A, :], preferred_element_type=f32)
         + jnp.dot(gb, w_ref[WIDTH_A:, :], preferred_element_type=f32))
    y = y * lax.rsqrt(jnp.mean(y * y, axis=-1, keepdims=True) + EPS) * g_ref[...]
    y_ref[...] = x_ref[...] + y


def _output_sublayer(o_a, p, o_b, w_out, g_post, x2d, *, tm, decode_dn=None):
    m = x2d.shape[0]
    row = lambda width, cb: pl.BlockSpec((tm, width), lambda i: (i, cb))
    dn_args, dn_in, dn_out, dn_shapes = decode_dn if decode_dn is not None else ([], [], [], [])
    res = pl.pallas_call(
        functools.partial(_out_kernel, with_decode=decode_dn is not None),
        grid=(m // tm,),
        in_specs=[row(WIDTH_A, 0), row(WIDTH_A, REST_ZA // WIDTH_A), row(WIDTH_B, 0),
                  row(WIDTH_B, REST_ZB // WIDTH_B),
                  pl.BlockSpec((WIDTH_A + WIDTH_B, D_MODEL), lambda i: (0, 0)),
                  pl.BlockSpec((1, D_MODEL), lambda i: (0, 0)),
                  row(D_MODEL, 0)] + dn_in,
        out_specs=[row(D_MODEL, 0)] + dn_out,
        out_shape=[jax.ShapeDtypeStruct((m, D_MODEL), jnp.float32)] + dn_shapes,
        compiler_params=_cparams(("arbitrary",)),
        name="out_proj",
    )(o_a, p, o_b, p, w_out, g_post, x2d, *dn_args)
    return res if decode_dn is not None else res[0]


def _decode_attn_kernel(q_ref, kn_ref, vn_ref, k1_ref, v1_ref, k4_ref, v4_ref, k16_ref, v16_ref, o_ref):
    f32 = jnp.float32
    q = q_ref[0].astype(f32)
    kn, vn = kn_ref[0].astype(f32), vn_ref[0].astype(f32)
    s_new = jnp.sum(q * kn, axis=-1, keepdims=True)
    scores = [jnp.sum(k_ref[0] * q[None], axis=-1, keepdims=True) for k_ref in (k1_ref, k4_ref, k16_ref)]
    m = s_new
    for s in scores:
        m = jnp.maximum(m, jnp.max(s, axis=0))
    p_new = len(DILATIONS) * jnp.exp(s_new - m)
    den = p_new
    acc = p_new * vn
    for s, v_ref in zip(scores, (v1_ref, v4_ref, v16_ref)):
        p = jnp.exp(s - m[None])
        den = den + jnp.sum(p, axis=0)
        acc = acc + jnp.sum(p * v_ref[0], axis=0)
    o_ref[0] = (acc / den).astype(o_ref.dtype)


N_DEC_OPERANDS = 9


def _decode_operands(q, k_new, v_new, cache_k, cache_v, seq_of):
    b, win, h, dd = cache_k.shape
    nb = 128
    views, specs = [], []
    for window, dil in DILATIONS:
        assert window // dil == nb and win % (nb * dil) == 0
        blk = win // (nb * dil) - 1
        if dil == 1:
            spec = pl.BlockSpec((1, nb, h, dd), lambda *ids, blk=blk: (seq_of(*ids), blk, 0, 0))
            view = lambda c: c
        else:
            spec = pl.BlockSpec((1, nb, None, h, dd), lambda *ids, blk=blk: (seq_of(*ids), blk, 0, 0, 0))
            view = lambda c, dil=dil: c.reshape(b, win // dil, dil, h, dd)
        specs += [spec, spec]
        views += [view(cache_k), view(cache_v)]
    tok = pl.BlockSpec((1, h, dd), lambda *ids: (seq_of(*ids), 0, 0))
    return ([q, k_new, v_new] + views, [tok, tok, tok] + specs, tok,
            jax.ShapeDtypeStruct((b, h, dd), jnp.bfloat16))


DEC_BB = 4


def _decode_dn_kernel(x_ref, cb_ref, cw_ref, gate_ref, prm_ref, onw_ref, s_ref,
                      o_ref, cbo_ref, so_ref):
    f32, bf16 = jnp.float32, jnp.bfloat16
    hh = N_HEADS_B
    cw = cw_ref[...]
    zeros6 = jnp.zeros((8 - 2, DK_B), bf16)
    zeros7 = jnp.zeros((8 - 1, DV_B), bf16)
    pairs, lhss, a_s, betas, vs, qks = [], [], [], [], [], []
    for b in range(x_ref.shape[0]):
        xn = x_ref[b]
        y = xn * cw[CONV_WIDTH - 1]
        for i in range(CONV_WIDTH - 1):
            y = y + cb_ref[b, i] * cw[i]
            cbo_ref[b, i] = cb_ref[b, i + 1] if i + 1 < CONV_WIDTH - 1 else xn
        y = _silu(y)
        q, k, v = y[:hh], y[hh:2 * hh], y[2 * hh:]
        q = q * lax.rsqrt(jnp.sum(q * q, axis=-1, keepdims=True) + EPS) * (DK_B ** -0.5)
        k = k * lax.rsqrt(jnp.sum(k * k, axis=-1, keepdims=True) + EPS)
        gate = gate_ref[b]
        beta = jax.nn.sigmoid(gate[:hh])
        decay = jnp.exp(-jnp.exp(prm_ref[:hh]) * _softplus(gate[hh:] + prm_ref[hh:]))
        qk = jnp.sum(q * k, axis=-1, keepdims=True)
        for h in range(hh):
            pairs.append((b, h))
            lhss.append(jnp.concatenate([k[h:h + 1].astype(bf16), q[h:h + 1].astype(bf16), zeros6], axis=0))
            a_s.append(decay[h:h + 1])
            betas.append(beta[h:h + 1])
            vs.append(v[h:h + 1])
            qks.append(qk[h:h + 1])
    states = [s_ref[b, h] for b, h in pairs]
    kss = [jnp.dot(lhs, st.astype(bf16), preferred_element_type=f32) for lhs, st in zip(lhss, states)]
    v_news = [beta * (v - a * ks[0:1]) for beta, v, a, ks in zip(betas, vs, a_s, kss)]
    upds = [lax.dot_general(lhs, jnp.concatenate([vn.astype(bf16), zeros7], axis=0),
                            (((0,), (0,)), ((), ())), preferred_element_type=f32)
            for lhs, vn in zip(lhss, v_news)]
    for (b, h), st, a, upd in zip(pairs, states, a_s, upds):
        so_ref[b, h] = st * a[:, 0:1] + upd
    onw = onw_ref[...]
    for b in range(x_ref.shape[0]):
        o = jnp.concatenate([a_s[b * hh + h] * kss[b * hh + h][1:2] + qks[b * hh + h] * v_news[b * hh + h]
                             for h in range(hh)], axis=0)
        o = o * lax.rsqrt(jnp.mean(o * o, axis=-1, keepdims=True) + EPS) * onw
        o_ref[b] = o.astype(o_ref.dtype)


def _decode_dn_operands(x_new, conv_buf, state, beta_in, a_in, conv_w, a_log, dt_bias, onorm_w, bb):
    b = x_new.shape[0]
    f32 = jnp.float32
    g3 = CONV_CH // LANES
    hh = N_HEADS_B
    gate = jnp.broadcast_to(jnp.concatenate([beta_in, a_in], axis=1).astype(f32)[:, :, None], (b, 2 * hh, LANES))
    prm = jnp.broadcast_to(jnp.concatenate([a_log, dt_bias]).astype(f32)[:, None], (2 * hh, LANES))
    full = lambda shp: pl.BlockSpec(shp, lambda i: (0,) * len(shp))
    per = lambda shp: pl.BlockSpec((bb,) + shp, lambda i: (i,) + (0,) * len(shp))
    args = [x_new.astype(f32).reshape(b, g3, LANES), conv_buf.astype(f32).reshape(b, CONV_WIDTH - 1, g3, LANES),
            conv_w.astype(f32).reshape(CONV_WIDTH, g3, LANES), gate, prm, onorm_w.astype(f32)[None],
            state.astype(f32)]
    in_specs = [per((g3, LANES)), per((CONV_WIDTH - 1, g3, LANES)), full((CONV_WIDTH, g3, LANES)),
                per((2 * hh, LANES)), full((2 * hh, LANES)), full((1, DV_B)), per((hh, DK_B, DV_B))]
    out_specs = [per((hh, DV_B)), per((CONV_WIDTH - 1, g3, LANES)), per((hh, DK_B, DV_B))]
    out_shapes = [jax.ShapeDtypeStruct((b, hh, DV_B), jnp.bfloat16),
                  jax.ShapeDtypeStruct((b, CONV_WIDTH - 1, g3, LANES), f32),
                  jax.ShapeDtypeStruct((b, hh, DK_B, DV_B), f32)]
    return args, in_specs, out_specs, out_shapes


def _decode_deltanet(*operands):
    b = operands[0].shape[0]
    bb = DEC_BB if b % DEC_BB == 0 else 1
    args, in_specs, out_specs, out_shapes = _decode_dn_operands(*operands, bb)
    return pl.pallas_call(
        _decode_dn_kernel,
        grid=(b // bb,),
        in_specs=in_specs,
        out_specs=out_specs,
        out_shape=out_shapes,
        compiler_params=_cparams(("arbitrary",)),
        name="decode_deltanet",
    )(*args)


def kernel(x_prompt, x_sample, cache_win_k, cache_win_v, state_conv, state_delta,
           g_pre, w_in, conv_w, a_log, dt_bias, onorm_w, w_out, g_post):
    f32, bf16 = jnp.float32, jnp.bfloat16
    b, s, _ = x_prompt.shape
    db, t, _ = x_sample.shape
    depth = w_in.shape[0]
    n_past = cache_win_k.shape[2]
    assert t == 1 and n_past == MAX_WINDOW and s % (RES * SLAB) == 0
    keep = min(MAX_WINDOW, s)

    cos_p, sin_p = _rope_tables(jnp.arange(s, dtype=jnp.int32))
    cos_p, sin_p = jnp.tile(cos_p, (b, 1)), jnp.tile(sin_p, (b, 1))
    cos_s, sin_s = _rope_tables(jnp.full((db,), PAST_LEN, jnp.int32))

    yp = x_prompt.reshape(b * s, D_MODEL)
    ys = x_sample.reshape(db, D_MODEL)
    outs = [[] for _ in range(8)]
    for l in range(depth):
        w_in_t = jnp.swapaxes(w_in, 1, 2)
        w_tail = jnp.pad(w_in_t[l, MAIN_COLS:, :], ((0, LANES - TAIL_COLS), (0, 0))).astype(bf16)
        gp, go = g_pre[l].astype(f32)[None], g_post[l].astype(f32)[None]

        qkv_s, kf_s, vf_s, rest_s, tail_s, _, w_main = _project(
            ys, gp, w_in_t, w_tail, cos_s, sin_s, seq=db, keep=db, tm=db, residue_major=False, cast_layer=l)
        k_new = kf_s.reshape(db, N_HEADS_A, HEAD_DIM)
        v_new = vf_s.reshape(db, N_HEADS_A, HEAD_DIM)
        q_s = qkv_s[:, :WIDTH_A].reshape(db, N_HEADS_A, HEAD_DIM)

        qkv16, kf, vf, rest, tail, tail_t, o_as = _project(
            yp, gp, w_main, w_tail, cos_p, sin_p, seq=s, keep=keep, tm=1024, residue_major=True,
            decode=(q_s, k_new, v_new, cache_win_k[l], cache_win_v[l]))
        o_a, o_b, s_fin, w_o = _prompt_mixers(qkv16, rest, tail, tail_t, conv_w[l], a_log[l], dt_bias[l],
                                              onorm_w[l], w_out, l)
        outs[0].append(kf.reshape(b, keep, N_HEADS_A, HEAD_DIM))
        outs[1].append(vf.reshape(b, keep, N_HEADS_A, HEAD_DIM))
        n_tail = min(CONV_WIDTH - 1, s)
        tail_rows = rest.reshape(b, s, REST_COLS)[:, s - n_tail:, REST_QKVB:REST_QKVB + CONV_CH].astype(f32)
        outs[2].append(jnp.pad(tail_rows, ((0, 0), (CONV_WIDTH - 1 - n_tail, 0), (0, 0))))
        outs[3].append(s_fin)
        dn_operands = (rest_s[:, REST_QKVB:REST_QKVB + CONV_CH], state_conv[l], state_delta[l],
                       tail_s[:, :N_HEADS_B], tail_s[:, N_HEADS_B:TAIL_COLS], conv_w[l], a_log[l],
                       dt_bias[l], onorm_w[l])
        tm_out = 256
        if (b * s) // tm_out == db:
            yp, o_bs, cb_new, st_new = _output_sublayer(
                o_a, rest, o_b, w_o, go, yp, tm=tm_out, decode_dn=_decode_dn_operands(*dn_operands, 1))
        else:
            yp = _output_sublayer(o_a, rest, o_b, w_o, go, yp, tm=tm_out)
            o_bs, cb_new, st_new = _decode_deltanet(*dn_operands)
        cb_new = cb_new.reshape(db, CONV_WIDTH - 1, CONV_CH)
        outs[4].append(k_new.reshape(db, t, N_HEADS_A, HEAD_DIM))
        outs[5].append(v_new.reshape(db, t, N_HEADS_A, HEAD_DIM))
        outs[6].append(cb_new)
        outs[7].append(st_new)
        ys = _output_sublayer(o_as.reshape(db, WIDTH_A), rest_s, o_bs.reshape(db, WIDTH_B), w_o, go, ys, tm=db)

    stk = [jnp.stack(o) for o in outs]
    return (yp.reshape(b, s, D_MODEL), ys.reshape(db, t, D_MODEL),
            stk[0], stk[1], stk[2], stk[3], stk[4], stk[5], stk[6], stk[7])
```

```python
import functools
import math

import jax
import jax.numpy as jnp
from jax import lax
from jax.experimental import pallas as pl
from jax.experimental.pallas import tpu as pltpu

D_MODEL = 2048
HEAD_DIM = 128
N_HEADS_A = 8
N_HEADS_B = 8
DK_B = 128
DV_B = 128
WIDTH_A = N_HEADS_A * HEAD_DIM
WIDTH_B = N_HEADS_B * DV_B
DILATIONS = ((128, 1), (512, 4), (2048, 16))
MAX_WINDOW = 2048
ROPE_THETA = 500000.0
ROPE_DIM = HEAD_DIM // 4
CONV_WIDTH = 4
CONV_CH = 2 * N_HEADS_B * DK_B + N_HEADS_B * DV_B
CHUNK = 64
EPS = 1e-6
PAST_LEN = 16384
MAIN_COLS = 4 * WIDTH_A + CONV_CH + WIDTH_B
TAIL_COLS = 2 * N_HEADS_B
LANES = 128
NEG_BIG = -1e30
VMEM_LIMIT = 60 * 1024 * 1024

REST_COLS = MAIN_COLS - 3 * WIDTH_A
REST_ZA, REST_QKVB, REST_ZB = 0, WIDTH_A, WIDTH_A + CONV_CH


def _cparams(sem):
    return pltpu.CompilerParams(dimension_semantics=sem, vmem_limit_bytes=VMEM_LIMIT)


def _nt_dot(a, b):
    return lax.dot_general(a, b, (((1,), (1,)), ((), ())), preferred_element_type=jnp.float32)


RES = 16
QKV_TILES = 3
SUB_COLS = 256


def _proj_kernel(*refs, tiles_per_seq, first_keep_tile, residue_major, n_dec, cast_w):
    x_ref, g_ref, w_ref, wt_ref, cos_ref, sin_ref = refs[:6]
    dec_in = refs[6:6 + N_DEC_OPERANDS] if n_dec else ()
    outs = refs[6 + len(dec_in):]
    qkv_ref, kf_ref, vf_ref, p_ref, tail_ref, tailt_ref = outs[:6]
    dec_out = outs[6:7] if n_dec else ()
    outs = outs[6 + len(dec_out):]
    if cast_w:
        wbf_ref, outs = outs[0], outs[1:]
        wbf_ref[...] = w_ref[...].astype(wbf_ref.dtype)
        w_ref = wbf_ref
    h_ref, de_ref, mid_ref = outs
    i = pl.program_id(0)
    j = pl.program_id(1)
    tm = x_ref.shape[0]

    @pl.when(j == 0)
    def _():
        x = x_ref[...]
        y = x * lax.rsqrt(jnp.mean(x * x, axis=-1, keepdims=True) + EPS)
        h = (y * g_ref[...]).astype(jnp.bfloat16)
        h_ref[...] = h
        wt = wt_ref[...]
        tail_ref[...] = _nt_dot(h, wt)
        tailt_ref[...] = _nt_dot(wt[:TAIL_COLS], h)

    keep_rows = (i % tiles_per_seq) >= first_keep_tile
    heads_per_sub = SUB_COLS // LANES

    def sub_dot(sub):
        return _nt_dot(h_ref[...], w_ref[sub * SUB_COLS:(sub + 1) * SUB_COLS, :])

    def emit_qkv(hd, r, win_ref):
        cs = slice(hd * LANES, (hd + 1) * LANES)
        if residue_major:
            slot = hd % de_ref.shape[0]
            de_ref[slot] = r
            q4 = tm // 4
            for a in range(4):
                mid_ref[slot, a * q4:(a + 1) * q4, :] = de_ref[slot, pl.ds(a, q4, stride=4), :]
            for a in range(4):
                for bq in range(4):
                    qkv_ref[0, a + 4 * bq, :, cs] = mid_ref[
                        slot, pl.ds(a * q4 + bq, tm // RES, stride=4), :].astype(qkv_ref.dtype)
        else:
            qkv_ref[:, cs] = r.astype(qkv_ref.dtype)
        if win_ref is not None:
            win_ref[pl.ds(hd, tm, stride=N_HEADS_A), :] = r

    def rotary_tile(scale, win_ref):
        c = cos_ref[...]
        s = sin_ref[...]
        lane = lax.broadcasted_iota(jnp.int32, c.shape, 1)
        for sub in range(WIDTH_A // SUB_COLS):
            acc = sub_dot(sub)
            for hs in range(heads_per_sub):
                a = acc[:, hs * LANES:(hs + 1) * LANES]
                swapped = jnp.where(lane < ROPE_DIM // 2,
                                    pltpu.roll(a, LANES - ROPE_DIM // 2, 1),
                                    pltpu.roll(a, ROPE_DIM // 2, 1))
                r = a * c + swapped * s
                emit_qkv(sub * heads_per_sub + hs, r if scale is None else r * scale, win_ref)

    def plain_tile(win_ref):
        for sub in range(WIDTH_A // SUB_COLS):
            acc = sub_dot(sub)
            for hs in range(heads_per_sub):
                emit_qkv(sub * heads_per_sub + hs, acc[:, hs * LANES:(hs + 1) * LANES], win_ref)

    pl.when(j == 0)(lambda: rotary_tile(HEAD_DIM ** -0.5, None))
    pl.when((j == 1) & keep_rows)(lambda: rotary_tile(None, kf_ref))
    pl.when((j == 1) & jnp.logical_not(keep_rows))(lambda: rotary_tile(None, None))
    pl.when((j == 2) & keep_rows)(lambda: plain_tile(vf_ref))
    pl.when((j == 2) & jnp.logical_not(keep_rows))(lambda: plain_tile(None))

    def rest_tile(with_decode):
        for sub in range(WIDTH_A // SUB_COLS):
            p_ref[:, sub * SUB_COLS:(sub + 1) * SUB_COLS] = sub_dot(sub).astype(p_ref.dtype)
        if with_decode:
            _decode_attn_kernel(*dec_in, *dec_out)

    if n_dec:
        rest_step = i * (pl.num_programs(1) - QKV_TILES) + (j - QKV_TILES)
        pl.when((j >= QKV_TILES) & (rest_step < n_dec))(lambda: rest_tile(True))
        pl.when((j >= QKV_TILES) & (rest_step >= n_dec))(lambda: rest_tile(False))
    else:
        pl.when(j >= QKV_TILES)(lambda: rest_tile(False))


def _rope_tables(pos):
    half = ROPE_DIM // 2
    inv = ROPE_THETA ** (-jnp.arange(half, dtype=jnp.float32) / half)
    ang = pos.astype(jnp.float32)[:, None] * inv[None, :]
    cos, sin = jnp.cos(ang), jnp.sin(ang)
    n = pos.shape[0]
    ones = jnp.ones((n, LANES - ROPE_DIM), jnp.float32)
    c = jnp.concatenate([cos, cos, ones], axis=1)
    s = jnp.concatenate([-sin, sin, jnp.zeros_like(ones)], axis=1)
    return c, s


def _project(x2d, g_pre, w_main_t, w_tail_t, cos_t, sin_t, *, seq, keep, tm, residue_major, decode=None,
             cast_layer=None):
    m = x2d.shape[0]
    tn = WIDTH_A
    n_i, n_j = m // tm, MAIN_COLS // tn
    tiles_per_seq = seq // tm
    first_keep = (seq - keep) // tm
    keep_tiles = keep // tm
    assert seq % tm == 0 and keep % tm == 0 and (seq - keep) % tm == 0
    n_dec = 0
    dec_args, dec_in_specs, dec_out_specs, dec_out_shapes = [], [], [], []
    if decode is not None:
        n_dec = decode[0].shape[0]
        n_rest = n_j - QKV_TILES
        assert n_i * n_rest >= n_dec
        seq_of = lambda i, j: jnp.minimum(i * n_rest + jnp.maximum(j - QKV_TILES, 0), n_dec - 1)
        dec_args, dec_in_specs, dec_out_spec, dec_out_shape = _decode_operands(*decode, seq_of)
        dec_out_specs, dec_out_shapes = [dec_out_spec], [dec_out_shape]
    cast_w = cast_layer is not None
    w_spec = pl.BlockSpec((tn, D_MODEL), lambda i, j: (j, 0))
    cast_specs, cast_shapes = [], []
    if cast_w:
        assert n_i == 1
        cast_specs, cast_shapes = [w_spec], [jax.ShapeDtypeStruct((MAIN_COLS, D_MODEL), jnp.bfloat16)]
        w_spec = pl.BlockSpec((None, tn, D_MODEL), lambda i, j: (cast_layer, j, 0))
    kern = functools.partial(_proj_kernel, tiles_per_seq=tiles_per_seq, first_keep_tile=first_keep,
                             residue_major=residue_major, n_dec=n_dec, cast_w=cast_w)
    qkv_col = lambda j: jnp.minimum(j, QKV_TILES - 1)
    if residue_major:
        assert tm % (RES * 16) == 0
        qkv_spec = pl.BlockSpec((1, RES, tm // RES, tn),
                                lambda i, j: (i // tiles_per_seq, 0, i % tiles_per_seq, qkv_col(j)))
        qkv_shape = jax.ShapeDtypeStruct((m // seq, RES, seq // RES, QKV_TILES * tn), jnp.bfloat16)
    else:
        qkv_spec = pl.BlockSpec((tm, tn), lambda i, j: (i, qkv_col(j)))
        qkv_shape = jax.ShapeDtypeStruct((m, QKV_TILES * tn), jnp.bfloat16)

    def win_index(i, j):
        il = i % tiles_per_seq
        return (i // tiles_per_seq) * keep_tiles + jnp.maximum(il - first_keep, 0), 0

    win_spec = pl.BlockSpec((tm * N_HEADS_A, LANES), win_index, pipeline_mode=pl.Buffered(1))
    win_shape = jax.ShapeDtypeStruct((m // seq * keep * N_HEADS_A, LANES), jnp.float32)

    return pl.pallas_call(
        kern,
        grid=(n_i, n_j),
        in_specs=[
            pl.BlockSpec((tm, D_MODEL), lambda i, j: (i, 0)),
            pl.BlockSpec((1, D_MODEL), lambda i, j: (0, 0)),
            w_spec,
            pl.BlockSpec((LANES, D_MODEL), lambda i, j: (0, 0)),
            pl.BlockSpec((tm, LANES), lambda i, j: (i % tiles_per_seq, 0)),
            pl.BlockSpec((tm, LANES), lambda i, j: (i % tiles_per_seq, 0)),
        ] + dec_in_specs,
        out_specs=[
            qkv_spec,
            win_spec,
            win_spec,
            pl.BlockSpec((tm, tn), lambda i, j: (i, jnp.maximum(j - QKV_TILES, 0))),
            pl.BlockSpec((tm, LANES), lambda i, j: (i, 0)),
            pl.BlockSpec((TAIL_COLS, tm), lambda i, j: (0, i)),
        ] + dec_out_specs + cast_specs,
        out_shape=[
            qkv_shape,
            win_shape,
            win_shape,
            jax.ShapeDtypeStruct((m, REST_COLS), jnp.bfloat16),
            jax.ShapeDtypeStruct((m, LANES), jnp.float32),
            jax.ShapeDtypeStruct((TAIL_COLS, m), jnp.float32),
        ] + dec_out_shapes + cast_shapes,
        scratch_shapes=[pltpu.VMEM((tm, D_MODEL), jnp.bfloat16),
                        pltpu.VMEM((SUB_COLS // LANES, tm, LANES), jnp.float32),
                        pltpu.VMEM((SUB_COLS // LANES, tm, LANES), jnp.float32)],
        compiler_params=_cparams(("arbitrary", "arbitrary")),
        name="proj",
    )(x2d, g_pre, w_main_t, w_tail_t, cos_t, sin_t, *dec_args)


SLAB = 128


def _band_bias(kind):
    import numpy as np
    if kind == 16:
        nq, nk = 128, 256
        lq = np.arange(nq)[:, None]
        kap = np.arange(nk)[None, :]
        lk = kap - 128
        prev = kap < 128
    elif kind == 4:
        nq, nk = 128, 256
        rho = np.arange(nq)[:, None]
        lq = 4 * (rho % 32) + rho // 32
        kap = np.arange(nk)[None, :]
        lk = 4 * (kap % 64 - 32) + kap // 64
        prev = (kap % 64) < 32
    else:
        nq, nk = 256, 512
        rho = np.arange(nq)[:, None]
        lq = 16 * (rho % 16) + rho // 16
        kap = np.arange(nk)[None, :]
        lk = 16 * (kap % 32 - 16) + kap // 32
        prev = (kap % 32) < 16
    dist = lq - lk
    band = (dist >= 0) & (dist <= 128)
    out = np.stack([band & ~prev, band])
    return jnp.asarray(np.where(out, 0.0, NEG_BIG), dtype=jnp.float32)


PIECE = 16


def _attn_body(hp, q_ref, kc_ref, kp_ref, vc_ref, vp_ref, b16_ref, b4_ref, b1_ref, o_ref, nat_ref):
    f32 = jnp.float32

    def block(q, k, v, bias, prev):
        s = _nt_dot(q, k) + bias
        rows, keys = s.shape
        m_new = jnp.broadcast_to(jnp.max(s, axis=1, keepdims=True), (rows, LANES))
        if prev is not None:
            m_prev, l_prev, acc_prev = prev
            m_new = jnp.maximum(m_prev, m_new)
        p = jnp.exp(s - jnp.concatenate([m_new] * (keys // LANES), axis=1))
        v1 = jnp.concatenate([v, jnp.ones_like(v)], axis=1)
        pv = jnp.dot(p.astype(v.dtype), v1, preferred_element_type=f32)
        acc_new, l_new = pv[:, :LANES], pv[:, LANES:]
        if prev is not None:
            alpha = jnp.exp(m_prev - m_new)
            l_new = alpha * l_prev + l_new
            acc_new = alpha * acc_prev + acc_new
        return m_new, l_new, acc_new

    def split(x, n_parts):
        n = x.shape[0] // n_parts
        return [x[i * n:(i + 1) * n] for i in range(n_parts)]

    state = {}
    pieces = SLAB // PIECE

    for r in range(RES):
        k = jnp.concatenate([kp_ref[0, r], kc_ref[0, r]], axis=0)
        v = jnp.concatenate([vp_ref[0, r], vc_ref[0, r]], axis=0)
        res = [split(x, pieces) for x in block(q_ref[0, r], k, v, b16_ref[hp], None)]
        for a in range(pieces):
            state[(r, a)] = tuple(x[a] for x in res)

    def update(keys_, q, k, v, bias):
        prev = tuple(jnp.concatenate([state[key][i] for key in keys_], axis=0) for i in range(3))
        res = [split(x, len(keys_)) for x in block(q, k, v, bias, prev)]
        for i, key in enumerate(keys_):
            state[key] = tuple(x[i] for x in res)

    def gather(cur_ref, prev_ref, slabs, start, n):
        if start == 0:
            parts = [jnp.concatenate([prev_ref[0, r, SLAB - n:SLAB, :], cur_ref[0, r, 0:n, :]], axis=0)
                     for r in slabs]
        else:
            parts = [cur_ref[0, r, start - n:start + n, :] for r in slabs]
        return jnp.concatenate(parts, axis=0)

    for r4 in range(4):
        slabs = [r4 + 4 * i for i in range(4)]
        for jj in range(SLAB // 32):
            sl = slice(32 * jj, 32 * jj + 32)
            q = jnp.concatenate([q_ref[0, r, sl, :] for r in slabs], axis=0)
            k = gather(kc_ref, kp_ref, slabs, 32 * jj, 32)
            v = gather(vc_ref, vp_ref, slabs, 32 * jj, 32)
            keys_ = [(r, 2 * jj + a) for r in slabs for a in range(2)]
            update(keys_, q, k, v, b4_ref[hp] if jj == 0 else b4_ref[1])

    slabs = list(range(RES))
    for jj in range(pieces):
        sl = slice(PIECE * jj, PIECE * jj + PIECE)
        q = jnp.concatenate([q_ref[0, r, sl, :] for r in slabs], axis=0)
        k = gather(kc_ref, kp_ref, slabs, PIECE * jj, PIECE)
        v = gather(vc_ref, vp_ref, slabs, PIECE * jj, PIECE)
        update([(r, jj) for r in slabs], q, k, v, b1_ref[hp] if jj == 0 else b1_ref[1])

    for r in range(RES):
        acc = jnp.concatenate([state[(r, a)][2] for a in range(pieces)], axis=0)
        l = jnp.concatenate([state[(r, a)][1] for a in range(pieces)], axis=0)
        nat_ref[pl.ds(r, SLAB, stride=RES), :] = acc / l
    o_ref[...] = nat_ref[...].astype(o_ref.dtype)


DN_STEP = 256
CARRY = 8


def _split_bf16(x, n):
    parts, r = [], x
    for _ in range(n):
        hi = r.astype(jnp.bfloat16)
        parts.append(hi)
        r = r - hi.astype(jnp.float32)
    return parts


def _bdot(a, b):
    return jnp.dot(a.astype(jnp.bfloat16), b.astype(jnp.bfloat16), preferred_element_type=jnp.float32)


def _softplus(x):
    return jnp.maximum(x, 0.0) + jnp.log1p(jnp.exp(-jnp.abs(x)))


def _silu(x):
    return x * jax.nn.sigmoid(x)


def _unit_lower_inverses(a_list):
    n = a_list[0].shape[0]
    row = lax.broadcasted_iota(jnp.int32, (n, n), 0)
    col = lax.broadcasted_iota(jnp.int32, (n, n), 1)
    eye = jnp.where(row == col, 1.0, 0.0)
    xs = [eye - a for a in a_list]
    ps = [_bdot(a, a) for a in a_list]
    k = 2
    while True:
        xs = [x + _bdot(x, p) for x, p in zip(xs, ps)]
        k *= 2
        if k >= n:
            return xs
        ps = [_bdot(p, p) for p in ps]


def _dn_init(c, s_ref, xe_ref):
    @pl.when(c == 0)
    def _():
        s_ref[...] = jnp.zeros_like(s_ref)
        xe_ref[0:CARRY, :] = jnp.zeros((CARRY, CONV_CH), jnp.float32)

    @pl.when(c > 0)
    def _():
        xe_ref[0:CARRY, :] = xe_ref[DN_STEP:DN_STEP + CARRY, :]


def _dn_body(qb_ref, kb_ref, vb_ref, tail_ref, tailt_ref, cw_ref, prow_ref, pcol_ref, onw_ref,
             o_ref, s_ref, xe_ref):
    f32, bf16 = jnp.float32, jnp.bfloat16
    tb, n, hh = DN_STEP, CHUNK, N_HEADS_B
    n_ch = tb // n

    conv = []
    for part, ref in enumerate((qb_ref, kb_ref, vb_ref)):
        cs = slice(part * WIDTH_B, (part + 1) * WIDTH_B)
        xe_ref[CARRY:CARRY + tb, cs] = ref[...].astype(f32)
        y = xe_ref[CARRY:CARRY + tb, cs] * cw_ref[CONV_WIDTH - 1:CONV_WIDTH, cs]
        for i in range(CONV_WIDTH - 1):
            off = CARRY - (CONV_WIDTH - 1) + i
            y = y + xe_ref[off:off + tb, cs] * cw_ref[i:i + 1, cs]
        conv.append(_silu(y))
    qc, kc, vc = conv

    t = tail_ref[...]
    beta_c = jax.nn.sigmoid(t)
    g_c = -jnp.exp(prow_ref[0:1]) * _softplus(t + prow_ref[1:2])
    tt = tailt_ref[...]
    g_r = -jnp.exp(pcol_ref[:, 0:1]) * _softplus(tt + pcol_ref[:, 1:2])

    row = lax.broadcasted_iota(jnp.int32, (n, n), 0)
    col = lax.broadcasted_iota(jnp.int32, (n, n), 1)
    incl = row >= col
    strict = row > col
    ltri = jnp.where(incl, 1.0, 0.0).astype(bf16)
    utri = jnp.where(row <= col, 1.0, 0.0).astype(bf16)
    dot = functools.partial(jnp.dot, preferred_element_type=f32)

    gcs, grs, eg_c, ed_c, glast = [], [], [], [], []
    for ci in range(n_ch):
        rs = slice(ci * n, (ci + 1) * n)
        gc = sum(dot(ltri, part) for part in _split_bf16(g_c[rs], 3))
        gr = sum(dot(part, utri) for part in _split_bf16(g_r[:, rs], 3))
        gl = gc[n - 1:n, :]
        gcs.append(gc)
        grs.append(gr)
        glast.append(gl)
        eg_c.append(jnp.exp(gc))
        ed_c.append(jnp.exp(gl - gc))

    ids = [(ci, h) for ci in range(n_ch) for h in range(hh)]

    def head_slice(x, ci, h):
        return x[ci * n:(ci + 1) * n, h * DK_B:(h + 1) * DK_B]

    def lane_col(x, lane):
        return x[:, lane:lane + 1]

    qs, ks, kbetas, vbetas, egs = [], [], [], [], []
    for ci, h in ids:
        q = head_slice(qc, ci, h)
        k = head_slice(kc, ci, h)
        v = head_slice(vc, ci, h)
        q = q * (lax.rsqrt(jnp.sum(q * q, axis=-1, keepdims=True) + EPS) * (DK_B ** -0.5))
        k = k * lax.rsqrt(jnp.sum(k * k, axis=-1, keepdims=True) + EPS)
        beta = lane_col(beta_c[ci * n:(ci + 1) * n], h)
        qs.append(q)
        ks.append(k)
        kbetas.append(k * beta)
        vbetas.append(v * beta)
        egs.append(lane_col(eg_c[ci], hh + h))

    kqs = [_nt_dot(jnp.concatenate([kb, q], axis=0).astype(bf16), k.astype(bf16))
           for kb, q, k in zip(kbetas, qs, ks)]
    a_mats, qks = [], []
    for (ci, h), kq in zip(ids, kqs):
        gcol = lane_col(gcs[ci], hh + h)
        grow = grs[ci][hh + h:hh + h + 1, :]
        decay = jnp.exp(jnp.where(incl, gcol - grow, NEG_BIG))
        a_mats.append(jnp.where(strict, kq[:n] * decay, 0.0))
        qks.append((kq[n:] * decay).astype(bf16))
    t_mats = _unit_lower_inverses(a_mats)
    uws = [dot(tm.astype(bf16), jnp.concatenate([vb, kb * eg], axis=1).astype(bf16))
           for tm, vb, kb, eg in zip(t_mats, vbetas, kbetas, egs)]

    states = [s_ref[0, h] for h in range(hh)]
    onw = onw_ref[...]
    for ci in range(n_ch):
        base = ci * hh
        wqs = [jnp.concatenate([uws[base + h][:, DV_B:], qs[base + h] * egs[base + h]], axis=0).astype(bf16)
               for h in range(hh)]
        wss = [dot(wq, st.astype(bf16)) for wq, st in zip(wqs, states)]
        v_news = [(uws[base + h][:, :DV_B] - wss[h][:n]).astype(bf16) for h in range(hh)]
        o_in = [dot(qks[base + h], v_news[h]) for h in range(hh)]
        k_decs = [(ks[base + h] * lane_col(ed_c[ci], hh + h)).astype(bf16) for h in range(hh)]
        upds = [lax.dot_general(k_decs[h], v_news[h], (((0,), (0,)), ((), ())), preferred_element_type=f32)
                for h in range(hh)]
        e_last = jnp.exp(glast[ci])
        states = [states[h] * lane_col(e_last, hh + h) + upds[h] for h in range(hh)]
        for h in range(hh):
            o = wss[h][n:] + o_in[h]
            o = o * lax.rsqrt(jnp.mean(o * o, axis=-1, keepdims=True) + EPS) * onw
            o_ref[ci * n:(ci + 1) * n, h * DV_B:(h + 1) * DV_B] = o.astype(o_ref.dtype)
    for h in range(hh):
        s_ref[0, h] = states[h]


N_ATTN_IN, N_DN_IN = 8, 9


def _mixers_kernel(*refs, steps):
    s = pl.program_id(0)
    attn_in = refs[:N_ATTN_IN]
    dn_in = refs[N_ATTN_IN:N_ATTN_IN + N_DN_IN]
    wo_ref = refs[N_ATTN_IN + N_DN_IN]
    o_a_ref, o_b_ref, s_ref, wo_bf_ref, nat_ref, xe_ref = refs[N_ATTN_IN + N_DN_IN + 1:]
    n_sb = steps // N_HEADS_A
    _dn_init(s % steps, s_ref, xe_ref)
    wo_bf_ref[...] = wo_ref[...].astype(wo_bf_ref.dtype)
    _attn_body(jnp.minimum(s % n_sb, 1), *attn_in, o_a_ref, nat_ref)
    _dn_body(*dn_in, o_b_ref, s_ref, xe_ref)


def _prompt_mixers(qkv16, p, tail, tail_t, conv_w, a_log, dt_bias, onorm_w, w_out, layer):
    batch, _, rows, _ = qkv16.shape
    m = p.shape[0]
    n_sb = rows // SLAB
    steps = m // batch // DN_STEP
    n_steps = batch * steps
    assert N_HEADS_A * n_sb == steps
    wo_rows, wo_cols = w_out.shape[1:]
    assert wo_rows % (n_steps * 16) == 0
    wo_blk = wo_rows // n_steps
    kcol, vcol = WIDTH_A // LANES, 2 * WIDTH_A // LANES
    blk = (1, RES, SLAB, LANES)
    a_idx = lambda s: (s // steps, (s // n_sb) % N_HEADS_A, s % n_sb)
    cur = lambda off: pl.BlockSpec(blk, lambda s: (a_idx(s)[0], 0, a_idx(s)[2], off + a_idx(s)[1]))
    prv = lambda off: pl.BlockSpec(
        blk, lambda s: (a_idx(s)[0], 0, jnp.maximum(a_idx(s)[2] - 1, 0), off + a_idx(s)[1]))
    const = lambda shp: pl.BlockSpec(shp, lambda s: (0,) * len(shp))
    wb = WIDTH_B
    qcol = REST_QKVB // wb
    zeros8 = jnp.zeros((N_HEADS_B,), jnp.float32)
    gate = jnp.stack([jnp.concatenate([zeros8, a_log.astype(jnp.float32)]),
                      jnp.concatenate([zeros8, dt_bias.astype(jnp.float32)])])
    prow = jnp.pad(gate, ((0, 6), (0, LANES - TAIL_COLS)))
    pcol = jnp.pad(gate.T, ((0, 0), (0, LANES - 2)))
    tok = lambda cb: pl.BlockSpec((DN_STEP, wb), lambda s: (s, cb))
    return pl.pallas_call(
        functools.partial(_mixers_kernel, steps=steps),
        grid=(n_steps,),
        in_specs=[cur(0), cur(kcol), prv(kcol), cur(vcol), prv(vcol),
                  const((2, 128, 256)), const((2, 128, 256)), const((2, 256, 512)),
                  tok(qcol), tok(qcol + 1), tok(qcol + 2),
                  pl.BlockSpec((DN_STEP, LANES), lambda s: (s, 0)),
                  pl.BlockSpec((TAIL_COLS, DN_STEP), lambda s: (0, s)),
                  const((CONV_WIDTH, CONV_CH)), const((8, LANES)), const((TAIL_COLS, LANES)),
                  const((1, DV_B)),
                  pl.BlockSpec((None, wo_blk, wo_cols), lambda s: (layer, s, 0))],
        out_specs=[pl.BlockSpec((RES * SLAB, LANES),
                                lambda s: (a_idx(s)[0] * n_sb + a_idx(s)[2], a_idx(s)[1])),
                   pl.BlockSpec((DN_STEP, WIDTH_B), lambda s: (s, 0)),
                   pl.BlockSpec((1, N_HEADS_B, DK_B, DV_B), lambda s: (s // steps, 0, 0, 0)),
                   pl.BlockSpec((wo_blk, wo_cols), lambda s: (s, 0))],
        out_shape=[jax.ShapeDtypeStruct((m, WIDTH_A), jnp.bfloat16),
                   jax.ShapeDtypeStruct((m, WIDTH_B), jnp.bfloat16),
                   jax.ShapeDtypeStruct((batch, N_HEADS_B, DK_B, DV_B), jnp.float32),
                   jax.ShapeDtypeStruct((wo_rows, wo_cols), jnp.bfloat16)],
        scratch_shapes=[pltpu.VMEM((RES * SLAB, LANES), jnp.float32),
                        pltpu.VMEM((CARRY + DN_STEP, CONV_CH), jnp.float32)],
        compiler_params=_cparams(("arbitrary",)),
        name="prompt_mixers",
    )(qkv16, qkv16, qkv16, qkv16, qkv16, _band_bias(16), _band_bias(4), _band_bias(1),
      p, p, p, tail, tail_t, conv_w.astype(jnp.float32), prow, pcol, onorm_w.astype(jnp.float32)[None],
      w_out)


N_OUT_IN, N_DDN_IN, N_DDN_OUT = 7, 7, 3


def _out_kernel(*refs, with_decode):
    oa_ref, za_ref, ob_ref, zb_ref, w_ref, g_ref, x_ref = refs[:N_OUT_IN]
    if with_decode:
        ddn_in = refs[N_OUT_IN:N_OUT_IN + N_DDN_IN]
        y_ref = refs[N_OUT_IN + N_DDN_IN]
        _decode_dn_kernel(*ddn_in, *refs[N_OUT_IN + N_DDN_IN + 1:])
    else:
        y_ref = refs[N_OUT_IN]
    f32 = jnp.float32
    ga = (oa_ref[...].astype(f32) * _silu(za_ref[...].astype(f32))).astype(jnp.bfloat16)
    gb = (ob_ref[...].astype(f32) * _silu(zb_ref[...].astype(f32))).astype(jnp.bfloat16)
    y = (jnp.dot(ga, w_ref[:WIDTH_A, :], preferred_element_type=f32)
         + jnp.dot(gb, w_ref[WIDTH_A:, :], preferred_element_type=f32))
    y = y * lax.rsqrt(jnp.mean(y * y, axis=-1, keepdims=True) + EPS) * g_ref[...]
    y_ref[...] = x_ref[...] + y


def _output_sublayer(o_a, p, o_b, w_out, g_post, x2d, *, tm, decode_dn=None):
    m = x2d.shape[0]
    row = lambda width, cb: pl.BlockSpec((tm, width), lambda i: (i, cb))
    dn_args, dn_in, dn_out, dn_shapes = decode_dn if decode_dn is not None else ([], [], [], [])
    res = pl.pallas_call(
        functools.partial(_out_kernel, with_decode=decode_dn is not None),
        grid=(m // tm,),
        in_specs=[row(WIDTH_A, 0), row(WIDTH_A, REST_ZA // WIDTH_A), row(WIDTH_B, 0),
                  row(WIDTH_B, REST_ZB // WIDTH_B),
                  pl.BlockSpec((WIDTH_A + WIDTH_B, D_MODEL), lambda i: (0, 0)),
                  pl.BlockSpec((1, D_MODEL), lambda i: (0, 0)),
                  row(D_MODEL, 0)] + dn_in,
        out_specs=[row(D_MODEL, 0)] + dn_out,
        out_shape=[jax.ShapeDtypeStruct((m, D_MODEL), jnp.float32)] + dn_shapes,
        compiler_params=_cparams(("arbitrary",)),
        name="out_proj",
    )(o_a, p, o_b, p, w_out, g_post, x2d, *dn_args)
    return res if decode_dn is not None else res[0]


def _decode_attn_kernel(q_ref, kn_ref, vn_ref, k1_ref, v1_ref, k4_ref, v4_ref, k16_ref, v16_ref, o_ref):
    f32 = jnp.float32
    q = q_ref[0].astype(f32)
    kn, vn = kn_ref[0].astype(f32), vn_ref[0].astype(f32)
    s_new = jnp.sum(q * kn, axis=-1, keepdims=True)
    scores = [jnp.sum(k_ref[0] * q[None], axis=-1, keepdims=True) for k_ref in (k1_ref, k4_ref, k16_ref)]
    m = s_new
    for s in scores:
        m = jnp.maximum(m, jnp.max(s, axis=0))
    p_new = len(DILATIONS) * jnp.exp(s_new - m)
    den = p_new
    acc = p_new * vn
    for s, v_ref in zip(scores, (v1_ref, v4_ref, v16_ref)):
        p = jnp.exp(s - m[None])
        den = den + jnp.sum(p, axis=0)
        acc = acc + jnp.sum(p * v_ref[0], axis=0)
    o_ref[0] = (acc / den).astype(o_ref.dtype)


N_DEC_OPERANDS = 9


def _decode_operands(q, k_new, v_new, cache_k, cache_v, seq_of):
    b, win, h, dd = cache_k.shape
    nb = 128
    views, specs = [], []
    for window, dil in DILATIONS:
        assert window // dil == nb and win % (nb * dil) == 0
        blk = win // (nb * dil) - 1
        if dil == 1:
            spec = pl.BlockSpec((1, nb, h, dd), lambda *ids, blk=blk: (seq_of(*ids), blk, 0, 0))
            view = lambda c: c
        else:
            spec = pl.BlockSpec((1, nb, None, h, dd), lambda *ids, blk=blk: (seq_of(*ids), blk, 0, 0, 0))
            view = lambda c, dil=dil: c.reshape(b, win // dil, dil, h, dd)
        specs += [spec, spec]
        views += [view(cache_k), view(cache_v)]
    tok = pl.BlockSpec((1, h, dd), lambda *ids: (seq_of(*ids), 0, 0))
    return ([q, k_new, v_new] + views, [tok, tok, tok] + specs, tok,
            jax.ShapeDtypeStruct((b, h, dd), jnp.bfloat16))


DEC_BB = 4


def _decode_dn_kernel(x_ref, cb_ref, cw_ref, gate_ref, prm_ref, onw_ref, s_ref,
                      o_ref, cbo_ref, so_ref):
    f32, bf16 = jnp.float32, jnp.bfloat16
    hh = N_HEADS_B
    cw = cw_ref[...]
    zeros6 = jnp.zeros((8 - 2, DK_B), bf16)
    zeros7 = jnp.zeros((8 - 1, DV_B), bf16)
    pairs, lhss, a_s, betas, vs, qks = [], [], [], [], [], []
    for b in range(x_ref.shape[0]):
        xn = x_ref[b]
        y = xn * cw[CONV_WIDTH - 1]
        for i in range(CONV_WIDTH - 1):
            y = y + cb_ref[b, i] * cw[i]
            cbo_ref[b, i] = cb_ref[b, i + 1] if i + 1 < CONV_WIDTH - 1 else xn
        y = _silu(y)
        q, k, v = y[:hh], y[hh:2 * hh], y[2 * hh:]
        q = q * lax.rsqrt(jnp.sum(q * q, axis=-1, keepdims=True) + EPS) * (DK_B ** -0.5)
        k = k * lax.rsqrt(jnp.sum(k * k, axis=-1, keepdims=True) + EPS)
        gate = gate_ref[b]
        beta = jax.nn.sigmoid(gate[:hh])
        decay = jnp.exp(-jnp.exp(prm_ref[:hh]) * _softplus(gate[hh:] + prm_ref[hh:]))
        qk = jnp.sum(q * k, axis=-1, keepdims=True)
        for h in range(hh):
            pairs.append((b, h))
            lhss.append(jnp.concatenate([k[h:h + 1].astype(bf16), q[h:h + 1].astype(bf16), zeros6], axis=0))
            a_s.append(decay[h:h + 1])
            betas.append(beta[h:h + 1])
            vs.append(v[h:h + 1])
            qks.append(qk[h:h + 1])
    states = [s_ref[b, h] for b, h in pairs]
    kss = [jnp.dot(lhs, st.astype(bf16), preferred_element_type=f32) for lhs, st in zip(lhss, states)]
    v_news = [beta * (v - a * ks[0:1]) for beta, v, a, ks in zip(betas, vs, a_s, kss)]
    upds = [lax.dot_general(lhs, jnp.concatenate([vn.astype(bf16), zeros7], axis=0),
                            (((0,), (0,)), ((), ())), preferred_element_type=f32)
            for lhs, vn in zip(lhss, v_news)]
    for (b, h), st, a, upd in zip(pairs, states, a_s, upds):
        so_ref[b, h] = st * a[:, 0:1] + upd
    onw = onw_ref[...]
    for b in range(x_ref.shape[0]):
        o = jnp.concatenate([a_s[b * hh + h] * kss[b * hh + h][1:2] + qks[b * hh + h] * v_news[b * hh + h]
                             for h in range(hh)], axis=0)
        o = o * lax.rsqrt(jnp.mean(o * o, axis=-1, keepdims=True) + EPS) * onw
        o_ref[b] = o.astype(o_ref.dtype)


def _decode_dn_operands(x_new, conv_buf, state, beta_in, a_in, conv_w, a_log, dt_bias, onorm_w, bb):
    b = x_new.shape[0]
    f32 = jnp.float32
    g3 = CONV_CH // LANES
    hh = N_HEADS_B
    gate = jnp.broadcast_to(jnp.concatenate([beta_in, a_in], axis=1).astype(f32)[:, :, None], (b, 2 * hh, LANES))
    prm = jnp.broadcast_to(jnp.concatenate([a_log, dt_bias]).astype(f32)[:, None], (2 * hh, LANES))
    full = lambda shp: pl.BlockSpec(shp, lambda i: (0,) * len(shp))
    per = lambda shp: pl.BlockSpec((bb,) + shp, lambda i: (i,) + (0,) * len(shp))
    args = [x_new.astype(f32).reshape(b, g3, LANES), conv_buf.astype(f32).reshape(b, CONV_WIDTH - 1, g3, LANES),
            conv_w.astype(f32).reshape(CONV_WIDTH, g3, LANES), gate, prm, onorm_w.astype(f32)[None],
            state.astype(f32)]
    in_specs = [per((g3, LANES)), per((CONV_WIDTH - 1, g3, LANES)), full((CONV_WIDTH, g3, LANES)),
                per((2 * hh, LANES)), full((2 * hh, LANES)), full((1, DV_B)), per((hh, DK_B, DV_B))]
    out_specs = [per((hh, DV_B)), per((CONV_WIDTH - 1, g3, LANES)), per((hh, DK_B, DV_B))]
    out_shapes = [jax.ShapeDtypeStruct((b, hh, DV_B), jnp.bfloat16),
                  jax.ShapeDtypeStruct((b, CONV_WIDTH - 1, g3, LANES), f32),
                  jax.ShapeDtypeStruct((b, hh, DK_B, DV_B), f32)]
    return args, in_specs, out_specs, out_shapes


def _decode_deltanet(*operands):
    b = operands[0].shape[0]
    bb = DEC_BB if b % DEC_BB == 0 else 1
    args, in_specs, out_specs, out_shapes = _decode_dn_operands(*operands, bb)
    return pl.pallas_call(
        _decode_dn_kernel,
        grid=(b // bb,),
        in_specs=in_specs,
        out_specs=out_specs,
        out_shape=out_shapes,
        compiler_params=_cparams(("arbitrary",)),
        name="decode_deltanet",
    )(*args)


def kernel(x_prompt, x_sample, cache_win_k, cache_win_v, state_conv, state_delta,
           g_pre, w_in, conv_w, a_log, dt_bias, onorm_w, w_out, g_post):
    f32, bf16 = jnp.float32, jnp.bfloat16
    b, s, _ = x_prompt.shape
    db, t, _ = x_sample.shape
    depth = w_in.shape[0]
    n_past = cache_win_k.shape[2]
    assert t == 1 and n_past == MAX_WINDOW and s % (RES * SLAB) == 0
    keep = min(MAX_WINDOW, s)

    cos_p, sin_p = _rope_tables(jnp.arange(s, dtype=jnp.int32))
    cos_s, sin_s = _rope_tables(jnp.full((db,), PAST_LEN, jnp.int32))

    yp = x_prompt.reshape(b * s, D_MODEL)
    ys = x_sample.reshape(db, D_MODEL)
    outs = [[] for _ in range(8)]
    for l in range(depth):
        w_in_t = jnp.swapaxes(w_in, 1, 2)
        w_tail = jnp.pad(w_in_t[l, MAIN_COLS:, :], ((0, LANES - TAIL_COLS), (0, 0))).astype(bf16)
        gp, go = g_pre[l].astype(f32)[None], g_post[l].astype(f32)[None]

        qkv_s, kf_s, vf_s, rest_s, tail_s, _, w_main = _project(
            ys, gp, w_in_t, w_tail, cos_s, sin_s, seq=db, keep=db, tm=db, residue_major=False, cast_layer=l)
        k_new = kf_s.reshape(db, N_HEADS_A, HEAD_DIM)
        v_new = vf_s.reshape(db, N_HEADS_A, HEAD_DIM)
        q_s = qkv_s[:, :WIDTH_A].reshape(db, N_HEADS_A, HEAD_DIM)

        qkv16, kf, vf, rest, tail, tail_t, o_as = _project(
            yp, gp, w_main, w_tail, cos_p, sin_p, seq=s, keep=keep, tm=1024, residue_major=True,
            decode=(q_s, k_new, v_new, cache_win_k[l], cache_win_v[l]))
        o_a, o_b, s_fin, w_o = _prompt_mixers(qkv16, rest, tail, tail_t, conv_w[l], a_log[l], dt_bias[l],
                                              onorm_w[l], w_out, l)
        outs[0].append(kf.reshape(b, keep, N_HEADS_A, HEAD_DIM))
        outs[1].append(vf.reshape(b, keep, N_HEADS_A, HEAD_DIM))
        n_tail = min(CONV_WIDTH - 1, s)
        tail_rows = rest.reshape(b, s, REST_COLS)[:, s - n_tail:, REST_QKVB:REST_QKVB + CONV_CH].astype(f32)
        outs[2].append(jnp.pad(tail_rows, ((0, 0), (CONV_WIDTH - 1 - n_tail, 0), (0, 0))))
        outs[3].append(s_fin)
        dn_operands = (rest_s[:, REST_QKVB:REST_QKVB + CONV_CH], state_conv[l], state_delta[l],
                       tail_s[:, :N_HEADS_B], tail_s[:, N_HEADS_B:TAIL_COLS], conv_w[l], a_log[l],
                       dt_bias[l], onorm_w[l])
        tm_out = 256
        if (b * s) // tm_out == db:
            yp, o_bs, cb_new, st_new = _output_sublayer(
                o_a, rest, o_b, w_o, go, yp, tm=tm_out, decode_dn=_decode_dn_operands(*dn_operands, 1))
        else:
            yp = _output_sublayer(o_a, rest, o_b, w_o, go, yp, tm=tm_out)
            o_bs, cb_new, st_new = _decode_deltanet(*dn_operands)
        cb_new = cb_new.reshape(db, CONV_WIDTH - 1, CONV_CH)
        outs[4].append(k_new.reshape(db, t, N_HEADS_A, HEAD_DIM))
        outs[5].append(v_new.reshape(db, t, N_HEADS_A, HEAD_DIM))
        outs[6].append(cb_new)
        outs[7].append(st_new)
        ys = _output_sublayer(o_as.reshape(db, WIDTH_A), rest_s, o_bs.reshape(db, WIDTH_B), w_o, go, ys, tm=db)

    stk = [jnp.stack(o) for o in outs]
    return (yp.reshape(b, s, D_MODEL), ys.reshape(db, t, D_MODEL),
            stk[0], stk[1], stk[2], stk[3], stk[4], stk[5], stk[6], stk[7])
```

```python
import functools

import jax
import jax.numpy as jnp
import numpy as np
from jax import lax
from jax.experimental import pallas as pl
from jax.experimental.pallas import tpu as pltpu

D_MODEL = 2048
HEAD_DIM = 128
N_HEADS_A = 8
N_HEADS_B = 8
DK_B = 128
DV_B = 128
WIDTH_A = N_HEADS_A * HEAD_DIM
WIDTH_B = N_HEADS_B * DV_B
DILATIONS = ((128, 1), (512, 4), (2048, 16))
MAX_WINDOW = 2048
ROPE_THETA = 500000.0
ROPE_DIM = HEAD_DIM // 4
CONV_WIDTH = 4
CONV_CH = 2 * N_HEADS_B * DK_B + N_HEADS_B * DV_B
CHUNK = 64
EPS = 1e-6
PAST_LEN = 16384
MAIN_COLS = 4 * WIDTH_A + CONV_CH + WIDTH_B
TAIL_COLS = 2 * N_HEADS_B
LANES = 128
NEG_BIG = -1e30
VMEM_LIMIT = 60 * 1024 * 1024

REST_COLS = MAIN_COLS - 3 * WIDTH_A
REST_ZA, REST_QKVB, REST_ZB = 0, WIDTH_A, WIDTH_A + CONV_CH


def _cparams(sem):
    return pltpu.CompilerParams(dimension_semantics=sem, vmem_limit_bytes=VMEM_LIMIT)


def _nt_dot(a, b):
    return lax.dot_general(a, b, (((1,), (1,)), ((), ())), preferred_element_type=jnp.float32)


RES = 16
QKV_TILES = 3
SUB_COLS = 256


def _proj_kernel(*refs, tiles_per_seq, first_keep_tile, residue_major, n_dec, cast_w):
    x_ref, g_ref, w_ref, wt_ref, cos_ref, sin_ref = refs[:6]
    dec_in = refs[6:6 + N_DEC_OPERANDS] if n_dec else ()
    outs = refs[6 + len(dec_in):]
    qkv_ref, kf_ref, vf_ref, p_ref, tail_ref, tailt_ref = outs[:6]
    dec_out = outs[6:7] if n_dec else ()
    outs = outs[6 + len(dec_out):]
    if cast_w:
        wbf_ref, outs = outs[0], outs[1:]
        wbf_ref[...] = w_ref[...].astype(wbf_ref.dtype)
        w_ref = wbf_ref
    h_ref, de_ref, mid_ref = outs
    i = pl.program_id(0)
    j = pl.program_id(1)
    tm = x_ref.shape[0]

    @pl.when(j == 0)
    def _():
        x = x_ref[...]
        y = x * lax.rsqrt(jnp.mean(x * x, axis=-1, keepdims=True) + EPS)
        h = (y * g_ref[...]).astype(jnp.bfloat16)
        h_ref[...] = h
        wt = wt_ref[...]
        tail_ref[...] = _nt_dot(h, wt)
        tailt_ref[...] = _nt_dot(wt[:TAIL_COLS], h)

    keep_rows = (i % tiles_per_seq) >= first_keep_tile
    heads_per_sub = SUB_COLS // LANES

    def sub_dot(sub):
        return _nt_dot(h_ref[...], w_ref[sub * SUB_COLS:(sub + 1) * SUB_COLS, :])

    def emit_qkv(hd, r, win_ref):
        cs = slice(hd * LANES, (hd + 1) * LANES)
        if residue_major:
            slot = hd % de_ref.shape[0]
            de_ref[slot] = r
            q4 = tm // 4
            for a in range(4):
                mid_ref[slot, a * q4:(a + 1) * q4, :] = de_ref[slot, pl.ds(a, q4, stride=4), :]
            for a in range(4):
                for bq in range(4):
                    qkv_ref[0, a + 4 * bq, :, cs] = mid_ref[
                        slot, pl.ds(a * q4 + bq, tm // RES, stride=4), :].astype(qkv_ref.dtype)
        else:
            qkv_ref[:, cs] = r.astype(qkv_ref.dtype)
        if win_ref is not None:
            win_ref[pl.ds(hd, tm, stride=N_HEADS_A), :] = r

    def rotary_tile(scale, win_ref):
        c = cos_ref[...]
        s = sin_ref[...]
        lane = lax.broadcasted_iota(jnp.int32, c.shape, 1)
        for sub in range(WIDTH_A // SUB_COLS):
            acc = sub_dot(sub)
            for hs in range(heads_per_sub):
                a = acc[:, hs * LANES:(hs + 1) * LANES]
                swapped = jnp.where(lane < ROPE_DIM // 2,
                                    pltpu.roll(a, LANES - ROPE_DIM // 2, 1),
                                    pltpu.roll(a, ROPE_DIM // 2, 1))
                r = a * c + swapped * s
                emit_qkv(sub * heads_per_sub + hs, r if scale is None else r * scale, win_ref)

    def plain_tile(win_ref):
        for sub in range(WIDTH_A // SUB_COLS):
            acc = sub_dot(sub)
            for hs in range(heads_per_sub):
                emit_qkv(sub * heads_per_sub + hs, acc[:, hs * LANES:(hs + 1) * LANES], win_ref)

    pl.when(j == 0)(lambda: rotary_tile(HEAD_DIM ** -0.5, None))
    pl.when((j == 1) & keep_rows)(lambda: rotary_tile(None, kf_ref))
    pl.when((j == 1) & jnp.logical_not(keep_rows))(lambda: rotary_tile(None, None))
    pl.when((j == 2) & keep_rows)(lambda: plain_tile(vf_ref))
    pl.when((j == 2) & jnp.logical_not(keep_rows))(lambda: plain_tile(None))

    def rest_tile(with_decode):
        for sub in range(WIDTH_A // SUB_COLS):
            p_ref[:, sub * SUB_COLS:(sub + 1) * SUB_COLS] = sub_dot(sub).astype(p_ref.dtype)
        if with_decode:
            _decode_attn_kernel(*dec_in, *dec_out)

    if n_dec:
        rest_step = i * (pl.num_programs(1) - QKV_TILES) + (j - QKV_TILES)
        pl.when((j >= QKV_TILES) & (rest_step < n_dec))(lambda: rest_tile(True))
        pl.when((j >= QKV_TILES) & (rest_step >= n_dec))(lambda: rest_tile(False))
    else:
        pl.when(j >= QKV_TILES)(lambda: rest_tile(False))


def _rope_tables(pos):
    half = ROPE_DIM // 2
    inv = ROPE_THETA ** (-jnp.arange(half, dtype=jnp.float32) / half)
    ang = pos.astype(jnp.float32)[:, None] * inv[None, :]
    cos, sin = jnp.cos(ang), jnp.sin(ang)
    n = pos.shape[0]
    ones = jnp.ones((n, LANES - ROPE_DIM), jnp.float32)
    c = jnp.concatenate([cos, cos, ones], axis=1)
    s = jnp.concatenate([-sin, sin, jnp.zeros_like(ones)], axis=1)
    return c, s


def _project(x2d, g_pre, w_main_t, w_tail_t, cos_t, sin_t, *, seq, keep, tm, residue_major, decode=None,
             cast_layer=None):
    m = x2d.shape[0]
    tn = WIDTH_A
    n_i, n_j = m // tm, MAIN_COLS // tn
    tiles_per_seq = seq // tm
    first_keep = (seq - keep) // tm
    keep_tiles = keep // tm
    assert seq % tm == 0 and keep % tm == 0 and (seq - keep) % tm == 0
    n_dec = 0
    dec_args, dec_in_specs, dec_out_specs, dec_out_shapes = [], [], [], []
    if decode is not None:
        n_dec = decode[0].shape[0]
        n_rest = n_j - QKV_TILES
        assert n_i * n_rest >= n_dec
        seq_of = lambda i, j: jnp.minimum(i * n_rest + jnp.maximum(j - QKV_TILES, 0), n_dec - 1)
        dec_args, dec_in_specs, dec_out_spec, dec_out_shape = _decode_operands(*decode, seq_of)
        dec_out_specs, dec_out_shapes = [dec_out_spec], [dec_out_shape]
    cast_w = cast_layer is not None
    w_spec = pl.BlockSpec((tn, D_MODEL), lambda i, j: (j, 0))
    cast_specs, cast_shapes = [], []
    if cast_w:
        assert n_i == 1
        cast_specs, cast_shapes = [w_spec], [jax.ShapeDtypeStruct((MAIN_COLS, D_MODEL), jnp.bfloat16)]
        w_spec = pl.BlockSpec((None, tn, D_MODEL), lambda i, j: (cast_layer, j, 0))
    kern = functools.partial(_proj_kernel, tiles_per_seq=tiles_per_seq, first_keep_tile=first_keep,
                             residue_major=residue_major, n_dec=n_dec, cast_w=cast_w)
    qkv_col = lambda j: jnp.minimum(j, QKV_TILES - 1)
    if residue_major:
        assert tm % (RES * 16) == 0
        qkv_spec = pl.BlockSpec((1, RES, tm // RES, tn),
                                lambda i, j: (i // tiles_per_seq, 0, i % tiles_per_seq, qkv_col(j)))
        qkv_shape = jax.ShapeDtypeStruct((m // seq, RES, seq // RES, QKV_TILES * tn), jnp.bfloat16)
    else:
        qkv_spec = pl.BlockSpec((tm, tn), lambda i, j: (i, qkv_col(j)))
        qkv_shape = jax.ShapeDtypeStruct((m, QKV_TILES * tn), jnp.bfloat16)

    def win_index(i, j):
        il = i % tiles_per_seq
        return (i // tiles_per_seq) * keep_tiles + jnp.maximum(il - first_keep, 0), 0

    win_spec = pl.BlockSpec((tm * N_HEADS_A, LANES), win_index, pipeline_mode=pl.Buffered(1))
    win_shape = jax.ShapeDtypeStruct((m // seq * keep * N_HEADS_A, LANES), jnp.float32)

    return pl.pallas_call(
        kern,
        grid=(n_i, n_j),
        in_specs=[
            pl.BlockSpec((tm, D_MODEL), lambda i, j: (i, 0)),
            pl.BlockSpec((1, D_MODEL), lambda i, j: (0, 0)),
            w_spec,
            pl.BlockSpec((LANES, D_MODEL), lambda i, j: (0, 0)),
            pl.BlockSpec((tm, LANES), lambda i, j: (i % tiles_per_seq, 0)),
            pl.BlockSpec((tm, LANES), lambda i, j: (i % tiles_per_seq, 0)),
        ] + dec_in_specs,
        out_specs=[
            qkv_spec,
            win_spec,
            win_spec,
            pl.BlockSpec((tm, tn), lambda i, j: (i, jnp.maximum(j - QKV_TILES, 0))),
            pl.BlockSpec((tm, LANES), lambda i, j: (i, 0)),
            pl.BlockSpec((TAIL_COLS, tm), lambda i, j: (0, i)),
        ] + dec_out_specs + cast_specs,
        out_shape=[
            qkv_shape,
            win_shape,
            win_shape,
            jax.ShapeDtypeStruct((m, REST_COLS), jnp.bfloat16),
            jax.ShapeDtypeStruct((m, LANES), jnp.float32),
            jax.ShapeDtypeStruct((TAIL_COLS, m), jnp.float32),
        ] + dec_out_shapes + cast_shapes,
        scratch_shapes=[pltpu.VMEM((tm, D_MODEL), jnp.bfloat16),
                        pltpu.VMEM((SUB_COLS // LANES, tm, LANES), jnp.float32),
                        pltpu.VMEM((SUB_COLS // LANES, tm, LANES), jnp.float32)],
        compiler_params=_cparams(("arbitrary", "arbitrary")),
        name="proj",
    )(x2d, g_pre, w_main_t, w_tail_t, cos_t, sin_t, *dec_args)


SLAB = 128
PIECE = 16
BLOCK_ROWS = {16: SLAB, 4: SLAB // 4, 1: PIECE}


def _band_bias(dilation):
    g = RES // dilation
    n = BLOCK_ROWS[dilation]
    rho = np.arange(g * n)[:, None]
    kap = np.arange(2 * g * n)[None, :]
    pos_q = g * (rho % n) + rho // n
    pos_k = g * (kap % (2 * n) - n) + kap // (2 * n)
    prev = (kap % (2 * n)) < n
    dist = pos_q - pos_k
    band = (dist >= 0) & (dist <= SLAB)
    out = np.stack([band & ~prev, band])
    return jnp.asarray(np.where(out, 0.0, NEG_BIG), dtype=jnp.float32)


def _attn_body(hp, q_ref, kc_ref, kp_ref, vc_ref, vp_ref, b16_ref, b4_ref, b1_ref, o_ref, nat_ref):
    f32 = jnp.float32

    def block(q, k, v, bias, prev):
        s = _nt_dot(q, k) + bias
        rows, keys = s.shape
        m_new = jnp.broadcast_to(jnp.max(s, axis=1, keepdims=True), (rows, LANES))
        if prev is not None:
            m_prev, l_prev, acc_prev = prev
            m_new = jnp.maximum(m_prev, m_new)
        p = jnp.exp(s - jnp.concatenate([m_new] * (keys // LANES), axis=1))
        v1 = jnp.concatenate([v, jnp.ones_like(v)], axis=1)
        pv = jnp.dot(p.astype(v.dtype), v1, preferred_element_type=f32)
        acc_new, l_new = pv[:, :LANES], pv[:, LANES:]
        if prev is not None:
            alpha = jnp.exp(m_prev - m_new)
            l_new = alpha * l_prev + l_new
            acc_new = alpha * acc_prev + acc_new
        return m_new, l_new, acc_new

    def split(x, n_parts):
        n = x.shape[0] // n_parts
        return [x[i * n:(i + 1) * n] for i in range(n_parts)]

    state = {}
    pieces = SLAB // PIECE

    for r in range(RES):
        k = jnp.concatenate([kp_ref[0, r], kc_ref[0, r]], axis=0)
        v = jnp.concatenate([vp_ref[0, r], vc_ref[0, r]], axis=0)
        res = [split(x, pieces) for x in block(q_ref[0, r], k, v, b16_ref[hp], None)]
        for a in range(pieces):
            state[(r, a)] = tuple(x[a] for x in res)

    def update(keys_, q, k, v, bias):
        prev = tuple(jnp.concatenate([state[key][i] for key in keys_], axis=0) for i in range(3))
        res = [split(x, len(keys_)) for x in block(q, k, v, bias, prev)]
        for i, key in enumerate(keys_):
            state[key] = tuple(x[i] for x in res)

    def gather(cur_ref, prev_ref, slabs, start, n):
        if start == 0:
            parts = [jnp.concatenate([prev_ref[0, r, SLAB - n:SLAB, :], cur_ref[0, r, 0:n, :]], axis=0)
                     for r in slabs]
        else:
            parts = [cur_ref[0, r, start - n:start + n, :] for r in slabs]
        return jnp.concatenate(parts, axis=0)

    n4 = BLOCK_ROWS[4]
    for r4 in range(4):
        slabs = [r4 + 4 * i for i in range(RES // 4)]
        for jj in range(SLAB // n4):
            sl = slice(n4 * jj, n4 * jj + n4)
            q = jnp.concatenate([q_ref[0, r, sl, :] for r in slabs], axis=0)
            k = gather(kc_ref, kp_ref, slabs, n4 * jj, n4)
            v = gather(vc_ref, vp_ref, slabs, n4 * jj, n4)
            keys_ = [(r, (n4 // PIECE) * jj + a) for r in slabs for a in range(n4 // PIECE)]
            update(keys_, q, k, v, b4_ref[hp] if jj == 0 else b4_ref[1])

    slabs = list(range(RES))
    for jj in range(pieces):
        sl = slice(PIECE * jj, PIECE * jj + PIECE)
        q = jnp.concatenate([q_ref[0, r, sl, :] for r in slabs], axis=0)
        k = gather(kc_ref, kp_ref, slabs, PIECE * jj, PIECE)
        v = gather(vc_ref, vp_ref, slabs, PIECE * jj, PIECE)
        update([(r, jj) for r in slabs], q, k, v, b1_ref[hp] if jj == 0 else b1_ref[1])

    for r in range(RES):
        acc = jnp.concatenate([state[(r, a)][2] for a in range(pieces)], axis=0)
        l = jnp.concatenate([state[(r, a)][1] for a in range(pieces)], axis=0)
        nat_ref[pl.ds(r, SLAB, stride=RES), :] = acc / l
    o_ref[...] = nat_ref[...].astype(o_ref.dtype)


DN_STEP = 256
CARRY = 8


def _split_bf16(x, n):
    parts, r = [], x
    for _ in range(n):
        hi = r.astype(jnp.bfloat16)
        parts.append(hi)
        r = r - hi.astype(jnp.float32)
    return parts


def _bdot(a, b):
    return jnp.dot(a.astype(jnp.bfloat16), b.astype(jnp.bfloat16), preferred_element_type=jnp.float32)


def _softplus(x):
    return jnp.maximum(x, 0.0) + jnp.log1p(jnp.exp(-jnp.abs(x)))


def _silu(x):
    return x * jax.nn.sigmoid(x)


def _unit_lower_inverses(a_list):
    n = a_list[0].shape[0]
    row = lax.broadcasted_iota(jnp.int32, (n, n), 0)
    col = lax.broadcasted_iota(jnp.int32, (n, n), 1)
    eye = jnp.where(row == col, 1.0, 0.0)
    xs = [eye - a for a in a_list]
    ps = [_bdot(a, a) for a in a_list]
    k = 2
    while True:
        xs = [x + _bdot(x, p) for x, p in zip(xs, ps)]
        k *= 2
        if k >= n:
            return xs
        ps = [_bdot(p, p) for p in ps]


def _dn_init(c, s_ref, xe_ref):
    @pl.when(c == 0)
    def _():
        s_ref[...] = jnp.zeros_like(s_ref)
        xe_ref[0:CARRY, :] = jnp.zeros((CARRY, CONV_CH), jnp.float32)

    @pl.when(c > 0)
    def _():
        xe_ref[0:CARRY, :] = xe_ref[DN_STEP:DN_STEP + CARRY, :]


def _dn_body(qb_ref, kb_ref, vb_ref, tail_ref, tailt_ref, cw_ref, prow_ref, pcol_ref, onw_ref,
             o_ref, s_ref, xe_ref):
    f32, bf16 = jnp.float32, jnp.bfloat16
    tb, n, hh = DN_STEP, CHUNK, N_HEADS_B
    n_ch = tb // n

    conv = []
    for part, ref in enumerate((qb_ref, kb_ref, vb_ref)):
        cs = slice(part * WIDTH_B, (part + 1) * WIDTH_B)
        xe_ref[CARRY:CARRY + tb, cs] = ref[...].astype(f32)
        y = xe_ref[CARRY:CARRY + tb, cs] * cw_ref[CONV_WIDTH - 1:CONV_WIDTH, cs]
        for i in range(CONV_WIDTH - 1):
            off = CARRY - (CONV_WIDTH - 1) + i
            y = y + xe_ref[off:off + tb, cs] * cw_ref[i:i + 1, cs]
        conv.append(_silu(y))
    qc, kc, vc = conv

    t = tail_ref[...]
    beta_c = jax.nn.sigmoid(t)
    g_c = -jnp.exp(prow_ref[0:1]) * _softplus(t + prow_ref[1:2])
    tt = tailt_ref[...]
    g_r = -jnp.exp(pcol_ref[:, 0:1]) * _softplus(tt + pcol_ref[:, 1:2])

    row = lax.broadcasted_iota(jnp.int32, (n, n), 0)
    col = lax.broadcasted_iota(jnp.int32, (n, n), 1)
    incl = row >= col
    strict = row > col
    ltri = jnp.where(incl, 1.0, 0.0).astype(bf16)
    utri = jnp.where(row <= col, 1.0, 0.0).astype(bf16)
    dot = functools.partial(jnp.dot, preferred_element_type=f32)

    gcs, grs, eg_c, ed_c, glast = [], [], [], [], []
    for ci in range(n_ch):
        rs = slice(ci * n, (ci + 1) * n)
        gc = sum(dot(ltri, part) for part in _split_bf16(g_c[rs], 3))
        gr = sum(dot(part, utri) for part in _split_bf16(g_r[:, rs], 3))
        gl = gc[n - 1:n, :]
        gcs.append(gc)
        grs.append(gr)
        glast.append(gl)
        eg_c.append(jnp.exp(gc))
        ed_c.append(jnp.exp(gl - gc))

    ids = [(ci, h) for ci in range(n_ch) for h in range(hh)]

    def head_slice(x, ci, h):
        return x[ci * n:(ci + 1) * n, h * DK_B:(h + 1) * DK_B]

    def lane_col(x, lane):
        return x[:, lane:lane + 1]

    qs, ks, kbetas, vbetas, egs = [], [], [], [], []
    for ci, h in ids:
        q = head_slice(qc, ci, h)
        k = head_slice(kc, ci, h)
        v = head_slice(vc, ci, h)
        q = q * (lax.rsqrt(jnp.sum(q * q, axis=-1, keepdims=True) + EPS) * (DK_B ** -0.5))
        k = k * lax.rsqrt(jnp.sum(k * k, axis=-1, keepdims=True) + EPS)
        beta = lane_col(beta_c[ci * n:(ci + 1) * n], h)
        qs.append(q)
        ks.append(k)
        kbetas.append(k * beta)
        vbetas.append(v * beta)
        egs.append(lane_col(eg_c[ci], hh + h))

    kqs = [_nt_dot(jnp.concatenate([kb, q], axis=0).astype(bf16), k.astype(bf16))
           for kb, q, k in zip(kbetas, qs, ks)]
    a_mats, qks = [], []
    for (ci, h), kq in zip(ids, kqs):
        gcol = lane_col(gcs[ci], hh + h)
        grow = grs[ci][hh + h:hh + h + 1, :]
        decay = jnp.exp(jnp.where(incl, gcol - grow, NEG_BIG))
        a_mats.append(jnp.where(strict, kq[:n] * decay, 0.0))
        qks.append((kq[n:] * decay).astype(bf16))
    t_mats = _unit_lower_inverses(a_mats)
    uws = [dot(tm.astype(bf16), jnp.concatenate([vb, kb * eg], axis=1).astype(bf16))
           for tm, vb, kb, eg in zip(t_mats, vbetas, kbetas, egs)]

    states = [s_ref[0, h] for h in range(hh)]
    onw = onw_ref[...]
    for ci in range(n_ch):
        base = ci * hh
        wqs = [jnp.concatenate([uws[base + h][:, DV_B:], qs[base + h] * egs[base + h]], axis=0).astype(bf16)
               for h in range(hh)]
        wss = [dot(wq, st.astype(bf16)) for wq, st in zip(wqs, states)]
        v_news = [(uws[base + h][:, :DV_B] - wss[h][:n]).astype(bf16) for h in range(hh)]
        o_in = [dot(qks[base + h], v_news[h]) for h in range(hh)]
        k_decs = [(ks[base + h] * lane_col(ed_c[ci], hh + h)).astype(bf16) for h in range(hh)]
        upds = [lax.dot_general(k_decs[h], v_news[h], (((0,), (0,)), ((), ())), preferred_element_type=f32)
                for h in range(hh)]
        e_last = jnp.exp(glast[ci])
        states = [states[h] * lane_col(e_last, hh + h) + upds[h] for h in range(hh)]
        for h in range(hh):
            o = wss[h][n:] + o_in[h]
            o = o * lax.rsqrt(jnp.mean(o * o, axis=-1, keepdims=True) + EPS) * onw
            o_ref[ci * n:(ci + 1) * n, h * DV_B:(h + 1) * DV_B] = o.astype(o_ref.dtype)
    for h in range(hh):
        s_ref[0, h] = states[h]


N_ATTN_IN, N_DN_IN = 8, 9


def _mixers_kernel(*refs, steps):
    s = pl.program_id(0)
    attn_in = refs[:N_ATTN_IN]
    dn_in = refs[N_ATTN_IN:N_ATTN_IN + N_DN_IN]
    wo_ref = refs[N_ATTN_IN + N_DN_IN]
    o_a_ref, o_b_ref, s_ref, wo_bf_ref, nat_ref, xe_ref = refs[N_ATTN_IN + N_DN_IN + 1:]
    n_sb = steps // N_HEADS_A
    _dn_init(s % steps, s_ref, xe_ref)
    wo_bf_ref[...] = wo_ref[...].astype(wo_bf_ref.dtype)
    _attn_body(jnp.minimum(s % n_sb, 1), *attn_in, o_a_ref, nat_ref)
    _dn_body(*dn_in, o_b_ref, s_ref, xe_ref)


def _prompt_mixers(qkv16, p, tail, tail_t, conv_w, a_log, dt_bias, onorm_w, w_out, layer):
    batch, _, rows, _ = qkv16.shape
    m = p.shape[0]
    n_sb = rows // SLAB
    steps = m // batch // DN_STEP
    n_steps = batch * steps
    assert N_HEADS_A * n_sb == steps
    wo_rows, wo_cols = w_out.shape[1:]
    assert wo_rows % (n_steps * 16) == 0
    wo_blk = wo_rows // n_steps
    kcol, vcol = WIDTH_A // LANES, 2 * WIDTH_A // LANES
    blk = (1, RES, SLAB, LANES)
    a_idx = lambda s: (s // steps, (s // n_sb) % N_HEADS_A, s % n_sb)
    cur = lambda off: pl.BlockSpec(blk, lambda s: (a_idx(s)[0], 0, a_idx(s)[2], off + a_idx(s)[1]))
    prv = lambda off: pl.BlockSpec(
        blk, lambda s: (a_idx(s)[0], 0, jnp.maximum(a_idx(s)[2] - 1, 0), off + a_idx(s)[1]))
    const = lambda shp: pl.BlockSpec(shp, lambda s: (0,) * len(shp))
    wb = WIDTH_B
    qcol = REST_QKVB // wb
    zeros8 = jnp.zeros((N_HEADS_B,), jnp.float32)
    gate = jnp.stack([jnp.concatenate([zeros8, a_log.astype(jnp.float32)]),
                      jnp.concatenate([zeros8, dt_bias.astype(jnp.float32)])])
    prow = jnp.pad(gate, ((0, 6), (0, LANES - TAIL_COLS)))
    pcol = jnp.pad(gate.T, ((0, 0), (0, LANES - 2)))
    tok = lambda cb: pl.BlockSpec((DN_STEP, wb), lambda s: (s, cb))
    biases = [_band_bias(dil) for _, dil in reversed(DILATIONS)]
    return pl.pallas_call(
        functools.partial(_mixers_kernel, steps=steps),
        grid=(n_steps,),
        in_specs=[cur(0), cur(kcol), prv(kcol), cur(vcol), prv(vcol)]
                 + [const(bias.shape) for bias in biases] + [
                  tok(qcol), tok(qcol + 1), tok(qcol + 2),
                  pl.BlockSpec((DN_STEP, LANES), lambda s: (s, 0)),
                  pl.BlockSpec((TAIL_COLS, DN_STEP), lambda s: (0, s)),
                  const((CONV_WIDTH, CONV_CH)), const((8, LANES)), const((TAIL_COLS, LANES)),
                  const((1, DV_B)),
                  pl.BlockSpec((None, wo_blk, wo_cols), lambda s: (layer, s, 0))],
        out_specs=[pl.BlockSpec((RES * SLAB, LANES),
                                lambda s: (a_idx(s)[0] * n_sb + a_idx(s)[2], a_idx(s)[1])),
                   pl.BlockSpec((DN_STEP, WIDTH_B), lambda s: (s, 0)),
                   pl.BlockSpec((1, N_HEADS_B, DK_B, DV_B), lambda s: (s // steps, 0, 0, 0)),
                   pl.BlockSpec((wo_blk, wo_cols), lambda s: (s, 0))],
        out_shape=[jax.ShapeDtypeStruct((m, WIDTH_A), jnp.bfloat16),
                   jax.ShapeDtypeStruct((m, WIDTH_B), jnp.bfloat16),
                   jax.ShapeDtypeStruct((batch, N_HEADS_B, DK_B, DV_B), jnp.float32),
                   jax.ShapeDtypeStruct((wo_rows, wo_cols), jnp.bfloat16)],
        scratch_shapes=[pltpu.VMEM((RES * SLAB, LANES), jnp.float32),
                        pltpu.VMEM((CARRY + DN_STEP, CONV_CH), jnp.float32)],
        compiler_params=_cparams(("arbitrary",)),
        name="prompt_mixers",
    )(qkv16, qkv16, qkv16, qkv16, qkv16, *biases,
      p, p, p, tail, tail_t, conv_w.astype(jnp.float32), prow, pcol, onorm_w.astype(jnp.float32)[None],
      w_out)


N_OUT_IN, N_DDN_IN = 7, 7


def _out_kernel(*refs, with_decode):
    oa_ref, za_ref, ob_ref, zb_ref, w_ref, g_ref, x_ref = refs[:N_OUT_IN]
    if with_decode:
        ddn_in = refs[N_OUT_IN:N_OUT_IN + N_DDN_IN]
        y_ref = refs[N_OUT_IN + N_DDN_IN]
        _decode_dn_kernel(*ddn_in, *refs[N_OUT_IN + N_DDN_IN + 1:])
    else:
        y_ref = refs[N_OUT_IN]
    f32 = jnp.float32
    ga = (oa_ref[...].astype(f32) * _silu(za_ref[...].astype(f32))).astype(jnp.bfloat16)
    gb = (ob_ref[...].astype(f32) * _silu(zb_ref[...].astype(f32))).astype(jnp.bfloat16)
    y = (jnp.dot(ga, w_ref[:WIDTH_A, :], preferred_element_type=f32)
         + jnp.dot(gb, w_ref[WIDTH_A:, :], preferred_element_type=f32))
    y = y * lax.rsqrt(jnp.mean(y * y, axis=-1, keepdims=True) + EPS) * g_ref[...]
    y_ref[...] = x_ref[...] + y


def _output_sublayer(o_a, p, o_b, w_out, g_post, x2d, *, tm, decode_dn=None):
    m = x2d.shape[0]
    row = lambda width, cb: pl.BlockSpec((tm, width), lambda i: (i, cb))
    dn_args, dn_in, dn_out, dn_shapes = decode_dn if decode_dn is not None else ([], [], [], [])
    res = pl.pallas_call(
        functools.partial(_out_kernel, with_decode=decode_dn is not None),
        grid=(m // tm,),
        in_specs=[row(WIDTH_A, 0), row(WIDTH_A, REST_ZA // WIDTH_A), row(WIDTH_B, 0),
                  row(WIDTH_B, REST_ZB // WIDTH_B),
                  pl.BlockSpec((WIDTH_A + WIDTH_B, D_MODEL), lambda i: (0, 0)),
                  pl.BlockSpec((1, D_MODEL), lambda i: (0, 0)),
                  row(D_MODEL, 0)] + dn_in,
        out_specs=[row(D_MODEL, 0)] + dn_out,
        out_shape=[jax.ShapeDtypeStruct((m, D_MODEL), jnp.float32)] + dn_shapes,
        compiler_params=_cparams(("arbitrary",)),
        name="out_proj",
    )(o_a, p, o_b, p, w_out, g_post, x2d, *dn_args)
    return res if decode_dn is not None else res[0]


def _decode_attn_kernel(q_ref, kn_ref, vn_ref, k1_ref, v1_ref, k4_ref, v4_ref, k16_ref, v16_ref, o_ref):
    f32 = jnp.float32
    q = q_ref[0].astype(f32)
    kn, vn = kn_ref[0].astype(f32), vn_ref[0].astype(f32)
    s_new = jnp.sum(q * kn, axis=-1, keepdims=True)
    scores = [jnp.sum(k_ref[0] * q[None], axis=-1, keepdims=True) for k_ref in (k1_ref, k4_ref, k16_ref)]
    m = s_new
    for s in scores:
        m = jnp.maximum(m, jnp.max(s, axis=0))
    p_new = len(DILATIONS) * jnp.exp(s_new - m)
    den = p_new
    acc = p_new * vn
    for s, v_ref in zip(scores, (v1_ref, v4_ref, v16_ref)):
        p = jnp.exp(s - m[None])
        den = den + jnp.sum(p, axis=0)
        acc = acc + jnp.sum(p * v_ref[0], axis=0)
    o_ref[0] = (acc / den).astype(o_ref.dtype)


N_DEC_OPERANDS = 9


def _decode_operands(q, k_new, v_new, cache_k, cache_v, seq_of):
    b, win, h, dd = cache_k.shape
    nb = 128
    views, specs = [], []
    for window, dil in DILATIONS:
        assert window // dil == nb and win % (nb * dil) == 0
        blk = win // (nb * dil) - 1
        if dil == 1:
            spec = pl.BlockSpec((1, nb, h, dd), lambda *ids, blk=blk: (seq_of(*ids), blk, 0, 0))
            view = lambda c: c
        else:
            spec = pl.BlockSpec((1, nb, None, h, dd), lambda *ids, blk=blk: (seq_of(*ids), blk, 0, 0, 0))
            view = lambda c, dil=dil: c.reshape(b, win // dil, dil, h, dd)
        specs += [spec, spec]
        views += [view(cache_k), view(cache_v)]
    tok = pl.BlockSpec((1, h, dd), lambda *ids: (seq_of(*ids), 0, 0))
    return ([q, k_new, v_new] + views, [tok, tok, tok] + specs, tok,
            jax.ShapeDtypeStruct((b, h, dd), jnp.bfloat16))


DEC_BB = 4


def _decode_dn_kernel(x_ref, cb_ref, cw_ref, gate_ref, prm_ref, onw_ref, s_ref,
                      o_ref, cbo_ref, so_ref):
    f32, bf16 = jnp.float32, jnp.bfloat16
    hh = N_HEADS_B
    cw = cw_ref[...]
    zeros6 = jnp.zeros((8 - 2, DK_B), bf16)
    zeros7 = jnp.zeros((8 - 1, DV_B), bf16)
    pairs, lhss, a_s, betas, vs, qks = [], [], [], [], [], []
    for b in range(x_ref.shape[0]):
        xn = x_ref[b]
        y = xn * cw[CONV_WIDTH - 1]
        for i in range(CONV_WIDTH - 1):
            y = y + cb_ref[b, i] * cw[i]
            cbo_ref[b, i] = cb_ref[b, i + 1] if i + 1 < CONV_WIDTH - 1 else xn
        y = _silu(y)
        q, k, v = y[:hh], y[hh:2 * hh], y[2 * hh:]
        q = q * lax.rsqrt(jnp.sum(q * q, axis=-1, keepdims=True) + EPS) * (DK_B ** -0.5)
        k = k * lax.rsqrt(jnp.sum(k * k, axis=-1, keepdims=True) + EPS)
        gate = gate_ref[b]
        beta = jax.nn.sigmoid(gate[:hh])
        decay = jnp.exp(-jnp.exp(prm_ref[:hh]) * _softplus(gate[hh:] + prm_ref[hh:]))
        qk = jnp.sum(q * k, axis=-1, keepdims=True)
        for h in range(hh):
            pairs.append((b, h))
            lhss.append(jnp.concatenate([k[h:h + 1].astype(bf16), q[h:h + 1].astype(bf16), zeros6], axis=0))
            a_s.append(decay[h:h + 1])
            betas.append(beta[h:h + 1])
            vs.append(v[h:h + 1])
            qks.append(qk[h:h + 1])
    states = [s_ref[b, h] for b, h in pairs]
    kss = [jnp.dot(lhs, st.astype(bf16), preferred_element_type=f32) for lhs, st in zip(lhss, states)]
    v_news = [beta * (v - a * ks[0:1]) for beta, v, a, ks in zip(betas, vs, a_s, kss)]
    upds = [lax.dot_general(lhs, jnp.concatenate([vn.astype(bf16), zeros7], axis=0),
                            (((0,), (0,)), ((), ())), preferred_element_type=f32)
            for lhs, vn in zip(lhss, v_news)]
    for (b, h), st, a, upd in zip(pairs, states, a_s, upds):
        so_ref[b, h] = st * a[:, 0:1] + upd
    onw = onw_ref[...]
    for b in range(x_ref.shape[0]):
        o = jnp.concatenate([a_s[b * hh + h] * kss[b * hh + h][1:2] + qks[b * hh + h] * v_news[b * hh + h]
                             for h in range(hh)], axis=0)
        o = o * lax.rsqrt(jnp.mean(o * o, axis=-1, keepdims=True) + EPS) * onw
        o_ref[b] = o.astype(o_ref.dtype)


def _decode_dn_operands(x_new, conv_buf, state, beta_in, a_in, conv_w, a_log, dt_bias, onorm_w, bb):
    b = x_new.shape[0]
    f32 = jnp.float32
    g3 = CONV_CH // LANES
    hh = N_HEADS_B
    gate = jnp.broadcast_to(jnp.concatenate([beta_in, a_in], axis=1).astype(f32)[:, :, None], (b, 2 * hh, LANES))
    prm = jnp.broadcast_to(jnp.concatenate([a_log, dt_bias]).astype(f32)[:, None], (2 * hh, LANES))
    full = lambda shp: pl.BlockSpec(shp, lambda i: (0,) * len(shp))
    per = lambda shp: pl.BlockSpec((bb,) + shp, lambda i: (i,) + (0,) * len(shp))
    args = [x_new.astype(f32).reshape(b, g3, LANES), conv_buf.astype(f32).reshape(b, CONV_WIDTH - 1, g3, LANES),
            conv_w.astype(f32).reshape(CONV_WIDTH, g3, LANES), gate, prm, onorm_w.astype(f32)[None],
            state.astype(f32)]
    in_specs = [per((g3, LANES)), per((CONV_WIDTH - 1, g3, LANES)), full((CONV_WIDTH, g3, LANES)),
                per((2 * hh, LANES)), full((2 * hh, LANES)), full((1, DV_B)), per((hh, DK_B, DV_B))]
    out_specs = [per((hh, DV_B)), per((CONV_WIDTH - 1, g3, LANES)), per((hh, DK_B, DV_B))]
    out_shapes = [jax.ShapeDtypeStruct((b, hh, DV_B), jnp.bfloat16),
                  jax.ShapeDtypeStruct((b, CONV_WIDTH - 1, g3, LANES), f32),
                  jax.ShapeDtypeStruct((b, hh, DK_B, DV_B), f32)]
    return args, in_specs, out_specs, out_shapes


def _decode_deltanet(*operands):
    b = operands[0].shape[0]
    bb = DEC_BB if b % DEC_BB == 0 else 1
    args, in_specs, out_specs, out_shapes = _decode_dn_operands(*operands, bb)
    return pl.pallas_call(
        _decode_dn_kernel,
        grid=(b // bb,),
        in_specs=in_specs,
        out_specs=out_specs,
        out_shape=out_shapes,
        compiler_params=_cparams(("arbitrary",)),
        name="decode_deltanet",
    )(*args)


def kernel(x_prompt, x_sample, cache_win_k, cache_win_v, state_conv, state_delta,
           g_pre, w_in, conv_w, a_log, dt_bias, onorm_w, w_out, g_post):
    f32, bf16 = jnp.float32, jnp.bfloat16
    b, s, _ = x_prompt.shape
    db, t, _ = x_sample.shape
    depth = w_in.shape[0]
    n_past = cache_win_k.shape[2]
    assert t == 1 and n_past == MAX_WINDOW and s % (RES * SLAB) == 0
    keep = min(MAX_WINDOW, s)

    cos_p, sin_p = _rope_tables(jnp.arange(s, dtype=jnp.int32))
    cos_s, sin_s = _rope_tables(jnp.full((db,), PAST_LEN, jnp.int32))

    yp = x_prompt.reshape(b * s, D_MODEL)
    ys = x_sample.reshape(db, D_MODEL)
    outs = [[] for _ in range(8)]
    for l in range(depth):
        w_in_t = jnp.swapaxes(w_in, 1, 2)
        w_tail = jnp.pad(w_in_t[l, MAIN_COLS:, :], ((0, LANES - TAIL_COLS), (0, 0))).astype(bf16)
        gp, go = g_pre[l].astype(f32)[None], g_post[l].astype(f32)[None]

        qkv_s, kf_s, vf_s, rest_s, tail_s, _, w_main = _project(
            ys, gp, w_in_t, w_tail, cos_s, sin_s, seq=db, keep=db, tm=db, residue_major=False, cast_layer=l)
        k_new = kf_s.reshape(db, N_HEADS_A, HEAD_DIM)
        v_new = vf_s.reshape(db, N_HEADS_A, HEAD_DIM)
        q_s = qkv_s[:, :WIDTH_A].reshape(db, N_HEADS_A, HEAD_DIM)

        qkv16, kf, vf, rest, tail, tail_t, o_as = _project(
            yp, gp, w_main, w_tail, cos_p, sin_p, seq=s, keep=keep, tm=1024, residue_major=True,
            decode=(q_s, k_new, v_new, cache_win_k[l], cache_win_v[l]))
        o_a, o_b, s_fin, w_o = _prompt_mixers(qkv16, rest, tail, tail_t, conv_w[l], a_log[l], dt_bias[l],
                                              onorm_w[l], w_out, l)
        outs[0].append(kf.reshape(b, keep, N_HEADS_A, HEAD_DIM))
        outs[1].append(vf.reshape(b, keep, N_HEADS_A, HEAD_DIM))
        n_tail = min(CONV_WIDTH - 1, s)
        tail_rows = rest.reshape(b, s, REST_COLS)[:, s - n_tail:, REST_QKVB:REST_QKVB + CONV_CH].astype(f32)
        outs[2].append(jnp.pad(tail_rows, ((0, 0), (CONV_WIDTH - 1 - n_tail, 0), (0, 0))))
        outs[3].append(s_fin)
        dn_operands = (rest_s[:, REST_QKVB:REST_QKVB + CONV_CH], state_conv[l], state_delta[l],
                       tail_s[:, :N_HEADS_B], tail_s[:, N_HEADS_B:TAIL_COLS], conv_w[l], a_log[l],
                       dt_bias[l], onorm_w[l])
        tm_out = 256
        if (b * s) // tm_out == db:
            yp, o_bs, cb_new, st_new = _output_sublayer(
                o_a, rest, o_b, w_o, go, yp, tm=tm_out, decode_dn=_decode_dn_operands(*dn_operands, 1))
        else:
            yp = _output_sublayer(o_a, rest, o_b, w_o, go, yp, tm=tm_out)
            o_bs, cb_new, st_new = _decode_deltanet(*dn_operands)
        cb_new = cb_new.reshape(db, CONV_WIDTH - 1, CONV_CH)
        outs[4].append(k_new.reshape(db, t, N_HEADS_A, HEAD_DIM))
        outs[5].append(v_new.reshape(db, t, N_HEADS_A, HEAD_DIM))
        outs[6].append(cb_new)
        outs[7].append(st_new)
        ys = _output_sublayer(o_as.reshape(db, WIDTH_A), rest_s, o_bs.reshape(db, WIDTH_B), w_o, go, ys, tm=db)

    stk = [jnp.stack(o) for o in outs]
    return (yp.reshape(b, s, D_MODEL), ys.reshape(db, t, D_MODEL),
            stk[0], stk[1], stk[2], stk[3], stk[4], stk[5], stk[6], stk[7])
```

```python
import functools

import jax
import jax.numpy as jnp
import numpy as np
from jax import lax
from jax.experimental import pallas as pl
from jax.experimental.pallas import tpu as pltpu

D_MODEL = 2048
HEAD_DIM = 128
N_HEADS_A = 8
N_HEADS_B = 8
DK_B = 128
DV_B = 128
WIDTH_A = N_HEADS_A * HEAD_DIM
WIDTH_B = N_HEADS_B * DV_B
DILATIONS = ((128, 1), (512, 4), (2048, 16))
MAX_WINDOW = 2048
ROPE_THETA = 500000.0
ROPE_DIM = HEAD_DIM // 4
CONV_WIDTH = 4
CONV_CH = 2 * N_HEADS_B * DK_B + N_HEADS_B * DV_B
CHUNK = 64
EPS = 1e-6
PAST_LEN = 16384
MAIN_COLS = 4 * WIDTH_A + CONV_CH + WIDTH_B
TAIL_COLS = 2 * N_HEADS_B
LANES = 128
NEG_BIG = -1e30
VMEM_LIMIT = 60 * 1024 * 1024

REST_COLS = MAIN_COLS - 3 * WIDTH_A
REST_ZA, REST_QKVB, REST_ZB = 0, WIDTH_A, WIDTH_A + CONV_CH


def _cparams(sem):
    return pltpu.CompilerParams(dimension_semantics=sem, vmem_limit_bytes=VMEM_LIMIT)


def _nt_dot(a, b):
    return lax.dot_general(a, b, (((1,), (1,)), ((), ())), preferred_element_type=jnp.float32)


RES = 16
QKV_TILES = 3
SUB_COLS = 256


def _proj_kernel(*refs, tiles_per_seq, first_keep_tile, residue_major, n_dec, cast_w):
    x_ref, g_ref, w_ref, wt_ref, cos_ref, sin_ref = refs[:6]
    dec_in = refs[6:6 + N_DEC_OPERANDS] if n_dec else ()
    outs = refs[6 + len(dec_in):]
    qkv_ref, kf_ref, vf_ref, p_ref, tail_ref, tailt_ref = outs[:6]
    dec_out = outs[6:7] if n_dec else ()
    outs = outs[6 + len(dec_out):]
    if cast_w:
        wbf_ref, outs = outs[0], outs[1:]
        wbf_ref[...] = w_ref[...].astype(wbf_ref.dtype)
        w_ref = wbf_ref
    h_ref, de_ref, mid_ref = outs
    i = pl.program_id(0)
    j = pl.program_id(1)
    tm = x_ref.shape[0]

    @pl.when(j == 0)
    def _():
        x = x_ref[...]
        y = x * lax.rsqrt(jnp.mean(x * x, axis=-1, keepdims=True) + EPS)
        h = (y * g_ref[...]).astype(jnp.bfloat16)
        h_ref[...] = h
        wt = wt_ref[...]
        tail = _nt_dot(h, wt)
        tail_ref[...] = tail
        tailt_ref[...] = tail.T[:TAIL_COLS]

    keep_rows = (i % tiles_per_seq) >= first_keep_tile
    heads_per_sub = SUB_COLS // LANES

    def sub_dot(sub):
        return _nt_dot(h_ref[...], w_ref[sub * SUB_COLS:(sub + 1) * SUB_COLS, :])

    def emit_qkv(hd, r, win_ref):
        cs = slice(hd * LANES, (hd + 1) * LANES)
        if residue_major:
            slot = hd % de_ref.shape[0]
            de_ref[slot] = r
            q4 = tm // 4
            for a in range(4):
                mid_ref[slot, a * q4:(a + 1) * q4, :] = de_ref[slot, pl.ds(a, q4, stride=4), :]
            for a in range(4):
                for bq in range(4):
                    qkv_ref[0, a + 4 * bq, :, cs] = mid_ref[
                        slot, pl.ds(a * q4 + bq, tm // RES, stride=4), :].astype(qkv_ref.dtype)
        else:
            qkv_ref[:, cs] = r.astype(qkv_ref.dtype)
        if win_ref is not None:
            win_ref[pl.ds(hd, tm, stride=N_HEADS_A), :] = r

    def rotary_tile(scale, win_ref):
        c = cos_ref[...]
        s = sin_ref[...]
        lane = lax.broadcasted_iota(jnp.int32, c.shape, 1)
        for sub in range(WIDTH_A // SUB_COLS):
            acc = sub_dot(sub)
            for hs in range(heads_per_sub):
                a = acc[:, hs * LANES:(hs + 1) * LANES]
                swapped = jnp.where(lane < ROPE_DIM // 2,
                                    pltpu.roll(a, LANES - ROPE_DIM // 2, 1),
                                    pltpu.roll(a, ROPE_DIM // 2, 1))
                r = a * c + swapped * s
                emit_qkv(sub * heads_per_sub + hs, r if scale is None else r * scale, win_ref)

    def plain_tile(win_ref):
        for sub in range(WIDTH_A // SUB_COLS):
            acc = sub_dot(sub)
            for hs in range(heads_per_sub):
                emit_qkv(sub * heads_per_sub + hs, acc[:, hs * LANES:(hs + 1) * LANES], win_ref)

    pl.when(j == 0)(lambda: rotary_tile(HEAD_DIM ** -0.5, None))
    pl.when((j == 1) & keep_rows)(lambda: rotary_tile(None, kf_ref))
    pl.when((j == 1) & jnp.logical_not(keep_rows))(lambda: rotary_tile(None, None))
    pl.when((j == 2) & keep_rows)(lambda: plain_tile(vf_ref))
    pl.when((j == 2) & jnp.logical_not(keep_rows))(lambda: plain_tile(None))

    def rest_tile(with_decode):
        for sub in range(WIDTH_A // SUB_COLS):
            p_ref[:, sub * SUB_COLS:(sub + 1) * SUB_COLS] = sub_dot(sub).astype(p_ref.dtype)
        if with_decode:
            _decode_attn_kernel(*dec_in, *dec_out)

    if n_dec:
        rest_step = i * (pl.num_programs(1) - QKV_TILES) + (j - QKV_TILES)
        pl.when((j >= QKV_TILES) & (rest_step < n_dec))(lambda: rest_tile(True))
        pl.when((j >= QKV_TILES) & (rest_step >= n_dec))(lambda: rest_tile(False))
    else:
        pl.when(j >= QKV_TILES)(lambda: rest_tile(False))


def _rope_tables(pos):
    half = ROPE_DIM // 2
    inv = ROPE_THETA ** (-jnp.arange(half, dtype=jnp.float32) / half)
    ang = pos.astype(jnp.float32)[:, None] * inv[None, :]
    cos, sin = jnp.cos(ang), jnp.sin(ang)
    n = pos.shape[0]
    ones = jnp.ones((n, LANES - ROPE_DIM), jnp.float32)
    c = jnp.concatenate([cos, cos, ones], axis=1)
    s = jnp.concatenate([-sin, sin, jnp.zeros_like(ones)], axis=1)
    return c, s


def _project(x2d, g_pre, w_main_t, w_tail_t, cos_t, sin_t, *, seq, keep, tm, residue_major, decode=None,
             cast_layer=None):
    m = x2d.shape[0]
    tn = WIDTH_A
    n_i, n_j = m // tm, MAIN_COLS // tn
    tiles_per_seq = seq // tm
    first_keep = (seq - keep) // tm
    keep_tiles = keep // tm
    assert seq % tm == 0 and keep % tm == 0 and (seq - keep) % tm == 0
    n_dec = 0
    dec_args, dec_in_specs, dec_out_specs, dec_out_shapes = [], [], [], []
    if decode is not None:
        n_dec = decode[0].shape[0]
        n_rest = n_j - QKV_TILES
        assert n_i * n_rest >= n_dec
        seq_of = lambda i, j: jnp.minimum(i * n_rest + jnp.maximum(j - QKV_TILES, 0), n_dec - 1)
        dec_args, dec_in_specs, dec_out_spec, dec_out_shape = _decode_operands(*decode, seq_of)
        dec_out_specs, dec_out_shapes = [dec_out_spec], [dec_out_shape]
    cast_w = cast_layer is not None
    w_spec = pl.BlockSpec((tn, D_MODEL), lambda i, j: (j, 0))
    cast_specs, cast_shapes = [], []
    if cast_w:
        assert n_i == 1
        cast_specs, cast_shapes = [w_spec], [jax.ShapeDtypeStruct((MAIN_COLS, D_MODEL), jnp.bfloat16)]
        w_spec = pl.BlockSpec((None, tn, D_MODEL), lambda i, j: (cast_layer, j, 0))
    kern = functools.partial(_proj_kernel, tiles_per_seq=tiles_per_seq, first_keep_tile=first_keep,
                             residue_major=residue_major, n_dec=n_dec, cast_w=cast_w)
    qkv_col = lambda j: jnp.minimum(j, QKV_TILES - 1)
    if residue_major:
        assert tm % (RES * 16) == 0
        qkv_spec = pl.BlockSpec((1, RES, tm // RES, tn),
                                lambda i, j: (i // tiles_per_seq, 0, i % tiles_per_seq, qkv_col(j)))
        qkv_shape = jax.ShapeDtypeStruct((m // seq, RES, seq // RES, QKV_TILES * tn), jnp.bfloat16)
    else:
        qkv_spec = pl.BlockSpec((tm, tn), lambda i, j: (i, qkv_col(j)))
        qkv_shape = jax.ShapeDtypeStruct((m, QKV_TILES * tn), jnp.bfloat16)

    def win_index(i, j):
        il = i % tiles_per_seq
        return (i // tiles_per_seq) * keep_tiles + jnp.maximum(il - first_keep, 0), 0

    win_spec = pl.BlockSpec((tm * N_HEADS_A, LANES), win_index, pipeline_mode=pl.Buffered(1))
    win_shape = jax.ShapeDtypeStruct((m // seq * keep * N_HEADS_A, LANES), jnp.float32)

    return pl.pallas_call(
        kern,
        grid=(n_i, n_j),
        in_specs=[
            pl.BlockSpec((tm, D_MODEL), lambda i, j: (i, 0)),
            pl.BlockSpec((1, D_MODEL), lambda i, j: (0, 0)),
            w_spec,
            pl.BlockSpec((LANES, D_MODEL), lambda i, j: (0, 0)),
            pl.BlockSpec((tm, LANES), lambda i, j: (i % tiles_per_seq, 0)),
            pl.BlockSpec((tm, LANES), lambda i, j: (i % tiles_per_seq, 0)),
        ] + dec_in_specs,
        out_specs=[
            qkv_spec,
            win_spec,
            win_spec,
            pl.BlockSpec((tm, tn), lambda i, j: (i, jnp.maximum(j - QKV_TILES, 0))),
            pl.BlockSpec((tm, LANES), lambda i, j: (i, 0)),
            pl.BlockSpec((TAIL_COLS, tm), lambda i, j: (0, i)),
        ] + dec_out_specs + cast_specs,
        out_shape=[
            qkv_shape,
            win_shape,
            win_shape,
            jax.ShapeDtypeStruct((m, REST_COLS), jnp.bfloat16),
            jax.ShapeDtypeStruct((m, LANES), jnp.float32),
            jax.ShapeDtypeStruct((TAIL_COLS, m), jnp.float32),
        ] + dec_out_shapes + cast_shapes,
        scratch_shapes=[pltpu.VMEM((tm, D_MODEL), jnp.bfloat16),
                        pltpu.VMEM((SUB_COLS // LANES, tm, LANES), jnp.float32),
                        pltpu.VMEM((SUB_COLS // LANES, tm, LANES), jnp.float32)],
        compiler_params=_cparams(("arbitrary", "arbitrary")),
        name="proj",
    )(x2d, g_pre, w_main_t, w_tail_t, cos_t, sin_t, *dec_args)


SLAB = 128
PIECE = 16
BLOCK_ROWS = {16: SLAB, 4: SLAB // 4, 1: PIECE}


def _band_bias(dilation):
    g = RES // dilation
    n = BLOCK_ROWS[dilation]
    rho = np.arange(g * n)[:, None]
    kap = np.arange(2 * g * n)[None, :]
    pos_q = g * (rho % n) + rho // n
    pos_k = g * (kap % (2 * n) - n) + kap // (2 * n)
    prev = (kap % (2 * n)) < n
    dist = pos_q - pos_k
    band = (dist >= 0) & (dist <= SLAB)
    out = np.stack([band & ~prev, band])
    return jnp.asarray(np.where(out, 0.0, NEG_BIG), dtype=jnp.float32)


def _attn_body(hp, q_ref, kc_ref, kp_ref, vc_ref, vp_ref, b16_ref, b4_ref, b1_ref, o_ref, nat_ref):
    f32 = jnp.float32

    def block(q, k, v, bias, prev):
        s = _nt_dot(q, k) + bias
        rows, keys = s.shape
        m_new = jnp.broadcast_to(jnp.max(s, axis=1, keepdims=True), (rows, LANES))
        if prev is not None:
            m_prev, l_prev, acc_prev = prev
            m_new = jnp.maximum(m_prev, m_new)
        p = jnp.exp(s - jnp.concatenate([m_new] * (keys // LANES), axis=1))
        v1 = jnp.concatenate([v, jnp.ones_like(v)], axis=1)
        pv = jnp.dot(p.astype(v.dtype), v1, preferred_element_type=f32)
        acc_new, l_new = pv[:, :LANES], pv[:, LANES:]
        if prev is not None:
            alpha = jnp.exp(m_prev - m_new)
            l_new = alpha * l_prev + l_new
            acc_new = alpha * acc_prev + acc_new
        return m_new, l_new, acc_new

    def split(x, n_parts):
        n = x.shape[0] // n_parts
        return [x[i * n:(i + 1) * n] for i in range(n_parts)]

    state = {}
    pieces = SLAB // PIECE

    for r in range(RES):
        k = jnp.concatenate([kp_ref[0, r], kc_ref[0, r]], axis=0)
        v = jnp.concatenate([vp_ref[0, r], vc_ref[0, r]], axis=0)
        res = [split(x, pieces) for x in block(q_ref[0, r], k, v, b16_ref[hp], None)]
        for a in range(pieces):
            state[(r, a)] = tuple(x[a] for x in res)

    def update(keys_, q, k, v, bias):
        prev = tuple(jnp.concatenate([state[key][i] for key in keys_], axis=0) for i in range(3))
        res = [split(x, len(keys_)) for x in block(q, k, v, bias, prev)]
        for i, key in enumerate(keys_):
            state[key] = tuple(x[i] for x in res)

    def gather(cur_ref, prev_ref, slabs, start, n):
        if start == 0:
            parts = [jnp.concatenate([prev_ref[0, r, SLAB - n:SLAB, :], cur_ref[0, r, 0:n, :]], axis=0)
                     for r in slabs]
        else:
            parts = [cur_ref[0, r, start - n:start + n, :] for r in slabs]
        return jnp.concatenate(parts, axis=0)

    n4 = BLOCK_ROWS[4]
    for r4 in range(4):
        slabs = [r4 + 4 * i for i in range(RES // 4)]
        for jj in range(SLAB // n4):
            sl = slice(n4 * jj, n4 * jj + n4)
            q = jnp.concatenate([q_ref[0, r, sl, :] for r in slabs], axis=0)
            k = gather(kc_ref, kp_ref, slabs, n4 * jj, n4)
            v = gather(vc_ref, vp_ref, slabs, n4 * jj, n4)
            keys_ = [(r, (n4 // PIECE) * jj + a) for r in slabs for a in range(n4 // PIECE)]
            update(keys_, q, k, v, b4_ref[hp] if jj == 0 else b4_ref[1])

    slabs = list(range(RES))
    for jj in range(pieces):
        sl = slice(PIECE * jj, PIECE * jj + PIECE)
        q = jnp.concatenate([q_ref[0, r, sl, :] for r in slabs], axis=0)
        k = gather(kc_ref, kp_ref, slabs, PIECE * jj, PIECE)
        v = gather(vc_ref, vp_ref, slabs, PIECE * jj, PIECE)
        update([(r, jj) for r in slabs], q, k, v, b1_ref[hp] if jj == 0 else b1_ref[1])

    for r in range(RES):
        acc = jnp.concatenate([state[(r, a)][2] for a in range(pieces)], axis=0)
        l = jnp.concatenate([state[(r, a)][1] for a in range(pieces)], axis=0)
        nat_ref[pl.ds(r, SLAB, stride=RES), :] = acc / l
    o_ref[...] = nat_ref[...].astype(o_ref.dtype)


DN_STEP = 256
CARRY = 8


def _split_bf16(x, n):
    parts, r = [], x
    for _ in range(n):
        hi = r.astype(jnp.bfloat16)
        parts.append(hi)
        r = r - hi.astype(jnp.float32)
    return parts


def _bdot(a, b):
    return jnp.dot(a.astype(jnp.bfloat16), b.astype(jnp.bfloat16), preferred_element_type=jnp.float32)


def _softplus(x):
    return jnp.maximum(x, 0.0) + jnp.log1p(jnp.exp(-jnp.abs(x)))


def _silu(x):
    return x * jax.nn.sigmoid(x)


def _unit_lower_inverses(a_list):
    n = a_list[0].shape[0]
    row = lax.broadcasted_iota(jnp.int32, (n, n), 0)
    col = lax.broadcasted_iota(jnp.int32, (n, n), 1)
    eye = jnp.where(row == col, 1.0, 0.0)
    xs = [eye - a for a in a_list]
    ps = [_bdot(a, a) for a in a_list]
    k = 2
    while True:
        xs = [x + _bdot(x, p) for x, p in zip(xs, ps)]
        k *= 2
        if k >= n:
            return xs
        ps = [_bdot(p, p) for p in ps]


def _dn_init(c, s_ref, xe_ref):
    @pl.when(c == 0)
    def _():
        s_ref[...] = jnp.zeros_like(s_ref)
        xe_ref[0:CARRY, :] = jnp.zeros((CARRY, CONV_CH), jnp.float32)

    @pl.when(c > 0)
    def _():
        xe_ref[0:CARRY, :] = xe_ref[DN_STEP:DN_STEP + CARRY, :]


def _dn_body(qb_ref, kb_ref, vb_ref, tail_ref, tailt_ref, cw_ref, prow_ref, pcol_ref, onw_ref,
             o_ref, s_ref, xe_ref):
    f32, bf16 = jnp.float32, jnp.bfloat16
    tb, n, hh = DN_STEP, CHUNK, N_HEADS_B
    n_ch = tb // n

    conv = []
    for part, ref in enumerate((qb_ref, kb_ref, vb_ref)):
        cs = slice(part * WIDTH_B, (part + 1) * WIDTH_B)
        xe_ref[CARRY:CARRY + tb, cs] = ref[...].astype(f32)
        y = xe_ref[CARRY:CARRY + tb, cs] * cw_ref[CONV_WIDTH - 1:CONV_WIDTH, cs]
        for i in range(CONV_WIDTH - 1):
            off = CARRY - (CONV_WIDTH - 1) + i
            y = y + xe_ref[off:off + tb, cs] * cw_ref[i:i + 1, cs]
        conv.append(_silu(y))
    qc, kc, vc = conv

    t = tail_ref[...]
    beta_c = jax.nn.sigmoid(t)
    g_c = -jnp.exp(prow_ref[0:1]) * _softplus(t + prow_ref[1:2])
    tt = tailt_ref[...]
    g_r = -jnp.exp(pcol_ref[:, 0:1]) * _softplus(tt + pcol_ref[:, 1:2])

    row = lax.broadcasted_iota(jnp.int32, (n, n), 0)
    col = lax.broadcasted_iota(jnp.int32, (n, n), 1)
    incl = row >= col
    strict = row > col
    ltri = jnp.where(incl, 1.0, 0.0).astype(bf16)
    utri = jnp.where(row <= col, 1.0, 0.0).astype(bf16)
    dot = functools.partial(jnp.dot, preferred_element_type=f32)

    gcs, grs, eg_c, ed_c, glast = [], [], [], [], []
    for ci in range(n_ch):
        rs = slice(ci * n, (ci + 1) * n)
        gc = sum(dot(ltri, part) for part in _split_bf16(g_c[rs], 3))
        gr = sum(dot(part, utri) for part in _split_bf16(g_r[:, rs], 3))
        gl = gc[n - 1:n, :]
        gcs.append(gc)
        grs.append(gr)
        glast.append(gl)
        eg_c.append(jnp.exp(gc))
        ed_c.append(jnp.exp(gl - gc))

    ids = [(ci, h) for ci in range(n_ch) for h in range(hh)]

    def head_slice(x, ci, h):
        return x[ci * n:(ci + 1) * n, h * DK_B:(h + 1) * DK_B]

    def lane_col(x, lane):
        return x[:, lane:lane + 1]

    qs, ks, kbetas, vbetas, egs = [], [], [], [], []
    for ci, h in ids:
        q = head_slice(qc, ci, h)
        k = head_slice(kc, ci, h)
        v = head_slice(vc, ci, h)
        q = q * (lax.rsqrt(jnp.sum(q * q, axis=-1, keepdims=True) + EPS) * (DK_B ** -0.5))
        k = k * lax.rsqrt(jnp.sum(k * k, axis=-1, keepdims=True) + EPS)
        beta = lane_col(beta_c[ci * n:(ci + 1) * n], h)
        qs.append(q)
        ks.append(k)
        kbetas.append(k * beta)
        vbetas.append(v * beta)
        egs.append(lane_col(eg_c[ci], hh + h))

    kqs = [_nt_dot(jnp.concatenate([kb, q], axis=0).astype(bf16), k.astype(bf16))
           for kb, q, k in zip(kbetas, qs, ks)]
    a_mats, qks = [], []
    for (ci, h), kq in zip(ids, kqs):
        gcol = lane_col(gcs[ci], hh + h)
        grow = grs[ci][hh + h:hh + h + 1, :]
        decay = jnp.exp(jnp.where(incl, gcol - grow, NEG_BIG))
        a_mats.append(jnp.where(strict, kq[:n] * decay, 0.0))
        qks.append((kq[n:] * decay).astype(bf16))
    t_mats = _unit_lower_inverses(a_mats)
    uws = [dot(tm.astype(bf16), jnp.concatenate([vb, kb * eg], axis=1).astype(bf16))
           for tm, vb, kb, eg in zip(t_mats, vbetas, kbetas, egs)]

    states = [s_ref[0, h] for h in range(hh)]
    onw = onw_ref[...]
    for ci in range(n_ch):
        base = ci * hh
        wqs = [jnp.concatenate([uws[base + h][:, DV_B:], qs[base + h] * egs[base + h]], axis=0).astype(bf16)
               for h in range(hh)]
        wss = [dot(wq, st.astype(bf16)) for wq, st in zip(wqs, states)]
        v_news = [(uws[base + h][:, :DV_B] - wss[h][:n]).astype(bf16) for h in range(hh)]
        o_in = [dot(qks[base + h], v_news[h]) for h in range(hh)]
        k_decs = [(ks[base + h] * lane_col(ed_c[ci], hh + h)).astype(bf16) for h in range(hh)]
        upds = [lax.dot_general(k_decs[h], v_news[h], (((0,), (0,)), ((), ())), preferred_element_type=f32)
                for h in range(hh)]
        e_last = jnp.exp(glast[ci])
        states = [states[h] * lane_col(e_last, hh + h) + upds[h] for h in range(hh)]
        for h in range(hh):
            o = wss[h][n:] + o_in[h]
            o = o * lax.rsqrt(jnp.mean(o * o, axis=-1, keepdims=True) + EPS) * onw
            o_ref[ci * n:(ci + 1) * n, h * DV_B:(h + 1) * DV_B] = o.astype(o_ref.dtype)
    for h in range(hh):
        s_ref[0, h] = states[h]


N_ATTN_IN, N_DN_IN = 8, 9


def _mixers_kernel(*refs, steps):
    s = pl.program_id(0)
    attn_in = refs[:N_ATTN_IN]
    dn_in = refs[N_ATTN_IN:N_ATTN_IN + N_DN_IN]
    wo_ref = refs[N_ATTN_IN + N_DN_IN]
    o_a_ref, o_b_ref, s_ref, wo_bf_ref, nat_ref, xe_ref = refs[N_ATTN_IN + N_DN_IN + 1:]
    n_sb = steps // N_HEADS_A
    _dn_init(s % steps, s_ref, xe_ref)
    wo_bf_ref[...] = wo_ref[...].astype(wo_bf_ref.dtype)
    _attn_body(jnp.minimum(s % n_sb, 1), *attn_in, o_a_ref, nat_ref)
    _dn_body(*dn_in, o_b_ref, s_ref, xe_ref)


def _prompt_mixers(qkv16, p, tail, tail_t, conv_w, a_log, dt_bias, onorm_w, w_out, layer):
    batch, _, rows, _ = qkv16.shape
    m = p.shape[0]
    n_sb = rows // SLAB
    steps = m // batch // DN_STEP
    n_steps = batch * steps
    assert N_HEADS_A * n_sb == steps
    wo_rows, wo_cols = w_out.shape[1:]
    assert wo_rows % (n_steps * 16) == 0
    wo_blk = wo_rows // n_steps
    kcol, vcol = WIDTH_A // LANES, 2 * WIDTH_A // LANES
    blk = (1, RES, SLAB, LANES)
    a_idx = lambda s: (s // steps, (s // n_sb) % N_HEADS_A, s % n_sb)
    cur = lambda off: pl.BlockSpec(blk, lambda s: (a_idx(s)[0], 0, a_idx(s)[2], off + a_idx(s)[1]))
    prv = lambda off: pl.BlockSpec(
        blk, lambda s: (a_idx(s)[0], 0, jnp.maximum(a_idx(s)[2] - 1, 0), off + a_idx(s)[1]))
    const = lambda shp: pl.BlockSpec(shp, lambda s: (0,) * len(shp))
    wb = WIDTH_B
    qcol = REST_QKVB // wb
    zeros8 = jnp.zeros((N_HEADS_B,), jnp.float32)
    gate = jnp.stack([jnp.concatenate([zeros8, a_log.astype(jnp.float32)]),
                      jnp.concatenate([zeros8, dt_bias.astype(jnp.float32)])])
    prow = jnp.pad(gate, ((0, 6), (0, LANES - TAIL_COLS)))
    pcol = jnp.pad(gate.T, ((0, 0), (0, LANES - 2)))
    tok = lambda cb: pl.BlockSpec((DN_STEP, wb), lambda s: (s, cb))
    biases = [_band_bias(dil) for _, dil in reversed(DILATIONS)]
    return pl.pallas_call(
        functools.partial(_mixers_kernel, steps=steps),
        grid=(n_steps,),
        in_specs=[cur(0), cur(kcol), prv(kcol), cur(vcol), prv(vcol)]
                 + [const(bias.shape) for bias in biases] + [
                  tok(qcol), tok(qcol + 1), tok(qcol + 2),
                  pl.BlockSpec((DN_STEP, LANES), lambda s: (s, 0)),
                  pl.BlockSpec((TAIL_COLS, DN_STEP), lambda s: (0, s)),
                  const((CONV_WIDTH, CONV_CH)), const((8, LANES)), const((TAIL_COLS, LANES)),
                  const((1, DV_B)),
                  pl.BlockSpec((None, wo_blk, wo_cols), lambda s: (layer, s, 0))],
        out_specs=[pl.BlockSpec((RES * SLAB, LANES),
                                lambda s: (a_idx(s)[0] * n_sb + a_idx(s)[2], a_idx(s)[1])),
                   pl.BlockSpec((DN_STEP, WIDTH_B), lambda s: (s, 0)),
                   pl.BlockSpec((1, N_HEADS_B, DK_B, DV_B), lambda s: (s // steps, 0, 0, 0)),
                   pl.BlockSpec((wo_blk, wo_cols), lambda s: (s, 0))],
        out_shape=[jax.ShapeDtypeStruct((m, WIDTH_A), jnp.bfloat16),
                   jax.ShapeDtypeStruct((m, WIDTH_B), jnp.bfloat16),
                   jax.ShapeDtypeStruct((batch, N_HEADS_B, DK_B, DV_B), jnp.float32),
                   jax.ShapeDtypeStruct((wo_rows, wo_cols), jnp.bfloat16)],
        scratch_shapes=[pltpu.VMEM((RES * SLAB, LANES), jnp.float32),
                        pltpu.VMEM((CARRY + DN_STEP, CONV_CH), jnp.float32)],
        compiler_params=_cparams(("arbitrary",)),
        name="prompt_mixers",
    )(qkv16, qkv16, qkv16, qkv16, qkv16, *biases,
      p, p, p, tail, tail_t, conv_w.astype(jnp.float32), prow, pcol, onorm_w.astype(jnp.float32)[None],
      w_out)


N_OUT_IN, N_DDN_IN = 7, 7


def _out_kernel(*refs, with_decode):
    oa_ref, za_ref, ob_ref, zb_ref, w_ref, g_ref, x_ref = refs[:N_OUT_IN]
    if with_decode:
        ddn_in = refs[N_OUT_IN:N_OUT_IN + N_DDN_IN]
        y_ref = refs[N_OUT_IN + N_DDN_IN]
        _decode_dn_kernel(*ddn_in, *refs[N_OUT_IN + N_DDN_IN + 1:])
    else:
        y_ref = refs[N_OUT_IN]
    f32 = jnp.float32
    ga = (oa_ref[...].astype(f32) * _silu(za_ref[...].astype(f32))).astype(jnp.bfloat16)
    gb = (ob_ref[...].astype(f32) * _silu(zb_ref[...].astype(f32))).astype(jnp.bfloat16)
    y = (jnp.dot(ga, w_ref[:WIDTH_A, :], preferred_element_type=f32)
         + jnp.dot(gb, w_ref[WIDTH_A:, :], preferred_element_type=f32))
    y = y * lax.rsqrt(jnp.mean(y * y, axis=-1, keepdims=True) + EPS) * g_ref[...]
    y_ref[...] = x_ref[...] + y


def _output_sublayer(o_a, p, o_b, w_out, g_post, x2d, *, tm, decode_dn=None):
    m = x2d.shape[0]
    row = lambda width, cb: pl.BlockSpec((tm, width), lambda i: (i, cb))
    dn_args, dn_in, dn_out, dn_shapes = decode_dn if decode_dn is not None else ([], [], [], [])
    res = pl.pallas_call(
        functools.partial(_out_kernel, with_decode=decode_dn is not None),
        grid=(m // tm,),
        in_specs=[row(WIDTH_A, 0), row(WIDTH_A, REST_ZA // WIDTH_A), row(WIDTH_B, 0),
                  row(WIDTH_B, REST_ZB // WIDTH_B),
                  pl.BlockSpec((WIDTH_A + WIDTH_B, D_MODEL), lambda i: (0, 0)),
                  pl.BlockSpec((1, D_MODEL), lambda i: (0, 0)),
                  row(D_MODEL, 0)] + dn_in,
        out_specs=[row(D_MODEL, 0)] + dn_out,
        out_shape=[jax.ShapeDtypeStruct((m, D_MODEL), jnp.float32)] + dn_shapes,
        compiler_params=_cparams(("arbitrary",)),
        name="out_proj",
    )(o_a, p, o_b, p, w_out, g_post, x2d, *dn_args)
    return res if decode_dn is not None else res[0]


def _decode_attn_kernel(q_ref, kn_ref, vn_ref, k1_ref, v1_ref, k4_ref, v4_ref, k16_ref, v16_ref, o_ref):
    f32 = jnp.float32
    q = q_ref[0].astype(f32)
    kn, vn = kn_ref[0].astype(f32), vn_ref[0].astype(f32)
    s_new = jnp.sum(q * kn, axis=-1, keepdims=True)
    scores = [jnp.sum(k_ref[0] * q[None], axis=-1, keepdims=True) for k_ref in (k1_ref, k4_ref, k16_ref)]
    m = s_new
    for s in scores:
        m = jnp.maximum(m, jnp.max(s, axis=0))
    p_new = len(DILATIONS) * jnp.exp(s_new - m)
    den = p_new
    acc = p_new * vn
    for s, v_ref in zip(scores, (v1_ref, v4_ref, v16_ref)):
        p = jnp.exp(s - m[None])
        den = den + jnp.sum(p, axis=0)
        acc = acc + jnp.sum(p * v_ref[0], axis=0)
    o_ref[0] = (acc / den).astype(o_ref.dtype)


N_DEC_OPERANDS = 9


def _decode_operands(q, k_new, v_new, cache_k, cache_v, seq_of):
    b, win, h, dd = cache_k.shape
    nb = 128
    views, specs = [], []
    for window, dil in DILATIONS:
        assert window // dil == nb and win % (nb * dil) == 0
        blk = win // (nb * dil) - 1
        if dil == 1:
            spec = pl.BlockSpec((1, nb, h, dd), lambda *ids, blk=blk: (seq_of(*ids), blk, 0, 0))
            view = lambda c: c
        else:
            spec = pl.BlockSpec((1, nb, None, h, dd), lambda *ids, blk=blk: (seq_of(*ids), blk, 0, 0, 0))
            view = lambda c, dil=dil: c.reshape(b, win // dil, dil, h, dd)
        specs += [spec, spec]
        views += [view(cache_k), view(cache_v)]
    tok = pl.BlockSpec((1, h, dd), lambda *ids: (seq_of(*ids), 0, 0))
    return ([q, k_new, v_new] + views, [tok, tok, tok] + specs, tok,
            jax.ShapeDtypeStruct((b, h, dd), jnp.bfloat16))


DEC_BB = 4


def _decode_dn_kernel(x_ref, cb_ref, cw_ref, gate_ref, prm_ref, onw_ref, s_ref,
                      o_ref, cbo_ref, so_ref):
    f32, bf16 = jnp.float32, jnp.bfloat16
    hh = N_HEADS_B
    cw = cw_ref[...]
    zeros6 = jnp.zeros((8 - 2, DK_B), bf16)
    zeros7 = jnp.zeros((8 - 1, DV_B), bf16)
    pairs, lhss, a_s, betas, vs, qks = [], [], [], [], [], []
    for b in range(x_ref.shape[0]):
        xn = x_ref[b]
        y = xn * cw[CONV_WIDTH - 1]
        for i in range(CONV_WIDTH - 1):
            y = y + cb_ref[b, i] * cw[i]
            cbo_ref[b, i] = cb_ref[b, i + 1] if i + 1 < CONV_WIDTH - 1 else xn
        y = _silu(y)
        q, k, v = y[:hh], y[hh:2 * hh], y[2 * hh:]
        q = q * lax.rsqrt(jnp.sum(q * q, axis=-1, keepdims=True) + EPS) * (DK_B ** -0.5)
        k = k * lax.rsqrt(jnp.sum(k * k, axis=-1, keepdims=True) + EPS)
        gate = gate_ref[b]
        beta = jax.nn.sigmoid(gate[:hh])
        decay = jnp.exp(-jnp.exp(prm_ref[:hh]) * _softplus(gate[hh:] + prm_ref[hh:]))
        qk = jnp.sum(q * k, axis=-1, keepdims=True)
        for h in range(hh):
            pairs.append((b, h))
            lhss.append(jnp.concatenate([k[h:h + 1].astype(bf16), q[h:h + 1].astype(bf16), zeros6], axis=0))
            a_s.append(decay[h:h + 1])
            betas.append(beta[h:h + 1])
            vs.append(v[h:h + 1])
            qks.append(qk[h:h + 1])
    states = [s_ref[b, h] for b, h in pairs]
    kss = [jnp.dot(lhs, st.astype(bf16), preferred_element_type=f32) for lhs, st in zip(lhss, states)]
    v_news = [beta * (v - a * ks[0:1]) for beta, v, a, ks in zip(betas, vs, a_s, kss)]
    upds = [lax.dot_general(lhs, jnp.concatenate([vn.astype(bf16), zeros7], axis=0),
                            (((0,), (0,)), ((), ())), preferred_element_type=f32)
            for lhs, vn in zip(lhss, v_news)]
    for (b, h), st, a, upd in zip(pairs, states, a_s, upds):
        so_ref[b, h] = st * a[:, 0:1] + upd
    onw = onw_ref[...]
    for b in range(x_ref.shape[0]):
        o = jnp.concatenate([a_s[b * hh + h] * kss[b * hh + h][1:2] + qks[b * hh + h] * v_news[b * hh + h]
                             for h in range(hh)], axis=0)
        o = o * lax.rsqrt(jnp.mean(o * o, axis=-1, keepdims=True) + EPS) * onw
        o_ref[b] = o.astype(o_ref.dtype)


def _decode_dn_operands(x_new, conv_buf, state, beta_in, a_in, conv_w, a_log, dt_bias, onorm_w, bb):
    b = x_new.shape[0]
    f32 = jnp.float32
    g3 = CONV_CH // LANES
    hh = N_HEADS_B
    gate = jnp.broadcast_to(jnp.concatenate([beta_in, a_in], axis=1).astype(f32)[:, :, None], (b, 2 * hh, LANES))
    prm = jnp.broadcast_to(jnp.concatenate([a_log, dt_bias]).astype(f32)[:, None], (2 * hh, LANES))
    full = lambda shp: pl.BlockSpec(shp, lambda i: (0,) * len(shp))
    per = lambda shp: pl.BlockSpec((bb,) + shp, lambda i: (i,) + (0,) * len(shp))
    args = [x_new.astype(f32).reshape(b, g3, LANES), conv_buf.astype(f32).reshape(b, CONV_WIDTH - 1, g3, LANES),
            conv_w.astype(f32).reshape(CONV_WIDTH, g3, LANES), gate, prm, onorm_w.astype(f32)[None],
            state.astype(f32)]
    in_specs = [per((g3, LANES)), per((CONV_WIDTH - 1, g3, LANES)), full((CONV_WIDTH, g3, LANES)),
                per((2 * hh, LANES)), full((2 * hh, LANES)), full((1, DV_B)), per((hh, DK_B, DV_B))]
    out_specs = [per((hh, DV_B)), per((CONV_WIDTH - 1, g3, LANES)), per((hh, DK_B, DV_B))]
    out_shapes = [jax.ShapeDtypeStruct((b, hh, DV_B), jnp.bfloat16),
                  jax.ShapeDtypeStruct((b, CONV_WIDTH - 1, g3, LANES), f32),
                  jax.ShapeDtypeStruct((b, hh, DK_B, DV_B), f32)]
    return args, in_specs, out_specs, out_shapes


def _decode_deltanet(*operands):
    b = operands[0].shape[0]
    bb = DEC_BB if b % DEC_BB == 0 else 1
    args, in_specs, out_specs, out_shapes = _decode_dn_operands(*operands, bb)
    return pl.pallas_call(
        _decode_dn_kernel,
        grid=(b // bb,),
        in_specs=in_specs,
        out_specs=out_specs,
        out_shape=out_shapes,
        compiler_params=_cparams(("arbitrary",)),
        name="decode_deltanet",
    )(*args)


def kernel(x_prompt, x_sample, cache_win_k, cache_win_v, state_conv, state_delta,
           g_pre, w_in, conv_w, a_log, dt_bias, onorm_w, w_out, g_post):
    f32, bf16 = jnp.float32, jnp.bfloat16
    b, s, _ = x_prompt.shape
    db, t, _ = x_sample.shape
    depth = w_in.shape[0]
    n_past = cache_win_k.shape[2]
    assert t == 1 and n_past == MAX_WINDOW and s % (RES * SLAB) == 0
    keep = min(MAX_WINDOW, s)

    cos_p, sin_p = _rope_tables(jnp.arange(s, dtype=jnp.int32))
    cos_s, sin_s = _rope_tables(jnp.full((db,), PAST_LEN, jnp.int32))

    yp = x_prompt.reshape(b * s, D_MODEL)
    ys = x_sample.reshape(db, D_MODEL)
    outs = [[] for _ in range(8)]
    for l in range(depth):
        w_in_t = jnp.swapaxes(w_in, 1, 2)
        w_tail = jnp.pad(w_in_t[l, MAIN_COLS:, :], ((0, LANES - TAIL_COLS), (0, 0))).astype(bf16)
        gp, go = g_pre[l].astype(f32)[None], g_post[l].astype(f32)[None]

        qkv_s, kf_s, vf_s, rest_s, tail_s, _, w_main = _project(
            ys, gp, w_in_t, w_tail, cos_s, sin_s, seq=db, keep=db, tm=db, residue_major=False, cast_layer=l)
        k_new = kf_s.reshape(db, N_HEADS_A, HEAD_DIM)
        v_new = vf_s.reshape(db, N_HEADS_A, HEAD_DIM)
        q_s = qkv_s[:, :WIDTH_A].reshape(db, N_HEADS_A, HEAD_DIM)

        qkv16, kf, vf, rest, tail, tail_t, o_as = _project(
            yp, gp, w_main, w_tail, cos_p, sin_p, seq=s, keep=keep, tm=1024, residue_major=True,
            decode=(q_s, k_new, v_new, cache_win_k[l], cache_win_v[l]))
        o_a, o_b, s_fin, w_o = _prompt_mixers(qkv16, rest, tail, tail_t, conv_w[l], a_log[l], dt_bias[l],
                                              onorm_w[l], w_out, l)
        outs[0].append(kf.reshape(b, keep, N_HEADS_A, HEAD_DIM))
        outs[1].append(vf.reshape(b, keep, N_HEADS_A, HEAD_DIM))
        n_tail = min(CONV_WIDTH - 1, s)
        tail_rows = rest.reshape(b, s, REST_COLS)[:, s - n_tail:, REST_QKVB:REST_QKVB + CONV_CH].astype(f32)
        outs[2].append(jnp.pad(tail_rows, ((0, 0), (CONV_WIDTH - 1 - n_tail, 0), (0, 0))))
        outs[3].append(s_fin)
        dn_operands = (rest_s[:, REST_QKVB:REST_QKVB + CONV_CH], state_conv[l], state_delta[l],
                       tail_s[:, :N_HEADS_B], tail_s[:, N_HEADS_B:TAIL_COLS], conv_w[l], a_log[l],
                       dt_bias[l], onorm_w[l])
        tm_out = 256
        if (b * s) // tm_out == db:
            yp, o_bs, cb_new, st_new = _output_sublayer(
                o_a, rest, o_b, w_o, go, yp, tm=tm_out, decode_dn=_decode_dn_operands(*dn_operands, 1))
        else:
            yp = _output_sublayer(o_a, rest, o_b, w_o, go, yp, tm=tm_out)
            o_bs, cb_new, st_new = _decode_deltanet(*dn_operands)
        cb_new = cb_new.reshape(db, CONV_WIDTH - 1, CONV_CH)
        outs[4].append(k_new.reshape(db, t, N_HEADS_A, HEAD_DIM))
        outs[5].append(v_new.reshape(db, t, N_HEADS_A, HEAD_DIM))
        outs[6].append(cb_new)
        outs[7].append(st_new)
        ys = _output_sublayer(o_as.reshape(db, WIDTH_A), rest_s, o_bs.reshape(db, WIDTH_B), w_o, go, ys, tm=db)

    stk = [jnp.stack(o) for o in outs]
    return (yp.reshape(b, s, D_MODEL), ys.reshape(db, t, D_MODEL),
            stk[0], stk[1], stk[2], stk[3], stk[4], stk[5], stk[6], stk[7])
```

```python
import functools

import jax
import jax.numpy as jnp
import numpy as np
from jax import lax
from jax.experimental import pallas as pl
from jax.experimental.pallas import tpu as pltpu

D_MODEL = 2048
HEAD_DIM = 128
N_HEADS_A = 8
N_HEADS_B = 8
DK_B = 128
DV_B = 128
WIDTH_A = N_HEADS_A * HEAD_DIM
WIDTH_B = N_HEADS_B * DV_B
DILATIONS = ((128, 1), (512, 4), (2048, 16))
MAX_WINDOW = 2048
ROPE_THETA = 500000.0
ROPE_DIM = HEAD_DIM // 4
CONV_WIDTH = 4
CONV_CH = 2 * N_HEADS_B * DK_B + N_HEADS_B * DV_B
CHUNK = 64
EPS = 1e-6
PAST_LEN = 16384
MAIN_COLS = 4 * WIDTH_A + CONV_CH + WIDTH_B
TAIL_COLS = 2 * N_HEADS_B
LANES = 128
NEG_BIG = -1e30
VMEM_LIMIT = 60 * 1024 * 1024

REST_COLS = MAIN_COLS - 3 * WIDTH_A
REST_ZA, REST_QKVB, REST_ZB = 0, WIDTH_A, WIDTH_A + CONV_CH


def _cparams(sem):
    return pltpu.CompilerParams(dimension_semantics=sem, vmem_limit_bytes=VMEM_LIMIT)


def _nt_dot(a, b):
    return lax.dot_general(a, b, (((1,), (1,)), ((), ())), preferred_element_type=jnp.float32)


RES = 16
QKV_TILES = 3
SUB_COLS = 256


def _proj_kernel(*refs, tiles_per_seq, first_keep_tile, residue_major, n_dec, cast_w):
    x_ref, g_ref, w_ref, wt_ref, cos_ref, sin_ref = refs[:6]
    dec_in = refs[6:6 + N_DEC_OPERANDS] if n_dec else ()
    outs = refs[6 + len(dec_in):]
    qkv_ref, kf_ref, vf_ref, p_ref, tail_ref, tailt_ref = outs[:6]
    dec_out = outs[6:7] if n_dec else ()
    outs = outs[6 + len(dec_out):]
    if cast_w:
        wbf_ref, outs = outs[0], outs[1:]
        wbf_ref[...] = w_ref[...].astype(wbf_ref.dtype)
        w_ref = wbf_ref
    h_ref, de_ref, mid_ref = outs
    i = pl.program_id(0)
    j = pl.program_id(1)
    tm = x_ref.shape[0]

    @pl.when(j == 0)
    def _():
        x = x_ref[...]
        y = x * lax.rsqrt(jnp.mean(x * x, axis=-1, keepdims=True) + EPS)
        h = (y * g_ref[...]).astype(jnp.bfloat16)
        h_ref[...] = h
        wt = wt_ref[...]
        tail = _nt_dot(h, wt)
        tail_ref[...] = tail
        tailt_ref[...] = tail.T[:TAIL_COLS]

    keep_rows = (i % tiles_per_seq) >= first_keep_tile
    heads_per_sub = SUB_COLS // LANES

    def sub_dot(sub):
        return _nt_dot(h_ref[...], w_ref[sub * SUB_COLS:(sub + 1) * SUB_COLS, :])

    def emit_qkv(hd, r, win_ref):
        cs = slice(hd * LANES, (hd + 1) * LANES)
        if residue_major:
            slot = hd % de_ref.shape[0]
            de_ref[slot] = r
            q4 = tm // 4
            for a in range(4):
                mid_ref[slot, a * q4:(a + 1) * q4, :] = de_ref[slot, pl.ds(a, q4, stride=4), :]
            for a in range(4):
                for bq in range(4):
                    qkv_ref[0, a + 4 * bq, :, cs] = mid_ref[
                        slot, pl.ds(a * q4 + bq, tm // RES, stride=4), :].astype(qkv_ref.dtype)
        else:
            qkv_ref[:, cs] = r.astype(qkv_ref.dtype)
        if win_ref is not None:
            win_ref[pl.ds(hd, tm, stride=N_HEADS_A), :] = r

    def rotary_tile(scale, win_ref):
        c = cos_ref[...]
        s = sin_ref[...]
        lane = lax.broadcasted_iota(jnp.int32, c.shape, 1)
        for sub in range(WIDTH_A // SUB_COLS):
            acc = sub_dot(sub)
            for hs in range(heads_per_sub):
                a = acc[:, hs * LANES:(hs + 1) * LANES]
                swapped = jnp.where(lane < ROPE_DIM // 2,
                                    pltpu.roll(a, LANES - ROPE_DIM // 2, 1),
                                    pltpu.roll(a, ROPE_DIM // 2, 1))
                r = a * c + swapped * s
                emit_qkv(sub * heads_per_sub + hs, r if scale is None else r * scale, win_ref)

    def plain_tile(win_ref):
        for sub in range(WIDTH_A // SUB_COLS):
            acc = sub_dot(sub)
            for hs in range(heads_per_sub):
                emit_qkv(sub * heads_per_sub + hs, acc[:, hs * LANES:(hs + 1) * LANES], win_ref)

    pl.when(j == 0)(lambda: rotary_tile(HEAD_DIM ** -0.5, None))
    pl.when((j == 1) & keep_rows)(lambda: rotary_tile(None, kf_ref))
    pl.when((j == 1) & jnp.logical_not(keep_rows))(lambda: rotary_tile(None, None))
    pl.when((j == 2) & keep_rows)(lambda: plain_tile(vf_ref))
    pl.when((j == 2) & jnp.logical_not(keep_rows))(lambda: plain_tile(None))

    def rest_tile(with_decode):
        for sub in range(WIDTH_A // SUB_COLS):
            p_ref[:, sub * SUB_COLS:(sub + 1) * SUB_COLS] = sub_dot(sub).astype(p_ref.dtype)
        if with_decode:
            _decode_attn_kernel(*dec_in, *dec_out)

    if n_dec:
        rest_step = i * (pl.num_programs(1) - QKV_TILES) + (j - QKV_TILES)
        pl.when((j >= QKV_TILES) & (rest_step < n_dec))(lambda: rest_tile(True))
        pl.when((j >= QKV_TILES) & (rest_step >= n_dec))(lambda: rest_tile(False))
    else:
        pl.when(j >= QKV_TILES)(lambda: rest_tile(False))


def _rope_tables(pos):
    half = ROPE_DIM // 2
    inv = np.float64(ROPE_THETA) ** (-np.arange(half, dtype=np.float64) / half)
    ang = np.asarray(pos, np.float64)[:, None] * inv[None, :]
    cos, sin = np.cos(ang), np.sin(ang)
    ones = np.ones((ang.shape[0], LANES - ROPE_DIM))
    c = np.concatenate([cos, cos, ones], axis=1)
    s = np.concatenate([-sin, sin, np.zeros_like(ones)], axis=1)
    return jnp.asarray(c, jnp.float32), jnp.asarray(s, jnp.float32)


def _project(x2d, g_pre, w_main_t, w_tail_t, cos_t, sin_t, *, seq, keep, tm, residue_major, decode=None,
             cast_layer=None):
    m = x2d.shape[0]
    tn = WIDTH_A
    n_i, n_j = m // tm, MAIN_COLS // tn
    tiles_per_seq = seq // tm
    first_keep = (seq - keep) // tm
    keep_tiles = keep // tm
    assert seq % tm == 0 and keep % tm == 0 and (seq - keep) % tm == 0
    n_dec = 0
    dec_args, dec_in_specs, dec_out_specs, dec_out_shapes = [], [], [], []
    if decode is not None:
        n_dec = decode[0].shape[0]
        n_rest = n_j - QKV_TILES
        assert n_i * n_rest >= n_dec
        seq_of = lambda i, j: jnp.minimum(i * n_rest + jnp.maximum(j - QKV_TILES, 0), n_dec - 1)
        dec_args, dec_in_specs, dec_out_spec, dec_out_shape = _decode_operands(*decode, seq_of)
        dec_out_specs, dec_out_shapes = [dec_out_spec], [dec_out_shape]
    cast_w = cast_layer is not None
    w_spec = pl.BlockSpec((tn, D_MODEL), lambda i, j: (j, 0))
    cast_specs, cast_shapes = [], []
    if cast_w:
        assert n_i == 1
        cast_specs, cast_shapes = [w_spec], [jax.ShapeDtypeStruct((MAIN_COLS, D_MODEL), jnp.bfloat16)]
        w_spec = pl.BlockSpec((None, tn, D_MODEL), lambda i, j: (cast_layer, j, 0))
    kern = functools.partial(_proj_kernel, tiles_per_seq=tiles_per_seq, first_keep_tile=first_keep,
                             residue_major=residue_major, n_dec=n_dec, cast_w=cast_w)
    qkv_col = lambda j: jnp.minimum(j, QKV_TILES - 1)
    if residue_major:
        assert tm % (RES * 16) == 0
        qkv_spec = pl.BlockSpec((1, RES, tm // RES, tn),
                                lambda i, j: (i // tiles_per_seq, 0, i % tiles_per_seq, qkv_col(j)))
        qkv_shape = jax.ShapeDtypeStruct((m // seq, RES, seq // RES, QKV_TILES * tn), jnp.bfloat16)
    else:
        qkv_spec = pl.BlockSpec((tm, tn), lambda i, j: (i, qkv_col(j)))
        qkv_shape = jax.ShapeDtypeStruct((m, QKV_TILES * tn), jnp.bfloat16)

    def win_index(i, j):
        il = i % tiles_per_seq
        return (i // tiles_per_seq) * keep_tiles + jnp.maximum(il - first_keep, 0), 0

    win_spec = pl.BlockSpec((tm * N_HEADS_A, LANES), win_index, pipeline_mode=pl.Buffered(1))
    win_shape = jax.ShapeDtypeStruct((m // seq * keep * N_HEADS_A, LANES), jnp.float32)

    return pl.pallas_call(
        kern,
        grid=(n_i, n_j),
        in_specs=[
            pl.BlockSpec((tm, D_MODEL), lambda i, j: (i, 0)),
            pl.BlockSpec((1, D_MODEL), lambda i, j: (0, 0)),
            w_spec,
            pl.BlockSpec((LANES, D_MODEL), lambda i, j: (0, 0)),
            pl.BlockSpec((tm, LANES), lambda i, j: (i % tiles_per_seq, 0)),
            pl.BlockSpec((tm, LANES), lambda i, j: (i % tiles_per_seq, 0)),
        ] + dec_in_specs,
        out_specs=[
            qkv_spec,
            win_spec,
            win_spec,
            pl.BlockSpec((tm, tn), lambda i, j: (i, jnp.maximum(j - QKV_TILES, 0))),
            pl.BlockSpec((tm, LANES), lambda i, j: (i, 0)),
            pl.BlockSpec((TAIL_COLS, tm), lambda i, j: (0, i)),
        ] + dec_out_specs + cast_specs,
        out_shape=[
            qkv_shape,
            win_shape,
            win_shape,
            jax.ShapeDtypeStruct((m, REST_COLS), jnp.bfloat16),
            jax.ShapeDtypeStruct((m, LANES), jnp.float32),
            jax.ShapeDtypeStruct((TAIL_COLS, m), jnp.float32),
        ] + dec_out_shapes + cast_shapes,
        scratch_shapes=[pltpu.VMEM((tm, D_MODEL), jnp.bfloat16),
                        pltpu.VMEM((SUB_COLS // LANES, tm, LANES), jnp.float32),
                        pltpu.VMEM((SUB_COLS // LANES, tm, LANES), jnp.float32)],
        compiler_params=_cparams(("arbitrary", "arbitrary")),
        name="proj",
    )(x2d, g_pre, w_main_t, w_tail_t, cos_t, sin_t, *dec_args)


SLAB = 128
PIECE = 16
BLOCK_ROWS = {16: SLAB, 4: SLAB // 4, 1: PIECE}


def _band_bias(dilation):
    g = RES // dilation
    n = BLOCK_ROWS[dilation]
    rho = np.arange(g * n)[:, None]
    kap = np.arange(2 * g * n)[None, :]
    pos_q = g * (rho % n) + rho // n
    pos_k = g * (kap % (2 * n) - n) + kap // (2 * n)
    prev = (kap % (2 * n)) < n
    dist = pos_q - pos_k
    band = (dist >= 0) & (dist <= SLAB)
    out = np.stack([band & ~prev, band])
    return jnp.asarray(np.where(out, 0.0, NEG_BIG), dtype=jnp.float32)


def _attn_body(hp, q_ref, kc_ref, kp_ref, vc_ref, vp_ref, b16_ref, b4_ref, b1_ref, o_ref, nat_ref):
    f32 = jnp.float32

    def block(q, k, v, bias, prev):
        s = _nt_dot(q, k) + bias
        rows, keys = s.shape
        m_new = jnp.broadcast_to(jnp.max(s, axis=1, keepdims=True), (rows, LANES))
        if prev is not None:
            m_prev, l_prev, acc_prev = prev
            m_new = jnp.maximum(m_prev, m_new)
        p = jnp.exp(s - jnp.concatenate([m_new] * (keys // LANES), axis=1))
        v1 = jnp.concatenate([v, jnp.ones_like(v)], axis=1)
        pv = jnp.dot(p.astype(v.dtype), v1, preferred_element_type=f32)
        acc_new, l_new = pv[:, :LANES], pv[:, LANES:]
        if prev is not None:
            alpha = jnp.exp(m_prev - m_new)
            l_new = alpha * l_prev + l_new
            acc_new = alpha * acc_prev + acc_new
        return m_new, l_new, acc_new

    def split(x, n_parts):
        n = x.shape[0] // n_parts
        return [x[i * n:(i + 1) * n] for i in range(n_parts)]

    state = {}
    pieces = SLAB // PIECE

    for r in range(RES):
        k = jnp.concatenate([kp_ref[0, r], kc_ref[0, r]], axis=0)
        v = jnp.concatenate([vp_ref[0, r], vc_ref[0, r]], axis=0)
        res = [split(x, pieces) for x in block(q_ref[0, r], k, v, b16_ref[hp], None)]
        for a in range(pieces):
            state[(r, a)] = tuple(x[a] for x in res)

    def update(keys_, q, k, v, bias):
        prev = tuple(jnp.concatenate([state[key][i] for key in keys_], axis=0) for i in range(3))
        res = [split(x, len(keys_)) for x in block(q, k, v, bias, prev)]
        for i, key in enumerate(keys_):
            state[key] = tuple(x[i] for x in res)

    def gather(cur_ref, prev_ref, slabs, start, n):
        if start == 0:
            parts = [jnp.concatenate([prev_ref[0, r, SLAB - n:SLAB, :], cur_ref[0, r, 0:n, :]], axis=0)
                     for r in slabs]
        else:
            parts = [cur_ref[0, r, start - n:start + n, :] for r in slabs]
        return jnp.concatenate(parts, axis=0)

    n4 = BLOCK_ROWS[4]
    for r4 in range(4):
        slabs = [r4 + 4 * i for i in range(RES // 4)]
        for jj in range(SLAB // n4):
            sl = slice(n4 * jj, n4 * jj + n4)
            q = jnp.concatenate([q_ref[0, r, sl, :] for r in slabs], axis=0)
            k = gather(kc_ref, kp_ref, slabs, n4 * jj, n4)
            v = gather(vc_ref, vp_ref, slabs, n4 * jj, n4)
            keys_ = [(r, (n4 // PIECE) * jj + a) for r in slabs for a in range(n4 // PIECE)]
            update(keys_, q, k, v, b4_ref[hp] if jj == 0 else b4_ref[1])

    slabs = list(range(RES))
    for jj in range(pieces):
        sl = slice(PIECE * jj, PIECE * jj + PIECE)
        q = jnp.concatenate([q_ref[0, r, sl, :] for r in slabs], axis=0)
        k = gather(kc_ref, kp_ref, slabs, PIECE * jj, PIECE)
        v = gather(vc_ref, vp_ref, slabs, PIECE * jj, PIECE)
        update([(r, jj) for r in slabs], q, k, v, b1_ref[hp] if jj == 0 else b1_ref[1])

    for r in range(RES):
        acc = jnp.concatenate([state[(r, a)][2] for a in range(pieces)], axis=0)
        l = jnp.concatenate([state[(r, a)][1] for a in range(pieces)], axis=0)
        nat_ref[pl.ds(r, SLAB, stride=RES), :] = acc / l
    o_ref[...] = nat_ref[...].astype(o_ref.dtype)


DN_STEP = 256
CARRY = 8


def _split_bf16(x, n):
    parts, r = [], x
    for _ in range(n):
        hi = r.astype(jnp.bfloat16)
        parts.append(hi)
        r = r - hi.astype(jnp.float32)
    return parts


def _bdot(a, b):
    return jnp.dot(a.astype(jnp.bfloat16), b.astype(jnp.bfloat16), preferred_element_type=jnp.float32)


def _softplus(x):
    return jnp.maximum(x, 0.0) + jnp.log1p(jnp.exp(-jnp.abs(x)))


def _silu(x):
    return x * jax.nn.sigmoid(x)


def _unit_lower_inverses(a_list):
    n = a_list[0].shape[0]
    row = lax.broadcasted_iota(jnp.int32, (n, n), 0)
    col = lax.broadcasted_iota(jnp.int32, (n, n), 1)
    eye = jnp.where(row == col, 1.0, 0.0)
    xs = [eye - a for a in a_list]
    ps = [_bdot(a, a) for a in a_list]
    k = 2
    while True:
        xs = [x + _bdot(x, p) for x, p in zip(xs, ps)]
        k *= 2
        if k >= n:
            return xs
        ps = [_bdot(p, p) for p in ps]


def _dn_init(c, s_ref, xe_ref):
    @pl.when(c == 0)
    def _():
        s_ref[...] = jnp.zeros_like(s_ref)
        xe_ref[0:CARRY, :] = jnp.zeros((CARRY, CONV_CH), jnp.float32)

    @pl.when(c > 0)
    def _():
        xe_ref[0:CARRY, :] = xe_ref[DN_STEP:DN_STEP + CARRY, :]


def _dn_body(qb_ref, kb_ref, vb_ref, tail_ref, tailt_ref, cw_ref, prow_ref, pcol_ref, onw_ref,
             o_ref, s_ref, xe_ref):
    f32, bf16 = jnp.float32, jnp.bfloat16
    tb, n, hh = DN_STEP, CHUNK, N_HEADS_B
    n_ch = tb // n

    conv = []
    for part, ref in enumerate((qb_ref, kb_ref, vb_ref)):
        cs = slice(part * WIDTH_B, (part + 1) * WIDTH_B)
        xe_ref[CARRY:CARRY + tb, cs] = ref[...].astype(f32)
        y = xe_ref[CARRY:CARRY + tb, cs] * cw_ref[CONV_WIDTH - 1:CONV_WIDTH, cs]
        for i in range(CONV_WIDTH - 1):
            off = CARRY - (CONV_WIDTH - 1) + i
            y = y + xe_ref[off:off + tb, cs] * cw_ref[i:i + 1, cs]
        conv.append(_silu(y))
    qc, kc, vc = conv

    t = tail_ref[...]
    beta_c = jax.nn.sigmoid(t)
    g_c = -jnp.exp(prow_ref[0:1]) * _softplus(t + prow_ref[1:2])
    tt = tailt_ref[...]
    g_r = -jnp.exp(pcol_ref[:, 0:1]) * _softplus(tt + pcol_ref[:, 1:2])

    row = lax.broadcasted_iota(jnp.int32, (n, n), 0)
    col = lax.broadcasted_iota(jnp.int32, (n, n), 1)
    incl = row >= col
    strict = row > col
    ltri = jnp.where(incl, 1.0, 0.0).astype(bf16)
    utri = jnp.where(row <= col, 1.0, 0.0).astype(bf16)
    dot = functools.partial(jnp.dot, preferred_element_type=f32)

    gcs, grs, eg_c, ed_c, glast = [], [], [], [], []
    for ci in range(n_ch):
        rs = slice(ci * n, (ci + 1) * n)
        gc = sum(dot(ltri, part) for part in _split_bf16(g_c[rs], 3))
        gr = sum(dot(part, utri) for part in _split_bf16(g_r[:, rs], 3))
        gl = gc[n - 1:n, :]
        gcs.append(gc)
        grs.append(gr)
        glast.append(gl)
        eg_c.append(jnp.exp(gc))
        ed_c.append(jnp.exp(gl - gc))

    ids = [(ci, h) for ci in range(n_ch) for h in range(hh)]

    def head_slice(x, ci, h):
        return x[ci * n:(ci + 1) * n, h * DK_B:(h + 1) * DK_B]

    def lane_col(x, lane):
        return x[:, lane:lane + 1]

    qs, ks, kbetas, vbetas, egs = [], [], [], [], []
    for ci, h in ids:
        q = head_slice(qc, ci, h)
        k = head_slice(kc, ci, h)
        v = head_slice(vc, ci, h)
        q = q * (lax.rsqrt(jnp.sum(q * q, axis=-1, keepdims=True) + EPS) * (DK_B ** -0.5))
        k = k * lax.rsqrt(jnp.sum(k * k, axis=-1, keepdims=True) + EPS)
        beta = lane_col(beta_c[ci * n:(ci + 1) * n], h)
        qs.append(q)
        ks.append(k)
        kbetas.append(k * beta)
        vbetas.append(v * beta)
        egs.append(lane_col(eg_c[ci], hh + h))

    kqs = [_nt_dot(jnp.concatenate([kb, q], axis=0).astype(bf16), k.astype(bf16))
           for kb, q, k in zip(kbetas, qs, ks)]
    a_mats, qks = [], []
    for (ci, h), kq in zip(ids, kqs):
        gcol = lane_col(gcs[ci], hh + h)
        grow = grs[ci][hh + h:hh + h + 1, :]
        decay = jnp.exp(jnp.where(incl, gcol - grow, NEG_BIG))
        a_mats.append(jnp.where(strict, kq[:n] * decay, 0.0))
        qks.append((kq[n:] * decay).astype(bf16))
    t_mats = _unit_lower_inverses(a_mats)
    uws = [dot(tm.astype(bf16), jnp.concatenate([vb, kb * eg], axis=1).astype(bf16))
           for tm, vb, kb, eg in zip(t_mats, vbetas, kbetas, egs)]

    states = [s_ref[0, h] for h in range(hh)]
    onw = onw_ref[...]
    for ci in range(n_ch):
        base = ci * hh
        wqs = [jnp.concatenate([uws[base + h][:, DV_B:], qs[base + h] * egs[base + h]], axis=0).astype(bf16)
               for h in range(hh)]
        wss = [dot(wq, st.astype(bf16)) for wq, st in zip(wqs, states)]
        v_news = [(uws[base + h][:, :DV_B] - wss[h][:n]).astype(bf16) for h in range(hh)]
        o_in = [dot(qks[base + h], v_news[h]) for h in range(hh)]
        k_decs = [(ks[base + h] * lane_col(ed_c[ci], hh + h)).astype(bf16) for h in range(hh)]
        upds = [lax.dot_general(k_decs[h], v_news[h], (((0,), (0,)), ((), ())), preferred_element_type=f32)
                for h in range(hh)]
        e_last = jnp.exp(glast[ci])
        states = [states[h] * lane_col(e_last, hh + h) + upds[h] for h in range(hh)]
        for h in range(hh):
            o = wss[h][n:] + o_in[h]
            o = o * lax.rsqrt(jnp.mean(o * o, axis=-1, keepdims=True) + EPS) * onw
            o_ref[ci * n:(ci + 1) * n, h * DV_B:(h + 1) * DV_B] = o.astype(o_ref.dtype)
    for h in range(hh):
        s_ref[0, h] = states[h]


N_ATTN_IN, N_DN_IN = 8, 9


def _mixers_kernel(*refs, steps):
    s = pl.program_id(0)
    attn_in = refs[:N_ATTN_IN]
    dn_in = refs[N_ATTN_IN:N_ATTN_IN + N_DN_IN]
    wo_ref = refs[N_ATTN_IN + N_DN_IN]
    o_a_ref, o_b_ref, s_ref, wo_bf_ref, nat_ref, xe_ref = refs[N_ATTN_IN + N_DN_IN + 1:]
    n_sb = steps // N_HEADS_A
    _dn_init(s % steps, s_ref, xe_ref)
    wo_bf_ref[...] = wo_ref[...].astype(wo_bf_ref.dtype)
    _attn_body(jnp.minimum(s % n_sb, 1), *attn_in, o_a_ref, nat_ref)
    _dn_body(*dn_in, o_b_ref, s_ref, xe_ref)


def _prompt_mixers(qkv16, p, tail, tail_t, conv_w, a_log, dt_bias, onorm_w, w_out, layer):
    batch, _, rows, _ = qkv16.shape
    m = p.shape[0]
    n_sb = rows // SLAB
    steps = m // batch // DN_STEP
    n_steps = batch * steps
    assert N_HEADS_A * n_sb == steps
    wo_rows, wo_cols = w_out.shape[1:]
    assert wo_rows % (n_steps * 16) == 0
    wo_blk = wo_rows // n_steps
    kcol, vcol = WIDTH_A // LANES, 2 * WIDTH_A // LANES
    blk = (1, RES, SLAB, LANES)
    a_idx = lambda s: (s // steps, (s // n_sb) % N_HEADS_A, s % n_sb)
    cur = lambda off: pl.BlockSpec(blk, lambda s: (a_idx(s)[0], 0, a_idx(s)[2], off + a_idx(s)[1]))
    prv = lambda off: pl.BlockSpec(
        blk, lambda s: (a_idx(s)[0], 0, jnp.maximum(a_idx(s)[2] - 1, 0), off + a_idx(s)[1]))
    const = lambda shp: pl.BlockSpec(shp, lambda s: (0,) * len(shp))
    wb = WIDTH_B
    qcol = REST_QKVB // wb
    zeros8 = jnp.zeros((N_HEADS_B,), jnp.float32)
    gate = jnp.stack([jnp.concatenate([zeros8, a_log.astype(jnp.float32)]),
                      jnp.concatenate([zeros8, dt_bias.astype(jnp.float32)])])
    prow = jnp.pad(gate, ((0, 6), (0, LANES - TAIL_COLS)))
    pcol = jnp.pad(gate.T, ((0, 0), (0, LANES - 2)))
    tok = lambda cb: pl.BlockSpec((DN_STEP, wb), lambda s: (s, cb))
    biases = [_band_bias(dil) for _, dil in reversed(DILATIONS)]
    return pl.pallas_call(
        functools.partial(_mixers_kernel, steps=steps),
        grid=(n_steps,),
        in_specs=[cur(0), cur(kcol), prv(kcol), cur(vcol), prv(vcol)]
                 + [const(bias.shape) for bias in biases] + [
                  tok(qcol), tok(qcol + 1), tok(qcol + 2),
                  pl.BlockSpec((DN_STEP, LANES), lambda s: (s, 0)),
                  pl.BlockSpec((TAIL_COLS, DN_STEP), lambda s: (0, s)),
                  const((CONV_WIDTH, CONV_CH)), const((8, LANES)), const((TAIL_COLS, LANES)),
                  const((1, DV_B)),
                  pl.BlockSpec((None, wo_blk, wo_cols), lambda s: (layer, s, 0))],
        out_specs=[pl.BlockSpec((RES * SLAB, LANES),
                                lambda s: (a_idx(s)[0] * n_sb + a_idx(s)[2], a_idx(s)[1])),
                   pl.BlockSpec((DN_STEP, WIDTH_B), lambda s: (s, 0)),
                   pl.BlockSpec((1, N_HEADS_B, DK_B, DV_B), lambda s: (s // steps, 0, 0, 0)),
                   pl.BlockSpec((wo_blk, wo_cols), lambda s: (s, 0))],
        out_shape=[jax.ShapeDtypeStruct((m, WIDTH_A), jnp.bfloat16),
                   jax.ShapeDtypeStruct((m, WIDTH_B), jnp.bfloat16),
                   jax.ShapeDtypeStruct((batch, N_HEADS_B, DK_B, DV_B), jnp.float32),
                   jax.ShapeDtypeStruct((wo_rows, wo_cols), jnp.bfloat16)],
        scratch_shapes=[pltpu.VMEM((RES * SLAB, LANES), jnp.float32),
                        pltpu.VMEM((CARRY + DN_STEP, CONV_CH), jnp.float32)],
        compiler_params=_cparams(("arbitrary",)),
        name="prompt_mixers",
    )(qkv16, qkv16, qkv16, qkv16, qkv16, *biases,
      p, p, p, tail, tail_t, conv_w.astype(jnp.float32), prow, pcol, onorm_w.astype(jnp.float32)[None],
      w_out)


N_OUT_IN, N_DDN_IN = 7, 7


def _out_kernel(*refs, with_decode):
    oa_ref, za_ref, ob_ref, zb_ref, w_ref, g_ref, x_ref = refs[:N_OUT_IN]
    if with_decode:
        ddn_in = refs[N_OUT_IN:N_OUT_IN + N_DDN_IN]
        y_ref = refs[N_OUT_IN + N_DDN_IN]
        _decode_dn_kernel(*ddn_in, *refs[N_OUT_IN + N_DDN_IN + 1:])
    else:
        y_ref = refs[N_OUT_IN]
    f32 = jnp.float32
    ga = (oa_ref[...].astype(f32) * _silu(za_ref[...].astype(f32))).astype(jnp.bfloat16)
    gb = (ob_ref[...].astype(f32) * _silu(zb_ref[...].astype(f32))).astype(jnp.bfloat16)
    y = (jnp.dot(ga, w_ref[:WIDTH_A, :], preferred_element_type=f32)
         + jnp.dot(gb, w_ref[WIDTH_A:, :], preferred_element_type=f32))
    y = y * lax.rsqrt(jnp.mean(y * y, axis=-1, keepdims=True) + EPS) * g_ref[...]
    y_ref[...] = x_ref[...] + y


def _output_sublayer(o_a, p, o_b, w_out, g_post, x2d, *, tm, decode_dn=None):
    m = x2d.shape[0]
    row = lambda width, cb: pl.BlockSpec((tm, width), lambda i: (i, cb))
    dn_args, dn_in, dn_out, dn_shapes = decode_dn if decode_dn is not None else ([], [], [], [])
    res = pl.pallas_call(
        functools.partial(_out_kernel, with_decode=decode_dn is not None),
        grid=(m // tm,),
        in_specs=[row(WIDTH_A, 0), row(WIDTH_A, REST_ZA // WIDTH_A), row(WIDTH_B, 0),
                  row(WIDTH_B, REST_ZB // WIDTH_B),
                  pl.BlockSpec((WIDTH_A + WIDTH_B, D_MODEL), lambda i: (0, 0)),
                  pl.BlockSpec((1, D_MODEL), lambda i: (0, 0)),
                  row(D_MODEL, 0)] + dn_in,
        out_specs=[row(D_MODEL, 0)] + dn_out,
        out_shape=[jax.ShapeDtypeStruct((m, D_MODEL), jnp.float32)] + dn_shapes,
        compiler_params=_cparams(("arbitrary",)),
        name="out_proj",
    )(o_a, p, o_b, p, w_out, g_post, x2d, *dn_args)
    return res if decode_dn is not None else res[0]


def _decode_attn_kernel(q_ref, kn_ref, vn_ref, k1_ref, v1_ref, k4_ref, v4_ref, k16_ref, v16_ref, o_ref):
    f32 = jnp.float32
    q = q_ref[0].astype(f32)
    kn, vn = kn_ref[0].astype(f32), vn_ref[0].astype(f32)
    s_new = jnp.sum(q * kn, axis=-1, keepdims=True)
    scores = [jnp.sum(k_ref[0] * q[None], axis=-1, keepdims=True) for k_ref in (k1_ref, k4_ref, k16_ref)]
    m = s_new
    for s in scores:
        m = jnp.maximum(m, jnp.max(s, axis=0))
    p_new = len(DILATIONS) * jnp.exp(s_new - m)
    den = p_new
    acc = p_new * vn
    for s, v_ref in zip(scores, (v1_ref, v4_ref, v16_ref)):
        p = jnp.exp(s - m[None])
        den = den + jnp.sum(p, axis=0)
        acc = acc + jnp.sum(p * v_ref[0], axis=0)
    o_ref[0] = (acc / den).astype(o_ref.dtype)


N_DEC_OPERANDS = 9


def _decode_operands(q, k_new, v_new, cache_k, cache_v, seq_of):
    b, win, h, dd = cache_k.shape
    nb = 128
    views, specs = [], []
    for window, dil in DILATIONS:
        assert window // dil == nb and win % (nb * dil) == 0
        blk = win // (nb * dil) - 1
        if dil == 1:
            spec = pl.BlockSpec((1, nb, h, dd), lambda *ids, blk=blk: (seq_of(*ids), blk, 0, 0))
            view = lambda c: c
        else:
            spec = pl.BlockSpec((1, nb, None, h, dd), lambda *ids, blk=blk: (seq_of(*ids), blk, 0, 0, 0))
            view = lambda c, dil=dil: c.reshape(b, win // dil, dil, h, dd)
        specs += [spec, spec]
        views += [view(cache_k), view(cache_v)]
    tok = pl.BlockSpec((1, h, dd), lambda *ids: (seq_of(*ids), 0, 0))
    return ([q, k_new, v_new] + views, [tok, tok, tok] + specs, tok,
            jax.ShapeDtypeStruct((b, h, dd), jnp.bfloat16))


DEC_BB = 4


def _decode_dn_kernel(x_ref, cb_ref, cw_ref, gate_ref, prm_ref, onw_ref, s_ref,
                      o_ref, cbo_ref, so_ref):
    f32, bf16 = jnp.float32, jnp.bfloat16
    hh = N_HEADS_B
    cw = cw_ref[...]
    zeros6 = jnp.zeros((8 - 2, DK_B), bf16)
    zeros7 = jnp.zeros((8 - 1, DV_B), bf16)
    pairs, lhss, a_s, betas, vs, qks = [], [], [], [], [], []
    for b in range(x_ref.shape[0]):
        xn = x_ref[b]
        y = xn * cw[CONV_WIDTH - 1]
        for i in range(CONV_WIDTH - 1):
            y = y + cb_ref[b, i] * cw[i]
            cbo_ref[b, i] = cb_ref[b, i + 1] if i + 1 < CONV_WIDTH - 1 else xn
        y = _silu(y)
        q, k, v = y[:hh], y[hh:2 * hh], y[2 * hh:]
        q = q * lax.rsqrt(jnp.sum(q * q, axis=-1, keepdims=True) + EPS) * (DK_B ** -0.5)
        k = k * lax.rsqrt(jnp.sum(k * k, axis=-1, keepdims=True) + EPS)
        gate = gate_ref[b]
        beta = jax.nn.sigmoid(gate[:hh])
        decay = jnp.exp(-jnp.exp(prm_ref[:hh]) * _softplus(gate[hh:] + prm_ref[hh:]))
        qk = jnp.sum(q * k, axis=-1, keepdims=True)
        for h in range(hh):
            pairs.append((b, h))
            lhss.append(jnp.concatenate([k[h:h + 1].astype(bf16), q[h:h + 1].astype(bf16), zeros6], axis=0))
            a_s.append(decay[h:h + 1])
            betas.append(beta[h:h + 1])
            vs.append(v[h:h + 1])
            qks.append(qk[h:h + 1])
    states = [s_ref[b, h] for b, h in pairs]
    kss = [jnp.dot(lhs, st.astype(bf16), preferred_element_type=f32) for lhs, st in zip(lhss, states)]
    v_news = [beta * (v - a * ks[0:1]) for beta, v, a, ks in zip(betas, vs, a_s, kss)]
    upds = [lax.dot_general(lhs, jnp.concatenate([vn.astype(bf16), zeros7], axis=0),
                            (((0,), (0,)), ((), ())), preferred_element_type=f32)
            for lhs, vn in zip(lhss, v_news)]
    for (b, h), st, a, upd in zip(pairs, states, a_s, upds):
        so_ref[b, h] = st * a[:, 0:1] + upd
    onw = onw_ref[...]
    for b in range(x_ref.shape[0]):
        o = jnp.concatenate([a_s[b * hh + h] * kss[b * hh + h][1:2] + qks[b * hh + h] * v_news[b * hh + h]
                             for h in range(hh)], axis=0)
        o = o * lax.rsqrt(jnp.mean(o * o, axis=-1, keepdims=True) + EPS) * onw
        o_ref[b] = o.astype(o_ref.dtype)


def _decode_dn_operands(x_new, conv_buf, state, beta_in, a_in, conv_w, a_log, dt_bias, onorm_w, bb):
    b = x_new.shape[0]
    f32 = jnp.float32
    g3 = CONV_CH // LANES
    hh = N_HEADS_B
    gate = jnp.broadcast_to(jnp.concatenate([beta_in, a_in], axis=1).astype(f32)[:, :, None], (b, 2 * hh, LANES))
    prm = jnp.broadcast_to(jnp.concatenate([a_log, dt_bias]).astype(f32)[:, None], (2 * hh, LANES))
    full = lambda shp: pl.BlockSpec(shp, lambda i: (0,) * len(shp))
    per = lambda shp: pl.BlockSpec((bb,) + shp, lambda i: (i,) + (0,) * len(shp))
    args = [x_new.astype(f32).reshape(b, g3, LANES), conv_buf.astype(f32).reshape(b, CONV_WIDTH - 1, g3, LANES),
            conv_w.astype(f32).reshape(CONV_WIDTH, g3, LANES), gate, prm, onorm_w.astype(f32)[None],
            state.astype(f32)]
    in_specs = [per((g3, LANES)), per((CONV_WIDTH - 1, g3, LANES)), full((CONV_WIDTH, g3, LANES)),
                per((2 * hh, LANES)), full((2 * hh, LANES)), full((1, DV_B)), per((hh, DK_B, DV_B))]
    out_specs = [per((hh, DV_B)), per((CONV_WIDTH - 1, g3, LANES)), per((hh, DK_B, DV_B))]
    out_shapes = [jax.ShapeDtypeStruct((b, hh, DV_B), jnp.bfloat16),
                  jax.ShapeDtypeStruct((b, CONV_WIDTH - 1, g3, LANES), f32),
                  jax.ShapeDtypeStruct((b, hh, DK_B, DV_B), f32)]
    return args, in_specs, out_specs, out_shapes


def _decode_deltanet(*operands):
    b = operands[0].shape[0]
    bb = DEC_BB if b % DEC_BB == 0 else 1
    args, in_specs, out_specs, out_shapes = _decode_dn_operands(*operands, bb)
    return pl.pallas_call(
        _decode_dn_kernel,
        grid=(b // bb,),
        in_specs=in_specs,
        out_specs=out_specs,
        out_shape=out_shapes,
        compiler_params=_cparams(("arbitrary",)),
        name="decode_deltanet",
    )(*args)


def kernel(x_prompt, x_sample, cache_win_k, cache_win_v, state_conv, state_delta,
           g_pre, w_in, conv_w, a_log, dt_bias, onorm_w, w_out, g_post):
    f32, bf16 = jnp.float32, jnp.bfloat16
    b, s, _ = x_prompt.shape
    db, t, _ = x_sample.shape
    depth = w_in.shape[0]
    n_past = cache_win_k.shape[2]
    assert t == 1 and n_past == MAX_WINDOW and s % (RES * SLAB) == 0
    keep = min(MAX_WINDOW, s)

    cos_p, sin_p = _rope_tables(np.arange(s))
    cos_s, sin_s = _rope_tables(np.full((db,), PAST_LEN))

    yp = x_prompt.reshape(b * s, D_MODEL)
    ys = x_sample.reshape(db, D_MODEL)
    outs = [[] for _ in range(8)]
    for l in range(depth):
        w_in_t = jnp.swapaxes(w_in, 1, 2)
        w_tail = jnp.pad(w_in_t[l, MAIN_COLS:, :], ((0, LANES - TAIL_COLS), (0, 0))).astype(bf16)
        gp, go = g_pre[l].astype(f32)[None], g_post[l].astype(f32)[None]

        qkv_s, kf_s, vf_s, rest_s, tail_s, _, w_main = _project(
            ys, gp, w_in_t, w_tail, cos_s, sin_s, seq=db, keep=db, tm=db, residue_major=False, cast_layer=l)
        k_new = kf_s.reshape(db, N_HEADS_A, HEAD_DIM)
        v_new = vf_s.reshape(db, N_HEADS_A, HEAD_DIM)
        q_s = qkv_s[:, :WIDTH_A].reshape(db, N_HEADS_A, HEAD_DIM)

        qkv16, kf, vf, rest, tail, tail_t, o_as = _project(
            yp, gp, w_main, w_tail, cos_p, sin_p, seq=s, keep=keep, tm=1024, residue_major=True,
            decode=(q_s, k_new, v_new, cache_win_k[l], cache_win_v[l]))
        o_a, o_b, s_fin, w_o = _prompt_mixers(qkv16, rest, tail, tail_t, conv_w[l], a_log[l], dt_bias[l],
                                              onorm_w[l], w_out, l)
        outs[0].append(kf.reshape(b, keep, N_HEADS_A, HEAD_DIM))
        outs[1].append(vf.reshape(b, keep, N_HEADS_A, HEAD_DIM))
        n_tail = min(CONV_WIDTH - 1, s)
        tail_rows = rest.reshape(b, s, REST_COLS)[:, s - n_tail:, REST_QKVB:REST_QKVB + CONV_CH].astype(f32)
        outs[2].append(jnp.pad(tail_rows, ((0, 0), (CONV_WIDTH - 1 - n_tail, 0), (0, 0))))
        outs[3].append(s_fin)
        dn_operands = (rest_s[:, REST_QKVB:REST_QKVB + CONV_CH], state_conv[l], state_delta[l],
                       tail_s[:, :N_HEADS_B], tail_s[:, N_HEADS_B:TAIL_COLS], conv_w[l], a_log[l],
                       dt_bias[l], onorm_w[l])
        tm_out = 256
        if (b * s) // tm_out == db:
            yp, o_bs, cb_new, st_new = _output_sublayer(
                o_a, rest, o_b, w_o, go, yp, tm=tm_out, decode_dn=_decode_dn_operands(*dn_operands, 1))
        else:
            yp = _output_sublayer(o_a, rest, o_b, w_o, go, yp, tm=tm_out)
            o_bs, cb_new, st_new = _decode_deltanet(*dn_operands)
        cb_new = cb_new.reshape(db, CONV_WIDTH - 1, CONV_CH)
        outs[4].append(k_new.reshape(db, t, N_HEADS_A, HEAD_DIM))
        outs[5].append(v_new.reshape(db, t, N_HEADS_A, HEAD_DIM))
        outs[6].append(cb_new)
        outs[7].append(st_new)
        ys = _output_sublayer(o_as.reshape(db, WIDTH_A), rest_s, o_bs.reshape(db, WIDTH_B), w_o, go, ys, tm=db)

    stk = [jnp.stack(o) for o in outs]
    return (yp.reshape(b, s, D_MODEL), ys.reshape(db, t, D_MODEL),
            stk[0], stk[1], stk[2], stk[3], stk[4], stk[5], stk[6], stk[7])
```

```python
import functools

import jax
import jax.numpy as jnp
import numpy as np
from jax import lax
from jax.experimental import pallas as pl
from jax.experimental.pallas import tpu as pltpu

D_MODEL = 2048
HEAD_DIM = 128
N_HEADS_A = 8
N_HEADS_B = 8
DK_B = 128
DV_B = 128
WIDTH_A = N_HEADS_A * HEAD_DIM
WIDTH_B = N_HEADS_B * DV_B
DILATIONS = ((128, 1), (512, 4), (2048, 16))
MAX_WINDOW = 2048
ROPE_THETA = 500000.0
ROPE_DIM = HEAD_DIM // 4
CONV_WIDTH = 4
CONV_CH = 2 * N_HEADS_B * DK_B + N_HEADS_B * DV_B
CHUNK = 64
EPS = 1e-6
PAST_LEN = 16384
MAIN_COLS = 4 * WIDTH_A + CONV_CH + WIDTH_B
TAIL_COLS = 2 * N_HEADS_B
LANES = 128
NEG_BIG = -1e30
VMEM_LIMIT = 60 * 1024 * 1024

REST_COLS = MAIN_COLS - 3 * WIDTH_A
REST_ZA, REST_QKVB, REST_ZB = 0, WIDTH_A, WIDTH_A + CONV_CH


def _cparams(sem):
    return pltpu.CompilerParams(dimension_semantics=sem, vmem_limit_bytes=VMEM_LIMIT)


def _nt_dot(a, b):
    return lax.dot_general(a, b, (((1,), (1,)), ((), ())), preferred_element_type=jnp.float32)


RES = 16
QKV_TILES = 3
SUB_COLS = 256


def _proj_kernel(*refs, tiles_per_seq, first_keep_tile, residue_major, n_dec, cast_w):
    x_ref, g_ref, w_ref, wt_ref, cos_ref, sin_ref = refs[:6]
    dec_in = refs[6:6 + N_DEC_OPERANDS] if n_dec else ()
    outs = refs[6 + len(dec_in):]
    qkv_ref, kf_ref, vf_ref, p_ref, tail_ref, tailt_ref = outs[:6]
    dec_out = outs[6:7] if n_dec else ()
    outs = outs[6 + len(dec_out):]
    if cast_w:
        wbf_ref, outs = outs[0], outs[1:]
        wbf_ref[...] = w_ref[...].astype(wbf_ref.dtype)
        w_ref = wbf_ref
    h_ref, de_ref, mid_ref = outs
    i = pl.program_id(0)
    j = pl.program_id(1)
    tm = x_ref.shape[0]

    @pl.when(j == 0)
    def _():
        x = x_ref[...]
        y = x * lax.rsqrt(jnp.mean(x * x, axis=-1, keepdims=True) + EPS)
        h = (y * g_ref[...]).astype(jnp.bfloat16)
        h_ref[...] = h
        wt = wt_ref[...]
        tail = _nt_dot(h, wt)
        tail_ref[...] = tail
        tailt_ref[...] = tail.T[:TAIL_COLS]

    keep_rows = (i % tiles_per_seq) >= first_keep_tile
    heads_per_sub = SUB_COLS // LANES

    def sub_dot(sub):
        return _nt_dot(h_ref[...], w_ref[sub * SUB_COLS:(sub + 1) * SUB_COLS, :])

    def emit_qkv(hd, r, win_ref):
        cs = slice(hd * LANES, (hd + 1) * LANES)
        if residue_major:
            slot = hd % de_ref.shape[0]
            de_ref[slot] = r
            q4 = tm // 4
            for a in range(4):
                mid_ref[slot, a * q4:(a + 1) * q4, :] = de_ref[slot, pl.ds(a, q4, stride=4), :]
            for a in range(4):
                for bq in range(4):
                    qkv_ref[0, a + 4 * bq, :, cs] = mid_ref[
                        slot, pl.ds(a * q4 + bq, tm // RES, stride=4), :].astype(qkv_ref.dtype)
        else:
            qkv_ref[:, cs] = r.astype(qkv_ref.dtype)
        if win_ref is not None:
            win_ref[pl.ds(hd, tm, stride=N_HEADS_A), :] = r

    def rotary_tile(scale, win_ref):
        c = cos_ref[...]
        s = sin_ref[...]
        lane = lax.broadcasted_iota(jnp.int32, c.shape, 1)
        for sub in range(WIDTH_A // SUB_COLS):
            acc = sub_dot(sub)
            for hs in range(heads_per_sub):
                a = acc[:, hs * LANES:(hs + 1) * LANES]
                swapped = jnp.where(lane < ROPE_DIM // 2,
                                    pltpu.roll(a, LANES - ROPE_DIM // 2, 1),
                                    pltpu.roll(a, ROPE_DIM // 2, 1))
                r = a * c + swapped * s
                emit_qkv(sub * heads_per_sub + hs, r if scale is None else r * scale, win_ref)

    def plain_tile(win_ref):
        for sub in range(WIDTH_A // SUB_COLS):
            acc = sub_dot(sub)
            for hs in range(heads_per_sub):
                emit_qkv(sub * heads_per_sub + hs, acc[:, hs * LANES:(hs + 1) * LANES], win_ref)

    pl.when(j == 0)(lambda: rotary_tile(HEAD_DIM ** -0.5, None))
    pl.when((j == 1) & keep_rows)(lambda: rotary_tile(None, kf_ref))
    pl.when((j == 1) & jnp.logical_not(keep_rows))(lambda: rotary_tile(None, None))
    pl.when((j == 2) & keep_rows)(lambda: plain_tile(vf_ref))
    pl.when((j == 2) & jnp.logical_not(keep_rows))(lambda: plain_tile(None))

    def rest_tile(with_decode):
        for sub in range(WIDTH_A // SUB_COLS):
            p_ref[:, sub * SUB_COLS:(sub + 1) * SUB_COLS] = sub_dot(sub).astype(p_ref.dtype)
        if with_decode:
            _decode_attn_kernel(*dec_in, *dec_out)

    if n_dec:
        rest_step = i * (pl.num_programs(1) - QKV_TILES) + (j - QKV_TILES)
        pl.when((j >= QKV_TILES) & (rest_step < n_dec))(lambda: rest_tile(True))
        pl.when((j >= QKV_TILES) & (rest_step >= n_dec))(lambda: rest_tile(False))
    else:
        pl.when(j >= QKV_TILES)(lambda: rest_tile(False))


def _rope_tables(pos):
    half = ROPE_DIM // 2
    inv = np.float64(ROPE_THETA) ** (-np.arange(half, dtype=np.float64) / half)
    ang = np.asarray(pos, np.float64)[:, None] * inv[None, :]
    cos, sin = np.cos(ang), np.sin(ang)
    ones = np.ones((ang.shape[0], LANES - ROPE_DIM))
    c = np.concatenate([cos, cos, ones], axis=1)
    s = np.concatenate([-sin, sin, np.zeros_like(ones)], axis=1)
    return jnp.asarray(c, jnp.float32), jnp.asarray(s, jnp.float32)


def _project(x2d, g_pre, w_main_t, w_tail_t, cos_t, sin_t, *, seq, keep, tm, residue_major, decode=None,
             cast_layer=None):
    m = x2d.shape[0]
    tn = WIDTH_A
    n_i, n_j = m // tm, MAIN_COLS // tn
    tiles_per_seq = seq // tm
    first_keep = (seq - keep) // tm
    keep_tiles = keep // tm
    assert seq % tm == 0 and keep % tm == 0 and (seq - keep) % tm == 0
    n_dec = 0
    dec_args, dec_in_specs, dec_out_specs, dec_out_shapes = [], [], [], []
    if decode is not None:
        n_dec = decode[0].shape[0]
        n_rest = n_j - QKV_TILES
        assert n_i * n_rest >= n_dec
        seq_of = lambda i, j: jnp.minimum(i * n_rest + jnp.maximum(j - QKV_TILES, 0), n_dec - 1)
        dec_args, dec_in_specs, dec_out_spec, dec_out_shape = _decode_operands(*decode, seq_of)
        dec_out_specs, dec_out_shapes = [dec_out_spec], [dec_out_shape]
    cast_w = cast_layer is not None
    w_spec = pl.BlockSpec((tn, D_MODEL), lambda i, j: (j, 0))
    cast_specs, cast_shapes = [], []
    if cast_w:
        assert n_i == 1
        cast_specs, cast_shapes = [w_spec], [jax.ShapeDtypeStruct((MAIN_COLS, D_MODEL), jnp.bfloat16)]
        w_spec = pl.BlockSpec((None, tn, D_MODEL), lambda i, j: (cast_layer, j, 0))
    kern = functools.partial(_proj_kernel, tiles_per_seq=tiles_per_seq, first_keep_tile=first_keep,
                             residue_major=residue_major, n_dec=n_dec, cast_w=cast_w)
    qkv_col = lambda j: jnp.minimum(j, QKV_TILES - 1)
    if residue_major:
        assert tm % (RES * 16) == 0
        qkv_spec = pl.BlockSpec((1, RES, tm // RES, tn),
                                lambda i, j: (i // tiles_per_seq, 0, i % tiles_per_seq, qkv_col(j)))
        qkv_shape = jax.ShapeDtypeStruct((m // seq, RES, seq // RES, QKV_TILES * tn), jnp.bfloat16)
    else:
        qkv_spec = pl.BlockSpec((tm, tn), lambda i, j: (i, qkv_col(j)))
        qkv_shape = jax.ShapeDtypeStruct((m, QKV_TILES * tn), jnp.bfloat16)

    def win_index(i, j):
        il = i % tiles_per_seq
        return (i // tiles_per_seq) * keep_tiles + jnp.maximum(il - first_keep, 0), 0

    win_spec = pl.BlockSpec((tm * N_HEADS_A, LANES), win_index, pipeline_mode=pl.Buffered(1))
    win_shape = jax.ShapeDtypeStruct((m // seq * keep * N_HEADS_A, LANES), jnp.float32)

    return pl.pallas_call(
        kern,
        grid=(n_i, n_j),
        in_specs=[
            pl.BlockSpec((tm, D_MODEL), lambda i, j: (i, 0)),
            pl.BlockSpec((1, D_MODEL), lambda i, j: (0, 0)),
            w_spec,
            pl.BlockSpec((LANES, D_MODEL), lambda i, j: (0, 0)),
            pl.BlockSpec((tm, LANES), lambda i, j: (i % tiles_per_seq, 0)),
            pl.BlockSpec((tm, LANES), lambda i, j: (i % tiles_per_seq, 0)),
        ] + dec_in_specs,
        out_specs=[
            qkv_spec,
            win_spec,
            win_spec,
            pl.BlockSpec((tm, tn), lambda i, j: (i, jnp.maximum(j - QKV_TILES, 0))),
            pl.BlockSpec((tm, LANES), lambda i, j: (i, 0)),
            pl.BlockSpec((TAIL_COLS, tm), lambda i, j: (0, i)),
        ] + dec_out_specs + cast_specs,
        out_shape=[
            qkv_shape,
            win_shape,
            win_shape,
            jax.ShapeDtypeStruct((m, REST_COLS), jnp.bfloat16),
            jax.ShapeDtypeStruct((m, LANES), jnp.float32),
            jax.ShapeDtypeStruct((TAIL_COLS, m), jnp.float32),
        ] + dec_out_shapes + cast_shapes,
        scratch_shapes=[pltpu.VMEM((tm, D_MODEL), jnp.bfloat16),
                        pltpu.VMEM((SUB_COLS // LANES, tm, LANES), jnp.float32),
                        pltpu.VMEM((SUB_COLS // LANES, tm, LANES), jnp.float32)],
        compiler_params=_cparams(("arbitrary", "arbitrary")),
        name="proj",
    )(x2d, g_pre, w_main_t, w_tail_t, cos_t, sin_t, *dec_args)


SLAB = 128
PIECE = 16
BLOCK_ROWS = {16: SLAB, 4: SLAB // 4, 1: PIECE}


def _band_bias(dilation):
    g = RES // dilation
    n = BLOCK_ROWS[dilation]
    rho = np.arange(g * n)[:, None]
    kap = np.arange(2 * g * n)[None, :]
    pos_q = g * (rho % n) + rho // n
    pos_k = g * (kap % (2 * n) - n) + kap // (2 * n)
    prev = (kap % (2 * n)) < n
    dist = pos_q - pos_k
    band = (dist >= 0) & (dist <= SLAB)
    out = np.stack([band & ~prev, band])
    return jnp.asarray(np.where(out, 0.0, NEG_BIG), dtype=jnp.float32)


def _attn_body(hp, q_ref, kc_ref, kp_ref, vc_ref, vp_ref, b16_ref, b4_ref, b1_ref, o_ref, nat_ref):
    f32 = jnp.float32

    def block(q, k, v, bias, prev):
        s = _nt_dot(q, k) + bias
        rows, keys = s.shape
        m_new = jnp.broadcast_to(jnp.max(s, axis=1, keepdims=True), (rows, LANES))
        if prev is not None:
            m_prev, l_prev, acc_prev = prev
            m_new = jnp.maximum(m_prev, m_new)
        p = jnp.exp(s - jnp.concatenate([m_new] * (keys // LANES), axis=1))
        v1 = jnp.concatenate([v, jnp.ones_like(v)], axis=1)
        pv = jnp.dot(p.astype(v.dtype), v1, preferred_element_type=f32)
        acc_new, l_new = pv[:, :LANES], pv[:, LANES:]
        if prev is not None:
            alpha = jnp.exp(m_prev - m_new)
            l_new = alpha * l_prev + l_new
            acc_new = alpha * acc_prev + acc_new
        return m_new, l_new, acc_new

    def split(x, n_parts):
        n = x.shape[0] // n_parts
        return [x[i * n:(i + 1) * n] for i in range(n_parts)]

    state = {}
    pieces = SLAB // PIECE

    for r in range(RES):
        k = jnp.concatenate([kp_ref[0, r], kc_ref[0, r]], axis=0)
        v = jnp.concatenate([vp_ref[0, r], vc_ref[0, r]], axis=0)
        res = [split(x, pieces) for x in block(q_ref[0, r], k, v, b16_ref[hp], None)]
        for a in range(pieces):
            state[(r, a)] = tuple(x[a] for x in res)

    def update(keys_, q, k, v, bias):
        prev = tuple(jnp.concatenate([state[key][i] for key in keys_], axis=0) for i in range(3))
        res = [split(x, len(keys_)) for x in block(q, k, v, bias, prev)]
        for i, key in enumerate(keys_):
            state[key] = tuple(x[i] for x in res)

    def gather(cur_ref, prev_ref, slabs, start, n):
        if start == 0:
            parts = [jnp.concatenate([prev_ref[0, r, SLAB - n:SLAB, :], cur_ref[0, r, 0:n, :]], axis=0)
                     for r in slabs]
        else:
            parts = [cur_ref[0, r, start - n:start + n, :] for r in slabs]
        return jnp.concatenate(parts, axis=0)

    n4 = BLOCK_ROWS[4]
    for r4 in range(4):
        slabs = [r4 + 4 * i for i in range(RES // 4)]
        for jj in range(SLAB // n4):
            sl = slice(n4 * jj, n4 * jj + n4)
            q = jnp.concatenate([q_ref[0, r, sl, :] for r in slabs], axis=0)
            k = gather(kc_ref, kp_ref, slabs, n4 * jj, n4)
            v = gather(vc_ref, vp_ref, slabs, n4 * jj, n4)
            keys_ = [(r, (n4 // PIECE) * jj + a) for r in slabs for a in range(n4 // PIECE)]
            update(keys_, q, k, v, b4_ref[hp] if jj == 0 else b4_ref[1])

    slabs = list(range(RES))
    for jj in range(pieces):
        sl = slice(PIECE * jj, PIECE * jj + PIECE)
        q = jnp.concatenate([q_ref[0, r, sl, :] for r in slabs], axis=0)
        k = gather(kc_ref, kp_ref, slabs, PIECE * jj, PIECE)
        v = gather(vc_ref, vp_ref, slabs, PIECE * jj, PIECE)
        update([(r, jj) for r in slabs], q, k, v, b1_ref[hp] if jj == 0 else b1_ref[1])

    for r in range(RES):
        acc = jnp.concatenate([state[(r, a)][2] for a in range(pieces)], axis=0)
        l = jnp.concatenate([state[(r, a)][1] for a in range(pieces)], axis=0)
        nat_ref[pl.ds(r, SLAB, stride=RES), :] = acc / l
    o_ref[...] = nat_ref[...].astype(o_ref.dtype)


DN_STEP = 256
CARRY = 8


def _split_bf16(x, n):
    parts, r = [], x
    for _ in range(n):
        hi = r.astype(jnp.bfloat16)
        parts.append(hi)
        r = r - hi.astype(jnp.float32)
    return parts


def _bdot(a, b):
    return jnp.dot(a.astype(jnp.bfloat16), b.astype(jnp.bfloat16), preferred_element_type=jnp.float32)


def _softplus(x):
    return jnp.maximum(x, 0.0) + jnp.log1p(jnp.exp(-jnp.abs(x)))


def _silu(x):
    return x * jax.nn.sigmoid(x)


def _unit_lower_inverses(a_list):
    n = a_list[0].shape[0]
    row = lax.broadcasted_iota(jnp.int32, (n, n), 0)
    col = lax.broadcasted_iota(jnp.int32, (n, n), 1)
    eye = jnp.where(row == col, 1.0, 0.0)
    xs = [eye - a for a in a_list]
    ps = [_bdot(a, a) for a in a_list]
    k = 2
    while True:
        xs = [x + _bdot(x, p) for x, p in zip(xs, ps)]
        k *= 2
        if k >= n:
            return xs
        ps = [_bdot(p, p) for p in ps]


def _dn_init(c, s_ref, xe_ref):
    @pl.when(c == 0)
    def _():
        s_ref[...] = jnp.zeros_like(s_ref)
        xe_ref[0:CARRY, :] = jnp.zeros((CARRY, CONV_CH), jnp.float32)

    @pl.when(c > 0)
    def _():
        xe_ref[0:CARRY, :] = xe_ref[DN_STEP:DN_STEP + CARRY, :]


def _dn_body(qb_ref, kb_ref, vb_ref, tail_ref, tailt_ref, cw_ref, prow_ref, pcol_ref, onw_ref,
             o_ref, s_ref, xe_ref):
    f32, bf16 = jnp.float32, jnp.bfloat16
    tb, n, hh = DN_STEP, CHUNK, N_HEADS_B
    n_ch = tb // n

    conv = []
    for part, ref in enumerate((qb_ref, kb_ref, vb_ref)):
        cs = slice(part * WIDTH_B, (part + 1) * WIDTH_B)
        xe_ref[CARRY:CARRY + tb, cs] = ref[...].astype(f32)
        y = xe_ref[CARRY:CARRY + tb, cs] * cw_ref[CONV_WIDTH - 1:CONV_WIDTH, cs]
        for i in range(CONV_WIDTH - 1):
            off = CARRY - (CONV_WIDTH - 1) + i
            y = y + xe_ref[off:off + tb, cs] * cw_ref[i:i + 1, cs]
        conv.append(_silu(y))
    qc, kc, vc = conv

    t = tail_ref[...]
    beta_c = jax.nn.sigmoid(t)
    g_c = -jnp.exp(prow_ref[0:1]) * _softplus(t + prow_ref[1:2])
    tt = tailt_ref[...]
    g_r = -jnp.exp(pcol_ref[:, 0:1]) * _softplus(tt + pcol_ref[:, 1:2])

    row = lax.broadcasted_iota(jnp.int32, (n, n), 0)
    col = lax.broadcasted_iota(jnp.int32, (n, n), 1)
    incl = row >= col
    strict = row > col
    ltri = jnp.where(incl, 1.0, 0.0).astype(bf16)
    utri = jnp.where(row <= col, 1.0, 0.0).astype(bf16)
    dot = functools.partial(jnp.dot, preferred_element_type=f32)

    gcs, grs, eg_c, ed_c, glast = [], [], [], [], []
    for ci in range(n_ch):
        rs = slice(ci * n, (ci + 1) * n)
        gc = sum(dot(ltri, part) for part in _split_bf16(g_c[rs], 3))
        gr = sum(dot(part, utri) for part in _split_bf16(g_r[:, rs], 3))
        gl = gc[n - 1:n, :]
        gcs.append(gc)
        grs.append(gr)
        glast.append(gl)
        eg_c.append(jnp.exp(gc))
        ed_c.append(jnp.exp(gl - gc))

    ids = [(ci, h) for ci in range(n_ch) for h in range(hh)]

    def head_slice(x, ci, h):
        return x[ci * n:(ci + 1) * n, h * DK_B:(h + 1) * DK_B]

    def lane_col(x, lane):
        return x[:, lane:lane + 1]

    qs, ks, kbetas, vbetas, egs = [], [], [], [], []
    for ci, h in ids:
        q = head_slice(qc, ci, h)
        k = head_slice(kc, ci, h)
        v = head_slice(vc, ci, h)
        q = q * (lax.rsqrt(jnp.sum(q * q, axis=-1, keepdims=True) + EPS) * (DK_B ** -0.5))
        k = k * lax.rsqrt(jnp.sum(k * k, axis=-1, keepdims=True) + EPS)
        beta = lane_col(beta_c[ci * n:(ci + 1) * n], h)
        qs.append(q)
        ks.append(k)
        kbetas.append(k * beta)
        vbetas.append(v * beta)
        egs.append(lane_col(eg_c[ci], hh + h))

    kqs = [_nt_dot(jnp.concatenate([kb, q], axis=0).astype(bf16), k.astype(bf16))
           for kb, q, k in zip(kbetas, qs, ks)]
    a_mats, qks = [], []
    for (ci, h), kq in zip(ids, kqs):
        gcol = lane_col(gcs[ci], hh + h)
        grow = grs[ci][hh + h:hh + h + 1, :]
        decay = jnp.exp(jnp.where(incl, gcol - grow, NEG_BIG))
        a_mats.append(jnp.where(strict, kq[:n] * decay, 0.0))
        qks.append((kq[n:] * decay).astype(bf16))
    t_mats = _unit_lower_inverses(a_mats)
    uws = [dot(tm.astype(bf16), jnp.concatenate([vb, kb * eg], axis=1).astype(bf16))
           for tm, vb, kb, eg in zip(t_mats, vbetas, kbetas, egs)]

    states = [s_ref[0, h] for h in range(hh)]
    onw = onw_ref[...]
    for ci in range(n_ch):
        base = ci * hh
        wqs = [jnp.concatenate([uws[base + h][:, DV_B:], qs[base + h] * egs[base + h]], axis=0).astype(bf16)
               for h in range(hh)]
        wss = [dot(wq, st.astype(bf16)) for wq, st in zip(wqs, states)]
        v_news = [(uws[base + h][:, :DV_B] - wss[h][:n]).astype(bf16) for h in range(hh)]
        o_in = [dot(qks[base + h], v_news[h]) for h in range(hh)]
        k_decs = [(ks[base + h] * lane_col(ed_c[ci], hh + h)).astype(bf16) for h in range(hh)]
        upds = [lax.dot_general(k_decs[h], v_news[h], (((0,), (0,)), ((), ())), preferred_element_type=f32)
                for h in range(hh)]
        e_last = jnp.exp(glast[ci])
        states = [states[h] * lane_col(e_last, hh + h) + upds[h] for h in range(hh)]
        for h in range(hh):
            o = wss[h][n:] + o_in[h]
            o = o * lax.rsqrt(jnp.mean(o * o, axis=-1, keepdims=True) + EPS) * onw
            o_ref[ci * n:(ci + 1) * n, h * DV_B:(h + 1) * DV_B] = o.astype(o_ref.dtype)
    for h in range(hh):
        s_ref[0, h] = states[h]


N_ATTN_IN, N_DN_IN = 8, 9


def _mixers_kernel(*refs, steps):
    s = pl.program_id(0)
    attn_in = refs[:N_ATTN_IN]
    dn_in = refs[N_ATTN_IN:N_ATTN_IN + N_DN_IN]
    wo_ref = refs[N_ATTN_IN + N_DN_IN]
    o_a_ref, o_b_ref, s_ref, wo_bf_ref, nat_ref, xe_ref = refs[N_ATTN_IN + N_DN_IN + 1:]
    n_sb = steps // N_HEADS_A
    _dn_init(s % steps, s_ref, xe_ref)
    wo_bf_ref[...] = wo_ref[...].astype(wo_bf_ref.dtype)
    _attn_body(jnp.minimum(s % n_sb, 1), *attn_in, o_a_ref, nat_ref)
    _dn_body(*dn_in, o_b_ref, s_ref, xe_ref)


def _prompt_mixers(qkv16, p, tail, tail_t, conv_w, a_log, dt_bias, onorm_w, w_out, layer):
    batch, _, rows, _ = qkv16.shape
    m = p.shape[0]
    n_sb = rows // SLAB
    steps = m // batch // DN_STEP
    n_steps = batch * steps
    assert N_HEADS_A * n_sb == steps
    wo_rows, wo_cols = w_out.shape[1:]
    assert wo_rows % (n_steps * 16) == 0
    wo_blk = wo_rows // n_steps
    kcol, vcol = WIDTH_A // LANES, 2 * WIDTH_A // LANES
    blk = (1, RES, SLAB, LANES)
    a_idx = lambda s: (s // steps, (s // n_sb) % N_HEADS_A, s % n_sb)
    cur = lambda off: pl.BlockSpec(blk, lambda s: (a_idx(s)[0], 0, a_idx(s)[2], off + a_idx(s)[1]))
    prv = lambda off: pl.BlockSpec(
        blk, lambda s: (a_idx(s)[0], 0, jnp.maximum(a_idx(s)[2] - 1, 0), off + a_idx(s)[1]))
    const = lambda shp: pl.BlockSpec(shp, lambda s: (0,) * len(shp))
    wb = WIDTH_B
    qcol = REST_QKVB // wb
    zeros8 = jnp.zeros((N_HEADS_B,), jnp.float32)
    gate = jnp.stack([jnp.concatenate([zeros8, a_log.astype(jnp.float32)]),
                      jnp.concatenate([zeros8, dt_bias.astype(jnp.float32)])])
    prow = jnp.pad(gate, ((0, 6), (0, LANES - TAIL_COLS)))
    pcol = jnp.pad(gate.T, ((0, 0), (0, LANES - 2)))
    tok = lambda cb: pl.BlockSpec((DN_STEP, wb), lambda s: (s, cb))
    biases = [_band_bias(dil) for _, dil in reversed(DILATIONS)]
    return pl.pallas_call(
        functools.partial(_mixers_kernel, steps=steps),
        grid=(n_steps,),
        in_specs=[cur(0), cur(kcol), prv(kcol), cur(vcol), prv(vcol)]
                 + [const(bias.shape) for bias in biases] + [
                  tok(qcol), tok(qcol + 1), tok(qcol + 2),
                  pl.BlockSpec((DN_STEP, LANES), lambda s: (s, 0)),
                  pl.BlockSpec((TAIL_COLS, DN_STEP), lambda s: (0, s)),
                  const((CONV_WIDTH, CONV_CH)), const((8, LANES)), const((TAIL_COLS, LANES)),
                  const((1, DV_B)),
                  pl.BlockSpec((None, wo_blk, wo_cols), lambda s: (layer, s, 0))],
        out_specs=[pl.BlockSpec((RES * SLAB, LANES),
                                lambda s: (a_idx(s)[0] * n_sb + a_idx(s)[2], a_idx(s)[1])),
                   pl.BlockSpec((DN_STEP, WIDTH_B), lambda s: (s, 0)),
                   pl.BlockSpec((1, N_HEADS_B, DK_B, DV_B), lambda s: (s // steps, 0, 0, 0)),
                   pl.BlockSpec((wo_blk, wo_cols), lambda s: (s, 0))],
        out_shape=[jax.ShapeDtypeStruct((m, WIDTH_A), jnp.bfloat16),
                   jax.ShapeDtypeStruct((m, WIDTH_B), jnp.bfloat16),
                   jax.ShapeDtypeStruct((batch, N_HEADS_B, DK_B, DV_B), jnp.float32),
                   jax.ShapeDtypeStruct((wo_rows, wo_cols), jnp.bfloat16)],
        scratch_shapes=[pltpu.VMEM((RES * SLAB, LANES), jnp.float32),
                        pltpu.VMEM((CARRY + DN_STEP, CONV_CH), jnp.float32)],
        compiler_params=_cparams(("arbitrary",)),
        name="prompt_mixers",
    )(qkv16, qkv16, qkv16, qkv16, qkv16, *biases,
      p, p, p, tail, tail_t, conv_w.astype(jnp.float32), prow, pcol, onorm_w.astype(jnp.float32)[None],
      w_out)


N_OUT_IN, N_DDN_IN = 7, 7


def _out_kernel(*refs, with_decode):
    oa_ref, za_ref, ob_ref, zb_ref, w_ref, g_ref, x_ref = refs[:N_OUT_IN]
    if with_decode:
        ddn_in = refs[N_OUT_IN:N_OUT_IN + N_DDN_IN]
        y_ref = refs[N_OUT_IN + N_DDN_IN]
        _decode_dn_kernel(*ddn_in, *refs[N_OUT_IN + N_DDN_IN + 1:])
    else:
        y_ref = refs[N_OUT_IN]
    f32 = jnp.float32
    ga = (oa_ref[...].astype(f32) * _silu(za_ref[...].astype(f32))).astype(jnp.bfloat16)
    gb = (ob_ref[...].astype(f32) * _silu(zb_ref[...].astype(f32))).astype(jnp.bfloat16)
    y = (jnp.dot(ga, w_ref[:WIDTH_A, :], preferred_element_type=f32)
         + jnp.dot(gb, w_ref[WIDTH_A:, :], preferred_element_type=f32))
    y = y * lax.rsqrt(jnp.mean(y * y, axis=-1, keepdims=True) + EPS) * g_ref[...]
    y_ref[...] = x_ref[...] + y


def _output_sublayer(o_a, p, o_b, w_out, g_post, x2d, *, tm, decode_dn=None):
    m = x2d.shape[0]
    row = lambda width, cb: pl.BlockSpec((tm, width), lambda i: (i, cb))
    dn_args, dn_in, dn_out, dn_shapes = decode_dn if decode_dn is not None else ([], [], [], [])
    res = pl.pallas_call(
        functools.partial(_out_kernel, with_decode=decode_dn is not None),
        grid=(m // tm,),
        in_specs=[row(WIDTH_A, 0), row(WIDTH_A, REST_ZA // WIDTH_A), row(WIDTH_B, 0),
                  row(WIDTH_B, REST_ZB // WIDTH_B),
                  pl.BlockSpec((WIDTH_A + WIDTH_B, D_MODEL), lambda i: (0, 0)),
                  pl.BlockSpec((1, D_MODEL), lambda i: (0, 0)),
                  row(D_MODEL, 0)] + dn_in,
        out_specs=[row(D_MODEL, 0)] + dn_out,
        out_shape=[jax.ShapeDtypeStruct((m, D_MODEL), jnp.float32)] + dn_shapes,
        compiler_params=_cparams(("arbitrary",)),
        name="out_proj",
    )(o_a, p, o_b, p, w_out, g_post, x2d, *dn_args)
    return res if decode_dn is not None else res[0]


def _decode_attn_kernel(q_ref, kn_ref, vn_ref, k1_ref, v1_ref, k4_ref, v4_ref, k16_ref, v16_ref, o_ref):
    f32 = jnp.float32
    q = q_ref[0].astype(f32)
    kn, vn = kn_ref[0].astype(f32), vn_ref[0].astype(f32)
    s_new = jnp.sum(q * kn, axis=-1, keepdims=True)
    scores = [jnp.sum(k_ref[0] * q[None], axis=-1, keepdims=True) for k_ref in (k1_ref, k4_ref, k16_ref)]
    m = s_new
    for s in scores:
        m = jnp.maximum(m, jnp.max(s, axis=0))
    p_new = len(DILATIONS) * jnp.exp(s_new - m)
    den = p_new
    acc = p_new * vn
    for s, v_ref in zip(scores, (v1_ref, v4_ref, v16_ref)):
        p = jnp.exp(s - m[None])
        den = den + jnp.sum(p, axis=0)
        acc = acc + jnp.sum(p * v_ref[0], axis=0)
    o_ref[0] = (acc / den).astype(o_ref.dtype)


N_DEC_OPERANDS = 9


def _decode_operands(q, k_new, v_new, cache_k, cache_v, seq_of):
    b, win, h, dd = cache_k.shape
    nb = 128
    views, specs = [], []
    for window, dil in DILATIONS:
        assert window // dil == nb and win % (nb * dil) == 0
        blk = win // (nb * dil) - 1
        if dil == 1:
            spec = pl.BlockSpec((1, nb, h, dd), lambda *ids, blk=blk: (seq_of(*ids), blk, 0, 0))
            view = lambda c: c
        else:
            spec = pl.BlockSpec((1, nb, None, h, dd), lambda *ids, blk=blk: (seq_of(*ids), blk, 0, 0, 0))
            view = lambda c, dil=dil: c.reshape(b, win // dil, dil, h, dd)
        specs += [spec, spec]
        views += [view(cache_k), view(cache_v)]
    tok = pl.BlockSpec((1, h, dd), lambda *ids: (seq_of(*ids), 0, 0))
    return ([q, k_new, v_new] + views, [tok, tok, tok] + specs, tok,
            jax.ShapeDtypeStruct((b, h, dd), jnp.bfloat16))


DEC_BB = 4


def _decode_dn_kernel(x_ref, cb_ref, cw_ref, gate_ref, prm_ref, onw_ref, s_ref,
                      o_ref, cbo_ref, so_ref):
    f32, bf16 = jnp.float32, jnp.bfloat16
    hh = N_HEADS_B
    cw = cw_ref[...]
    zeros6 = jnp.zeros((8 - 2, DK_B), bf16)
    zeros7 = jnp.zeros((8 - 1, DV_B), bf16)
    pairs, lhss, a_s, betas, vs, qks = [], [], [], [], [], []
    for b in range(x_ref.shape[0]):
        xn = x_ref[b]
        y = xn * cw[CONV_WIDTH - 1]
        for i in range(CONV_WIDTH - 1):
            y = y + cb_ref[b, i] * cw[i]
            cbo_ref[b, i] = cb_ref[b, i + 1] if i + 1 < CONV_WIDTH - 1 else xn
        y = _silu(y)
        q, k, v = y[:hh], y[hh:2 * hh], y[2 * hh:]
        q = q * lax.rsqrt(jnp.sum(q * q, axis=-1, keepdims=True) + EPS) * (DK_B ** -0.5)
        k = k * lax.rsqrt(jnp.sum(k * k, axis=-1, keepdims=True) + EPS)
        gate = gate_ref[b]
        beta = jax.nn.sigmoid(gate[:hh])
        decay = jnp.exp(-jnp.exp(prm_ref[:hh]) * _softplus(gate[hh:] + prm_ref[hh:]))
        qk = jnp.sum(q * k, axis=-1, keepdims=True)
        for h in range(hh):
            pairs.append((b, h))
            lhss.append(jnp.concatenate([k[h:h + 1].astype(bf16), q[h:h + 1].astype(bf16), zeros6], axis=0))
            a_s.append(decay[h:h + 1])
            betas.append(beta[h:h + 1])
            vs.append(v[h:h + 1])
            qks.append(qk[h:h + 1])
    states = [s_ref[b, h] for b, h in pairs]
    kss = [jnp.dot(lhs, st.astype(bf16), preferred_element_type=f32) for lhs, st in zip(lhss, states)]
    v_news = [beta * (v - a * ks[0:1]) for beta, v, a, ks in zip(betas, vs, a_s, kss)]
    upds = [lax.dot_general(lhs, jnp.concatenate([vn.astype(bf16), zeros7], axis=0),
                            (((0,), (0,)), ((), ())), preferred_element_type=f32)
            for lhs, vn in zip(lhss, v_news)]
    for (b, h), st, a, upd in zip(pairs, states, a_s, upds):
        so_ref[b, h] = st * a[:, 0:1] + upd
    onw = onw_ref[...]
    for b in range(x_ref.shape[0]):
        o = jnp.concatenate([a_s[b * hh + h] * kss[b * hh + h][1:2] + qks[b * hh + h] * v_news[b * hh + h]
                             for h in range(hh)], axis=0)
        o = o * lax.rsqrt(jnp.mean(o * o, axis=-1, keepdims=True) + EPS) * onw
        o_ref[b] = o.astype(o_ref.dtype)


def _decode_dn_operands(x_new, conv_buf, state, beta_in, a_in, conv_w, a_log, dt_bias, onorm_w, bb):
    b = x_new.shape[0]
    f32 = jnp.float32
    g3 = CONV_CH // LANES
    hh = N_HEADS_B
    gate = jnp.broadcast_to(jnp.concatenate([beta_in, a_in], axis=1).astype(f32)[:, :, None], (b, 2 * hh, LANES))
    prm = jnp.broadcast_to(jnp.concatenate([a_log, dt_bias]).astype(f32)[:, None], (2 * hh, LANES))
    full = lambda shp: pl.BlockSpec(shp, lambda i: (0,) * len(shp))
    per = lambda shp: pl.BlockSpec((bb,) + shp, lambda i: (i,) + (0,) * len(shp))
    args = [x_new.astype(f32).reshape(b, g3, LANES), conv_buf.astype(f32).reshape(b, CONV_WIDTH - 1, g3, LANES),
            conv_w.astype(f32).reshape(CONV_WIDTH, g3, LANES), gate, prm, onorm_w.astype(f32)[None],
            state.astype(f32)]
    in_specs = [per((g3, LANES)), per((CONV_WIDTH - 1, g3, LANES)), full((CONV_WIDTH, g3, LANES)),
                per((2 * hh, LANES)), full((2 * hh, LANES)), full((1, DV_B)), per((hh, DK_B, DV_B))]
    out_specs = [per((hh, DV_B)), per((CONV_WIDTH - 1, g3, LANES)), per((hh, DK_B, DV_B))]
    out_shapes = [jax.ShapeDtypeStruct((b, hh, DV_B), jnp.bfloat16),
                  jax.ShapeDtypeStruct((b, CONV_WIDTH - 1, g3, LANES), f32),
                  jax.ShapeDtypeStruct((b, hh, DK_B, DV_B), f32)]
    return args, in_specs, out_specs, out_shapes


def _decode_deltanet(*operands):
    b = operands[0].shape[0]
    bb = DEC_BB if b % DEC_BB == 0 else 1
    args, in_specs, out_specs, out_shapes = _decode_dn_operands(*operands, bb)
    return pl.pallas_call(
        _decode_dn_kernel,
        grid=(b // bb,),
        in_specs=in_specs,
        out_specs=out_specs,
        out_shape=out_shapes,
        compiler_params=_cparams(("arbitrary",)),
        name="decode_deltanet",
    )(*args)


def kernel(x_prompt, x_sample, cache_win_k, cache_win_v, state_conv, state_delta,
           g_pre, w_in, conv_w, a_log, dt_bias, onorm_w, w_out, g_post):
    f32, bf16 = jnp.float32, jnp.bfloat16
    b, s, _ = x_prompt.shape
    db, t, _ = x_sample.shape
    depth = w_in.shape[0]
    n_past = cache_win_k.shape[2]
    assert t == 1 and n_past == MAX_WINDOW and s % (RES * SLAB) == 0
    keep = min(MAX_WINDOW, s)

    cos_p, sin_p = _rope_tables(np.arange(s))
    cos_s, sin_s = _rope_tables(np.full((db,), PAST_LEN))

    yp = x_prompt.reshape(b * s, D_MODEL)
    ys = x_sample.reshape(db, D_MODEL)
    outs = [[] for _ in range(8)]
    for l in range(depth):
        w_in_t = jnp.swapaxes(w_in, 1, 2)
        w_tail = jnp.pad(w_in_t[l, MAIN_COLS:, :], ((0, LANES - TAIL_COLS), (0, 0))).astype(bf16)
        gp, go = g_pre[l].astype(f32)[None], g_post[l].astype(f32)[None]

        qkv_s, kf_s, vf_s, rest_s, tail_s, _, w_main = _project(
            ys, gp, w_in_t, w_tail, cos_s, sin_s, seq=db, keep=db, tm=db, residue_major=False, cast_layer=l)
        k_new = kf_s.reshape(db, N_HEADS_A, HEAD_DIM)
        v_new = vf_s.reshape(db, N_HEADS_A, HEAD_DIM)
        q_s = qkv_s[:, :WIDTH_A].reshape(db, N_HEADS_A, HEAD_DIM)

        qkv16, kf, vf, rest, tail, tail_t, o_as = _project(
            yp, gp, w_main, w_tail, cos_p, sin_p, seq=s, keep=keep, tm=1024, residue_major=True,
            decode=(q_s, k_new, v_new, cache_win_k[l], cache_win_v[l]))
        o_a, o_b, s_fin, w_o = _prompt_mixers(qkv16, rest, tail, tail_t, conv_w[l], a_log[l], dt_bias[l],
                                              onorm_w[l], w_out, l)
        outs[0].append(kf.reshape(b, keep, N_HEADS_A, HEAD_DIM))
        outs[1].append(vf.reshape(b, keep, N_HEADS_A, HEAD_DIM))
        n_tail = min(CONV_WIDTH - 1, s)
        tail_rows = rest.reshape(b, s, REST_COLS)[:, s - n_tail:, REST_QKVB:REST_QKVB + CONV_CH].astype(f32)
        outs[2].append(jnp.pad(tail_rows, ((0, 0), (CONV_WIDTH - 1 - n_tail, 0), (0, 0))))
        outs[3].append(s_fin)
        dn_operands = (rest_s[:, REST_QKVB:REST_QKVB + CONV_CH], state_conv[l], state_delta[l],
                       tail_s[:, :N_HEADS_B], tail_s[:, N_HEADS_B:TAIL_COLS], conv_w[l], a_log[l],
                       dt_bias[l], onorm_w[l])
        tm_out = 512
        if (b * s) // tm_out * 2 == db:
            yp, o_bs, cb_new, st_new = _output_sublayer(
                o_a, rest, o_b, w_o, go, yp, tm=tm_out, decode_dn=_decode_dn_operands(*dn_operands, 2))
        else:
            yp = _output_sublayer(o_a, rest, o_b, w_o, go, yp, tm=tm_out)
            o_bs, cb_new, st_new = _decode_deltanet(*dn_operands)
        cb_new = cb_new.reshape(db, CONV_WIDTH - 1, CONV_CH)
        outs[4].append(k_new.reshape(db, t, N_HEADS_A, HEAD_DIM))
        outs[5].append(v_new.reshape(db, t, N_HEADS_A, HEAD_DIM))
        outs[6].append(cb_new)
        outs[7].append(st_new)
        ys = _output_sublayer(o_as.reshape(db, WIDTH_A), rest_s, o_bs.reshape(db, WIDTH_B), w_o, go, ys, tm=db)

    stk = [jnp.stack(o) for o in outs]
    return (yp.reshape(b, s, D_MODEL), ys.reshape(db, t, D_MODEL),
            stk[0], stk[1], stk[2], stk[3], stk[4], stk[5], stk[6], stk[7])
```
